```python
import jax, jax.numpy as jnp
from jax import lax
import numpy as np

D_MODEL = 1024
BATCH = 8
SEQ = 2048
DEPTH = 2

GRID_W = 64
CTX_LEN = 256
EPS = 1e-6
F32 = jnp.float32

A_HEADS = 4
A_HEAD_DIM = 64
A_WIDTH = A_HEADS * A_HEAD_DIM
A_CHUNK = 128

B_HEADS = 8
B_NOPE = 64
B_ROPE = 32
B_QK = B_NOPE + B_ROPE
B_V = 64
B_WIDTH = B_HEADS * B_V
B_Q_RANK = 256
B_KV_RANK = 128
B_QBLOCK = 128
ROPE_THETA = 10000.0

C_HEADS = 4
C_DK = 64
C_DV = 64
C_KWIDTH = C_HEADS * C_DK
C_WIDTH = C_HEADS * C_DV
C_CHUNK = 64

D_MIX = A_WIDTH + B_WIDTH + C_WIDTH
IN_SPLITS = (A_WIDTH, A_WIDTH, B_Q_RANK, B_KV_RANK, B_ROPE, C_KWIDTH, C_KWIDTH, C_KWIDTH, C_WIDTH, C_WIDTH)
D_IN = sum(IN_SPLITS)
SPLIT_IDX = tuple(sum(IN_SPLITS[:i + 1]) for i in range(len(IN_SPLITS) - 1))

P_HEADS = 8
P_KEY_DIM = 256
P_N_KEYS = 128
P_N_EXPERTS = P_N_KEYS * P_N_KEYS
P_TOPK = 16
P_TOKEN_BLOCK = 128

kernel_name = 'hybrid_dit_gmlp_mla_hgrn2_peer'


def rmsnorm(x, w):
    xf = x.astype(F32)
    y = xf * lax.rsqrt(jnp.mean(xf * xf, axis=-1, keepdims=True) + EPS)
    return (y * w.astype(F32)).astype(x.dtype)


def modulate(x, shift, scale):
    return x * (1 + scale) + shift


def axial_rope_angles(rows):
    n_freq = B_ROPE // 4
    row = jnp.repeat(jnp.arange(rows, dtype=F32), GRID_W)
    col = jnp.tile(jnp.arange(GRID_W, dtype=F32), rows)
    inv_freq = ROPE_THETA ** (-jnp.arange(n_freq, dtype=F32) / n_freq)
    ang = jnp.stack([row[:, None] * inv_freq, col[:, None] * inv_freq], axis=1)
    return jnp.cos(ang), jnp.sin(ang)


def apply_rope(x, cos, sin):
    shp = x.shape
    xr = x.astype(F32).reshape(shp[:-1] + (2, 2, B_ROPE // 4))
    x1, x2 = xr[..., 0, :], xr[..., 1, :]
    c, s = cos[:, None], sin[:, None]
    out = jnp.stack([x1 * c - x2 * s, x2 * c + x1 * s], axis=-2)
    return out.reshape(shp).astype(x.dtype)


def chunk_gating_mlp(u, v, norm_w, w_s, b_s):
    bsz, n, _ = v.shape
    vn = rmsnorm(v.reshape(bsz, n, A_HEADS, A_HEAD_DIM), norm_w.reshape(A_HEADS, A_HEAD_DIM))
    vn = vn.reshape(bsz, n // A_CHUNK, A_CHUNK, A_HEADS, A_HEAD_DIM)
    mixed = jnp.einsum('hpq,bnqhc->bnphc', w_s, vn) + b_s.T[:, :, None]
    return u * mixed.reshape(bsz, n, A_WIDTH)


def mla_project(c_q, c_kv, k_rope, q_norm_w, w_uq, kv_norm_w, w_ukv, qn_w, kn_w, rope):
    bsz, n, _ = c_q.shape
    q = (rmsnorm(c_q, q_norm_w) @ w_uq).reshape(bsz, n, B_HEADS, B_QK)
    kv = (rmsnorm(c_kv, kv_norm_w) @ w_ukv).reshape(bsz, n, B_HEADS, B_NOPE + B_V)
    k_nope, v = kv[..., :B_NOPE], kv[..., B_NOPE:]
    k = jnp.concatenate([k_nope, jnp.broadcast_to(k_rope[:, :, None, :], (bsz, n, B_HEADS, B_ROPE))], axis=-1)
    q, k = rmsnorm(q, qn_w), rmsnorm(k, kn_w)
    if rope is not None:
        cos, sin = rope
        q = jnp.concatenate([q[..., :B_NOPE], apply_rope(q[..., B_NOPE:], cos, sin)], axis=-1)
        k = jnp.concatenate([k[..., :B_NOPE], apply_rope(k[..., B_NOPE:], cos, sin)], axis=-1)
    return q, k, v


def attend(q, k, v):
    s = jnp.einsum('bhqd,bhkd->bhqk', q.astype(F32), k.astype(F32)) * (B_QK ** -0.5)
    p = jax.nn.softmax(s, axis=-1)
    return jnp.einsum('bhqk,bhkd->bhqd', p, v.astype(F32)).astype(v.dtype)


def mla_latent(q, k, v, k_ctx, v_ctx):
    bsz, n, _, _ = q.shape
    kt = jnp.concatenate([k_ctx, k], axis=1).transpose(0, 2, 1, 3)
    vt = jnp.concatenate([v_ctx, v], axis=1).transpose(0, 2, 1, 3)
    nb = n // B_QBLOCK
    qb = q.reshape(bsz, nb, B_QBLOCK, B_HEADS, B_QK).transpose(1, 0, 3, 2, 4)
    ob = lax.map(lambda qi: attend(qi, kt, vt), qb)
    return ob.transpose(1, 0, 3, 2, 4).reshape(bsz, n, B_WIDTH)


def mla_context(q, k, v):
    bsz, n, _, _ = q.shape
    o = attend(q.transpose(0, 2, 1, 3), k.transpose(0, 2, 1, 3), v.transpose(0, 2, 1, 3))
    return o.transpose(0, 2, 1, 3).reshape(bsz, n, B_WIDTH)


def to_heads(t):
    bsz, n, _ = t.shape
    return t.reshape(bsz, n, C_HEADS, -1).transpose(0, 2, 1, 3)


def hgrn2_gates(z, lb):
    z = z.astype(F32)
    lb = lb.astype(F32)
    logf = jnp.logaddexp(jnp.log(lb), jnp.log1p(-lb) + jax.nn.log_sigmoid(z))
    k = (1.0 - lb) * jax.nn.sigmoid(-z)
    return to_heads(logf), to_heads(k)


def gla_scan(q, k, v, logf, s0):
    bsz, h, n, _ = q.shape
    nc = n // C_CHUNK

    def to_chunks(t):
        return jnp.moveaxis(t.reshape(bsz, h, nc, C_CHUNK, t.shape[-1]), 2, 0)

    incl = jnp.tril(jnp.ones((C_CHUNK, C_CHUNK), dtype=bool))

    def step(state, inp):
        qc, kc, vc, lf = inp
        b = jnp.cumsum(lf, axis=2)
        o_inter = jnp.einsum('bhtk,bhkv->bhtv', qc * jnp.exp(b), state)
        diff = jnp.where(incl[:, :, None], b[:, :, :, None, :] - b[:, :, None, :, :], -jnp.inf)
        att = jnp.einsum('bhtk,bhsk,bhtsk->bhts', qc, kc, jnp.exp(diff))
        o = o_inter + jnp.einsum('bhts,bhsv->bhtv', att, vc)
        b_last = b[:, :, -1:, :]
        state = jnp.exp(b_last[:, :, 0, :, None]) * state + jnp.einsum('bhsk,bhsv->bhkv', kc * jnp.exp(b_last - b), vc)
        return state, o

    s_final, o = lax.scan(step, s0, (to_chunks(q), to_chunks(k), to_chunks(v), to_chunks(logf)))
    return jnp.moveaxis(o, 0, 2).reshape(bsz, h, n, v.shape[-1]), s_final


def hgrn2_scan(cq, zf, zb, ci, lb_f, lb_b, s0_f, s0_b):
    dt = cq.dtype
    q = to_heads(cq.astype(F32)) * (C_DK ** -0.5)
    v = to_heads(ci.astype(F32))
    logf_f, k_f = hgrn2_gates(zf, lb_f)
    logf_b, k_b = hgrn2_gates(zb, lb_b)
    flip = lambda t: jnp.flip(t, axis=2)
    o_f, s_f = gla_scan(q, k_f, v, logf_f, s0_f)
    o_b, s_b = gla_scan(flip(q), flip(k_b), flip(v), flip(logf_b), s0_b)
    return (o_f + flip(o_b)).astype(dt), s_f, s_b


def hgrn2_readout(o, g, norm_w):
    bsz, _, n, _ = o.shape
    o = o.transpose(0, 2, 1, 3)
    g = g.reshape(bsz, n, C_HEADS, C_DV)
    return (rmsnorm(o, norm_w) * jax.nn.silu(g)).reshape(bsz, n, C_WIDTH)


def peer_route(h, w_q, sub_keys):
    t = h.shape[0]
    q = (h @ w_q).reshape(t, P_HEADS, 2, P_KEY_DIM // 2).astype(F32)
    s = jnp.einsum('thpd,hpnd->thpn', q, sub_keys.astype(F32))
    sv, si = lax.top_k(s, P_TOPK)
    cand = sv[:, :, 0, :, None] + sv[:, :, 1, None, :]
    best, ci = lax.top_k(cand.reshape(t, P_HEADS, P_TOPK * P_TOPK), P_TOPK)
    i1 = jnp.take_along_axis(si[:, :, 0], ci // P_TOPK, axis=-1)
    i2 = jnp.take_along_axis(si[:, :, 1], ci % P_TOPK, axis=-1)
    experts = (i1 * P_N_KEYS + i2).reshape(t, P_HEADS * P_TOPK)
    gates = jax.nn.softmax(best, axis=-1).reshape(t, P_HEADS * P_TOPK)
    return experts, gates


def peer(h, w_q, sub_keys, u_tab, v_tab):
    bsz, n, d = h.shape
    t = bsz * n
    hf = h.reshape(t, d)
    experts, gates = peer_route(hf, w_q, sub_keys)
    nb = t // P_TOKEN_BLOCK

    def block(args):
        hb, eb, gb = args
        a = jnp.einsum('td,tkd->tk', hb, u_tab[eb])
        w = gb.astype(hb.dtype) * jax.nn.gelu(a, approximate=False)
        return jnp.einsum('tk,tkd->td', w, v_tab[eb])

    y = lax.map(block, (hf.reshape(nb, P_TOKEN_BLOCK, d),
                        experts.reshape(nb, P_TOKEN_BLOCK, P_HEADS * P_TOPK),
                        gates.reshape(nb, P_TOKEN_BLOCK, P_HEADS * P_TOPK)))
    return y.reshape(bsz, n, d)


def setup_inputs(seed: int = 0) -> dict:
    key = jax.random.key(seed)
    ks = jax.random.split(key, 26)

    def nrm(k, shape, scale=1.0):
        return jax.random.normal(k, shape, dtype=F32) * scale

    def gain(k, shape):
        return 1.0 + 0.02 * jax.random.normal(k, shape, dtype=F32)

    return {
        'x': nrm(ks[0], (BATCH, SEQ, D_MODEL)),
        'c': nrm(ks[1], (BATCH, D_MODEL)),
        'ctx': nrm(ks[2], (BATCH, CTX_LEN, D_MODEL)),
        'c_ctx': nrm(ks[3], (D_MODEL,)),
        'ln1_w': gain(ks[4], (DEPTH, D_MODEL)),
        'ln2_w': gain(ks[5], (DEPTH, D_MODEL)),
        'w_mod': nrm(ks[6], (DEPTH, D_MODEL, 6 * D_MODEL), 0.5 * D_MODEL ** -0.5),
        'b_mod': nrm(ks[7], (DEPTH, 6 * D_MODEL), 0.02),
        'w_in': nrm(ks[8], (DEPTH, D_MODEL, D_IN), D_MODEL ** -0.5),
        'w_out': nrm(ks[9], (DEPTH, D_MIX, D_MODEL), D_MIX ** -0.5),
        'a_norm_w': gain(ks[10], (DEPTH, A_WIDTH)),
        'a_w_s': nrm(ks[11], (DEPTH, A_HEADS, A_CHUNK, A_CHUNK), A_CHUNK ** -0.5),
        'a_b_s': gain(ks[12], (DEPTH, A_HEADS, A_CHUNK)),
        'b_q_norm_w': gain(ks[13], (DEPTH, B_Q_RANK)),
        'b_w_uq': nrm(ks[14], (DEPTH, B_Q_RANK, B_HEADS * B_QK), B_Q_RANK ** -0.5),
        'b_kv_norm_w': gain(ks[15], (DEPTH, B_KV_RANK)),
        'b_w_ukv': nrm(ks[16], (DEPTH, B_KV_RANK, B_HEADS * (B_NOPE + B_V)), B_KV_RANK ** -0.5),
        'b_q_head_norm_w': gain(ks[17], (DEPTH, B_QK)),
        'b_k_head_norm_w': gain(ks[18], (DEPTH, B_QK)),
        'c_lb_logits': nrm(ks[19], (DEPTH, 2, C_KWIDTH)),
        'c_out_norm_w': gain(ks[20], (DEPTH, C_DV)),
        'p_w_q': nrm(ks[21], (DEPTH, D_MODEL, P_HEADS * P_KEY_DIM), D_MODEL ** -0.5),
        'p_sub_keys': nrm(ks[22], (DEPTH, P_HEADS, 2, P_N_KEYS, P_KEY_DIM // 2), (P_KEY_DIM // 2) ** -0.5),
        'p_u': nrm(ks[23], (DEPTH, P_N_EXPERTS, D_MODEL), D_MODEL ** -0.5),
        'p_v': nrm(ks[24], (DEPTH, P_N_EXPERTS, D_MODEL), 1.0),
    }


def reference(x, c, ctx, c_ctx, ln1_w, ln2_w, w_mod, b_mod, w_in, w_out,
              a_norm_w, a_w_s, a_b_s,
              b_q_norm_w, b_w_uq, b_kv_norm_w, b_w_ukv, b_q_head_norm_w, b_k_head_norm_w,
              c_lb_logits, c_out_norm_w,
              p_w_q, p_sub_keys, p_u, p_v):
    bsz, n_lat, _ = x.shape
    rows = n_lat // GRID_W
    rope = axial_rope_angles(rows)
    lb = jnp.cumsum(jax.nn.softmax(c_lb_logits.astype(F32), axis=0), axis=0)
    lb = lb - lb[0:1]
    silu_c = jax.nn.silu(c)
    silu_cc = jax.nn.silu(c_ctx)

    for layer in range(DEPTH):
        last = layer == DEPTH - 1
        mod = jnp.split((silu_c @ w_mod[layer] + b_mod[layer])[:, None, :], 6, axis=-1)
        mod_c = jnp.split((silu_cc @ w_mod[layer] + b_mod[layer])[None, None, :], 6, axis=-1)

        h = modulate(rmsnorm(x, ln1_w[layer]), mod[0], mod[1])
        hc = modulate(rmsnorm(ctx, ln1_w[layer]), mod_c[0], mod_c[1])
        au, av, bq, bkv, bkr, cq, cff, cfb, ci, cg = jnp.split(h @ w_in[layer], SPLIT_IDX, axis=-1)
        au_c, av_c, bq_c, bkv_c, bkr_c, cq_c, cff_c, cfb_c, ci_c, cg_c = jnp.split(hc @ w_in[layer], SPLIT_IDX, axis=-1)

        a_out = chunk_gating_mlp(au, av, a_norm_w[layer], a_w_s[layer], a_b_s[layer])

        mla_w = (b_q_norm_w[layer], b_w_uq[layer], b_kv_norm_w[layer], b_w_ukv[layer],
                 b_q_head_norm_w[layer], b_k_head_norm_w[layer])
        q_l, k_l, v_l = mla_project(bq, bkv, bkr, *mla_w, rope=rope)
        q_c, k_c, v_c = mla_project(bq_c, bkv_c, bkr_c, *mla_w, rope=None)
        b_out = mla_latent(q_l, k_l, v_l, k_c, v_c)

        zeros = jnp.zeros((bsz, C_HEADS, C_DK, C_DV), F32)
        o_c, s_f, s_b = hgrn2_scan(cq_c, cff_c, cfb_c, ci_c, lb[layer, 0], lb[layer, 1], zeros, zeros)
        o_l, _, _ = hgrn2_scan(cq, cff, cfb, ci, lb[layer, 0], lb[layer, 1], s_f, s_b)
        c_out = hgrn2_readout(o_l, cg, c_out_norm_w[layer])

        x = x + mod[2] * (jnp.concatenate([a_out, b_out, c_out], axis=-1) @ w_out[layer])
        if not last:
            a_out_c = chunk_gating_mlp(au_c, av_c, a_norm_w[layer], a_w_s[layer], a_b_s[layer])
            b_out_c = mla_context(q_c, k_c, v_c)
            c_out_c = hgrn2_readout(o_c, cg_c, c_out_norm_w[layer])
            ctx = ctx + mod_c[2] * (jnp.concatenate([a_out_c, b_out_c, c_out_c], axis=-1) @ w_out[layer])

        x = x + mod[5] * peer(modulate(rmsnorm(x, ln2_w[layer]), mod[3], mod[4]),
                              p_w_q[layer], p_sub_keys[layer], p_u[layer], p_v[layer])
        if not last:
            ctx = ctx + mod_c[5] * peer(modulate(rmsnorm(ctx, ln2_w[layer]), mod_c[3], mod_c[4]),
                                        p_w_q[layer], p_sub_keys[layer], p_u[layer], p_v[layer])
    return x
```

```python
import functools
import math

import jax
import jax.numpy as jnp
import numpy as np
from jax import lax
from jax.experimental import pallas as pl
from jax.experimental.pallas import tpu as pltpu

F32 = jnp.float32
BF16 = jnp.bfloat16
HIGHEST = lax.Precision.HIGHEST

EPS = 1e-6
GRID_W = 64
ROPE_THETA = 10000.0

A_HEADS, A_HD, A_CHUNK = 4, 64, 128
A_W = A_HEADS * A_HD
B_HEADS, B_NOPE, B_ROPE, B_V = 8, 64, 32, 64
B_QK = B_NOPE + B_ROPE
B_HP = 128
B_Q_RANK, B_KV_RANK = 256, 128
C_HEADS, C_DK, C_DV, C_CHUNK = 4, 64, 64, 64
C_W = C_HEADS * C_DK
C_SUB = 16
P_HEADS, P_KEY_DIM, P_N_KEYS, P_TOPK = 8, 256, 128, 16
P_HALF = P_KEY_DIM // 2

TB = 256
PEER_TT = 512
PEER_ET = 1024
PEER_I1 = PEER_ET // P_N_KEYS
VMEM_LIMIT = 56 * 1024 * 1024


def _cparams(sem):
    return pltpu.CompilerParams(dimension_semantics=sem, vmem_limit_bytes=VMEM_LIMIT)


def _dot_nt(a, b):
    return lax.dot_general(a, b, (((1,), (1,)), ((), ())), preferred_element_type=F32)


def _dot_tn(a, b):
    return lax.dot_general(a, b, (((0,), (0,)), ((), ())), preferred_element_type=F32)


def _dot(a, b):
    return jnp.dot(a, b, preferred_element_type=F32)


def _sigmoid(x):
    return 1.0 / (1.0 + jnp.exp(-x))


def _block_ones(n, blk, dtype):
    r = lax.broadcasted_iota(jnp.int32, (n, n), 0) // blk
    c = lax.broadcasted_iota(jnp.int32, (n, n), 1) // blk
    return (r == c).astype(dtype)


def _mod_kernel(c_ref, w_ref, b_ref, o_ref):
    c = c_ref[...]
    sc = c * _sigmoid(c)
    o_ref[0] = _dot(sc.astype(BF16), w_ref[0].astype(BF16)) + b_ref[0]


def _modulation(cvec, w_mod, b_mod):
    depth, d, n6 = w_mod.shape
    rows = cvec.shape[0]
    tn = 1024
    return pl.pallas_call(
        _mod_kernel,
        grid=(depth, n6 // tn),
        in_specs=[
            pl.BlockSpec((rows, d), lambda l, n: (0, 0)),
            pl.BlockSpec((1, d, tn), lambda l, n: (l, 0, n)),
            pl.BlockSpec((1, 1, tn), lambda l, n: (l, 0, n)),
        ],
        out_specs=pl.BlockSpec((1, rows, tn), lambda l, n: (l, 0, n)),
        out_shape=jax.ShapeDtypeStruct((depth, rows, n6), F32),
        compiler_params=_cparams(("parallel", "parallel")),
        name="adaln_mod",
    )(cvec, w_mod, b_mod.reshape(depth, 1, n6))


def _inproj_kernel(x_ref, lnw_ref, mod_ref, w_ref, oa_ref, ob_ref, oc_ref, og_ref, *, d):
    x = x_ref[0]
    ms = jnp.mean(x * x, axis=-1, keepdims=True)
    y = x * lax.rsqrt(ms + EPS) * lnw_ref[...]
    shift = mod_ref[0, 0, :, 0:d]
    scale = mod_ref[0, 0, :, d:2 * d]
    h = (y * (1.0 + scale) + shift).astype(BF16)
    p = _dot(h, w_ref[...])
    na = oa_ref.shape[-1]
    nb = ob_ref.shape[-1]
    nc = oc_ref.shape[-1]
    oa_ref[0] = p[:, 0:na]
    ob_ref[0] = p[:, na:na + nb]
    oc_ref[0] = p[:, na + nb:na + nb + nc]
    og_ref[0] = p[:, na + nb + nc:]


def _inproj(x, lnw, modsel, w_all, widths):
    bsz, n, d = x.shape
    na, nb, nc, ng = widths
    nout = w_all.shape[1]
    return pl.pallas_call(
        functools.partial(_inproj_kernel, d=d),
        grid=(bsz, n // TB),
        in_specs=[
            pl.BlockSpec((1, TB, d), lambda b, j: (b, j, 0)),
            pl.BlockSpec((1, d), lambda b, j: (0, 0)),
            pl.BlockSpec((1, 1, 1, modsel.shape[-1]), lambda b, j: (b, jnp.minimum(j, 1), 0, 0)),
            pl.BlockSpec((d, nout), lambda b, j: (0, 0)),
        ],
        out_specs=[
            pl.BlockSpec((1, TB, na), lambda b, j: (b, j, 0)),
            pl.BlockSpec((1, TB, nb), lambda b, j: (b, j, 0)),
            pl.BlockSpec((1, TB, nc), lambda b, j: (b, j, 0)),
            pl.BlockSpec((1, TB, ng), lambda b, j: (b, j, 0)),
        ],
        out_shape=[
            jax.ShapeDtypeStruct((bsz, n, na), F32),
            jax.ShapeDtypeStruct((bsz, n, nb), F32),
            jax.ShapeDtypeStruct((bsz, n, nc), F32),
            jax.ShapeDtypeStruct((bsz, n, ng), F32),
        ],
        compiler_params=_cparams(("parallel", "parallel")),
        name="in_proj",
    )(x, lnw, modsel, w_all)


def _amix_kernel(a_ref, nw_ref, ws_ref, bias_ref, o_ref):
    u = a_ref[0, :, 0:A_W]
    v = a_ref[0, :, A_W:2 * A_W]
    ssq = jnp.dot(v * v, _block_ones(A_W, A_HD, F32), precision=HIGHEST, preferred_element_type=F32)
    vn = v * lax.rsqrt(ssq * (1.0 / A_HD) + EPS) * nw_ref[...]
    lane_head = lax.broadcasted_iota(jnp.int32, vn.shape, 1) // A_HD
    acc = bias_ref[...]
    for h in range(A_HEADS):
        vm = jnp.where(lane_head == h, vn, 0.0).astype(BF16)
        acc = acc + _dot(ws_ref[h], vm)
    o_ref[0] = (u * acc).astype(BF16)


def _amix(oa, nw, ws, bias, first_chunk):
    bsz, n, _ = oa.shape
    nchunk = n // A_CHUNK - first_chunk
    return pl.pallas_call(
        _amix_kernel,
        grid=(bsz, nchunk),
        in_specs=[
            pl.BlockSpec((1, A_CHUNK, 2 * A_W), lambda b, j: (b, j + first_chunk, 0)),
            pl.BlockSpec((1, A_W), lambda b, j: (0, 0)),
            pl.BlockSpec((A_HEADS, A_CHUNK, A_CHUNK), lambda b, j: (0, 0, 0)),
            pl.BlockSpec((A_CHUNK, A_W), lambda b, j: (0, 0)),
        ],
        out_specs=pl.BlockSpec((1, A_CHUNK, A_W), lambda b, j: (b, j + first_chunk, 0)),
        out_shape=jax.ShapeDtypeStruct((bsz, n, A_W), BF16),
        compiler_params=_cparams(("parallel", "parallel")),
        name="mixer_a",
    )(oa, nw, ws, bias)


def _mla_prep_kernel(ob_ref, cos_ref, sin_ref, qnw_ref, kvnw_ref, wq_ref, wqs_ref, wk_ref, wv_ref,
                     hw_ref, q_ref, k_ref, v_ref):
    cq = ob_ref[0, :, 0:B_Q_RANK]
    ckv = ob_ref[0, :, B_Q_RANK:B_Q_RANK + B_KV_RANK]
    krp = ob_ref[0, :, B_Q_RANK + B_KV_RANK:B_Q_RANK + B_KV_RANK + B_HP]
    krs = ob_ref[0, :, B_Q_RANK + B_KV_RANK + B_HP:B_Q_RANK + B_KV_RANK + 2 * B_HP]
    cos = cos_ref[...]
    sin = sin_ref[...]
    cqn = (cq * lax.rsqrt(jnp.mean(cq * cq, axis=-1, keepdims=True) + EPS) * qnw_ref[...]).astype(BF16)
    ckn = (ckv * lax.rsqrt(jnp.mean(ckv * ckv, axis=-1, keepdims=True) + EPS) * kvnw_ref[...]).astype(BF16)
    q_raw = _dot(cqn, wq_ref[...])
    q_swp = _dot(cqn, wqs_ref[...])
    k_raw = _dot(ckn, wk_ref[...])
    v_all = _dot(ckn, wv_ref[...])
    qw, qws, kw, kws = hw_ref[0:1, :], hw_ref[1:2, :], hw_ref[2:3, :], hw_ref[3:4, :]
    k_rot_sw = krs * kws * sin
    for h in range(B_HEADS):
        sl = slice(h * B_HP, (h + 1) * B_HP)
        qh = q_raw[:, sl]
        rq = lax.rsqrt(jnp.sum(qh * qh, axis=-1, keepdims=True) * (1.0 / B_QK) + EPS)
        q_ref[0, h] = (rq * (qh * qw * cos + q_swp[:, sl] * qws * sin)).astype(BF16)
        kh = k_raw[:, sl] + krp
        rk = lax.rsqrt(jnp.sum(kh * kh, axis=-1, keepdims=True) * (1.0 / B_QK) + EPS)
        k_ref[0, h] = (rk * (kh * kw * cos + k_rot_sw)).astype(BF16)
        v_ref[0, h] = v_all[:, h * B_V:(h + 1) * B_V].astype(BF16)


def _mla_prep(ob, cos_t, sin_t, qnw, kvnw, wq, wqs, wk, wv, hw):
    bsz, n, nb = ob.shape
    full = lambda *s: pl.BlockSpec(s, lambda b, j: (0,) * len(s))
    return pl.pallas_call(
        _mla_prep_kernel,
        grid=(bsz, n // TB),
        in_specs=[
            pl.BlockSpec((1, TB, nb), lambda b, j: (b, j, 0)),
            pl.BlockSpec((TB, B_HP), lambda b, j: (j, 0)),
            pl.BlockSpec((TB, B_HP), lambda b, j: (j, 0)),
            full(1, B_Q_RANK), full(1, B_KV_RANK),
            full(B_Q_RANK, B_HEADS * B_HP), full(B_Q_RANK, B_HEADS * B_HP),
            full(B_KV_RANK, B_HEADS * B_HP), full(B_KV_RANK, B_HEADS * B_V),
            full(8, B_HP),
        ],
        out_specs=[
            pl.BlockSpec((1, B_HEADS, TB, B_HP), lambda b, j: (b, 0, j, 0)),
            pl.BlockSpec((1, B_HEADS, TB, B_HP), lambda b, j: (b, 0, j, 0)),
            pl.BlockSpec((1, B_HEADS, TB, B_V), lambda b, j: (b, 0, j, 0)),
        ],
        out_shape=[
            jax.ShapeDtypeStruct((bsz, B_HEADS, n, B_HP), BF16),
            jax.ShapeDtypeStruct((bsz, B_HEADS, n, B_HP), BF16),
            jax.ShapeDtypeStruct((bsz, B_HEADS, n, B_V), BF16),
        ],
        compiler_params=_cparams(("parallel", "parallel")),
        name="mla_prep",
    )(ob, cos_t, sin_t, qnw, kvnw, wq, wqs, wk, wv, hw)


def _attn_kernel(q_ref, k_ref, v_ref, o_ref, o_scr):
    scale = B_QK ** -0.5
    for h in range(B_HEADS):
        s = _dot_nt(q_ref[0, h], k_ref[0, h])
        m = jnp.max(s, axis=-1, keepdims=True)
        p = jnp.exp((s - m) * scale)
        l = jnp.sum(p, axis=-1, keepdims=True)
        o = _dot(p.astype(BF16), v_ref[0, h])
        o_scr[:, h * B_V:(h + 1) * B_V] = o / l
    o_ref[0] = o_scr[...].astype(BF16)


def _attention(q, k, v, first_qblock, n_qblocks, n_keys, out_rows):
    bsz = q.shape[0]
    return pl.pallas_call(
        _attn_kernel,
        grid=(bsz, n_qblocks),
        in_specs=[
            pl.BlockSpec((1, B_HEADS, TB, B_HP), lambda b, j: (b, 0, j + first_qblock, 0)),
            pl.BlockSpec((1, B_HEADS, n_keys, B_HP), lambda b, j: (b, 0, 0, 0)),
            pl.BlockSpec((1, B_HEADS, n_keys, B_V), lambda b, j: (b, 0, 0, 0)),
        ],
        out_specs=pl.BlockSpec((1, TB, B_HEADS * B_V), lambda b, j: (b, j, 0)),
        out_shape=jax.ShapeDtypeStruct((bsz, out_rows, B_HEADS * B_V), BF16),
        scratch_shapes=[pltpu.VMEM((TB, B_HEADS * B_V), F32)],
        compiler_params=_cparams(("parallel", "arbitrary")),
        name="mla_attention",
    )(q, k, v)


def _hgrn_chunk(blk, zcol, lbc, st_ref, rev):
    cc, w = C_CHUNK, C_W
    q = blk[:, 0:w] * (C_DK ** -0.5)
    z = blk[:, zcol * w:(zcol + 1) * w]
    v = blk[:, 3 * w:4 * w]
    log_lb, log1m_lb, one_m_lb = lbc[0:1, :], lbc[1:2, :], lbc[2:3, :]
    az = jnp.abs(z)
    sp = jnp.log1p(jnp.exp(-az))
    lsig = jnp.minimum(z, 0.0) - sp
    t2 = log1m_lb + lsig
    mx = jnp.maximum(log_lb, t2)
    mn = jnp.minimum(log_lb, t2)
    logf = mx + jnp.log1p(jnp.exp(mn - mx))
    kk = one_m_lb * _sigmoid(-z)

    ti = lax.broadcasted_iota(jnp.int32, (cc, cc), 0)
    ui = lax.broadcasted_iota(jnp.int32, (cc, cc), 1)
    tri = ((ui >= ti) if rev else (ui <= ti)).astype(F32)
    b = jnp.dot(tri, logf, precision=HIGHEST, preferred_element_type=F32)
    b_tot = b[0:1, :] if rev else b[cc - 1:cc, :]

    row = lax.broadcasted_iota(jnp.int32, (cc, w), 0)
    lane_head = lax.broadcasted_iota(jnp.int32, (cc, w), 1) // C_DK
    nsub = cc // C_SUB
    row_blk = row // C_SUB

    beta_rows = []
    for i in range(nsub):
        if rev:
            src = None if i == nsub - 1 else b[(i + 1) * C_SUB:(i + 1) * C_SUB + 1, :]
        else:
            src = None if i == 0 else b[i * C_SUB - 1:i * C_SUB, :]
        beta_rows.append(src)
    beta_full = jnp.concatenate(
        [jnp.broadcast_to(b[i * C_SUB:i * C_SUB + 1, :] if r is None else r, (C_SUB, w))
         for i, r in enumerate(beta_rows)], axis=0)
    has_prev = (row_blk < nsub - 1) if rev else (row_blk > 0)
    qs = jnp.where(has_prev, q * jnp.exp(b - beta_full), 0.0)

    q_stack = jnp.concatenate([jnp.where(lane_head == h, qs, 0.0) for h in range(C_HEADS)], axis=0).astype(BF16)
    qblocks = [i for i in range(nsub) if beta_rows[i] is not None]
    ks_parts = []
    for i in qblocks:
        prev = (row_blk > i) if rev else (row_blk < i)
        ks_parts.append(jnp.where(prev, kk * jnp.exp(beta_rows[i] - b), 0.0))
    ks_all = jnp.concatenate(ks_parts, axis=0).astype(BF16)
    a_all = _dot_nt(q_stack, ks_all)
    ar = lax.broadcasted_iota(jnp.int32, a_all.shape, 0)
    ac = lax.broadcasted_iota(jnp.int32, a_all.shape, 1)
    r_blk = (ar % cc) // C_SUB
    c_blk = ac // cc + (0 if rev else 1)
    a_all = jnp.where(r_blk == c_blk, a_all, 0.0).astype(BF16)
    v_bf = v.astype(BF16)
    r_all = _dot(a_all, jnp.concatenate([v_bf] * len(qblocks), axis=0))
    o = jnp.zeros((cc, w), F32)
    for h in range(C_HEADS):
        o = o + jnp.where(lane_head == h, r_all[h * cc:(h + 1) * cc, :], 0.0)

    ones_bd = _block_ones(w, C_DK, BF16)
    tsub = lax.broadcasted_iota(jnp.int32, (C_SUB, w), 0)
    diag_parts = []
    for i in range(nsub):
        r0 = i * C_SUB
        bb = b[r0:r0 + C_SUB, :]
        qq = q[r0:r0 + C_SUB, :]
        ps = []
        for s in range(C_SUB):
            keep = (tsub <= s) if rev else (tsub >= s)
            e = jnp.where(keep, jnp.exp(bb - b[r0 + s:r0 + s + 1, :]), 0.0)
            ps.append(qq * e * kk[r0 + s:r0 + s + 1, :])
        red = _dot(jnp.concatenate(ps, axis=0).astype(BF16), ones_bd)
        od = jnp.zeros((C_SUB, w), F32)
        for s in range(C_SUB):
            od = od + red[s * C_SUB:(s + 1) * C_SUB, :] * v[r0 + s:r0 + s + 1, :]
        diag_parts.append(od)
    o = o + jnp.concatenate(diag_parts, axis=0)

    st = st_ref[...]
    o = o + _dot_nt((q * jnp.exp(b)).astype(BF16), st.astype(BF16))
    kd = (kk * jnp.exp(b_tot - b)).astype(BF16)
    upd = _dot_tn(v_bf, kd)
    st_ref[...] = st * jnp.exp(b_tot) + upd * _block_ones(w, C_DK, F32)
    return o


def _hgrn_kernel(cf_ref, cb_ref, lbc_ref, of_ref, ob_ref, sf_ref, sb_ref):
    @pl.when(pl.program_id(1) == 0)
    def _():
        sf_ref[...] = jnp.zeros_like(sf_ref)
        sb_ref[...] = jnp.zeros_like(sb_ref)

    of_ref[0] = _hgrn_chunk(cf_ref[0], 1, lbc_ref[0], sf_ref, rev=False)
    ob_ref[0] = _hgrn_chunk(cb_ref[0], 2, lbc_ref[1], sb_ref, rev=True)


def _hgrn(oc, lbc, n_ctx):
    bsz, n, wc = oc.shape
    nch = n // C_CHUNK
    nctx = n_ctx // C_CHUNK

    def bwd_idx(c):
        return jnp.where(c < nctx, nctx - 1 - c, nch + nctx - 1 - c)

    return pl.pallas_call(
        _hgrn_kernel,
        grid=(bsz, nch),
        in_specs=[
            pl.BlockSpec((1, C_CHUNK, wc), lambda b, c: (b, c, 0)),
            pl.BlockSpec((1, C_CHUNK, wc), lambda b, c: (b, bwd_idx(c), 0)),
            pl.BlockSpec((2, 8, C_W), lambda b, c: (0, 0, 0)),
        ],
        out_specs=[
            pl.BlockSpec((1, C_CHUNK, C_W), lambda b, c: (b, c, 0)),
            pl.BlockSpec((1, C_CHUNK, C_W), lambda b, c: (b, bwd_idx(c), 0)),
        ],
        out_shape=[jax.ShapeDtypeStruct((bsz, n, C_W), F32)] * 2,
        scratch_shapes=[pltpu.VMEM((C_W, C_W), F32)] * 2,
        compiler_params=_cparams(("parallel", "arbitrary")),
        name="hgrn2_scan",
    )(oc, oc, lbc)


def _outproj_kernel(x_ref, a_ref, b_ref, of_ref, ob_ref, g_ref, cnw_ref, wa_ref, wb_ref, wc_ref,
                    mod_ref, ln2_ref, xo_ref, h2_ref, *, d):
    o = of_ref[0] + ob_ref[0]
    ssq = jnp.dot(o * o, _block_ones(C_W, C_DV, F32), precision=HIGHEST, preferred_element_type=F32)
    g = g_ref[0]
    c_out = o * lax.rsqrt(ssq * (1.0 / C_DV) + EPS) * cnw_ref[...] * (g * _sigmoid(g))
    mix = _dot(a_ref[0], wa_ref[...]) + _dot(b_ref[0], wb_ref[...]) + _dot(c_out.astype(BF16), wc_ref[...])
    gate1 = mod_ref[0, 0, :, 2 * d:3 * d]
    shift2 = mod_ref[0, 0, :, 3 * d:4 * d]
    scale2 = mod_ref[0, 0, :, 4 * d:5 * d]
    x = x_ref[0] + gate1 * mix
    xo_ref[0] = x
    y = x * lax.rsqrt(jnp.mean(x * x, axis=-1, keepdims=True) + EPS) * ln2_ref[...]
    h2_ref[0] = (y * (1.0 + scale2) + shift2).astype(BF16)


def _outproj(x, a_out, b_out, o_f, o_b, og, cnw, wa, wb, wc, modsel, ln2, first_block):
    bsz, n, d = x.shape
    nblk = n // TB - first_block
    full = lambda *s: pl.BlockSpec(s, lambda b, j: (0,) * len(s))
    tok = lambda w: pl.BlockSpec((1, TB, w), lambda b, j: (b, j + first_block, 0))
    return pl.pallas_call(
        functools.partial(_outproj_kernel, d=d),
        grid=(bsz, nblk),
        in_specs=[
            tok(d), tok(A_W),
            pl.BlockSpec((1, TB, B_HEADS * B_V), lambda b, j: (b, j, 0)) if first_block else tok(B_HEADS * B_V),
            tok(C_W), tok(C_W), tok(C_W),
            full(1, C_W), full(A_W, d), full(B_HEADS * B_V, d), full(C_W, d),
            pl.BlockSpec((1, 1, 1, modsel.shape[-1]), lambda b, j: (b, jnp.minimum(j + first_block, 1), 0, 0)),
            full(1, d),
        ],
        out_specs=[
            pl.BlockSpec((1, TB, d), lambda b, j: (b, j, 0)),
            pl.BlockSpec((1, TB, d), lambda b, j: (b, j, 0)),
        ],
        out_shape=[
            jax.ShapeDtypeStruct((bsz, nblk * TB, d), F32),
            jax.ShapeDtypeStruct((bsz, nblk * TB, d), BF16),
        ],
        compiler_params=_cparams(("parallel", "parallel")),
        name="out_proj",
    )(x, a_out, b_out, o_f, o_b, og, cnw, wa, wb, wc, modsel, ln2)


def _top16(s, nrows):
    iota = lax.broadcasted_iota(jnp.int32, s.shape, 0).astype(F32)
    rank = jnp.full(s.shape, float(P_TOPK), F32)
    vals = []
    for r in range(P_TOPK):
        m = jnp.max(s, axis=0, keepdims=True)
        idx = jnp.min(jnp.where(s == m, iota, float(nrows)), axis=0, keepdims=True)
        hit = iota == idx
        rank = jnp.where(hit, float(r), rank)
        s = jnp.where(hit, -jnp.inf, s)
        vals.append(m)
    return vals, rank


def _route_kernel(h_ref, wq_ref, sk_ref, r2_ref, e2_ref, n1_ref, e1_ref):
    q = _dot(h_ref[...], wq_ref[...]).astype(BF16)
    for h in range(P_HEADS):
        c0 = h * P_KEY_DIM
        s1 = _dot_nt(sk_ref[h, 0], q[:, c0:c0 + P_HALF])
        s2 = _dot_nt(sk_ref[h, 1], q[:, c0 + P_HALF:c0 + P_KEY_DIM])
        v1, rank1 = _top16(s1, P_N_KEYS)
        v2, rank2 = _top16(s2, P_N_KEYS)
        v2m = jnp.concatenate(v2, axis=0)
        cand0 = jnp.concatenate([v1[a] + v2m for a in range(P_TOPK)], axis=0)
        _, crank = _top16(cand0, P_TOPK * P_TOPK)
        sel = crank < float(P_TOPK)
        cmax = v1[0] + v2[0]
        z = jnp.sum(jnp.where(sel, jnp.exp(cand0 - cmax), 0.0), axis=0, keepdims=True)
        self32 = jnp.where(sel, 1.0, 0.0)
        n1 = jnp.zeros(rank1.shape, F32)
        for a in range(P_TOPK):
            n_a = jnp.sum(self32[a * P_TOPK:(a + 1) * P_TOPK, :], axis=0, keepdims=True)
            n1 = n1 + jnp.where(rank1 == float(a), n_a, 0.0)
        r2_ref[h] = rank2
        n1_ref[h] = n1
        e1_ref[h] = jnp.exp(s1 - v1[0])
        e2_ref[h] = jnp.exp(s2 - v2[0]) / z


def _route(h2, wq, sk):
    t, d = h2.shape
    tr = TB
    outs = jax.ShapeDtypeStruct((P_HEADS, P_N_KEYS, t), F32)
    ospec = pl.BlockSpec((P_HEADS, P_N_KEYS, tr), lambda i: (0, 0, i))
    return pl.pallas_call(
        _route_kernel,
        grid=(t // tr,),
        in_specs=[
            pl.BlockSpec((tr, d), lambda i: (i, 0)),
            pl.BlockSpec((d, P_HEADS * P_KEY_DIM), lambda i: (0, 0)),
            pl.BlockSpec((P_HEADS, 2, P_N_KEYS, P_HALF), lambda i: (0, 0, 0, 0)),
        ],
        out_specs=[ospec] * 4,
        out_shape=[outs] * 4,
        compiler_params=_cparams(("parallel",)),
        name="peer_route",
    )(h2, wq, sk)


def _peer_kernel(h_ref, u_ref, vt_ref, r2_ref, e2_ref, n1_ref, e1_ref, x_ref, g5_ref, o_ref,
                 acc_ref, w_ref, *, blocks_per_batch, ctx_blocks):
    i = pl.program_id(0)
    k = pl.program_id(1)

    @pl.when(k == 0)
    def _():
        acc_ref[...] = jnp.zeros_like(acc_ref)

    a_t = _dot_nt(u_ref[...], h_ref[...])
    for j in range(PEER_I1):
        rows = slice(j * P_N_KEYS, (j + 1) * P_N_KEYS)
        g = jnp.zeros((P_N_KEYS, PEER_TT), F32)
        for h in range(P_HEADS):
            keep = r2_ref[h] < n1_ref[h, j:j + 1, :]
            g = g + jnp.where(keep, e2_ref[h], 0.0) * e1_ref[h, j:j + 1, :]
        a = a_t[rows, :]
        act = 0.5 * a * (1.0 + lax.erf(a * (2.0 ** -0.5)))
        w_ref[rows, :] = (act * g).astype(BF16)
    acc_ref[...] += _dot(vt_ref[...], w_ref[...])

    @pl.when(k == pl.num_programs(1) - 1)
    def _():
        y = acc_ref[...].T
        for u in range(PEER_TT // TB):
            sblk = i * (PEER_TT // TB) + u
            bidx = sblk // blocks_per_batch
            is_lat = (sblk - bidx * blocks_per_batch) >= ctx_blocks
            gate = g5_ref[2 * bidx + is_lat.astype(jnp.int32)]
            rs = slice(u * TB, (u + 1) * TB)
            o_ref[rs, :] = x_ref[rs, :] + gate * y[rs, :]


def _peer(h2, u_bf, vt_bf, r2, e2, n1, e1, x, g5, blocks_per_batch, ctx_blocks):
    t, d = h2.shape
    ne = u_bf.shape[0]
    rspec = pl.BlockSpec((P_HEADS, P_N_KEYS, PEER_TT), lambda i, k: (0, 0, i))
    nspec = pl.BlockSpec((P_HEADS, PEER_I1, PEER_TT), lambda i, k: (0, k, i))
    return pl.pallas_call(
        functools.partial(_peer_kernel, blocks_per_batch=blocks_per_batch, ctx_blocks=ctx_blocks),
        grid=(t // PEER_TT, ne // PEER_ET),
        in_specs=[
            pl.BlockSpec((PEER_TT, d), lambda i, k: (i, 0)),
            pl.BlockSpec((PEER_ET, d), lambda i, k: (k, 0)),
            pl.BlockSpec((d, PEER_ET), lambda i, k: (0, k)),
            rspec, rspec, nspec, nspec,
            pl.BlockSpec((PEER_TT, d), lambda i, k: (i, 0)),
            pl.BlockSpec(g5.shape, lambda i, k: (0, 0, 0)),
        ],
        out_specs=pl.BlockSpec((PEER_TT, d), lambda i, k: (i, 0)),
        out_shape=jax.ShapeDtypeStruct((t, d), F32),
        scratch_shapes=[pltpu.VMEM((d, PEER_TT), F32), pltpu.VMEM((PEER_ET, PEER_TT), BF16)],
        compiler_params=_cparams(("parallel", "arbitrary")),
        name="peer_experts",
    )(h2, u_bf, vt_bf, r2, e2, n1, e1, x, g5)


def _transpose_cast_kernel(x_ref, o_ref):
    o_ref[...] = x_ref[...].T.astype(BF16)


def _transpose_cast(v):
    ne, d = v.shape
    te = 512
    return pl.pallas_call(
        _transpose_cast_kernel,
        grid=(ne // te,),
        in_specs=[pl.BlockSpec((te, d), lambda e: (e, 0))],
        out_specs=pl.BlockSpec((d, te), lambda e: (0, e)),
        out_shape=jax.ShapeDtypeStruct((d, ne), BF16),
        compiler_params=_cparams(("parallel",)),
        name="expert_value_transpose",
    )(v)


def _rope_tables(n_ctx, n_lat):
    n_freq = B_ROPE // 4
    pos = np.arange(n_lat)
    inv_freq = ROPE_THETA ** (-np.arange(n_freq, dtype=np.float32) / n_freq)
    inv_freq = jnp.asarray(inv_freq, F32)
    rowp = jnp.asarray(pos // GRID_W, F32)
    colp = jnp.asarray(pos % GRID_W, F32)
    ang = jnp.stack([rowp[:, None] * inv_freq, colp[:, None] * inv_freq], axis=1)
    cos, sin = jnp.cos(ang), jnp.sin(ang)
    cos32 = jnp.concatenate([cos, cos], axis=2).reshape(n_lat, B_ROPE)
    sin32 = jnp.concatenate([-sin, sin], axis=2).reshape(n_lat, B_ROPE)
    pad_l = jnp.ones((n_lat, B_NOPE), F32)
    pad_r = jnp.ones((n_lat, B_HP - B_QK), F32)
    cos_l = jnp.concatenate([pad_l, cos32, pad_r], axis=1)
    sin_l = jnp.concatenate([0 * pad_l, sin32, 0 * pad_r], axis=1)
    cos_t = jnp.concatenate([jnp.ones((n_ctx, B_HP), F32), cos_l], axis=0)
    sin_t = jnp.concatenate([jnp.zeros((n_ctx, B_HP), F32), sin_l], axis=0)
    return cos_t, sin_t


_SWAP32 = np.arange(B_ROPE) ^ (B_ROPE // 4)


def _pad_head(nope, rope):
    z = jnp.zeros(nope.shape[:-1] + (B_HP - B_QK,), nope.dtype)
    out = jnp.concatenate([nope, rope, z], axis=-1)
    return out.reshape(out.shape[:-2] + (out.shape[-2] * B_HP,))


def _layer_weights(layer, w_in, w_out, a_norm_w, a_w_s, a_b_s, b_q_norm_w, b_w_uq, b_kv_norm_w, b_w_ukv,
                   b_q_head_norm_w, b_k_head_norm_w, c_out_norm_w, p_w_q, p_sub_keys):
    d = w_in.shape[1]
    wi = w_in[layer]
    offs = np.cumsum([0, A_W, A_W, B_Q_RANK, B_KV_RANK, B_ROPE, C_W, C_W, C_W, C_W, C_W])
    col = lambda i: wi[:, offs[i]:offs[i + 1]]
    w_kr = col(4)
    zl = jnp.zeros((d, B_NOPE), F32)
    zr = jnp.zeros((d, B_HP - B_QK), F32)
    kr_placed = jnp.concatenate([zl, w_kr, zr], axis=1)
    kr_swapped = jnp.concatenate([zl, w_kr[:, _SWAP32], zr], axis=1)
    w_all = jnp.concatenate(
        [col(0), col(1), col(2), col(3), kr_placed, kr_swapped, col(5), col(6), col(7), col(8), col(9)],
        axis=1).astype(BF16)
    widths = (2 * A_W, B_Q_RANK + B_KV_RANK + 2 * B_HP, 4 * C_W, C_W)

    wuq = b_w_uq[layer].reshape(B_Q_RANK, B_HEADS, B_QK)
    wq_p = _pad_head(wuq[..., :B_NOPE], wuq[..., B_NOPE:]).astype(BF16)
    wq_s = _pad_head(0 * wuq[..., :B_NOPE], wuq[..., B_NOPE:][..., _SWAP32]).astype(BF16)
    wukv = b_w_ukv[layer].reshape(B_KV_RANK, B_HEADS, B_NOPE + B_V)
    wk_p = _pad_head(wukv[..., :B_NOPE], jnp.zeros((B_KV_RANK, B_HEADS, B_ROPE), F32)).astype(BF16)
    wv = wukv[..., B_NOPE:].reshape(B_KV_RANK, B_HEADS * B_V).astype(BF16)
    qn, kn = b_q_head_norm_w[layer], b_k_head_norm_w[layer]
    zpad = jnp.zeros((B_HP - B_QK,), F32)
    z64 = jnp.zeros((B_NOPE,), F32)
    hw = jnp.stack([
        jnp.concatenate([qn, zpad]),
        jnp.concatenate([z64, qn[B_NOPE:][_SWAP32], zpad]),
        jnp.concatenate([kn, zpad]),
        jnp.concatenate([z64, kn[B_NOPE:][_SWAP32], zpad]),
    ] + [jnp.zeros((B_HP,), F32)] * 4, axis=0)

    wo = w_out[layer].astype(BF16)
    return dict(
        w_all=w_all, widths=widths,
        a_nw=a_norm_w[layer].reshape(1, A_W),
        a_ws=a_w_s[layer].astype(BF16),
        a_bias=jnp.repeat(a_b_s[layer].T, A_HD, axis=1),
        qnw=b_q_norm_w[layer].reshape(1, B_Q_RANK), kvnw=b_kv_norm_w[layer].reshape(1, B_KV_RANK),
        wq_p=wq_p, wq_s=wq_s, wk_p=wk_p, wv=wv, hw=hw,
        cnw=jnp.tile(c_out_norm_w[layer], C_HEADS).reshape(1, C_W),
        wo_a=wo[:A_W], wo_b=wo[A_W:A_W + B_HEADS * B_V], wo_c=wo[A_W + B_HEADS * B_V:],
        p_wq=p_w_q[layer].astype(BF16), p_sk=p_sub_keys[layer].astype(BF16),
    )


def kernel(x, c, ctx, c_ctx, ln1_w, ln2_w, w_mod, b_mod, w_in, w_out, a_norm_w, a_w_s, a_b_s, b_q_norm_w,
           b_w_uq, b_kv_norm_w, b_w_ukv, b_q_head_norm_w, b_k_head_norm_w, c_lb_logits, c_out_norm_w,
           p_w_q, p_sub_keys, p_u, p_v):
    bsz, n_lat, d = x.shape
    n_ctx = ctx.shape[1]
    depth = w_in.shape[0]
    n = n_ctx + n_lat
    ctx_blocks = n_ctx // TB
    blocks = n // TB

    mrows = -(-(bsz + 1) // 8) * 8
    cvec = jnp.concatenate([c, c_ctx[None, :], jnp.zeros((mrows - bsz - 1, d), F32)], axis=0)
    mod_all = _modulation(cvec, w_mod, b_mod)

    lb = jnp.cumsum(jax.nn.softmax(c_lb_logits.astype(F32), axis=0), axis=0)
    lb = lb - lb[0:1]
    lbc_all = jnp.stack([jnp.log(lb), jnp.log1p(-lb), 1.0 - lb] + [jnp.zeros_like(lb)] * 5, axis=2)

    cos_t, sin_t = _rope_tables(n_ctx, n_lat)
    xc = jnp.concatenate([ctx, x], axis=1)

    for layer in range(depth):
        last = layer == depth - 1
        w = _layer_weights(layer, w_in, w_out, a_norm_w, a_w_s, a_b_s, b_q_norm_w, b_w_uq, b_kv_norm_w,
                           b_w_ukv, b_q_head_norm_w, b_k_head_norm_w, c_out_norm_w, p_w_q, p_sub_keys)
        mod_b = mod_all[layer, :bsz]
        mod_c = jnp.broadcast_to(mod_all[layer, bsz][None, :], mod_b.shape)
        modsel = jnp.stack([mod_c, mod_b], axis=1)[:, :, None, :]
        first_block = ctx_blocks if last else 0

        oa, ob, oc, og = _inproj(xc, ln1_w[layer].reshape(1, d), modsel, w["w_all"], w["widths"])
        a_out = _amix(oa, w["a_nw"], w["a_ws"], w["a_bias"], first_block * (TB // A_CHUNK))
        q, k, v = _mla_prep(ob, cos_t, sin_t, w["qnw"], w["kvnw"], w["wq_p"], w["wq_s"], w["wk_p"], w["wv"], w["hw"])
        b_lat = _attention(q, k, v, ctx_blocks, blocks - ctx_blocks, n, n_lat)
        if last:
            b_out = b_lat
        else:
            b_ctx = _attention(q, k, v, 0, ctx_blocks, n_ctx, n_ctx)
            b_out = jnp.concatenate([b_ctx, b_lat], axis=1)
        o_f, o_b = _hgrn(oc, lbc_all[layer], n_ctx)
        x_new, h2 = _outproj(xc, a_out, b_out, o_f, o_b, og, w["cnw"], w["wo_a"], w["wo_b"], w["wo_c"],
                             modsel, ln2_w[layer].reshape(1, d), first_block)

        t = x_new.shape[0] * x_new.shape[1]
        h2f = h2.reshape(t, d)
        r2, e2, n1, e1 = _route(h2f, w["p_wq"], w["p_sk"])
        g5 = modsel[:, :, :, 5 * d:6 * d].reshape(2 * bsz, 1, d)
        u_bf = p_u[layer].astype(BF16)
        vt_bf = _transpose_cast(p_v[layer])
        out = _peer(h2f, u_bf, vt_bf, r2, e2, n1, e1, x_new.reshape(t, d), g5,
                    blocks - first_block, ctx_blocks - first_block)
        xc = out.reshape(bsz, t // bsz, d)
    return xc
```

```python
import functools
import math

import jax
import jax.numpy as jnp
import numpy as np
from jax import lax
from jax.experimental import pallas as pl
from jax.experimental.pallas import tpu as pltpu

F32 = jnp.float32
BF16 = jnp.bfloat16
HIGHEST = lax.Precision.HIGHEST

EPS = 1e-6
GRID_W = 64
ROPE_THETA = 10000.0

A_HEADS, A_HD, A_CHUNK = 4, 64, 128
A_W = A_HEADS * A_HD
B_HEADS, B_NOPE, B_ROPE, B_V = 8, 64, 32, 64
B_QK = B_NOPE + B_ROPE
B_HP = 128
B_Q_RANK, B_KV_RANK = 256, 128
C_HEADS, C_DK, C_DV, C_CHUNK = 4, 64, 64, 64
C_W = C_HEADS * C_DK
C_SUB = 16
P_HEADS, P_KEY_DIM, P_N_KEYS, P_TOPK = 8, 256, 128, 16
P_HALF = P_KEY_DIM // 2

TB = 256
PEER_TT = 512
PEER_ET = 1024
PEER_I1 = PEER_ET // P_N_KEYS
VMEM_LIMIT = 56 * 1024 * 1024


def _cparams(sem):
    return pltpu.CompilerParams(dimension_semantics=sem, vmem_limit_bytes=VMEM_LIMIT)


def _dot_nt(a, b):
    return lax.dot_general(a, b, (((1,), (1,)), ((), ())), preferred_element_type=F32)


def _dot_tn(a, b):
    return lax.dot_general(a, b, (((0,), (0,)), ((), ())), preferred_element_type=F32)


def _dot(a, b):
    return jnp.dot(a, b, preferred_element_type=F32)


def _sigmoid(x):
    return 1.0 / (1.0 + jnp.exp(-x))


def _block_ones(n, blk, dtype):
    r = lax.broadcasted_iota(jnp.int32, (n, n), 0) // blk
    c = lax.broadcasted_iota(jnp.int32, (n, n), 1) // blk
    return (r == c).astype(dtype)


def _mod_kernel(c_ref, w_ref, b_ref, o_ref):
    c = c_ref[...]
    sc = c * _sigmoid(c)
    o_ref[0] = _dot(sc.astype(BF16), w_ref[0].astype(BF16)) + b_ref[0]


def _modulation(cvec, w_mod, b_mod):
    depth, d, n6 = w_mod.shape
    rows = cvec.shape[0]
    tn = 1024
    return pl.pallas_call(
        _mod_kernel,
        grid=(depth, n6 // tn),
        in_specs=[
            pl.BlockSpec((rows, d), lambda l, n: (0, 0)),
            pl.BlockSpec((1, d, tn), lambda l, n: (l, 0, n)),
            pl.BlockSpec((1, 1, tn), lambda l, n: (l, 0, n)),
        ],
        out_specs=pl.BlockSpec((1, rows, tn), lambda l, n: (l, 0, n)),
        out_shape=jax.ShapeDtypeStruct((depth, rows, n6), F32),
        compiler_params=_cparams(("parallel", "parallel")),
        name="adaln_mod",
    )(cvec, w_mod, b_mod.reshape(depth, 1, n6))


def _inproj_kernel(x_ref, lnw_ref, mod_ref, w_ref, oa_ref, ob_ref, oc_ref, og_ref, *, d):
    x = x_ref[0]
    ms = jnp.mean(x * x, axis=-1, keepdims=True)
    y = x * lax.rsqrt(ms + EPS) * lnw_ref[...]
    shift = mod_ref[0, 0, :, 0:d]
    scale = mod_ref[0, 0, :, d:2 * d]
    h = (y * (1.0 + scale) + shift).astype(BF16)
    p = _dot(h, w_ref[...])
    na = oa_ref.shape[-1]
    nb = ob_ref.shape[-1]
    nc = oc_ref.shape[-1]
    oa_ref[0] = p[:, 0:na]
    ob_ref[0] = p[:, na:na + nb]
    oc_ref[0] = p[:, na + nb:na + nb + nc]
    og_ref[0] = p[:, na + nb + nc:]


def _inproj(x, lnw, modsel, w_all, widths):
    bsz, n, d = x.shape
    na, nb, nc, ng = widths
    nout = w_all.shape[1]
    return pl.pallas_call(
        functools.partial(_inproj_kernel, d=d),
        grid=(bsz, n // TB),
        in_specs=[
            pl.BlockSpec((1, TB, d), lambda b, j: (b, j, 0)),
            pl.BlockSpec((1, d), lambda b, j: (0, 0)),
            pl.BlockSpec((1, 1, 1, modsel.shape[-1]), lambda b, j: (b, jnp.minimum(j, 1), 0, 0)),
            pl.BlockSpec((d, nout), lambda b, j: (0, 0)),
        ],
        out_specs=[
            pl.BlockSpec((1, TB, na), lambda b, j: (b, j, 0)),
            pl.BlockSpec((1, TB, nb), lambda b, j: (b, j, 0)),
            pl.BlockSpec((1, TB, nc), lambda b, j: (b, j, 0)),
            pl.BlockSpec((1, TB, ng), lambda b, j: (b, j, 0)),
        ],
        out_shape=[
            jax.ShapeDtypeStruct((bsz, n, na), F32),
            jax.ShapeDtypeStruct((bsz, n, nb), F32),
            jax.ShapeDtypeStruct((bsz, n, nc), F32),
            jax.ShapeDtypeStruct((bsz, n, ng), F32),
        ],
        compiler_params=_cparams(("parallel", "parallel")),
        name="in_proj",
    )(x, lnw, modsel, w_all)


def _amix_kernel(a_ref, nw_ref, ws_ref, bias_ref, o_ref):
    u = a_ref[0, :, 0:A_W]
    v = a_ref[0, :, A_W:2 * A_W]
    ssq = jnp.dot(v * v, _block_ones(A_W, A_HD, F32), precision=HIGHEST, preferred_element_type=F32)
    vn = v * lax.rsqrt(ssq * (1.0 / A_HD) + EPS) * nw_ref[...]
    lane_head = lax.broadcasted_iota(jnp.int32, vn.shape, 1) // A_HD
    acc = bias_ref[...]
    for h in range(A_HEADS):
        vm = jnp.where(lane_head == h, vn, 0.0).astype(BF16)
        acc = acc + _dot(ws_ref[h], vm)
    o_ref[0] = (u * acc).astype(BF16)


def _amix(oa, nw, ws, bias, first_chunk):
    bsz, n, _ = oa.shape
    nchunk = n // A_CHUNK - first_chunk
    return pl.pallas_call(
        _amix_kernel,
        grid=(bsz, nchunk),
        in_specs=[
            pl.BlockSpec((1, A_CHUNK, 2 * A_W), lambda b, j: (b, j + first_chunk, 0)),
            pl.BlockSpec((1, A_W), lambda b, j: (0, 0)),
            pl.BlockSpec((A_HEADS, A_CHUNK, A_CHUNK), lambda b, j: (0, 0, 0)),
            pl.BlockSpec((A_CHUNK, A_W), lambda b, j: (0, 0)),
        ],
        out_specs=pl.BlockSpec((1, A_CHUNK, A_W), lambda b, j: (b, j + first_chunk, 0)),
        out_shape=jax.ShapeDtypeStruct((bsz, n, A_W), BF16),
        compiler_params=_cparams(("parallel", "parallel")),
        name="mixer_a",
    )(oa, nw, ws, bias)


def _mla_prep_kernel(ob_ref, cos_ref, sin_ref, qnw_ref, kvnw_ref, wq_ref, wqs_ref, wk_ref, wv_ref,
                     hw_ref, q_ref, k_ref, v_ref):
    cq = ob_ref[0, :, 0:B_Q_RANK]
    ckv = ob_ref[0, :, B_Q_RANK:B_Q_RANK + B_KV_RANK]
    krp = ob_ref[0, :, B_Q_RANK + B_KV_RANK:B_Q_RANK + B_KV_RANK + B_HP]
    krs = ob_ref[0, :, B_Q_RANK + B_KV_RANK + B_HP:B_Q_RANK + B_KV_RANK + 2 * B_HP]
    cos = cos_ref[...]
    sin = sin_ref[...]
    cqn = (cq * lax.rsqrt(jnp.mean(cq * cq, axis=-1, keepdims=True) + EPS) * qnw_ref[...]).astype(BF16)
    ckn = (ckv * lax.rsqrt(jnp.mean(ckv * ckv, axis=-1, keepdims=True) + EPS) * kvnw_ref[...]).astype(BF16)
    q_raw = _dot(cqn, wq_ref[...])
    q_swp = _dot(cqn, wqs_ref[...])
    k_raw = _dot(ckn, wk_ref[...])
    v_all = _dot(ckn, wv_ref[...])
    qw, qws, kw, kws = hw_ref[0:1, :], hw_ref[1:2, :], hw_ref[2:3, :], hw_ref[3:4, :]
    k_rot_sw = krs * kws * sin
    for h in range(B_HEADS):
        sl = slice(h * B_HP, (h + 1) * B_HP)
        qh = q_raw[:, sl]
        rq = lax.rsqrt(jnp.sum(qh * qh, axis=-1, keepdims=True) * (1.0 / B_QK) + EPS)
        q_ref[0, h] = (rq * (qh * qw * cos + q_swp[:, sl] * qws * sin)).astype(BF16)
        kh = k_raw[:, sl] + krp
        rk = lax.rsqrt(jnp.sum(kh * kh, axis=-1, keepdims=True) * (1.0 / B_QK) + EPS)
        k_ref[0, h] = (rk * (kh * kw * cos + k_rot_sw)).astype(BF16)
        v_ref[0, h] = v_all[:, h * B_V:(h + 1) * B_V].astype(BF16)


def _mla_prep(ob, cos_t, sin_t, qnw, kvnw, wq, wqs, wk, wv, hw):
    bsz, n, nb = ob.shape
    full = lambda *s: pl.BlockSpec(s, lambda b, j: (0,) * len(s))
    return pl.pallas_call(
        _mla_prep_kernel,
        grid=(bsz, n // TB),
        in_specs=[
            pl.BlockSpec((1, TB, nb), lambda b, j: (b, j, 0)),
            pl.BlockSpec((TB, B_HP), lambda b, j: (j, 0)),
            pl.BlockSpec((TB, B_HP), lambda b, j: (j, 0)),
            full(1, B_Q_RANK), full(1, B_KV_RANK),
            full(B_Q_RANK, B_HEADS * B_HP), full(B_Q_RANK, B_HEADS * B_HP),
            full(B_KV_RANK, B_HEADS * B_HP), full(B_KV_RANK, B_HEADS * B_V),
            full(8, B_HP),
        ],
        out_specs=[
            pl.BlockSpec((1, B_HEADS, TB, B_HP), lambda b, j: (b, 0, j, 0)),
            pl.BlockSpec((1, B_HEADS, TB, B_HP), lambda b, j: (b, 0, j, 0)),
            pl.BlockSpec((1, B_HEADS, TB, B_V), lambda b, j: (b, 0, j, 0)),
        ],
        out_shape=[
            jax.ShapeDtypeStruct((bsz, B_HEADS, n, B_HP), BF16),
            jax.ShapeDtypeStruct((bsz, B_HEADS, n, B_HP), BF16),
            jax.ShapeDtypeStruct((bsz, B_HEADS, n, B_V), BF16),
        ],
        compiler_params=_cparams(("parallel", "parallel")),
        name="mla_prep",
    )(ob, cos_t, sin_t, qnw, kvnw, wq, wqs, wk, wv, hw)


def _attn_kernel(q_ref, k_ref, v_ref, o_ref, o_scr):
    scale = B_QK ** -0.5
    for h in range(B_HEADS):
        s = _dot_nt(q_ref[0, h], k_ref[0, h])
        m = jnp.max(s, axis=-1, keepdims=True)
        p = jnp.exp((s - m) * scale)
        l = jnp.sum(p, axis=-1, keepdims=True)
        o = _dot(p.astype(BF16), v_ref[0, h])
        o_scr[:, h * B_V:(h + 1) * B_V] = o / l
    o_ref[0] = o_scr[...].astype(BF16)


def _attention(q, k, v, first_qblock, n_qblocks, n_keys, out_rows):
    bsz = q.shape[0]
    return pl.pallas_call(
        _attn_kernel,
        grid=(bsz, n_qblocks),
        in_specs=[
            pl.BlockSpec((1, B_HEADS, TB, B_HP), lambda b, j: (b, 0, j + first_qblock, 0)),
            pl.BlockSpec((1, B_HEADS, n_keys, B_HP), lambda b, j: (b, 0, 0, 0)),
            pl.BlockSpec((1, B_HEADS, n_keys, B_V), lambda b, j: (b, 0, 0, 0)),
        ],
        out_specs=pl.BlockSpec((1, TB, B_HEADS * B_V), lambda b, j: (b, j, 0)),
        out_shape=jax.ShapeDtypeStruct((bsz, out_rows, B_HEADS * B_V), BF16),
        scratch_shapes=[pltpu.VMEM((TB, B_HEADS * B_V), F32)],
        compiler_params=_cparams(("parallel", "arbitrary")),
        name="mla_attention",
    )(q, k, v)


def _hgrn_chunk(blk, zcol, lbc, st_ref, rev):
    cc, w = C_CHUNK, C_W
    q = blk[:, 0:w] * (C_DK ** -0.5)
    z = blk[:, zcol * w:(zcol + 1) * w]
    v = blk[:, 3 * w:4 * w]
    log_lb, log1m_lb, one_m_lb = lbc[0:1, :], lbc[1:2, :], lbc[2:3, :]
    az = jnp.abs(z)
    sp = jnp.log1p(jnp.exp(-az))
    lsig = jnp.minimum(z, 0.0) - sp
    t2 = log1m_lb + lsig
    mx = jnp.maximum(log_lb, t2)
    mn = jnp.minimum(log_lb, t2)
    logf = mx + jnp.log1p(jnp.exp(mn - mx))
    kk = one_m_lb * _sigmoid(-z)

    ti = lax.broadcasted_iota(jnp.int32, (cc, cc), 0)
    ui = lax.broadcasted_iota(jnp.int32, (cc, cc), 1)
    tri = ((ui >= ti) if rev else (ui <= ti)).astype(F32)
    b = jnp.dot(tri, logf, precision=HIGHEST, preferred_element_type=F32)
    b_tot = b[0:1, :] if rev else b[cc - 1:cc, :]

    row = lax.broadcasted_iota(jnp.int32, (cc, w), 0)
    lane_head = lax.broadcasted_iota(jnp.int32, (cc, w), 1) // C_DK
    nsub = cc // C_SUB
    row_blk = row // C_SUB

    beta_rows = []
    for i in range(nsub):
        if rev:
            src = None if i == nsub - 1 else b[(i + 1) * C_SUB:(i + 1) * C_SUB + 1, :]
        else:
            src = None if i == 0 else b[i * C_SUB - 1:i * C_SUB, :]
        beta_rows.append(src)
    beta_full = jnp.concatenate(
        [jnp.broadcast_to(b[i * C_SUB:i * C_SUB + 1, :] if r is None else r, (C_SUB, w))
         for i, r in enumerate(beta_rows)], axis=0)
    has_prev = (row_blk < nsub - 1) if rev else (row_blk > 0)
    qs = jnp.where(has_prev, q * jnp.exp(b - beta_full), 0.0)

    q_stack = jnp.concatenate([jnp.where(lane_head == h, qs, 0.0) for h in range(C_HEADS)], axis=0).astype(BF16)
    qblocks = [i for i in range(nsub) if beta_rows[i] is not None]
    ks_parts = []
    for i in qblocks:
        prev = (row_blk > i) if rev else (row_blk < i)
        ks_parts.append(jnp.where(prev, kk * jnp.exp(beta_rows[i] - b), 0.0))
    ks_all = jnp.concatenate(ks_parts, axis=0).astype(BF16)
    a_all = _dot_nt(q_stack, ks_all)
    ar = lax.broadcasted_iota(jnp.int32, a_all.shape, 0)
    ac = lax.broadcasted_iota(jnp.int32, a_all.shape, 1)
    r_blk = (ar % cc) // C_SUB
    c_blk = ac // cc + (0 if rev else 1)
    a_all = jnp.where(r_blk == c_blk, a_all, 0.0).astype(BF16)
    v_bf = v.astype(BF16)
    r_all = _dot(a_all, jnp.concatenate([v_bf] * len(qblocks), axis=0))
    o = jnp.zeros((cc, w), F32)
    for h in range(C_HEADS):
        o = o + jnp.where(lane_head == h, r_all[h * cc:(h + 1) * cc, :], 0.0)

    ones_bd = _block_ones(w, C_DK, BF16)
    tsub = lax.broadcasted_iota(jnp.int32, (C_SUB, w), 0)
    diag_parts = []
    for i in range(nsub):
        r0 = i * C_SUB
        bb = b[r0:r0 + C_SUB, :]
        qq = q[r0:r0 + C_SUB, :]
        ps = []
        for s in range(C_SUB):
            keep = (tsub <= s) if rev else (tsub >= s)
            e = jnp.where(keep, jnp.exp(bb - b[r0 + s:r0 + s + 1, :]), 0.0)
            ps.append(qq * e * kk[r0 + s:r0 + s + 1, :])
        red = _dot(jnp.concatenate(ps, axis=0).astype(BF16), ones_bd)
        od = jnp.zeros((C_SUB, w), F32)
        for s in range(C_SUB):
            od = od + red[s * C_SUB:(s + 1) * C_SUB, :] * v[r0 + s:r0 + s + 1, :]
        diag_parts.append(od)
    o = o + jnp.concatenate(diag_parts, axis=0)

    st = st_ref[...]
    o = o + _dot_nt((q * jnp.exp(b)).astype(BF16), st.astype(BF16))
    kd = (kk * jnp.exp(b_tot - b)).astype(BF16)
    upd = _dot_tn(v_bf, kd)
    st_ref[...] = st * jnp.exp(b_tot) + upd * _block_ones(w, C_DK, F32)
    return o


def _hgrn_kernel(cf_ref, cb_ref, lbc_ref, of_ref, ob_ref, sf_ref, sb_ref):
    @pl.when(pl.program_id(1) == 0)
    def _():
        sf_ref[...] = jnp.zeros_like(sf_ref)
        sb_ref[...] = jnp.zeros_like(sb_ref)

    of_ref[0] = _hgrn_chunk(cf_ref[0], 1, lbc_ref[0], sf_ref, rev=False)
    ob_ref[0] = _hgrn_chunk(cb_ref[0], 2, lbc_ref[1], sb_ref, rev=True)


def _hgrn(oc, lbc, n_ctx):
    bsz, n, wc = oc.shape
    nch = n // C_CHUNK
    nctx = n_ctx // C_CHUNK

    def bwd_idx(c):
        return jnp.where(c < nctx, nctx - 1 - c, nch + nctx - 1 - c)

    return pl.pallas_call(
        _hgrn_kernel,
        grid=(bsz, nch),
        in_specs=[
            pl.BlockSpec((1, C_CHUNK, wc), lambda b, c: (b, c, 0)),
            pl.BlockSpec((1, C_CHUNK, wc), lambda b, c: (b, bwd_idx(c), 0)),
            pl.BlockSpec((2, 8, C_W), lambda b, c: (0, 0, 0)),
        ],
        out_specs=[
            pl.BlockSpec((1, C_CHUNK, C_W), lambda b, c: (b, c, 0)),
            pl.BlockSpec((1, C_CHUNK, C_W), lambda b, c: (b, bwd_idx(c), 0)),
        ],
        out_shape=[jax.ShapeDtypeStruct((bsz, n, C_W), F32)] * 2,
        scratch_shapes=[pltpu.VMEM((C_W, C_W), F32)] * 2,
        compiler_params=_cparams(("parallel", "arbitrary")),
        name="hgrn2_scan",
    )(oc, oc, lbc)


def _outproj_kernel(x_ref, a_ref, b_ref, of_ref, ob_ref, g_ref, cnw_ref, wa_ref, wb_ref, wc_ref,
                    mod_ref, ln2_ref, xo_ref, h2_ref, *, d):
    o = of_ref[0] + ob_ref[0]
    ssq = jnp.dot(o * o, _block_ones(C_W, C_DV, F32), precision=HIGHEST, preferred_element_type=F32)
    g = g_ref[0]
    c_out = o * lax.rsqrt(ssq * (1.0 / C_DV) + EPS) * cnw_ref[...] * (g * _sigmoid(g))
    mix = _dot(a_ref[0], wa_ref[...]) + _dot(b_ref[0], wb_ref[...]) + _dot(c_out.astype(BF16), wc_ref[...])
    gate1 = mod_ref[0, 0, :, 2 * d:3 * d]
    shift2 = mod_ref[0, 0, :, 3 * d:4 * d]
    scale2 = mod_ref[0, 0, :, 4 * d:5 * d]
    x = x_ref[0] + gate1 * mix
    xo_ref[0] = x
    y = x * lax.rsqrt(jnp.mean(x * x, axis=-1, keepdims=True) + EPS) * ln2_ref[...]
    h2_ref[0] = (y * (1.0 + scale2) + shift2).astype(BF16)


def _outproj(x, a_out, b_out, o_f, o_b, og, cnw, wa, wb, wc, modsel, ln2, first_block):
    bsz, n, d = x.shape
    nblk = n // TB - first_block
    full = lambda *s: pl.BlockSpec(s, lambda b, j: (0,) * len(s))
    tok = lambda w: pl.BlockSpec((1, TB, w), lambda b, j: (b, j + first_block, 0))
    return pl.pallas_call(
        functools.partial(_outproj_kernel, d=d),
        grid=(bsz, nblk),
        in_specs=[
            tok(d), tok(A_W),
            pl.BlockSpec((1, TB, B_HEADS * B_V), lambda b, j: (b, j, 0)) if first_block else tok(B_HEADS * B_V),
            tok(C_W), tok(C_W), tok(C_W),
            full(1, C_W), full(A_W, d), full(B_HEADS * B_V, d), full(C_W, d),
            pl.BlockSpec((1, 1, 1, modsel.shape[-1]), lambda b, j: (b, jnp.minimum(j + first_block, 1), 0, 0)),
            full(1, d),
        ],
        out_specs=[
            pl.BlockSpec((1, TB, d), lambda b, j: (b, j, 0)),
            pl.BlockSpec((1, TB, d), lambda b, j: (b, j, 0)),
        ],
        out_shape=[
            jax.ShapeDtypeStruct((bsz, nblk * TB, d), F32),
            jax.ShapeDtypeStruct((bsz, nblk * TB, d), BF16),
        ],
        compiler_params=_cparams(("parallel", "parallel")),
        name="out_proj",
    )(x, a_out, b_out, o_f, o_b, og, cnw, wa, wb, wc, modsel, ln2)


def _top16_exact(s):
    nrows = s.shape[0]
    iota = lax.broadcasted_iota(jnp.int32, s.shape, 0).astype(F32)
    rank = jnp.full(s.shape, float(P_TOPK), F32)
    vals = []
    for r in range(P_TOPK):
        m = jnp.max(s, axis=0, keepdims=True)
        idx = jnp.min(jnp.where(s == m, iota, float(nrows)), axis=0, keepdims=True)
        hit = iota == idx
        rank = jnp.where(hit, float(r), rank)
        s = jnp.where(hit, -jnp.inf, s)
        vals.append(m)
    return jnp.concatenate(vals, axis=0), rank


_MARK0 = int(np.array(0xFF7FFFFF, np.uint32).view(np.int32))


def _top16_marked(s):
    vals = []
    for r in range(P_TOPK):
        m = jnp.max(s, axis=0, keepdims=True)
        mark = float(np.array(_MARK0 - r, np.int32).view(np.float32))
        s = jnp.where(s == m, mark, s)
        vals.append(m)
    rr = _MARK0 - pltpu.bitcast(s, jnp.int32)
    rank = jnp.where(rr < 0, P_TOPK, jnp.where(rr > P_TOPK - 1, P_TOPK, rr)).astype(F32)
    cnt = jnp.sum(jnp.where(rank < float(P_TOPK), 1.0, 0.0), axis=0, keepdims=True)
    return jnp.concatenate(vals, axis=0), rank, cnt


def _bf16_bits(x):
    return lax.shift_right_logical(pltpu.bitcast(x.astype(BF16).astype(F32), jnp.int32), 16)


def _pack_rows(x):
    b = _bf16_bits(x)
    return b | lax.shift_left(b, 16)


def _pack_halves(x):
    b = _bf16_bits(x)
    half = x.shape[0] // 2
    return b[0:half] | lax.shift_left(b[half:], 16)


_PAIR_PERM = np.arange(P_N_KEYS).reshape(2, P_N_KEYS // 2).T.reshape(-1)


def _unpack_row(row):
    n = row.shape[-1]
    tile = pltpu.bitcast(jnp.broadcast_to(row, (8, n)), BF16)
    return jnp.concatenate([tile] * (P_N_KEYS // 16), axis=0)


def _route_kernel(h_ref, wq_ref, sk_ref, r2_ref, e2_ref, n1_ref, e1_ref,
                  q_scr, s_scr, v_scr, r1_scr, e1_scr):
    nl = P_N_KEYS
    q = _dot(h_ref[...], wq_ref[...]).astype(BF16)
    for l in range(2 * P_HEADS):
        q_scr[l] = q[:, l * P_HALF:(l + 1) * P_HALF]

    def lane_tile(lt, carry):
        row0 = pl.multiple_of(lt * nl, nl)

        def head(h, c):
            s1 = _dot_nt(sk_ref[h, 0], q_scr[2 * h, pl.ds(row0, nl), :])
            s2 = _dot_nt(sk_ref[h, 1], q_scr[2 * h + 1, pl.ds(row0, nl), :])
            s_scr[0] = s1
            s_scr[1] = s2
            v1, rank1, cnt1 = _top16_marked(s1)
            v2, rank2, cnt2 = _top16_marked(s2)
            v_scr[0, h] = v1
            v_scr[1, h] = v2
            r1_scr[h] = rank1
            r2_ref[lt, h] = _pack_halves(rank2)
            e1_scr[h] = jnp.exp(s1 - v1[0:1, :])
            e2_ref[lt, h] = _pack_halves(jnp.exp(s2 - v2[0:1, :]))
            tie = jnp.max(jnp.abs(cnt1 - float(P_TOPK)) + jnp.abs(cnt2 - float(P_TOPK))) > 0.0

            @pl.when(tie)
            def _():
                ve1, re1 = _top16_exact(s_scr[0])
                ve2, re2 = _top16_exact(s_scr[1])
                v_scr[0, h] = ve1
                v_scr[1, h] = ve2
                r1_scr[h] = re1
                r2_ref[lt, h] = _pack_halves(re2)

            return c

        lax.fori_loop(0, P_HEADS, head, 0)

        v1 = v_scr[0]
        v2 = v_scr[1]
        ia = lax.broadcasted_iota(jnp.int32, v1.shape, 1).astype(F32)
        n = jnp.zeros(v1.shape, F32)
        g = jnp.broadcast_to(v2[:, 0:1, :], v1.shape)
        cmax = v1[:, 0:1, :] + v2[:, 0:1, :]
        z = jnp.zeros(cmax.shape, F32)
        for _ in range(P_TOPK):
            f = v1 + g
            m = jnp.max(f, axis=1, keepdims=True)
            a_star = jnp.min(jnp.where(f == m, ia, float(P_TOPK)), axis=1, keepdims=True)
            hit = ia == a_star
            n = n + jnp.where(hit, 1.0, 0.0)
            nsel = jnp.sum(jnp.where(hit, n, 0.0), axis=1, keepdims=True)
            nxt = jnp.sum(jnp.where(ia == nsel, v2, 0.0), axis=1, keepdims=True)
            nxt = jnp.where(nsel > P_TOPK - 0.5, -jnp.inf, nxt)
            g = jnp.where(hit, nxt, g)
            z = z + jnp.exp(m - cmax)
        zinv = 1.0 / z
        for h in range(P_HEADS):
            rank1 = r1_scr[h]
            n1 = jnp.zeros(rank1.shape, F32)
            for a in range(P_TOPK):
                n1 = jnp.where(rank1 == float(a), n[h, a:a + 1, :], n1)
            n1_ref[lt, h] = _pack_rows(n1)
            e1_ref[lt, h] = _pack_rows(e1_scr[h] * zinv[h])
        return carry

    lax.fori_loop(0, h_ref.shape[0] // nl, lane_tile, 0)


def _route(h2, wq, sk):
    t, d = h2.shape
    tr = PEER_TT
    nl = P_N_KEYS
    oshape = (t // nl, P_HEADS, P_N_KEYS, nl)
    hshape = (t // nl, P_HEADS, P_N_KEYS // 2, nl)
    ospec = pl.BlockSpec((tr // nl, P_HEADS, P_N_KEYS, nl), lambda i: (i, 0, 0, 0))
    hspec = pl.BlockSpec((tr // nl, P_HEADS, P_N_KEYS // 2, nl), lambda i: (i, 0, 0, 0))
    return pl.pallas_call(
        _route_kernel,
        grid=(t // tr,),
        in_specs=[
            pl.BlockSpec((tr, d), lambda i: (i, 0)),
            pl.BlockSpec((d, P_HEADS * P_KEY_DIM), lambda i: (0, 0)),
            pl.BlockSpec((P_HEADS, 2, P_N_KEYS, P_HALF), lambda i: (0, 0, 0, 0)),
        ],
        out_specs=[hspec, hspec, ospec, ospec],
        out_shape=[jax.ShapeDtypeStruct(hshape, jnp.int32), jax.ShapeDtypeStruct(hshape, jnp.int32),
                   jax.ShapeDtypeStruct(oshape, jnp.int32), jax.ShapeDtypeStruct(oshape, jnp.int32)],
        scratch_shapes=[
            pltpu.VMEM((2 * P_HEADS, tr, P_HALF), BF16),
            pltpu.VMEM((2, P_N_KEYS, nl), F32),
            pltpu.VMEM((2, P_HEADS, P_TOPK, nl), F32),
            pltpu.VMEM((P_HEADS, P_N_KEYS, nl), F32),
            pltpu.VMEM((P_HEADS, P_N_KEYS, nl), F32),
        ],
        compiler_params=_cparams(("parallel",)),
        name="peer_route",
    )(h2, wq, sk)


def _peer_kernel(h_ref, u_ref, vt_ref, r2_ref, e2_ref, n1_ref, e1_ref, x_ref, g5_ref, o_ref,
                 acc_ref, w_ref, a_scr, *, blocks_per_batch, ctx_blocks):
    i = pl.program_id(0)
    k = pl.program_id(1)

    @pl.when(k == 0)
    def _():
        acc_ref[...] = jnp.zeros_like(acc_ref)

    nl = P_N_KEYS
    a_scr[...] = _dot_nt(u_ref[...], h_ref[...])
    zero = jnp.zeros((P_N_KEYS, nl), BF16)
    for lt in range(PEER_TT // nl):
        cols = slice(lt * nl, (lt + 1) * nl)
        for j in range(PEER_I1):
            rows = slice(j * P_N_KEYS, (j + 1) * P_N_KEYS)
            g = zero
            for h in range(P_HEADS):
                n1row = _unpack_row(n1_ref[lt, h, j:j + 1, :])
                e1row = _unpack_row(e1_ref[lt, h, j:j + 1, :])
                r2 = pltpu.bitcast(r2_ref[lt, h], BF16)
                e2 = pltpu.bitcast(e2_ref[lt, h], BF16)
                g = g + jnp.where(r2 < n1row, e2, zero) * e1row
            a = a_scr[rows, cols]
            act = 0.5 * a * (1.0 + lax.erf(a * (2.0 ** -0.5)))
            w_ref[rows, cols] = act.astype(BF16) * g
    acc_ref[...] += _dot(vt_ref[...], w_ref[...])

    @pl.when(k == pl.num_programs(1) - 1)
    def _():
        y = acc_ref[...].T
        for u in range(PEER_TT // TB):
            sblk = i * (PEER_TT // TB) + u
            bidx = sblk // blocks_per_batch
            is_lat = (sblk - bidx * blocks_per_batch) >= ctx_blocks
            gate = g5_ref[2 * bidx + is_lat.astype(jnp.int32)]
            rs = slice(u * TB, (u + 1) * TB)
            o_ref[rs, :] = x_ref[rs, :] + gate * y[rs, :]


def _peer(h2, u_bf, vt_bf, r2, e2, n1, e1, x, g5, blocks_per_batch, ctx_blocks):
    t, d = h2.shape
    ne = u_bf.shape[0]
    nl = P_N_KEYS
    rspec = pl.BlockSpec((PEER_TT // nl, P_HEADS, P_N_KEYS // 2, nl), lambda i, k: (i, 0, 0, 0))
    nspec = pl.BlockSpec((PEER_TT // nl, P_HEADS, PEER_I1, nl), lambda i, k: (i, 0, k, 0))
    return pl.pallas_call(
        functools.partial(_peer_kernel, blocks_per_batch=blocks_per_batch, ctx_blocks=ctx_blocks),
        grid=(t // PEER_TT, ne // PEER_ET),
        in_specs=[
            pl.BlockSpec((PEER_TT, d), lambda i, k: (i, 0)),
            pl.BlockSpec((PEER_ET, d), lambda i, k: (k, 0)),
            pl.BlockSpec((d, PEER_ET), lambda i, k: (0, k)),
            rspec, rspec, nspec, nspec,
            pl.BlockSpec((PEER_TT, d), lambda i, k: (i, 0)),
            pl.BlockSpec(g5.shape, lambda i, k: (0, 0, 0)),
        ],
        out_specs=pl.BlockSpec((PEER_TT, d), lambda i, k: (i, 0)),
        out_shape=jax.ShapeDtypeStruct((t, d), F32),
        scratch_shapes=[pltpu.VMEM((d, PEER_TT), F32), pltpu.VMEM((PEER_ET, PEER_TT), BF16),
                        pltpu.VMEM((PEER_ET, PEER_TT), F32)],
        compiler_params=_cparams(("parallel", "arbitrary")),
        name="peer_experts",
    )(h2, u_bf, vt_bf, r2, e2, n1, e1, x, g5)


def _transpose_cast_kernel(x_ref, o_ref):
    o_ref[...] = x_ref[...].T.astype(BF16)


def _transpose_cast(v):
    ne, d = v.shape
    te = 512
    return pl.pallas_call(
        _transpose_cast_kernel,
        grid=(ne // te,),
        in_specs=[pl.BlockSpec((te, d), lambda e: (e, 0))],
        out_specs=pl.BlockSpec((d, te), lambda e: (0, e)),
        out_shape=jax.ShapeDtypeStruct((d, ne), BF16),
        compiler_params=_cparams(("parallel",)),
        name="expert_value_transpose",
    )(v)


def _rope_tables(n_ctx, n_lat):
    n_freq = B_ROPE // 4
    pos = np.arange(n_lat)
    inv_freq = ROPE_THETA ** (-np.arange(n_freq, dtype=np.float32) / n_freq)
    inv_freq = jnp.asarray(inv_freq, F32)
    rowp = jnp.asarray(pos // GRID_W, F32)
    colp = jnp.asarray(pos % GRID_W, F32)
    ang = jnp.stack([rowp[:, None] * inv_freq, colp[:, None] * inv_freq], axis=1)
    cos, sin = jnp.cos(ang), jnp.sin(ang)
    cos32 = jnp.concatenate([cos, cos], axis=2).reshape(n_lat, B_ROPE)
    sin32 = jnp.concatenate([-sin, sin], axis=2).reshape(n_lat, B_ROPE)
    pad_l = jnp.ones((n_lat, B_NOPE), F32)
    pad_r = jnp.ones((n_lat, B_HP - B_QK), F32)
    cos_l = jnp.concatenate([pad_l, cos32, pad_r], axis=1)
    sin_l = jnp.concatenate([0 * pad_l, sin32, 0 * pad_r], axis=1)
    cos_t = jnp.concatenate([jnp.ones((n_ctx, B_HP), F32), cos_l], axis=0)
    sin_t = jnp.concatenate([jnp.zeros((n_ctx, B_HP), F32), sin_l], axis=0)
    return cos_t, sin_t


_SWAP32 = np.arange(B_ROPE) ^ (B_ROPE // 4)


def _pad_head(nope, rope):
    z = jnp.zeros(nope.shape[:-1] + (B_HP - B_QK,), nope.dtype)
    out = jnp.concatenate([nope, rope, z], axis=-1)
    return out.reshape(out.shape[:-2] + (out.shape[-2] * B_HP,))


def _layer_weights(layer, w_in, w_out, a_norm_w, a_w_s, a_b_s, b_q_norm_w, b_w_uq, b_kv_norm_w, b_w_ukv,
                   b_q_head_norm_w, b_k_head_norm_w, c_out_norm_w, p_w_q, p_sub_keys):
    d = w_in.shape[1]
    wi = w_in[layer]
    offs = np.cumsum([0, A_W, A_W, B_Q_RANK, B_KV_RANK, B_ROPE, C_W, C_W, C_W, C_W, C_W])
    col = lambda i: wi[:, offs[i]:offs[i + 1]]
    w_kr = col(4)
    zl = jnp.zeros((d, B_NOPE), F32)
    zr = jnp.zeros((d, B_HP - B_QK), F32)
    kr_placed = jnp.concatenate([zl, w_kr, zr], axis=1)
    kr_swapped = jnp.concatenate([zl, w_kr[:, _SWAP32], zr], axis=1)
    w_all = jnp.concatenate(
        [col(0), col(1), col(2), col(3), kr_placed, kr_swapped, col(5), col(6), col(7), col(8), col(9)],
        axis=1).astype(BF16)
    widths = (2 * A_W, B_Q_RANK + B_KV_RANK + 2 * B_HP, 4 * C_W, C_W)

    wuq = b_w_uq[layer].reshape(B_Q_RANK, B_HEADS, B_QK)
    wq_p = _pad_head(wuq[..., :B_NOPE], wuq[..., B_NOPE:]).astype(BF16)
    wq_s = _pad_head(0 * wuq[..., :B_NOPE], wuq[..., B_NOPE:][..., _SWAP32]).astype(BF16)
    wukv = b_w_ukv[layer].reshape(B_KV_RANK, B_HEADS, B_NOPE + B_V)
    wk_p = _pad_head(wukv[..., :B_NOPE], jnp.zeros((B_KV_RANK, B_HEADS, B_ROPE), F32)).astype(BF16)
    wv = wukv[..., B_NOPE:].reshape(B_KV_RANK, B_HEADS * B_V).astype(BF16)
    qn, kn = b_q_head_norm_w[layer], b_k_head_norm_w[layer]
    zpad = jnp.zeros((B_HP - B_QK,), F32)
    z64 = jnp.zeros((B_NOPE,), F32)
    hw = jnp.stack([
        jnp.concatenate([qn, zpad]),
        jnp.concatenate([z64, qn[B_NOPE:][_SWAP32], zpad]),
        jnp.concatenate([kn, zpad]),
        jnp.concatenate([z64, kn[B_NOPE:][_SWAP32], zpad]),
    ] + [jnp.zeros((B_HP,), F32)] * 4, axis=0)

    wo = w_out[layer].astype(BF16)
    return dict(
        w_all=w_all, widths=widths,
        a_nw=a_norm_w[layer].reshape(1, A_W),
        a_ws=a_w_s[layer].astype(BF16),
        a_bias=jnp.repeat(a_b_s[layer].T, A_HD, axis=1),
        qnw=b_q_norm_w[layer].reshape(1, B_Q_RANK), kvnw=b_kv_norm_w[layer].reshape(1, B_KV_RANK),
        wq_p=wq_p, wq_s=wq_s, wk_p=wk_p, wv=wv, hw=hw,
        cnw=jnp.tile(c_out_norm_w[layer], C_HEADS).reshape(1, C_W),
        wo_a=wo[:A_W], wo_b=wo[A_W:A_W + B_HEADS * B_V], wo_c=wo[A_W + B_HEADS * B_V:],
        p_wq=p_w_q[layer].astype(BF16), p_sk=p_sub_keys[layer].astype(BF16),
    )


def kernel(x, c, ctx, c_ctx, ln1_w, ln2_w, w_mod, b_mod, w_in, w_out, a_norm_w, a_w_s, a_b_s, b_q_norm_w,
           b_w_uq, b_kv_norm_w, b_w_ukv, b_q_head_norm_w, b_k_head_norm_w, c_lb_logits, c_out_norm_w,
           p_w_q, p_sub_keys, p_u, p_v):
    bsz, n_lat, d = x.shape
    n_ctx = ctx.shape[1]
    depth = w_in.shape[0]
    n = n_ctx + n_lat
    ctx_blocks = n_ctx // TB
    blocks = n // TB

    mrows = -(-(bsz + 1) // 8) * 8
    cvec = jnp.concatenate([c, c_ctx[None, :], jnp.zeros((mrows - bsz - 1, d), F32)], axis=0)
    mod_all = _modulation(cvec, w_mod, b_mod)

    lb = jnp.cumsum(jax.nn.softmax(c_lb_logits.astype(F32), axis=0), axis=0)
    lb = lb - lb[0:1]
    lbc_all = jnp.stack([jnp.log(lb), jnp.log1p(-lb), 1.0 - lb] + [jnp.zeros_like(lb)] * 5, axis=2)

    cos_t, sin_t = _rope_tables(n_ctx, n_lat)
    xc = jnp.concatenate([ctx, x], axis=1)

    for layer in range(depth):
        last = layer == depth - 1
        w = _layer_weights(layer, w_in, w_out, a_norm_w, a_w_s, a_b_s, b_q_norm_w, b_w_uq, b_kv_norm_w,
                           b_w_ukv, b_q_head_norm_w, b_k_head_norm_w, c_out_norm_w, p_w_q, p_sub_keys)
        mod_b = mod_all[layer, :bsz]
        mod_c = jnp.broadcast_to(mod_all[layer, bsz][None, :], mod_b.shape)
        modsel = jnp.stack([mod_c, mod_b], axis=1)[:, :, None, :]
        first_block = ctx_blocks if last else 0

        oa, ob, oc, og = _inproj(xc, ln1_w[layer].reshape(1, d), modsel, w["w_all"], w["widths"])
        a_out = _amix(oa, w["a_nw"], w["a_ws"], w["a_bias"], first_block * (TB // A_CHUNK))
        q, k, v = _mla_prep(ob, cos_t, sin_t, w["qnw"], w["kvnw"], w["wq_p"], w["wq_s"], w["wk_p"], w["wv"], w["hw"])
        b_lat = _attention(q, k, v, ctx_blocks, blocks - ctx_blocks, n, n_lat)
        if last:
            b_out = b_lat
        else:
            b_ctx = _attention(q, k, v, 0, ctx_blocks, n_ctx, n_ctx)
            b_out = jnp.concatenate([b_ctx, b_lat], axis=1)
        o_f, o_b = _hgrn(oc, lbc_all[layer], n_ctx)
        x_new, h2 = _outproj(xc, a_out, b_out, o_f, o_b, og, w["cnw"], w["wo_a"], w["wo_b"], w["wo_c"],
                             modsel, ln2_w[layer].reshape(1, d), first_block)

        t = x_new.shape[0] * x_new.shape[1]
        h2f = h2.reshape(t, d)
        r2, e2, n1, e1 = _route(h2f, w["p_wq"], w["p_sk"])
        g5 = modsel[:, :, :, 5 * d:6 * d].reshape(2 * bsz, 1, d)
        perm = lambda tab: tab.reshape(P_N_KEYS, P_N_KEYS, d)[:, _PAIR_PERM, :].reshape(P_N_KEYS * P_N_KEYS, d)
        u_bf = perm(p_u[layer]).astype(BF16)
        vt_bf = _transpose_cast(perm(p_v[layer]))
        out = _peer(h2f, u_bf, vt_bf, r2, e2, n1, e1, x_new.reshape(t, d), g5,
                    blocks - first_block, ctx_blocks - first_block)
        xc = out.reshape(bsz, t // bsz, d)
    return xc
```

```python
import functools
import math

import jax
import jax.numpy as jnp
import numpy as np
from jax import lax
from jax.experimental import pallas as pl
from jax.experimental.pallas import tpu as pltpu

F32 = jnp.float32
BF16 = jnp.bfloat16
HIGHEST = lax.Precision.HIGHEST

EPS = 1e-6
GRID_W = 64
ROPE_THETA = 10000.0

A_HEADS, A_HD, A_CHUNK = 4, 64, 128
A_W = A_HEADS * A_HD
B_HEADS, B_NOPE, B_ROPE, B_V = 8, 64, 32, 64
B_QK = B_NOPE + B_ROPE
B_HP = 128
B_Q_RANK, B_KV_RANK = 256, 128
C_HEADS, C_DK, C_DV, C_CHUNK = 4, 64, 64, 64
C_W = C_HEADS * C_DK
C_SUB = 16
P_HEADS, P_KEY_DIM, P_N_KEYS, P_TOPK = 8, 256, 128, 16
P_HALF = P_KEY_DIM // 2

TB = 256
PEER_TT = 1024
PEER_TC = 512
PEER_ET = 1024
PEER_I1 = PEER_ET // P_N_KEYS
PEER_JG = 4
ROUTE_HPT = 2
VMEM_LIMIT = 56 * 1024 * 1024


def _cparams(sem, flags=None):
    return pltpu.CompilerParams(dimension_semantics=sem, vmem_limit_bytes=VMEM_LIMIT, flags=flags)


def _dot_nt(a, b):
    return lax.dot_general(a, b, (((1,), (1,)), ((), ())), preferred_element_type=F32)


def _dot_tn(a, b):
    return lax.dot_general(a, b, (((0,), (0,)), ((), ())), preferred_element_type=F32)


def _dot(a, b):
    return jnp.dot(a, b, preferred_element_type=F32)


def _sigmoid(x):
    return 1.0 / (1.0 + jnp.exp(-x))


def _block_ones(n, blk, dtype):
    r = lax.broadcasted_iota(jnp.int32, (n, n), 0) // blk
    c = lax.broadcasted_iota(jnp.int32, (n, n), 1) // blk
    return (r == c).astype(dtype)


def _mod_kernel(c_ref, w_ref, b_ref, o_ref):
    c = c_ref[...]
    sc = c * _sigmoid(c)
    o_ref[0] = _dot(sc.astype(BF16), w_ref[0].astype(BF16)) + b_ref[0]


def _modulation(cvec, w_mod, b_mod):
    depth, d, n6 = w_mod.shape
    rows = cvec.shape[0]
    tn = 1024
    return pl.pallas_call(
        _mod_kernel,
        grid=(depth, n6 // tn),
        in_specs=[
            pl.BlockSpec((rows, d), lambda l, n: (0, 0)),
            pl.BlockSpec((1, d, tn), lambda l, n: (l, 0, n)),
            pl.BlockSpec((1, 1, tn), lambda l, n: (l, 0, n)),
        ],
        out_specs=pl.BlockSpec((1, rows, tn), lambda l, n: (l, 0, n)),
        out_shape=jax.ShapeDtypeStruct((depth, rows, n6), F32),
        compiler_params=_cparams(("parallel", "parallel")),
        name="adaln_mod",
    )(cvec, w_mod, b_mod.reshape(depth, 1, n6))


def _inproj_kernel(x_ref, lnw_ref, mod_ref, w_ref, oa_ref, ob_ref, oc_ref, og_ref, *, d):
    x = x_ref[0]
    ms = jnp.mean(x * x, axis=-1, keepdims=True)
    y = x * lax.rsqrt(ms + EPS) * lnw_ref[...]
    shift = mod_ref[0, 0, :, 0:d]
    scale = mod_ref[0, 0, :, d:2 * d]
    h = (y * (1.0 + scale) + shift).astype(BF16)
    p = _dot(h, w_ref[...])
    na = oa_ref.shape[-1]
    nb = ob_ref.shape[-1]
    nc = oc_ref.shape[-1]
    oa_ref[0] = p[:, 0:na]
    ob_ref[0] = p[:, na:na + nb]
    oc_ref[0] = p[:, na + nb:na + nb + nc]
    og_ref[0] = p[:, na + nb + nc:]


def _inproj(x, lnw, modsel, w_all, widths):
    bsz, n, d = x.shape
    na, nb, nc, ng = widths
    nout = w_all.shape[1]
    return pl.pallas_call(
        functools.partial(_inproj_kernel, d=d),
        grid=(bsz, n // TB),
        in_specs=[
            pl.BlockSpec((1, TB, d), lambda b, j: (b, j, 0)),
            pl.BlockSpec((1, d), lambda b, j: (0, 0)),
            pl.BlockSpec((1, 1, 1, modsel.shape[-1]), lambda b, j: (b, jnp.minimum(j, 1), 0, 0)),
            pl.BlockSpec((d, nout), lambda b, j: (0, 0)),
        ],
        out_specs=[
            pl.BlockSpec((1, TB, na), lambda b, j: (b, j, 0)),
            pl.BlockSpec((1, TB, nb), lambda b, j: (b, j, 0)),
            pl.BlockSpec((1, TB, nc), lambda b, j: (b, j, 0)),
            pl.BlockSpec((1, TB, ng), lambda b, j: (b, j, 0)),
        ],
        out_shape=[
            jax.ShapeDtypeStruct((bsz, n, na), F32),
            jax.ShapeDtypeStruct((bsz, n, nb), F32),
            jax.ShapeDtypeStruct((bsz, n, nc), F32),
            jax.ShapeDtypeStruct((bsz, n, ng), F32),
        ],
        compiler_params=_cparams(("parallel", "parallel")),
        name="in_proj",
    )(x, lnw, modsel, w_all)


def _amix_kernel(a_ref, nw_ref, ws_ref, bias_ref, o_ref):
    u = a_ref[0, :, 0:A_W]
    v = a_ref[0, :, A_W:2 * A_W]
    ssq = jnp.dot(v * v, _block_ones(A_W, A_HD, F32), precision=HIGHEST, preferred_element_type=F32)
    vn = v * lax.rsqrt(ssq * (1.0 / A_HD) + EPS) * nw_ref[...]
    lane_head = lax.broadcasted_iota(jnp.int32, vn.shape, 1) // A_HD
    acc = bias_ref[...]
    for h in range(A_HEADS):
        vm = jnp.where(lane_head == h, vn, 0.0).astype(BF16)
        acc = acc + _dot(ws_ref[h], vm)
    o_ref[0] = (u * acc).astype(BF16)


def _amix(oa, nw, ws, bias, first_chunk):
    bsz, n, _ = oa.shape
    nchunk = n // A_CHUNK - first_chunk
    return pl.pallas_call(
        _amix_kernel,
        grid=(bsz, nchunk),
        in_specs=[
            pl.BlockSpec((1, A_CHUNK, 2 * A_W), lambda b, j: (b, j + first_chunk, 0)),
            pl.BlockSpec((1, A_W), lambda b, j: (0, 0)),
            pl.BlockSpec((A_HEADS, A_CHUNK, A_CHUNK), lambda b, j: (0, 0, 0)),
            pl.BlockSpec((A_CHUNK, A_W), lambda b, j: (0, 0)),
        ],
        out_specs=pl.BlockSpec((1, A_CHUNK, A_W), lambda b, j: (b, j + first_chunk, 0)),
        out_shape=jax.ShapeDtypeStruct((bsz, n, A_W), BF16),
        compiler_params=_cparams(("parallel", "parallel")),
        name="mixer_a",
    )(oa, nw, ws, bias)


def _mla_prep_kernel(ob_ref, cos_ref, sin_ref, qnw_ref, kvnw_ref, wq_ref, wqs_ref, wk_ref, wv_ref,
                     hw_ref, q_ref, k_ref, v_ref):
    cq = ob_ref[0, :, 0:B_Q_RANK]
    ckv = ob_ref[0, :, B_Q_RANK:B_Q_RANK + B_KV_RANK]
    krp = ob_ref[0, :, B_Q_RANK + B_KV_RANK:B_Q_RANK + B_KV_RANK + B_HP]
    krs = ob_ref[0, :, B_Q_RANK + B_KV_RANK + B_HP:B_Q_RANK + B_KV_RANK + 2 * B_HP]
    cos = cos_ref[...]
    sin = sin_ref[...]
    cqn = (cq * lax.rsqrt(jnp.mean(cq * cq, axis=-1, keepdims=True) + EPS) * qnw_ref[...]).astype(BF16)
    ckn = (ckv * lax.rsqrt(jnp.mean(ckv * ckv, axis=-1, keepdims=True) + EPS) * kvnw_ref[...]).astype(BF16)
    q_raw = _dot(cqn, wq_ref[...])
    q_swp = _dot(cqn, wqs_ref[...])
    k_raw = _dot(ckn, wk_ref[...])
    v_all = _dot(ckn, wv_ref[...])
    qw, qws, kw, kws = hw_ref[0:1, :], hw_ref[1:2, :], hw_ref[2:3, :], hw_ref[3:4, :]
    k_rot_sw = krs * kws * sin
    for h in range(B_HEADS):
        sl = slice(h * B_HP, (h + 1) * B_HP)
        qh = q_raw[:, sl]
        rq = lax.rsqrt(jnp.sum(qh * qh, axis=-1, keepdims=True) * (1.0 / B_QK) + EPS)
        q_ref[0, h] = (rq * (qh * qw * cos + q_swp[:, sl] * qws * sin)).astype(BF16)
        kh = k_raw[:, sl] + krp
        rk = lax.rsqrt(jnp.sum(kh * kh, axis=-1, keepdims=True) * (1.0 / B_QK) + EPS)
        k_ref[0, h] = (rk * (kh * kw * cos + k_rot_sw)).astype(BF16)
        v_ref[0, h] = v_all[:, h * B_V:(h + 1) * B_V].astype(BF16)


def _mla_prep(ob, cos_t, sin_t, qnw, kvnw, wq, wqs, wk, wv, hw):
    bsz, n, nb = ob.shape
    full = lambda *s: pl.BlockSpec(s, lambda b, j: (0,) * len(s))
    return pl.pallas_call(
        _mla_prep_kernel,
        grid=(bsz, n // TB),
        in_specs=[
            pl.BlockSpec((1, TB, nb), lambda b, j: (b, j, 0)),
            pl.BlockSpec((TB, B_HP), lambda b, j: (j, 0)),
            pl.BlockSpec((TB, B_HP), lambda b, j: (j, 0)),
            full(1, B_Q_RANK), full(1, B_KV_RANK),
            full(B_Q_RANK, B_HEADS * B_HP), full(B_Q_RANK, B_HEADS * B_HP),
            full(B_KV_RANK, B_HEADS * B_HP), full(B_KV_RANK, B_HEADS * B_V),
            full(8, B_HP),
        ],
        out_specs=[
            pl.BlockSpec((1, B_HEADS, TB, B_HP), lambda b, j: (b, 0, j, 0)),
            pl.BlockSpec((1, B_HEADS, TB, B_HP), lambda b, j: (b, 0, j, 0)),
            pl.BlockSpec((1, B_HEADS, TB, B_V), lambda b, j: (b, 0, j, 0)),
        ],
        out_shape=[
            jax.ShapeDtypeStruct((bsz, B_HEADS, n, B_HP), BF16),
            jax.ShapeDtypeStruct((bsz, B_HEADS, n, B_HP), BF16),
            jax.ShapeDtypeStruct((bsz, B_HEADS, n, B_V), BF16),
        ],
        compiler_params=_cparams(("parallel", "parallel")),
        name="mla_prep",
    )(ob, cos_t, sin_t, qnw, kvnw, wq, wqs, wk, wv, hw)


def _attn_kernel(q_ref, k_ref, v_ref, o_ref, o_scr):
    scale = B_QK ** -0.5
    for h in range(B_HEADS):
        s = _dot_nt(q_ref[0, h], k_ref[0, h])
        m = jnp.max(s, axis=-1, keepdims=True)
        p = jnp.exp((s - m) * scale)
        l = jnp.sum(p, axis=-1, keepdims=True)
        o = _dot(p.astype(BF16), v_ref[0, h])
        o_scr[:, h * B_V:(h + 1) * B_V] = o / l
    o_ref[0] = o_scr[...].astype(BF16)


def _attention(q, k, v, first_qblock, n_qblocks, n_keys, out_rows):
    bsz = q.shape[0]
    return pl.pallas_call(
        _attn_kernel,
        grid=(bsz, n_qblocks),
        in_specs=[
            pl.BlockSpec((1, B_HEADS, TB, B_HP), lambda b, j: (b, 0, j + first_qblock, 0)),
            pl.BlockSpec((1, B_HEADS, n_keys, B_HP), lambda b, j: (b, 0, 0, 0)),
            pl.BlockSpec((1, B_HEADS, n_keys, B_V), lambda b, j: (b, 0, 0, 0)),
        ],
        out_specs=pl.BlockSpec((1, TB, B_HEADS * B_V), lambda b, j: (b, j, 0)),
        out_shape=jax.ShapeDtypeStruct((bsz, out_rows, B_HEADS * B_V), BF16),
        scratch_shapes=[pltpu.VMEM((TB, B_HEADS * B_V), F32)],
        compiler_params=_cparams(("parallel", "arbitrary")),
        name="mla_attention",
    )(q, k, v)


def _hgrn_chunk(blk, zcol, lbc, st_ref, rev):
    cc, w = C_CHUNK, C_W
    q = blk[:, 0:w] * (C_DK ** -0.5)
    z = blk[:, zcol * w:(zcol + 1) * w]
    v = blk[:, 3 * w:4 * w]
    log_lb, log1m_lb, one_m_lb = lbc[0:1, :], lbc[1:2, :], lbc[2:3, :]
    az = jnp.abs(z)
    sp = jnp.log1p(jnp.exp(-az))
    lsig = jnp.minimum(z, 0.0) - sp
    t2 = log1m_lb + lsig
    mx = jnp.maximum(log_lb, t2)
    mn = jnp.minimum(log_lb, t2)
    logf = mx + jnp.log1p(jnp.exp(mn - mx))
    kk = one_m_lb * _sigmoid(-z)

    ti = lax.broadcasted_iota(jnp.int32, (cc, cc), 0)
    ui = lax.broadcasted_iota(jnp.int32, (cc, cc), 1)
    tri = ((ui >= ti) if rev else (ui <= ti)).astype(F32)
    b = jnp.dot(tri, logf, precision=HIGHEST, preferred_element_type=F32)
    b_tot = b[0:1, :] if rev else b[cc - 1:cc, :]

    row = lax.broadcasted_iota(jnp.int32, (cc, w), 0)
    lane_head = lax.broadcasted_iota(jnp.int32, (cc, w), 1) // C_DK
    nsub = cc // C_SUB
    row_blk = row // C_SUB

    beta_rows = []
    for i in range(nsub):
        if rev:
            src = None if i == nsub - 1 else b[(i + 1) * C_SUB:(i + 1) * C_SUB + 1, :]
        else:
            src = None if i == 0 else b[i * C_SUB - 1:i * C_SUB, :]
        beta_rows.append(src)
    beta_full = jnp.concatenate(
        [jnp.broadcast_to(b[i * C_SUB:i * C_SUB + 1, :] if r is None else r, (C_SUB, w))
         for i, r in enumerate(beta_rows)], axis=0)
    has_prev = (row_blk < nsub - 1) if rev else (row_blk > 0)
    qs = jnp.where(has_prev, q * jnp.exp(b - beta_full), 0.0)

    q_stack = jnp.concatenate([jnp.where(lane_head == h, qs, 0.0) for h in range(C_HEADS)], axis=0).astype(BF16)
    qblocks = [i for i in range(nsub) if beta_rows[i] is not None]
    ks_parts = []
    for i in qblocks:
        prev = (row_blk > i) if rev else (row_blk < i)
        ks_parts.append(jnp.where(prev, kk * jnp.exp(beta_rows[i] - b), 0.0))
    ks_all = jnp.concatenate(ks_parts, axis=0).astype(BF16)
    a_all = _dot_nt(q_stack, ks_all)
    ar = lax.broadcasted_iota(jnp.int32, a_all.shape, 0)
    ac = lax.broadcasted_iota(jnp.int32, a_all.shape, 1)
    r_blk = (ar % cc) // C_SUB
    c_blk = ac // cc + (0 if rev else 1)
    a_all = jnp.where(r_blk == c_blk, a_all, 0.0).astype(BF16)
    v_bf = v.astype(BF16)
    r_all = _dot(a_all, jnp.concatenate([v_bf] * len(qblocks), axis=0))
    o = jnp.zeros((cc, w), F32)
    for h in range(C_HEADS):
        o = o + jnp.where(lane_head == h, r_all[h * cc:(h + 1) * cc, :], 0.0)

    ones_bd = _block_ones(w, C_DK, BF16)
    tsub = lax.broadcasted_iota(jnp.int32, (C_SUB, w), 0)
    diag_parts = []
    for i in range(nsub):
        r0 = i * C_SUB
        bb = b[r0:r0 + C_SUB, :]
        qq = q[r0:r0 + C_SUB, :]
        ps = []
        for s in range(C_SUB):
            keep = (tsub <= s) if rev else (tsub >= s)
            e = jnp.where(keep, jnp.exp(bb - b[r0 + s:r0 + s + 1, :]), 0.0)
            ps.append(qq * e * kk[r0 + s:r0 + s + 1, :])
        red = _dot(jnp.concatenate(ps, axis=0).astype(BF16), ones_bd)
        od = jnp.zeros((C_SUB, w), F32)
        for s in range(C_SUB):
            od = od + red[s * C_SUB:(s + 1) * C_SUB, :] * v[r0 + s:r0 + s + 1, :]
        diag_parts.append(od)
    o = o + jnp.concatenate(diag_parts, axis=0)

    st = st_ref[...]
    o = o + _dot_nt((q * jnp.exp(b)).astype(BF16), st.astype(BF16))
    kd = (kk * jnp.exp(b_tot - b)).astype(BF16)
    upd = _dot_tn(v_bf, kd)
    st_ref[...] = st * jnp.exp(b_tot) + upd * _block_ones(w, C_DK, F32)
    return o


def _hgrn_kernel(cf_ref, cb_ref, lbc_ref, of_ref, ob_ref, sf_ref, sb_ref):
    @pl.when(pl.program_id(1) == 0)
    def _():
        sf_ref[...] = jnp.zeros_like(sf_ref)
        sb_ref[...] = jnp.zeros_like(sb_ref)

    of_ref[0] = _hgrn_chunk(cf_ref[0], 1, lbc_ref[0], sf_ref, rev=False)
    ob_ref[0] = _hgrn_chunk(cb_ref[0], 2, lbc_ref[1], sb_ref, rev=True)


def _hgrn(oc, lbc, n_ctx):
    bsz, n, wc = oc.shape
    nch = n // C_CHUNK
    nctx = n_ctx // C_CHUNK

    def bwd_idx(c):
        return jnp.where(c < nctx, nctx - 1 - c, nch + nctx - 1 - c)

    return pl.pallas_call(
        _hgrn_kernel,
        grid=(bsz, nch),
        in_specs=[
            pl.BlockSpec((1, C_CHUNK, wc), lambda b, c: (b, c, 0)),
            pl.BlockSpec((1, C_CHUNK, wc), lambda b, c: (b, bwd_idx(c), 0)),
            pl.BlockSpec((2, 8, C_W), lambda b, c: (0, 0, 0)),
        ],
        out_specs=[
            pl.BlockSpec((1, C_CHUNK, C_W), lambda b, c: (b, c, 0)),
            pl.BlockSpec((1, C_CHUNK, C_W), lambda b, c: (b, bwd_idx(c), 0)),
        ],
        out_shape=[jax.ShapeDtypeStruct((bsz, n, C_W), F32)] * 2,
        scratch_shapes=[pltpu.VMEM((C_W, C_W), F32)] * 2,
        compiler_params=_cparams(("parallel", "arbitrary")),
        name="hgrn2_scan",
    )(oc, oc, lbc)


def _outproj_kernel(x_ref, a_ref, b_ref, of_ref, ob_ref, g_ref, cnw_ref, wa_ref, wb_ref, wc_ref,
                    mod_ref, ln2_ref, xo_ref, h2_ref, *, d):
    o = of_ref[0] + ob_ref[0]
    ssq = jnp.dot(o * o, _block_ones(C_W, C_DV, F32), precision=HIGHEST, preferred_element_type=F32)
    g = g_ref[0]
    c_out = o * lax.rsqrt(ssq * (1.0 / C_DV) + EPS) * cnw_ref[...] * (g * _sigmoid(g))
    mix = _dot(a_ref[0], wa_ref[...]) + _dot(b_ref[0], wb_ref[...]) + _dot(c_out.astype(BF16), wc_ref[...])
    gate1 = mod_ref[0, 0, :, 2 * d:3 * d]
    shift2 = mod_ref[0, 0, :, 3 * d:4 * d]
    scale2 = mod_ref[0, 0, :, 4 * d:5 * d]
    x = x_ref[0] + gate1 * mix
    xo_ref[0] = x
    y = x * lax.rsqrt(jnp.mean(x * x, axis=-1, keepdims=True) + EPS) * ln2_ref[...]
    h2_ref[0] = (y * (1.0 + scale2) + shift2).astype(BF16)


def _outproj(x, a_out, b_out, o_f, o_b, og, cnw, wa, wb, wc, modsel, ln2, first_block):
    bsz, n, d = x.shape
    nblk = n // TB - first_block
    full = lambda *s: pl.BlockSpec(s, lambda b, j: (0,) * len(s))
    tok = lambda w: pl.BlockSpec((1, TB, w), lambda b, j: (b, j + first_block, 0))
    return pl.pallas_call(
        functools.partial(_outproj_kernel, d=d),
        grid=(bsz, nblk),
        in_specs=[
            tok(d), tok(A_W),
            pl.BlockSpec((1, TB, B_HEADS * B_V), lambda b, j: (b, j, 0)) if first_block else tok(B_HEADS * B_V),
            tok(C_W), tok(C_W), tok(C_W),
            full(1, C_W), full(A_W, d), full(B_HEADS * B_V, d), full(C_W, d),
            pl.BlockSpec((1, 1, 1, modsel.shape[-1]), lambda b, j: (b, jnp.minimum(j + first_block, 1), 0, 0)),
            full(1, d),
        ],
        out_specs=[
            pl.BlockSpec((1, TB, d), lambda b, j: (b, j, 0)),
            pl.BlockSpec((1, TB, d), lambda b, j: (b, j, 0)),
        ],
        out_shape=[
            jax.ShapeDtypeStruct((bsz, nblk * TB, d), F32),
            jax.ShapeDtypeStruct((bsz, nblk * TB, d), BF16),
        ],
        compiler_params=_cparams(("parallel", "parallel")),
        name="out_proj",
    )(x, a_out, b_out, o_f, o_b, og, cnw, wa, wb, wc, modsel, ln2)


def _top16_exact(s):
    nrows = s.shape[0]
    iota = lax.broadcasted_iota(jnp.int32, s.shape, 0).astype(F32)
    rank = jnp.full(s.shape, float(P_TOPK), F32)
    vals = []
    for r in range(P_TOPK):
        m = jnp.max(s, axis=0, keepdims=True)
        idx = jnp.min(jnp.where(s == m, iota, float(nrows)), axis=0, keepdims=True)
        hit = iota == idx
        rank = jnp.where(hit, float(r), rank)
        s = jnp.where(hit, -jnp.inf, s)
        vals.append(m)
    return jnp.concatenate(vals, axis=0), rank


_MARK0 = int(np.array(0xFF7FFFFF, np.uint32).view(np.int32))


def _top16_marked(s):
    vals = []
    for r in range(P_TOPK):
        m = jnp.max(s, axis=0, keepdims=True)
        mark = float(np.array(_MARK0 - r, np.int32).view(np.float32))
        s = jnp.where(s == m, mark, s)
        vals.append(m)
    rr = _MARK0 - pltpu.bitcast(s, jnp.int32)
    rank = jnp.where(rr < 0, P_TOPK, jnp.where(rr > P_TOPK - 1, P_TOPK, rr)).astype(F32)
    cnt = jnp.sum(jnp.where(rank < float(P_TOPK), 1.0, 0.0), axis=0, keepdims=True)
    return jnp.concatenate(vals, axis=0), rank, cnt


def _bf16_bits(x):
    return lax.shift_right_logical(pltpu.bitcast(x.astype(BF16).astype(F32), jnp.int32), 16)


def _pack_rows(x):
    b = _bf16_bits(x)
    return b | lax.shift_left(b, 16)


def _pack_halves(x):
    b = _bf16_bits(x)
    half = x.shape[0] // 2
    return b[0:half] | lax.shift_left(b[half:], 16)


_PAIR_PERM = np.arange(P_N_KEYS).reshape(2, P_N_KEYS // 2).T.reshape(-1)


def _unpack_row(row):
    n = row.shape[-1]
    tile = pltpu.bitcast(jnp.broadcast_to(row, (8, n)), BF16)
    return jnp.concatenate([tile] * (P_N_KEYS // 16), axis=0)


def _route_kernel(h_ref, wq_ref, sk_ref, r2_ref, e2_ref, n1_ref, e1_ref,
                  q_scr, s_scr, v_scr, r1_scr, e1_scr):
    nl = P_N_KEYS
    q = _dot(h_ref[...], wq_ref[...]).astype(BF16)
    for l in range(2 * P_HEADS):
        q_scr[l] = q[:, l * P_HALF:(l + 1) * P_HALF]

    def lane_tile(lt, carry):
        row0 = pl.multiple_of(lt * nl, nl)

        def put_rank(h, p, rank):
            if p == 0:
                r1_scr[h] = rank
            else:
                r2_ref[lt, h] = _pack_halves(rank)

        def heads(hp, c):
            bad = jnp.zeros((1, nl), F32)
            for u in range(2 * ROUTE_HPT):
                h, p = hp * ROUTE_HPT + u // 2, u % 2
                s = _dot_nt(sk_ref[h, p], q_scr[2 * h + p, pl.ds(row0, nl), :])
                s_scr[u] = s
                v, rank, cnt = _top16_marked(s)
                v_scr[p, h] = v
                put_rank(h, p, rank)
                e = jnp.exp(s - v[0:1, :])
                if p == 0:
                    e1_scr[h] = e
                else:
                    e2_ref[lt, h] = _pack_halves(e)
                bad = bad + jnp.abs(cnt - float(P_TOPK))

            @pl.when(jnp.max(bad) > 0.0)
            def _():
                for u in range(2 * ROUTE_HPT):
                    h, p = hp * ROUTE_HPT + u // 2, u % 2
                    ve, re = _top16_exact(s_scr[u])
                    v_scr[p, h] = ve
                    put_rank(h, p, re)

            return c

        lax.fori_loop(0, P_HEADS // ROUTE_HPT, heads, 0)

        v1 = v_scr[0]
        v2 = v_scr[1]
        ia = lax.broadcasted_iota(jnp.int32, v1.shape, 1).astype(F32)
        n = jnp.zeros(v1.shape, F32)
        g = jnp.broadcast_to(v2[:, 0:1, :], v1.shape)
        cmax = v1[:, 0:1, :] + v2[:, 0:1, :]
        z = jnp.zeros(cmax.shape, F32)
        for _ in range(P_TOPK):
            f = v1 + g
            m = jnp.max(f, axis=1, keepdims=True)
            a_star = jnp.min(jnp.where(f == m, ia, float(P_TOPK)), axis=1, keepdims=True)
            hit = ia == a_star
            n = n + jnp.where(hit, 1.0, 0.0)
            nsel = jnp.sum(jnp.where(hit, n, 0.0), axis=1, keepdims=True)
            nxt = jnp.sum(jnp.where(ia == nsel, v2, 0.0), axis=1, keepdims=True)
            nxt = jnp.where(nsel > P_TOPK - 0.5, -jnp.inf, nxt)
            g = jnp.where(hit, nxt, g)
            z = z + jnp.exp(m - cmax)
        zinv = 1.0 / z
        for h in range(P_HEADS):
            rank1 = r1_scr[h]
            n1 = jnp.zeros(rank1.shape, F32)
            for a in range(P_TOPK):
                n1 = jnp.where(rank1 == float(a), n[h, a:a + 1, :], n1)
            n1_ref[lt, h] = _pack_rows(n1)
            e1_ref[lt, h] = _pack_rows(e1_scr[h] * zinv[h])
        return carry

    lax.fori_loop(0, h_ref.shape[0] // nl, lane_tile, 0)


def _route(h2, wq, sk):
    t, d = h2.shape
    tr = PEER_TT
    nl = P_N_KEYS
    oshape = (t // nl, P_HEADS, P_N_KEYS, nl)
    hshape = (t // nl, P_HEADS, P_N_KEYS // 2, nl)
    ospec = pl.BlockSpec((tr // nl, P_HEADS, P_N_KEYS, nl), lambda i: (i, 0, 0, 0))
    hspec = pl.BlockSpec((tr // nl, P_HEADS, P_N_KEYS // 2, nl), lambda i: (i, 0, 0, 0))
    return pl.pallas_call(
        _route_kernel,
        grid=(t // tr,),
        in_specs=[
            pl.BlockSpec((tr, d), lambda i: (i, 0)),
            pl.BlockSpec((d, P_HEADS * P_KEY_DIM), lambda i: (0, 0)),
            pl.BlockSpec((P_HEADS, 2, P_N_KEYS, P_HALF), lambda i: (0, 0, 0, 0)),
        ],
        out_specs=[hspec, hspec, ospec, ospec],
        out_shape=[jax.ShapeDtypeStruct(hshape, jnp.int32), jax.ShapeDtypeStruct(hshape, jnp.int32),
                   jax.ShapeDtypeStruct(oshape, jnp.int32), jax.ShapeDtypeStruct(oshape, jnp.int32)],
        scratch_shapes=[
            pltpu.VMEM((2 * P_HEADS, tr, P_HALF), BF16),
            pltpu.VMEM((2 * ROUTE_HPT, P_N_KEYS, nl), F32),
            pltpu.VMEM((2, P_HEADS, P_TOPK, nl), F32),
            pltpu.VMEM((P_HEADS, P_N_KEYS, nl), F32),
            pltpu.VMEM((P_HEADS, P_N_KEYS, nl), F32),
        ],
        compiler_params=_cparams(("parallel",)),
        name="peer_route",
    )(h2, wq, sk)


def _peer_kernel(h_ref, u_ref, vt_ref, r2_ref, e2_ref, n1_ref, e1_ref, x_ref, g5_ref, o_ref,
                 acc_ref, w_ref, a_scr, *, blocks_per_batch, ctx_blocks):
    i = pl.program_id(0)
    k = pl.program_id(1)

    @pl.when(k == 0)
    def _():
        acc_ref[...] = jnp.zeros_like(acc_ref)

    nl = P_N_KEYS
    zero = jnp.zeros((P_N_KEYS, nl), BF16)
    ltc = PEER_TC // nl
    njg = PEER_I1 // PEER_JG

    def gate_block(lt, j0):
        g = [zero] * PEER_JG
        for h in range(P_HEADS):
            r2 = pltpu.bitcast(r2_ref[lt, h], BF16)
            e2 = pltpu.bitcast(e2_ref[lt, h], BF16)
            for jj in range(PEER_JG):
                n1row = _unpack_row(n1_ref[lt, h, pl.ds(j0 + jj, 1), :])
                e1row = _unpack_row(e1_ref[lt, h, pl.ds(j0 + jj, 1), :])
                g[jj] = g[jj] + jnp.where(r2 < n1row, e2, zero) * e1row
        for jj in range(PEER_JG):
            rows = pl.ds(pl.multiple_of((j0 + jj) * P_N_KEYS, P_N_KEYS), P_N_KEYS)
            a = a_scr[lt, rows, :]
            act = 0.5 * a * (1.0 + lax.erf(a * (2.0 ** -0.5)))
            w_ref[lt, rows, :] = act.astype(BF16) * g[jj]

    for c in range(PEER_TT // PEER_TC):
        tok = slice(c * PEER_TC, (c + 1) * PEER_TC)
        a_t = _dot_nt(u_ref[...], h_ref[tok, :])
        for lc in range(ltc):
            a_scr[c * ltc + lc] = a_t[:, lc * nl:(lc + 1) * nl]

        def body(it, carry, c=c):
            gate_block(c * ltc + it // njg, (it % njg) * PEER_JG)
            return carry

        lax.fori_loop(0, ltc * njg, body, 0)
        w_t = jnp.concatenate([w_ref[c * ltc + lc] for lc in range(ltc)], axis=1)
        acc_ref[:, tok] += _dot(vt_ref[...], w_t)

    @pl.when(k == pl.num_programs(1) - 1)
    def _():
        y = acc_ref[...].T
        for u in range(PEER_TT // TB):
            sblk = i * (PEER_TT // TB) + u
            bidx = sblk // blocks_per_batch
            is_lat = (sblk - bidx * blocks_per_batch) >= ctx_blocks
            gate = g5_ref[2 * bidx + is_lat.astype(jnp.int32)]
            rs = slice(u * TB, (u + 1) * TB)
            o_ref[rs, :] = x_ref[rs, :] + gate * y[rs, :]


def _peer(h2, u_bf, vt_bf, r2, e2, n1, e1, x, g5, blocks_per_batch, ctx_blocks):
    t, d = h2.shape
    ne = u_bf.shape[0]
    nl = P_N_KEYS
    rspec = pl.BlockSpec((PEER_TT // nl, P_HEADS, P_N_KEYS // 2, nl), lambda i, k: (i, 0, 0, 0))
    nspec = pl.BlockSpec((PEER_TT // nl, P_HEADS, PEER_I1, nl), lambda i, k: (i, 0, k, 0))
    return pl.pallas_call(
        functools.partial(_peer_kernel, blocks_per_batch=blocks_per_batch, ctx_blocks=ctx_blocks),
        grid=(t // PEER_TT, ne // PEER_ET),
        in_specs=[
            pl.BlockSpec((PEER_TT, d), lambda i, k: (i, 0)),
            pl.BlockSpec((PEER_ET, d), lambda i, k: (k, 0)),
            pl.BlockSpec((d, PEER_ET), lambda i, k: (0, k)),
            rspec, rspec, nspec, nspec,
            pl.BlockSpec((PEER_TT, d), lambda i, k: (i, 0)),
            pl.BlockSpec(g5.shape, lambda i, k: (0, 0, 0)),
        ],
        out_specs=pl.BlockSpec((PEER_TT, d), lambda i, k: (i, 0)),
        out_shape=jax.ShapeDtypeStruct((t, d), F32),
        scratch_shapes=[pltpu.VMEM((d, PEER_TT), F32),
                        pltpu.VMEM((PEER_TT // nl, PEER_ET, nl), BF16),
                        pltpu.VMEM((PEER_TT // nl, PEER_ET, nl), F32)],
        compiler_params=_cparams(("parallel", "arbitrary")),
        name="peer_experts",
    )(h2, u_bf, vt_bf, r2, e2, n1, e1, x, g5)


def _transpose_cast_kernel(x_ref, o_ref):
    o_ref[...] = x_ref[...].T.astype(BF16)


def _transpose_cast(v):
    ne, d = v.shape
    te = 512
    return pl.pallas_call(
        _transpose_cast_kernel,
        grid=(ne // te,),
        in_specs=[pl.BlockSpec((te, d), lambda e: (e, 0))],
        out_specs=pl.BlockSpec((d, te), lambda e: (0, e)),
        out_shape=jax.ShapeDtypeStruct((d, ne), BF16),
        compiler_params=_cparams(("parallel",)),
        name="expert_value_transpose",
    )(v)


def _rope_tables(n_ctx, n_lat):
    n_freq = B_ROPE // 4
    pos = np.arange(n_lat)
    inv_freq = ROPE_THETA ** (-np.arange(n_freq, dtype=np.float32) / n_freq)
    inv_freq = jnp.asarray(inv_freq, F32)
    rowp = jnp.asarray(pos // GRID_W, F32)
    colp = jnp.asarray(pos % GRID_W, F32)
    ang = jnp.stack([rowp[:, None] * inv_freq, colp[:, None] * inv_freq], axis=1)
    cos, sin = jnp.cos(ang), jnp.sin(ang)
    cos32 = jnp.concatenate([cos, cos], axis=2).reshape(n_lat, B_ROPE)
    sin32 = jnp.concatenate([-sin, sin], axis=2).reshape(n_lat, B_ROPE)
    pad_l = jnp.ones((n_lat, B_NOPE), F32)
    pad_r = jnp.ones((n_lat, B_HP - B_QK), F32)
    cos_l = jnp.concatenate([pad_l, cos32, pad_r], axis=1)
    sin_l = jnp.concatenate([0 * pad_l, sin32, 0 * pad_r], axis=1)
    cos_t = jnp.concatenate([jnp.ones((n_ctx, B_HP), F32), cos_l], axis=0)
    sin_t = jnp.concatenate([jnp.zeros((n_ctx, B_HP), F32), sin_l], axis=0)
    return cos_t, sin_t


_SWAP32 = np.arange(B_ROPE) ^ (B_ROPE // 4)


def _pad_head(nope, rope):
    z = jnp.zeros(nope.shape[:-1] + (B_HP - B_QK,), nope.dtype)
    out = jnp.concatenate([nope, rope, z], axis=-1)
    return out.reshape(out.shape[:-2] + (out.shape[-2] * B_HP,))


def _layer_weights(layer, w_in, w_out, a_norm_w, a_w_s, a_b_s, b_q_norm_w, b_w_uq, b_kv_norm_w, b_w_ukv,
                   b_q_head_norm_w, b_k_head_norm_w, c_out_norm_w, p_w_q, p_sub_keys):
    d = w_in.shape[1]
    wi = w_in[layer]
    offs = np.cumsum([0, A_W, A_W, B_Q_RANK, B_KV_RANK, B_ROPE, C_W, C_W, C_W, C_W, C_W])
    col = lambda i: wi[:, offs[i]:offs[i + 1]]
    w_kr = col(4)
    zl = jnp.zeros((d, B_NOPE), F32)
    zr = jnp.zeros((d, B_HP - B_QK), F32)
    kr_placed = jnp.concatenate([zl, w_kr, zr], axis=1)
    kr_swapped = jnp.concatenate([zl, w_kr[:, _SWAP32], zr], axis=1)
    w_all = jnp.concatenate(
        [col(0), col(1), col(2), col(3), kr_placed, kr_swapped, col(5), col(6), col(7), col(8), col(9)],
        axis=1).astype(BF16)
    widths = (2 * A_W, B_Q_RANK + B_KV_RANK + 2 * B_HP, 4 * C_W, C_W)

    wuq = b_w_uq[layer].reshape(B_Q_RANK, B_HEADS, B_QK)
    wq_p = _pad_head(wuq[..., :B_NOPE], wuq[..., B_NOPE:]).astype(BF16)
    wq_s = _pad_head(0 * wuq[..., :B_NOPE], wuq[..., B_NOPE:][..., _SWAP32]).astype(BF16)
    wukv = b_w_ukv[layer].reshape(B_KV_RANK, B_HEADS, B_NOPE + B_V)
    wk_p = _pad_head(wukv[..., :B_NOPE], jnp.zeros((B_KV_RANK, B_HEADS, B_ROPE), F32)).astype(BF16)
    wv = wukv[..., B_NOPE:].reshape(B_KV_RANK, B_HEADS * B_V).astype(BF16)
    qn, kn = b_q_head_norm_w[layer], b_k_head_norm_w[layer]
    zpad = jnp.zeros((B_HP - B_QK,), F32)
    z64 = jnp.zeros((B_NOPE,), F32)
    hw = jnp.stack([
        jnp.concatenate([qn, zpad]),
        jnp.concatenate([z64, qn[B_NOPE:][_SWAP32], zpad]),
        jnp.concatenate([kn, zpad]),
        jnp.concatenate([z64, kn[B_NOPE:][_SWAP32], zpad]),
    ] + [jnp.zeros((B_HP,), F32)] * 4, axis=0)

    wo = w_out[layer].astype(BF16)
    return dict(
        w_all=w_all, widths=widths,
        a_nw=a_norm_w[layer].reshape(1, A_W),
        a_ws=a_w_s[layer].astype(BF16),
        a_bias=jnp.repeat(a_b_s[layer].T, A_HD, axis=1),
        qnw=b_q_norm_w[layer].reshape(1, B_Q_RANK), kvnw=b_kv_norm_w[layer].reshape(1, B_KV_RANK),
        wq_p=wq_p, wq_s=wq_s, wk_p=wk_p, wv=wv, hw=hw,
        cnw=jnp.tile(c_out_norm_w[layer], C_HEADS).reshape(1, C_W),
        wo_a=wo[:A_W], wo_b=wo[A_W:A_W + B_HEADS * B_V], wo_c=wo[A_W + B_HEADS * B_V:],
        p_wq=p_w_q[layer].astype(BF16), p_sk=p_sub_keys[layer].astype(BF16),
    )


def kernel(x, c, ctx, c_ctx, ln1_w, ln2_w, w_mod, b_mod, w_in, w_out, a_norm_w, a_w_s, a_b_s, b_q_norm_w,
           b_w_uq, b_kv_norm_w, b_w_ukv, b_q_head_norm_w, b_k_head_norm_w, c_lb_logits, c_out_norm_w,
           p_w_q, p_sub_keys, p_u, p_v):
    bsz, n_lat, d = x.shape
    n_ctx = ctx.shape[1]
    depth = w_in.shape[0]
    n = n_ctx + n_lat
    ctx_blocks = n_ctx // TB
    blocks = n // TB

    mrows = -(-(bsz + 1) // 8) * 8
    cvec = jnp.concatenate([c, c_ctx[None, :], jnp.zeros((mrows - bsz - 1, d), F32)], axis=0)
    mod_all = _modulation(cvec, w_mod, b_mod)

    lb = jnp.cumsum(jax.nn.softmax(c_lb_logits.astype(F32), axis=0), axis=0)
    lb = lb - lb[0:1]
    lbc_all = jnp.stack([jnp.log(lb), jnp.log1p(-lb), 1.0 - lb] + [jnp.zeros_like(lb)] * 5, axis=2)

    cos_t, sin_t = _rope_tables(n_ctx, n_lat)
    xc = jnp.concatenate([ctx, x], axis=1)

    for layer in range(depth):
        last = layer == depth - 1
        w = _layer_weights(layer, w_in, w_out, a_norm_w, a_w_s, a_b_s, b_q_norm_w, b_w_uq, b_kv_norm_w,
                           b_w_ukv, b_q_head_norm_w, b_k_head_norm_w, c_out_norm_w, p_w_q, p_sub_keys)
        mod_b = mod_all[layer, :bsz]
        mod_c = jnp.broadcast_to(mod_all[layer, bsz][None, :], mod_b.shape)
        modsel = jnp.stack([mod_c, mod_b], axis=1)[:, :, None, :]
        first_block = ctx_blocks if last else 0

        oa, ob, oc, og = _inproj(xc, ln1_w[layer].reshape(1, d), modsel, w["w_all"], w["widths"])
        a_out = _amix(oa, w["a_nw"], w["a_ws"], w["a_bias"], first_block * (TB // A_CHUNK))
        q, k, v = _mla_prep(ob, cos_t, sin_t, w["qnw"], w["kvnw"], w["wq_p"], w["wq_s"], w["wk_p"], w["wv"], w["hw"])
        b_lat = _attention(q, k, v, ctx_blocks, blocks - ctx_blocks, n, n_lat)
        if last:
            b_out = b_lat
        else:
            b_ctx = _attention(q, k, v, 0, ctx_blocks, n_ctx, n_ctx)
            b_out = jnp.concatenate([b_ctx, b_lat], axis=1)
        o_f, o_b = _hgrn(oc, lbc_all[layer], n_ctx)
        x_new, h2 = _outproj(xc, a_out, b_out, o_f, o_b, og, w["cnw"], w["wo_a"], w["wo_b"], w["wo_c"],
                             modsel, ln2_w[layer].reshape(1, d), first_block)

        t = x_new.shape[0] * x_new.shape[1]
        h2f = h2.reshape(t, d)
        r2, e2, n1, e1 = _route(h2f, w["p_wq"], w["p_sk"])
        g5 = modsel[:, :, :, 5 * d:6 * d].reshape(2 * bsz, 1, d)
        perm = lambda tab: tab.reshape(P_N_KEYS, P_N_KEYS, d)[:, _PAIR_PERM, :].reshape(P_N_KEYS * P_N_KEYS, d)
        u_bf = perm(p_u[layer]).astype(BF16)
        vt_bf = _transpose_cast(perm(p_v[layer]))
        out = _peer(h2f, u_bf, vt_bf, r2, e2, n1, e1, x_new.reshape(t, d), g5,
                    blocks - first_block, ctx_blocks - first_block)
        xc = out.reshape(bsz, t // bsz, d)
    return xc
```

```python
import functools
import math

import jax
import jax.numpy as jnp
import numpy as np
from jax import lax
from jax.experimental import pallas as pl
from jax.experimental.pallas import tpu as pltpu

F32 = jnp.float32
BF16 = jnp.bfloat16
HIGHEST = lax.Precision.HIGHEST

EPS = 1e-6
GRID_W = 64
ROPE_THETA = 10000.0

A_HEADS, A_HD, A_CHUNK = 4, 64, 128
A_W = A_HEADS * A_HD
B_HEADS, B_NOPE, B_ROPE, B_V = 8, 64, 32, 64
B_QK = B_NOPE + B_ROPE
B_HP = 128
B_Q_RANK, B_KV_RANK = 256, 128
C_HEADS, C_DK, C_DV, C_CHUNK = 4, 64, 64, 64
C_W = C_HEADS * C_DK
C_SUB = 16
P_HEADS, P_KEY_DIM, P_N_KEYS, P_TOPK = 8, 256, 128, 16
P_HALF = P_KEY_DIM // 2

TB = 256
PEER_TT = 1024
PEER_TC = 512
PEER_ET = 1024
PEER_I1 = PEER_ET // P_N_KEYS
PEER_JG = 4
ROUTE_HPT = 2
VMEM_LIMIT = 56 * 1024 * 1024


def _cparams(sem, flags=None):
    return pltpu.CompilerParams(dimension_semantics=sem, vmem_limit_bytes=VMEM_LIMIT, flags=flags)


def _dot_nt(a, b):
    return lax.dot_general(a, b, (((1,), (1,)), ((), ())), preferred_element_type=F32)


def _dot_tn(a, b):
    return lax.dot_general(a, b, (((0,), (0,)), ((), ())), preferred_element_type=F32)


def _dot(a, b):
    return jnp.dot(a, b, preferred_element_type=F32)


def _sigmoid(x):
    return 1.0 / (1.0 + jnp.exp(-x))


def _block_ones(n, blk, dtype):
    r = lax.broadcasted_iota(jnp.int32, (n, n), 0) // blk
    c = lax.broadcasted_iota(jnp.int32, (n, n), 1) // blk
    return (r == c).astype(dtype)


def _mod_kernel(c_ref, w_ref, b_ref, o_ref):
    c = c_ref[...]
    sc = c * _sigmoid(c)
    o_ref[0] = _dot(sc.astype(BF16), w_ref[0].astype(BF16)) + b_ref[0]


def _modulation(cvec, w_mod, b_mod):
    depth, d, n6 = w_mod.shape
    rows = cvec.shape[0]
    tn = 1024
    return pl.pallas_call(
        _mod_kernel,
        grid=(depth, n6 // tn),
        in_specs=[
            pl.BlockSpec((rows, d), lambda l, n: (0, 0)),
            pl.BlockSpec((1, d, tn), lambda l, n: (l, 0, n)),
            pl.BlockSpec((1, 1, tn), lambda l, n: (l, 0, n)),
        ],
        out_specs=pl.BlockSpec((1, rows, tn), lambda l, n: (l, 0, n)),
        out_shape=jax.ShapeDtypeStruct((depth, rows, n6), F32),
        compiler_params=_cparams(("parallel", "parallel")),
        name="adaln_mod",
    )(cvec, w_mod, b_mod.reshape(depth, 1, n6))


def _inproj_kernel(x_ref, lnw_ref, mod_ref, w_ref, oa_ref, ob_ref, oc_ref, og_ref, *, d):
    x = x_ref[0]
    ms = jnp.mean(x * x, axis=-1, keepdims=True)
    y = x * lax.rsqrt(ms + EPS) * lnw_ref[...]
    shift = mod_ref[0, 0, :, 0:d]
    scale = mod_ref[0, 0, :, d:2 * d]
    h = (y * (1.0 + scale) + shift).astype(BF16)
    p = _dot(h, w_ref[...])
    na = oa_ref.shape[-1]
    nb = ob_ref.shape[-1]
    nc = oc_ref.shape[-1]
    oa_ref[0] = p[:, 0:na]
    ob_ref[0] = p[:, na:na + nb]
    oc_ref[0] = p[:, na + nb:na + nb + nc]
    og_ref[0] = p[:, na + nb + nc:]


def _inproj(x, lnw, modsel, w_all, widths):
    bsz, n, d = x.shape
    na, nb, nc, ng = widths
    nout = w_all.shape[1]
    return pl.pallas_call(
        functools.partial(_inproj_kernel, d=d),
        grid=(bsz, n // TB),
        in_specs=[
            pl.BlockSpec((1, TB, d), lambda b, j: (b, j, 0)),
            pl.BlockSpec((1, d), lambda b, j: (0, 0)),
            pl.BlockSpec((1, 1, 1, modsel.shape[-1]), lambda b, j: (b, jnp.minimum(j, 1), 0, 0)),
            pl.BlockSpec((d, nout), lambda b, j: (0, 0)),
        ],
        out_specs=[
            pl.BlockSpec((1, TB, na), lambda b, j: (b, j, 0)),
            pl.BlockSpec((1, TB, nb), lambda b, j: (b, j, 0)),
            pl.BlockSpec((1, TB, nc), lambda b, j: (b, j, 0)),
            pl.BlockSpec((1, TB, ng), lambda b, j: (b, j, 0)),
        ],
        out_shape=[
            jax.ShapeDtypeStruct((bsz, n, na), F32),
            jax.ShapeDtypeStruct((bsz, n, nb), F32),
            jax.ShapeDtypeStruct((bsz, n, nc), F32),
            jax.ShapeDtypeStruct((bsz, n, ng), F32),
        ],
        compiler_params=_cparams(("parallel", "parallel")),
        name="in_proj",
    )(x, lnw, modsel, w_all)


def _amix_kernel(a_ref, nw_ref, ws_ref, bias_ref, o_ref):
    u = a_ref[0, :, 0:A_W]
    v = a_ref[0, :, A_W:2 * A_W]
    ssq = jnp.dot(v * v, _block_ones(A_W, A_HD, F32), precision=HIGHEST, preferred_element_type=F32)
    vn = v * lax.rsqrt(ssq * (1.0 / A_HD) + EPS) * nw_ref[...]
    lane_head = lax.broadcasted_iota(jnp.int32, vn.shape, 1) // A_HD
    acc = bias_ref[...]
    for h in range(A_HEADS):
        vm = jnp.where(lane_head == h, vn, 0.0).astype(BF16)
        acc = acc + _dot(ws_ref[h], vm)
    o_ref[0] = (u * acc).astype(BF16)


def _amix(oa, nw, ws, bias, first_chunk):
    bsz, n, _ = oa.shape
    nchunk = n // A_CHUNK - first_chunk
    return pl.pallas_call(
        _amix_kernel,
        grid=(bsz, nchunk),
        in_specs=[
            pl.BlockSpec((1, A_CHUNK, 2 * A_W), lambda b, j: (b, j + first_chunk, 0)),
            pl.BlockSpec((1, A_W), lambda b, j: (0, 0)),
            pl.BlockSpec((A_HEADS, A_CHUNK, A_CHUNK), lambda b, j: (0, 0, 0)),
            pl.BlockSpec((A_CHUNK, A_W), lambda b, j: (0, 0)),
        ],
        out_specs=pl.BlockSpec((1, A_CHUNK, A_W), lambda b, j: (b, j + first_chunk, 0)),
        out_shape=jax.ShapeDtypeStruct((bsz, n, A_W), BF16),
        compiler_params=_cparams(("parallel", "parallel")),
        name="mixer_a",
    )(oa, nw, ws, bias)


def _mla_prep_kernel(ob_ref, cos_ref, sin_ref, qnw_ref, kvnw_ref, wq_ref, wqs_ref, wk_ref, wv_ref,
                     hw_ref, q_ref, k_ref, v_ref):
    cq = ob_ref[0, :, 0:B_Q_RANK]
    ckv = ob_ref[0, :, B_Q_RANK:B_Q_RANK + B_KV_RANK]
    krp = ob_ref[0, :, B_Q_RANK + B_KV_RANK:B_Q_RANK + B_KV_RANK + B_HP]
    krs = ob_ref[0, :, B_Q_RANK + B_KV_RANK + B_HP:B_Q_RANK + B_KV_RANK + 2 * B_HP]
    cos = cos_ref[...]
    sin = sin_ref[...]
    cqn = (cq * lax.rsqrt(jnp.mean(cq * cq, axis=-1, keepdims=True) + EPS) * qnw_ref[...]).astype(BF16)
    ckn = (ckv * lax.rsqrt(jnp.mean(ckv * ckv, axis=-1, keepdims=True) + EPS) * kvnw_ref[...]).astype(BF16)
    q_raw = _dot(cqn, wq_ref[...])
    q_swp = _dot(cqn, wqs_ref[...])
    k_raw = _dot(ckn, wk_ref[...])
    v_all = _dot(ckn, wv_ref[...])
    qw, qws, kw, kws = hw_ref[0:1, :], hw_ref[1:2, :], hw_ref[2:3, :], hw_ref[3:4, :]
    k_rot_sw = krs * kws * sin
    for h in range(B_HEADS):
        sl = slice(h * B_HP, (h + 1) * B_HP)
        qh = q_raw[:, sl]
        rq = lax.rsqrt(jnp.sum(qh * qh, axis=-1, keepdims=True) * (1.0 / B_QK) + EPS)
        q_ref[0, h] = (rq * (qh * qw * cos + q_swp[:, sl] * qws * sin)).astype(BF16)
        kh = k_raw[:, sl] + krp
        rk = lax.rsqrt(jnp.sum(kh * kh, axis=-1, keepdims=True) * (1.0 / B_QK) + EPS)
        k_ref[0, h] = (rk * (kh * kw * cos + k_rot_sw)).astype(BF16)
        v_ref[0, h] = v_all[:, h * B_V:(h + 1) * B_V].astype(BF16)


def _mla_prep(ob, cos_t, sin_t, qnw, kvnw, wq, wqs, wk, wv, hw):
    bsz, n, nb = ob.shape
    full = lambda *s: pl.BlockSpec(s, lambda b, j: (0,) * len(s))
    return pl.pallas_call(
        _mla_prep_kernel,
        grid=(bsz, n // TB),
        in_specs=[
            pl.BlockSpec((1, TB, nb), lambda b, j: (b, j, 0)),
            pl.BlockSpec((TB, B_HP), lambda b, j: (j, 0)),
            pl.BlockSpec((TB, B_HP), lambda b, j: (j, 0)),
            full(1, B_Q_RANK), full(1, B_KV_RANK),
            full(B_Q_RANK, B_HEADS * B_HP), full(B_Q_RANK, B_HEADS * B_HP),
            full(B_KV_RANK, B_HEADS * B_HP), full(B_KV_RANK, B_HEADS * B_V),
            full(8, B_HP),
        ],
        out_specs=[
            pl.BlockSpec((1, B_HEADS, TB, B_HP), lambda b, j: (b, 0, j, 0)),
            pl.BlockSpec((1, B_HEADS, TB, B_HP), lambda b, j: (b, 0, j, 0)),
            pl.BlockSpec((1, B_HEADS, TB, B_V), lambda b, j: (b, 0, j, 0)),
        ],
        out_shape=[
            jax.ShapeDtypeStruct((bsz, B_HEADS, n, B_HP), BF16),
            jax.ShapeDtypeStruct((bsz, B_HEADS, n, B_HP), BF16),
            jax.ShapeDtypeStruct((bsz, B_HEADS, n, B_V), BF16),
        ],
        compiler_params=_cparams(("parallel", "parallel")),
        name="mla_prep",
    )(ob, cos_t, sin_t, qnw, kvnw, wq, wqs, wk, wv, hw)


def _attn_kernel(q_ref, k_ref, v_ref, o_ref, o_scr):
    scale = B_QK ** -0.5
    for h in range(B_HEADS):
        s = _dot_nt(q_ref[0, h], k_ref[0, h])
        m = jnp.max(s, axis=-1, keepdims=True)
        p = jnp.exp((s - m) * scale)
        l = jnp.sum(p, axis=-1, keepdims=True)
        o = _dot(p.astype(BF16), v_ref[0, h])
        o_scr[:, h * B_V:(h + 1) * B_V] = o / l
    o_ref[0] = o_scr[...].astype(BF16)


def _attention(q, k, v, first_qblock, n_qblocks, n_keys, out_rows):
    bsz = q.shape[0]
    return pl.pallas_call(
        _attn_kernel,
        grid=(bsz, n_qblocks),
        in_specs=[
            pl.BlockSpec((1, B_HEADS, TB, B_HP), lambda b, j: (b, 0, j + first_qblock, 0)),
            pl.BlockSpec((1, B_HEADS, n_keys, B_HP), lambda b, j: (b, 0, 0, 0)),
            pl.BlockSpec((1, B_HEADS, n_keys, B_V), lambda b, j: (b, 0, 0, 0)),
        ],
        out_specs=pl.BlockSpec((1, TB, B_HEADS * B_V), lambda b, j: (b, j, 0)),
        out_shape=jax.ShapeDtypeStruct((bsz, out_rows, B_HEADS * B_V), BF16),
        scratch_shapes=[pltpu.VMEM((TB, B_HEADS * B_V), F32)],
        compiler_params=_cparams(("parallel", "arbitrary")),
        name="mla_attention",
    )(q, k, v)


def _hgrn_chunk(blk, zcol, lbc, st_ref, rev):
    cc, w = C_CHUNK, C_W
    q = blk[:, 0:w] * (C_DK ** -0.5)
    z = blk[:, zcol * w:(zcol + 1) * w]
    v = blk[:, 3 * w:4 * w]
    log_lb, log1m_lb, one_m_lb = lbc[0:1, :], lbc[1:2, :], lbc[2:3, :]
    az = jnp.abs(z)
    sp = jnp.log1p(jnp.exp(-az))
    lsig = jnp.minimum(z, 0.0) - sp
    t2 = log1m_lb + lsig
    mx = jnp.maximum(log_lb, t2)
    mn = jnp.minimum(log_lb, t2)
    logf = mx + jnp.log1p(jnp.exp(mn - mx))
    kk = one_m_lb * _sigmoid(-z)

    ti = lax.broadcasted_iota(jnp.int32, (cc, cc), 0)
    ui = lax.broadcasted_iota(jnp.int32, (cc, cc), 1)
    tri = ((ui >= ti) if rev else (ui <= ti)).astype(F32)
    b = jnp.dot(tri, logf, precision=HIGHEST, preferred_element_type=F32)
    b_tot = b[0:1, :] if rev else b[cc - 1:cc, :]

    row = lax.broadcasted_iota(jnp.int32, (cc, w), 0)
    lane_head = lax.broadcasted_iota(jnp.int32, (cc, w), 1) // C_DK
    nsub = cc // C_SUB
    row_blk = row // C_SUB

    beta_rows = []
    for i in range(nsub):
        if rev:
            src = None if i == nsub - 1 else b[(i + 1) * C_SUB:(i + 1) * C_SUB + 1, :]
        else:
            src = None if i == 0 else b[i * C_SUB - 1:i * C_SUB, :]
        beta_rows.append(src)
    beta_full = jnp.concatenate(
        [jnp.broadcast_to(b[i * C_SUB:i * C_SUB + 1, :] if r is None else r, (C_SUB, w))
         for i, r in enumerate(beta_rows)], axis=0)
    has_prev = (row_blk < nsub - 1) if rev else (row_blk > 0)
    qs = jnp.where(has_prev, q * jnp.exp(b - beta_full), 0.0)

    q_stack = jnp.concatenate([jnp.where(lane_head == h, qs, 0.0) for h in range(C_HEADS)], axis=0).astype(BF16)
    qblocks = [i for i in range(nsub) if beta_rows[i] is not None]
    ks_parts = []
    for i in qblocks:
        prev = (row_blk > i) if rev else (row_blk < i)
        ks_parts.append(jnp.where(prev, kk * jnp.exp(beta_rows[i] - b), 0.0))
    ks_all = jnp.concatenate(ks_parts, axis=0).astype(BF16)
    a_all = _dot_nt(q_stack, ks_all)
    ar = lax.broadcasted_iota(jnp.int32, a_all.shape, 0)
    ac = lax.broadcasted_iota(jnp.int32, a_all.shape, 1)
    r_blk = (ar % cc) // C_SUB
    c_blk = ac // cc + (0 if rev else 1)
    a_all = jnp.where(r_blk == c_blk, a_all, 0.0).astype(BF16)
    v_bf = v.astype(BF16)
    r_all = _dot(a_all, jnp.concatenate([v_bf] * len(qblocks), axis=0))
    o = jnp.zeros((cc, w), F32)
    for h in range(C_HEADS):
        o = o + jnp.where(lane_head == h, r_all[h * cc:(h + 1) * cc, :], 0.0)

    ones_bd = _block_ones(w, C_DK, BF16)
    tsub = lax.broadcasted_iota(jnp.int32, (C_SUB, w), 0)
    diag_parts = []
    for i in range(nsub):
        r0 = i * C_SUB
        bb = b[r0:r0 + C_SUB, :]
        qq = q[r0:r0 + C_SUB, :]
        ps = []
        for s in range(C_SUB):
            keep = (tsub <= s) if rev else (tsub >= s)
            e = jnp.where(keep, jnp.exp(bb - b[r0 + s:r0 + s + 1, :]), 0.0)
            ps.append(qq * e * kk[r0 + s:r0 + s + 1, :])
        red = _dot(jnp.concatenate(ps, axis=0).astype(BF16), ones_bd)
        od = jnp.zeros((C_SUB, w), F32)
        for s in range(C_SUB):
            od = od + red[s * C_SUB:(s + 1) * C_SUB, :] * v[r0 + s:r0 + s + 1, :]
        diag_parts.append(od)
    o = o + jnp.concatenate(diag_parts, axis=0)

    st = st_ref[...]
    o = o + _dot_nt((q * jnp.exp(b)).astype(BF16), st.astype(BF16))
    kd = (kk * jnp.exp(b_tot - b)).astype(BF16)
    upd = _dot_tn(v_bf, kd)
    st_ref[...] = st * jnp.exp(b_tot) + upd * _block_ones(w, C_DK, F32)
    return o


def _hgrn_kernel(cf_ref, cb_ref, lbc_ref, of_ref, ob_ref, sf_ref, sb_ref):
    @pl.when(pl.program_id(1) == 0)
    def _():
        sf_ref[...] = jnp.zeros_like(sf_ref)
        sb_ref[...] = jnp.zeros_like(sb_ref)

    of_ref[0] = _hgrn_chunk(cf_ref[0], 1, lbc_ref[0], sf_ref, rev=False)
    ob_ref[0] = _hgrn_chunk(cb_ref[0], 2, lbc_ref[1], sb_ref, rev=True)


def _hgrn(oc, lbc, n_ctx):
    bsz, n, wc = oc.shape
    nch = n // C_CHUNK
    nctx = n_ctx // C_CHUNK

    def bwd_idx(c):
        return jnp.where(c < nctx, nctx - 1 - c, nch + nctx - 1 - c)

    return pl.pallas_call(
        _hgrn_kernel,
        grid=(bsz, nch),
        in_specs=[
            pl.BlockSpec((1, C_CHUNK, wc), lambda b, c: (b, c, 0)),
            pl.BlockSpec((1, C_CHUNK, wc), lambda b, c: (b, bwd_idx(c), 0)),
            pl.BlockSpec((2, 8, C_W), lambda b, c: (0, 0, 0)),
        ],
        out_specs=[
            pl.BlockSpec((1, C_CHUNK, C_W), lambda b, c: (b, c, 0)),
            pl.BlockSpec((1, C_CHUNK, C_W), lambda b, c: (b, bwd_idx(c), 0)),
        ],
        out_shape=[jax.ShapeDtypeStruct((bsz, n, C_W), F32)] * 2,
        scratch_shapes=[pltpu.VMEM((C_W, C_W), F32)] * 2,
        compiler_params=_cparams(("parallel", "arbitrary")),
        name="hgrn2_scan",
    )(oc, oc, lbc)


def _outproj_kernel(x_ref, a_ref, b_ref, of_ref, ob_ref, g_ref, cnw_ref, wa_ref, wb_ref, wc_ref,
                    mod_ref, ln2_ref, xo_ref, h2_ref, *, d):
    o = of_ref[0] + ob_ref[0]
    ssq = jnp.dot(o * o, _block_ones(C_W, C_DV, F32), precision=HIGHEST, preferred_element_type=F32)
    g = g_ref[0]
    c_out = o * lax.rsqrt(ssq * (1.0 / C_DV) + EPS) * cnw_ref[...] * (g * _sigmoid(g))
    mix = _dot(a_ref[0], wa_ref[...]) + _dot(b_ref[0], wb_ref[...]) + _dot(c_out.astype(BF16), wc_ref[...])
    gate1 = mod_ref[0, 0, :, 2 * d:3 * d]
    shift2 = mod_ref[0, 0, :, 3 * d:4 * d]
    scale2 = mod_ref[0, 0, :, 4 * d:5 * d]
    x = x_ref[0] + gate1 * mix
    xo_ref[0] = x
    y = x * lax.rsqrt(jnp.mean(x * x, axis=-1, keepdims=True) + EPS) * ln2_ref[...]
    h2_ref[0] = (y * (1.0 + scale2) + shift2).astype(BF16)


def _outproj(x, a_out, b_out, o_f, o_b, og, cnw, wa, wb, wc, modsel, ln2, first_block):
    bsz, n, d = x.shape
    nblk = n // TB - first_block
    full = lambda *s: pl.BlockSpec(s, lambda b, j: (0,) * len(s))
    tok = lambda w: pl.BlockSpec((1, TB, w), lambda b, j: (b, j + first_block, 0))
    return pl.pallas_call(
        functools.partial(_outproj_kernel, d=d),
        grid=(bsz, nblk),
        in_specs=[
            tok(d), tok(A_W),
            pl.BlockSpec((1, TB, B_HEADS * B_V), lambda b, j: (b, j, 0)) if first_block else tok(B_HEADS * B_V),
            tok(C_W), tok(C_W), tok(C_W),
            full(1, C_W), full(A_W, d), full(B_HEADS * B_V, d), full(C_W, d),
            pl.BlockSpec((1, 1, 1, modsel.shape[-1]), lambda b, j: (b, jnp.minimum(j + first_block, 1), 0, 0)),
            full(1, d),
        ],
        out_specs=[
            pl.BlockSpec((1, TB, d), lambda b, j: (b, j, 0)),
            pl.BlockSpec((1, TB, d), lambda b, j: (b, j, 0)),
        ],
        out_shape=[
            jax.ShapeDtypeStruct((bsz, nblk * TB, d), F32),
            jax.ShapeDtypeStruct((bsz, nblk * TB, d), BF16),
        ],
        compiler_params=_cparams(("parallel", "parallel")),
        name="out_proj",
    )(x, a_out, b_out, o_f, o_b, og, cnw, wa, wb, wc, modsel, ln2)


def _top16_exact(s):
    nrows = s.shape[0]
    iota = lax.broadcasted_iota(jnp.int32, s.shape, 0).astype(F32)
    rank = jnp.full(s.shape, P_TOPK, jnp.int32)
    vals = []
    for r in range(P_TOPK):
        m = jnp.max(s, axis=0, keepdims=True)
        idx = jnp.min(jnp.where(s == m, iota, float(nrows)), axis=0, keepdims=True)
        hit = iota == idx
        rank = jnp.where(hit, r, rank)
        s = jnp.where(hit, -jnp.inf, s)
        vals.append(m)
    return jnp.concatenate(vals, axis=0), rank


_MARK0 = int(np.array(0xFF7FFFFF, np.uint32).view(np.int32))


def _top16_marked(s):
    vals = []
    for r in range(P_TOPK):
        m = jnp.max(s, axis=0, keepdims=True)
        mark = float(np.array(_MARK0 - r, np.int32).view(np.float32))
        s = jnp.where(s == m, mark, s)
        vals.append(m)
    rr = _MARK0 - pltpu.bitcast(s, jnp.int32)
    rank = jnp.where(rr < 0, P_TOPK, jnp.where(rr > P_TOPK - 1, P_TOPK, rr))
    slack = jnp.sum(P_TOPK - rank, axis=0, keepdims=True) - (P_TOPK * (P_TOPK + 1)) // 2
    return jnp.concatenate(vals, axis=0), rank, slack


def _int_bits(x):
    return lax.shift_right_logical(pltpu.bitcast(x, jnp.int32), 16)


def _gate_bits(x):
    return lax.shift_right_logical(pltpu.bitcast(x, jnp.int32) + 0x8000, 16)


def _dup(b):
    return b | lax.shift_left(b, 16)


def _pair_rows(b, scr):
    half = b.shape[0] // 2
    scr[...] = b
    return scr[pl.ds(0, half, stride=2), :] | lax.shift_left(scr[pl.ds(1, half, stride=2), :], 16)


def _unpack_row(row):
    n = row.shape[-1]
    tile = pltpu.bitcast(jnp.broadcast_to(row, (8, n)), BF16)
    return jnp.concatenate([tile] * (P_N_KEYS // 16), axis=0)


def _route_kernel(h_ref, wq_ref, sk_ref, r2_ref, e2_ref, n1_ref, e1_ref,
                  q_scr, s_scr, v_scr, r1_scr, e1_scr, pk_scr):
    nl = P_N_KEYS
    q = _dot(h_ref[...], wq_ref[...]).astype(BF16)
    for l in range(2 * P_HEADS):
        q_scr[l] = q[:, l * P_HALF:(l + 1) * P_HALF]

    def lane_tile(lt, carry):
        row0 = pl.multiple_of(lt * nl, nl)

        def put_rank(h, p, rank, scr):
            if p == 0:
                r1_scr[h] = rank
            else:
                r2_ref[lt, h] = _pair_rows(_int_bits(rank.astype(F32)), scr)

        def heads(hp, c):
            bad = jnp.zeros((1, nl), jnp.int32)
            for u in range(2 * ROUTE_HPT):
                h, p = hp * ROUTE_HPT + u // 2, u % 2
                s = _dot_nt(sk_ref[h, p], q_scr[2 * h + p, pl.ds(row0, nl), :])
                s_scr[u] = s
                v, rank, slack = _top16_marked(s)
                v_scr[p, h] = v
                put_rank(h, p, rank, pk_scr.at[u])
                e = jnp.exp(s - v[0:1, :])
                if p == 0:
                    e1_scr[h] = e
                else:
                    e2_ref[lt, h] = _pair_rows(_gate_bits(e), pk_scr.at[u - 1])
                bad = bad + slack

            @pl.when(jnp.max(bad) > 0)
            def _():
                for u in range(2 * ROUTE_HPT):
                    h, p = hp * ROUTE_HPT + u // 2, u % 2
                    ve, re = _top16_exact(s_scr[u])
                    v_scr[p, h] = ve
                    put_rank(h, p, re, pk_scr.at[u])

            return c

        lax.fori_loop(0, P_HEADS // ROUTE_HPT, heads, 0)

        v1 = v_scr[0]
        v2 = v_scr[1]
        ia = lax.broadcasted_iota(jnp.int32, v1.shape, 1).astype(F32)
        n = jnp.zeros(v1.shape, F32)
        g = jnp.broadcast_to(v2[:, 0:1, :], v1.shape)
        cmax = v1[:, 0:1, :] + v2[:, 0:1, :]
        z = jnp.zeros(cmax.shape, F32)
        for _ in range(P_TOPK):
            f = v1 + g
            m = jnp.max(f, axis=1, keepdims=True)
            a_star = jnp.min(jnp.where(f == m, ia, float(P_TOPK)), axis=1, keepdims=True)
            hit = ia == a_star
            n = n + jnp.where(hit, 1.0, 0.0)
            nsel = jnp.sum(jnp.where(hit, n, 0.0), axis=1, keepdims=True)
            nxt = jnp.sum(jnp.where(ia == nsel, v2, 0.0), axis=1, keepdims=True)
            nxt = jnp.where(nsel > P_TOPK - 0.5, -jnp.inf, nxt)
            g = jnp.where(hit, nxt, g)
            z = z + jnp.exp(m - cmax)
        zinv = 1.0 / z
        nw = _dup(_int_bits(n))
        for h in range(P_HEADS):
            rank1 = r1_scr[h]
            n1w = jnp.zeros(rank1.shape, jnp.int32)
            for a in range(P_TOPK):
                n1w = jnp.where(rank1 == a, nw[h, a:a + 1, :], n1w)
            n1_ref[lt, h] = n1w
            e1_ref[lt, h] = _dup(_gate_bits(e1_scr[h] * zinv[h]))
        return carry

    lax.fori_loop(0, h_ref.shape[0] // nl, lane_tile, 0)


def _route(h2, wq, sk):
    t, d = h2.shape
    tr = PEER_TT
    nl = P_N_KEYS
    oshape = (t // nl, P_HEADS, P_N_KEYS, nl)
    hshape = (t // nl, P_HEADS, P_N_KEYS // 2, nl)
    ospec = pl.BlockSpec((tr // nl, P_HEADS, P_N_KEYS, nl), lambda i: (i, 0, 0, 0))
    hspec = pl.BlockSpec((tr // nl, P_HEADS, P_N_KEYS // 2, nl), lambda i: (i, 0, 0, 0))
    return pl.pallas_call(
        _route_kernel,
        grid=(t // tr,),
        in_specs=[
            pl.BlockSpec((tr, d), lambda i: (i, 0)),
            pl.BlockSpec((d, P_HEADS * P_KEY_DIM), lambda i: (0, 0)),
            pl.BlockSpec((P_HEADS, 2, P_N_KEYS, P_HALF), lambda i: (0, 0, 0, 0)),
        ],
        out_specs=[hspec, hspec, ospec, ospec],
        out_shape=[jax.ShapeDtypeStruct(hshape, jnp.int32), jax.ShapeDtypeStruct(hshape, jnp.int32),
                   jax.ShapeDtypeStruct(oshape, jnp.int32), jax.ShapeDtypeStruct(oshape, jnp.int32)],
        scratch_shapes=[
            pltpu.VMEM((2 * P_HEADS, tr, P_HALF), BF16),
            pltpu.VMEM((2 * ROUTE_HPT, P_N_KEYS, nl), F32),
            pltpu.VMEM((2, P_HEADS, P_TOPK, nl), F32),
            pltpu.VMEM((P_HEADS, P_N_KEYS, nl), jnp.int32),
            pltpu.VMEM((P_HEADS, P_N_KEYS, nl), F32),
            pltpu.VMEM((2 * ROUTE_HPT, P_N_KEYS, nl), jnp.int32),
        ],
        compiler_params=_cparams(("parallel",)),
        name="peer_route",
    )(h2, wq, sk)


def _peer_kernel(h_ref, u_ref, vt_ref, r2_ref, e2_ref, n1_ref, e1_ref, x_ref, g5_ref, o_ref,
                 acc_ref, w_ref, a_scr, *, blocks_per_batch, ctx_blocks):
    i = pl.program_id(0)
    k = pl.program_id(1)

    @pl.when(k == 0)
    def _():
        acc_ref[...] = jnp.zeros_like(acc_ref)

    nl = P_N_KEYS
    zero = jnp.zeros((P_N_KEYS, nl), BF16)
    ltc = PEER_TC // nl
    njg = PEER_I1 // PEER_JG

    def gate_block(lt, j0):
        g = [zero] * PEER_JG
        for h in range(P_HEADS):
            r2 = pltpu.bitcast(r2_ref[lt, h], BF16)
            e2 = pltpu.bitcast(e2_ref[lt, h], BF16)
            for jj in range(PEER_JG):
                n1row = _unpack_row(n1_ref[lt, h, pl.ds(j0 + jj, 1), :])
                e1row = _unpack_row(e1_ref[lt, h, pl.ds(j0 + jj, 1), :])
                g[jj] = g[jj] + jnp.where(r2 < n1row, e2, zero) * e1row
        for jj in range(PEER_JG):
            rows = pl.ds(pl.multiple_of((j0 + jj) * P_N_KEYS, P_N_KEYS), P_N_KEYS)
            a = a_scr[lt, rows, :]
            half_cdf = (0.5 * lax.erf(a * (2.0 ** -0.5))).astype(BF16) + 0.5
            w_ref[lt, rows, :] = a.astype(BF16) * half_cdf * g[jj]

    for c in range(PEER_TT // PEER_TC):
        tok = slice(c * PEER_TC, (c + 1) * PEER_TC)
        a_t = _dot_nt(u_ref[...], h_ref[tok, :])
        for lc in range(ltc):
            a_scr[c * ltc + lc] = a_t[:, lc * nl:(lc + 1) * nl]

        def body(it, carry, c=c):
            gate_block(c * ltc + it // njg, (it % njg) * PEER_JG)
            return carry

        lax.fori_loop(0, ltc * njg, body, 0)
        w_t = jnp.concatenate([w_ref[c * ltc + lc] for lc in range(ltc)], axis=1)
        acc_ref[:, tok] += _dot(vt_ref[...], w_t)

    @pl.when(k == pl.num_programs(1) - 1)
    def _():
        y = acc_ref[...].T
        for u in range(PEER_TT // TB):
            sblk = i * (PEER_TT // TB) + u
            bidx = sblk // blocks_per_batch
            is_lat = (sblk - bidx * blocks_per_batch) >= ctx_blocks
            gate = g5_ref[2 * bidx + is_lat.astype(jnp.int32)]
            rs = slice(u * TB, (u + 1) * TB)
            o_ref[rs, :] = x_ref[rs, :] + gate * y[rs, :]


def _peer(h2, u_bf, vt_bf, r2, e2, n1, e1, x, g5, blocks_per_batch, ctx_blocks):
    t, d = h2.shape
    ne = u_bf.shape[0]
    nl = P_N_KEYS
    rspec = pl.BlockSpec((PEER_TT // nl, P_HEADS, P_N_KEYS // 2, nl), lambda i, k: (i, 0, 0, 0))
    nspec = pl.BlockSpec((PEER_TT // nl, P_HEADS, PEER_I1, nl), lambda i, k: (i, 0, k, 0))
    return pl.pallas_call(
        functools.partial(_peer_kernel, blocks_per_batch=blocks_per_batch, ctx_blocks=ctx_blocks),
        grid=(t // PEER_TT, ne // PEER_ET),
        in_specs=[
            pl.BlockSpec((PEER_TT, d), lambda i, k: (i, 0)),
            pl.BlockSpec((PEER_ET, d), lambda i, k: (k, 0)),
            pl.BlockSpec((d, PEER_ET), lambda i, k: (0, k)),
            rspec, rspec, nspec, nspec,
            pl.BlockSpec((PEER_TT, d), lambda i, k: (i, 0)),
            pl.BlockSpec(g5.shape, lambda i, k: (0, 0, 0)),
        ],
        out_specs=pl.BlockSpec((PEER_TT, d), lambda i, k: (i, 0)),
        out_shape=jax.ShapeDtypeStruct((t, d), F32),
        scratch_shapes=[pltpu.VMEM((d, PEER_TT), F32),
                        pltpu.VMEM((PEER_TT // nl, PEER_ET, nl), BF16),
                        pltpu.VMEM((PEER_TT // nl, PEER_ET, nl), F32)],
        compiler_params=_cparams(("parallel", "arbitrary")),
        name="peer_experts",
    )(h2, u_bf, vt_bf, r2, e2, n1, e1, x, g5)


def _transpose_cast_kernel(x_ref, o_ref):
    o_ref[...] = x_ref[...].T.astype(BF16)


def _transpose_cast(v):
    ne, d = v.shape
    te = 512
    return pl.pallas_call(
        _transpose_cast_kernel,
        grid=(ne // te,),
        in_specs=[pl.BlockSpec((te, d), lambda e: (e, 0))],
        out_specs=pl.BlockSpec((d, te), lambda e: (0, e)),
        out_shape=jax.ShapeDtypeStruct((d, ne), BF16),
        compiler_params=_cparams(("parallel",)),
        name="expert_value_transpose",
    )(v)


def _rope_tables(n_ctx, n_lat):
    n_freq = B_ROPE // 4
    pos = np.arange(n_lat)
    inv_freq = ROPE_THETA ** (-np.arange(n_freq, dtype=np.float32) / n_freq)
    inv_freq = jnp.asarray(inv_freq, F32)
    rowp = jnp.asarray(pos // GRID_W, F32)
    colp = jnp.asarray(pos % GRID_W, F32)
    ang = jnp.stack([rowp[:, None] * inv_freq, colp[:, None] * inv_freq], axis=1)
    cos, sin = jnp.cos(ang), jnp.sin(ang)
    cos32 = jnp.concatenate([cos, cos], axis=2).reshape(n_lat, B_ROPE)
    sin32 = jnp.concatenate([-sin, sin], axis=2).reshape(n_lat, B_ROPE)
    pad_l = jnp.ones((n_lat, B_NOPE), F32)
    pad_r = jnp.ones((n_lat, B_HP - B_QK), F32)
    cos_l = jnp.concatenate([pad_l, cos32, pad_r], axis=1)
    sin_l = jnp.concatenate([0 * pad_l, sin32, 0 * pad_r], axis=1)
    cos_t = jnp.concatenate([jnp.ones((n_ctx, B_HP), F32), cos_l], axis=0)
    sin_t = jnp.concatenate([jnp.zeros((n_ctx, B_HP), F32), sin_l], axis=0)
    return cos_t, sin_t


_SWAP32 = np.arange(B_ROPE) ^ (B_ROPE // 4)


def _pad_head(nope, rope):
    z = jnp.zeros(nope.shape[:-1] + (B_HP - B_QK,), nope.dtype)
    out = jnp.concatenate([nope, rope, z], axis=-1)
    return out.reshape(out.shape[:-2] + (out.shape[-2] * B_HP,))


def _layer_weights(layer, w_in, w_out, a_norm_w, a_w_s, a_b_s, b_q_norm_w, b_w_uq, b_kv_norm_w, b_w_ukv,
                   b_q_head_norm_w, b_k_head_norm_w, c_out_norm_w, p_w_q, p_sub_keys):
    d = w_in.shape[1]
    wi = w_in[layer]
    offs = np.cumsum([0, A_W, A_W, B_Q_RANK, B_KV_RANK, B_ROPE, C_W, C_W, C_W, C_W, C_W])
    col = lambda i: wi[:, offs[i]:offs[i + 1]]
    w_kr = col(4)
    zl = jnp.zeros((d, B_NOPE), F32)
    zr = jnp.zeros((d, B_HP - B_QK), F32)
    kr_placed = jnp.concatenate([zl, w_kr, zr], axis=1)
    kr_swapped = jnp.concatenate([zl, w_kr[:, _SWAP32], zr], axis=1)
    w_all = jnp.concatenate(
        [col(0), col(1), col(2), col(3), kr_placed, kr_swapped, col(5), col(6), col(7), col(8), col(9)],
        axis=1).astype(BF16)
    widths = (2 * A_W, B_Q_RANK + B_KV_RANK + 2 * B_HP, 4 * C_W, C_W)

    wuq = b_w_uq[layer].reshape(B_Q_RANK, B_HEADS, B_QK)
    wq_p = _pad_head(wuq[..., :B_NOPE], wuq[..., B_NOPE:]).astype(BF16)
    wq_s = _pad_head(0 * wuq[..., :B_NOPE], wuq[..., B_NOPE:][..., _SWAP32]).astype(BF16)
    wukv = b_w_ukv[layer].reshape(B_KV_RANK, B_HEADS, B_NOPE + B_V)
    wk_p = _pad_head(wukv[..., :B_NOPE], jnp.zeros((B_KV_RANK, B_HEADS, B_ROPE), F32)).astype(BF16)
    wv = wukv[..., B_NOPE:].reshape(B_KV_RANK, B_HEADS * B_V).astype(BF16)
    qn, kn = b_q_head_norm_w[layer], b_k_head_norm_w[layer]
    zpad = jnp.zeros((B_HP - B_QK,), F32)
    z64 = jnp.zeros((B_NOPE,), F32)
    hw = jnp.stack([
        jnp.concatenate([qn, zpad]),
        jnp.concatenate([z64, qn[B_NOPE:][_SWAP32], zpad]),
        jnp.concatenate([kn, zpad]),
        jnp.concatenate([z64, kn[B_NOPE:][_SWAP32], zpad]),
    ] + [jnp.zeros((B_HP,), F32)] * 4, axis=0)

    wo = w_out[layer].astype(BF16)
    return dict(
        w_all=w_all, widths=widths,
        a_nw=a_norm_w[layer].reshape(1, A_W),
        a_ws=a_w_s[layer].astype(BF16),
        a_bias=jnp.repeat(a_b_s[layer].T, A_HD, axis=1),
        qnw=b_q_norm_w[layer].reshape(1, B_Q_RANK), kvnw=b_kv_norm_w[layer].reshape(1, B_KV_RANK),
        wq_p=wq_p, wq_s=wq_s, wk_p=wk_p, wv=wv, hw=hw,
        cnw=jnp.tile(c_out_norm_w[layer], C_HEADS).reshape(1, C_W),
        wo_a=wo[:A_W], wo_b=wo[A_W:A_W + B_HEADS * B_V], wo_c=wo[A_W + B_HEADS * B_V:],
        p_wq=p_w_q[layer].astype(BF16), p_sk=p_sub_keys[layer].astype(BF16),
    )


def kernel(x, c, ctx, c_ctx, ln1_w, ln2_w, w_mod, b_mod, w_in, w_out, a_norm_w, a_w_s, a_b_s, b_q_norm_w,
           b_w_uq, b_kv_norm_w, b_w_ukv, b_q_head_norm_w, b_k_head_norm_w, c_lb_logits, c_out_norm_w,
           p_w_q, p_sub_keys, p_u, p_v):
    bsz, n_lat, d = x.shape
    n_ctx = ctx.shape[1]
    depth = w_in.shape[0]
    n = n_ctx + n_lat
    ctx_blocks = n_ctx // TB
    blocks = n // TB

    mrows = -(-(bsz + 1) // 8) * 8
    cvec = jnp.concatenate([c, c_ctx[None, :], jnp.zeros((mrows - bsz - 1, d), F32)], axis=0)
    mod_all = _modulation(cvec, w_mod, b_mod)

    lb = jnp.cumsum(jax.nn.softmax(c_lb_logits.astype(F32), axis=0), axis=0)
    lb = lb - lb[0:1]
    lbc_all = jnp.stack([jnp.log(lb), jnp.log1p(-lb), 1.0 - lb] + [jnp.zeros_like(lb)] * 5, axis=2)

    cos_t, sin_t = _rope_tables(n_ctx, n_lat)
    xc = jnp.concatenate([ctx, x], axis=1)

    for layer in range(depth):
        last = layer == depth - 1
        w = _layer_weights(layer, w_in, w_out, a_norm_w, a_w_s, a_b_s, b_q_norm_w, b_w_uq, b_kv_norm_w,
                           b_w_ukv, b_q_head_norm_w, b_k_head_norm_w, c_out_norm_w, p_w_q, p_sub_keys)
        mod_b = mod_all[layer, :bsz]
        mod_c = jnp.broadcast_to(mod_all[layer, bsz][None, :], mod_b.shape)
        modsel = jnp.stack([mod_c, mod_b], axis=1)[:, :, None, :]
        first_block = ctx_blocks if last else 0

        oa, ob, oc, og = _inproj(xc, ln1_w[layer].reshape(1, d), modsel, w["w_all"], w["widths"])
        a_out = _amix(oa, w["a_nw"], w["a_ws"], w["a_bias"], first_block * (TB // A_CHUNK))
        q, k, v = _mla_prep(ob, cos_t, sin_t, w["qnw"], w["kvnw"], w["wq_p"], w["wq_s"], w["wk_p"], w["wv"], w["hw"])
        b_lat = _attention(q, k, v, ctx_blocks, blocks - ctx_blocks, n, n_lat)
        if last:
            b_out = b_lat
        else:
            b_ctx = _attention(q, k, v, 0, ctx_blocks, n_ctx, n_ctx)
            b_out = jnp.concatenate([b_ctx, b_lat], axis=1)
        o_f, o_b = _hgrn(oc, lbc_all[layer], n_ctx)
        x_new, h2 = _outproj(xc, a_out, b_out, o_f, o_b, og, w["cnw"], w["wo_a"], w["wo_b"], w["wo_c"],
                             modsel, ln2_w[layer].reshape(1, d), first_block)

        t = x_new.shape[0] * x_new.shape[1]
        h2f = h2.reshape(t, d)
        r2, e2, n1, e1 = _route(h2f, w["p_wq"], w["p_sk"])
        g5 = modsel[:, :, :, 5 * d:6 * d].reshape(2 * bsz, 1, d)
        u_bf = p_u[layer].astype(BF16)
        vt_bf = _transpose_cast(p_v[layer])
        out = _peer(h2f, u_bf, vt_bf, r2, e2, n1, e1, x_new.reshape(t, d), g5,
                    blocks - first_block, ctx_blocks - first_block)
        xc = out.reshape(bsz, t // bsz, d)
    return xc
```

```python
import functools
import math

import jax
import jax.numpy as jnp
import numpy as np
from jax import lax
from jax.experimental import pallas as pl
from jax.experimental.pallas import tpu as pltpu

F32 = jnp.float32
BF16 = jnp.bfloat16
HIGHEST = lax.Precision.HIGHEST

EPS = 1e-6
GRID_W = 64
ROPE_THETA = 10000.0

A_HEADS, A_HD, A_CHUNK = 4, 64, 128
A_W = A_HEADS * A_HD
B_HEADS, B_NOPE, B_ROPE, B_V = 8, 64, 32, 64
B_QK = B_NOPE + B_ROPE
B_HP = 128
B_Q_RANK, B_KV_RANK = 256, 128
C_HEADS, C_DK, C_DV, C_CHUNK = 4, 64, 64, 64
C_W = C_HEADS * C_DK
C_SUB = 16
P_HEADS, P_KEY_DIM, P_N_KEYS, P_TOPK = 8, 256, 128, 16
P_HALF = P_KEY_DIM // 2

TB = 256
PEER_TT = 1024
PEER_TC = 512
PEER_ET = 1024
PEER_I1 = PEER_ET // P_N_KEYS
PEER_JG = 4
ROUTE_HPT = 2
VMEM_LIMIT = 56 * 1024 * 1024


def _cparams(sem, flags=None):
    return pltpu.CompilerParams(dimension_semantics=sem, vmem_limit_bytes=VMEM_LIMIT, flags=flags)


def _dot_nt(a, b):
    return lax.dot_general(a, b, (((1,), (1,)), ((), ())), preferred_element_type=F32)


def _dot_tn(a, b):
    return lax.dot_general(a, b, (((0,), (0,)), ((), ())), preferred_element_type=F32)


def _dot(a, b):
    return jnp.dot(a, b, preferred_element_type=F32)


def _sigmoid(x):
    return 1.0 / (1.0 + jnp.exp(-x))


def _block_ones(n, blk, dtype):
    r = lax.broadcasted_iota(jnp.int32, (n, n), 0) // blk
    c = lax.broadcasted_iota(jnp.int32, (n, n), 1) // blk
    return (r == c).astype(dtype)


def _mod_kernel(c_ref, w_ref, b_ref, o_ref):
    c = c_ref[...]
    sc = c * _sigmoid(c)
    o_ref[0] = _dot(sc.astype(BF16), w_ref[0].astype(BF16)) + b_ref[0]


def _modulation(cvec, w_mod, b_mod):
    depth, d, n6 = w_mod.shape
    rows = cvec.shape[0]
    tn = 1024
    return pl.pallas_call(
        _mod_kernel,
        grid=(depth, n6 // tn),
        in_specs=[
            pl.BlockSpec((rows, d), lambda l, n: (0, 0)),
            pl.BlockSpec((1, d, tn), lambda l, n: (l, 0, n)),
            pl.BlockSpec((1, 1, tn), lambda l, n: (l, 0, n)),
        ],
        out_specs=pl.BlockSpec((1, rows, tn), lambda l, n: (l, 0, n)),
        out_shape=jax.ShapeDtypeStruct((depth, rows, n6), F32),
        compiler_params=_cparams(("parallel", "parallel")),
        name="adaln_mod",
    )(cvec, w_mod, b_mod.reshape(depth, 1, n6))


def _inproj_kernel(x_ref, lnw_ref, mod_ref, w_ref, oa_ref, ob_ref, oc_ref, og_ref, *, d):
    x = x_ref[0]
    ms = jnp.mean(x * x, axis=-1, keepdims=True)
    y = x * lax.rsqrt(ms + EPS) * lnw_ref[...]
    shift = mod_ref[0, 0, :, 0:d]
    scale = mod_ref[0, 0, :, d:2 * d]
    h = (y * (1.0 + scale) + shift).astype(BF16)
    p = _dot(h, w_ref[...])
    na = oa_ref.shape[-1]
    nb = ob_ref.shape[-1]
    nc = oc_ref.shape[-1]
    oa_ref[0] = p[:, 0:na]
    ob_ref[0] = p[:, na:na + nb]
    oc_ref[0] = p[:, na + nb:na + nb + nc]
    og_ref[0] = p[:, na + nb + nc:]


def _inproj(x, lnw, modsel, w_all, widths):
    bsz, n, d = x.shape
    na, nb, nc, ng = widths
    nout = w_all.shape[1]
    return pl.pallas_call(
        functools.partial(_inproj_kernel, d=d),
        grid=(bsz, n // TB),
        in_specs=[
            pl.BlockSpec((1, TB, d), lambda b, j: (b, j, 0)),
            pl.BlockSpec((1, d), lambda b, j: (0, 0)),
            pl.BlockSpec((1, 1, 1, modsel.shape[-1]), lambda b, j: (b, jnp.minimum(j, 1), 0, 0)),
            pl.BlockSpec((d, nout), lambda b, j: (0, 0)),
        ],
        out_specs=[
            pl.BlockSpec((1, TB, na), lambda b, j: (b, j, 0)),
            pl.BlockSpec((1, TB, nb), lambda b, j: (b, j, 0)),
            pl.BlockSpec((1, TB, nc), lambda b, j: (b, j, 0)),
            pl.BlockSpec((1, TB, ng), lambda b, j: (b, j, 0)),
        ],
        out_shape=[
            jax.ShapeDtypeStruct((bsz, n, na), F32),
            jax.ShapeDtypeStruct((bsz, n, nb), F32),
            jax.ShapeDtypeStruct((bsz, n, nc), F32),
            jax.ShapeDtypeStruct((bsz, n, ng), F32),
        ],
        compiler_params=_cparams(("parallel", "parallel")),
        name="in_proj",
    )(x, lnw, modsel, w_all)


def _amix_kernel(a_ref, nw_ref, ws_ref, bias_ref, o_ref):
    u = a_ref[0, :, 0:A_W]
    v = a_ref[0, :, A_W:2 * A_W]
    ssq = jnp.dot(v * v, _block_ones(A_W, A_HD, F32), precision=HIGHEST, preferred_element_type=F32)
    vn = v * lax.rsqrt(ssq * (1.0 / A_HD) + EPS) * nw_ref[...]
    lane_head = lax.broadcasted_iota(jnp.int32, vn.shape, 1) // A_HD
    acc = bias_ref[...]
    for h in range(A_HEADS):
        vm = jnp.where(lane_head == h, vn, 0.0).astype(BF16)
        acc = acc + _dot(ws_ref[h], vm)
    o_ref[0] = (u * acc).astype(BF16)


def _amix(oa, nw, ws, bias, first_chunk):
    bsz, n, _ = oa.shape
    nchunk = n // A_CHUNK - first_chunk
    return pl.pallas_call(
        _amix_kernel,
        grid=(bsz, nchunk),
        in_specs=[
            pl.BlockSpec((1, A_CHUNK, 2 * A_W), lambda b, j: (b, j + first_chunk, 0)),
            pl.BlockSpec((1, A_W), lambda b, j: (0, 0)),
            pl.BlockSpec((A_HEADS, A_CHUNK, A_CHUNK), lambda b, j: (0, 0, 0)),
            pl.BlockSpec((A_CHUNK, A_W), lambda b, j: (0, 0)),
        ],
        out_specs=pl.BlockSpec((1, A_CHUNK, A_W), lambda b, j: (b, j + first_chunk, 0)),
        out_shape=jax.ShapeDtypeStruct((bsz, n, A_W), BF16),
        compiler_params=_cparams(("parallel", "parallel")),
        name="mixer_a",
    )(oa, nw, ws, bias)


def _mla_prep_kernel(ob_ref, cos_ref, sin_ref, qnw_ref, kvnw_ref, wq_ref, wqs_ref, wk_ref, wv_ref,
                     hw_ref, q_ref, k_ref, v_ref):
    cq = ob_ref[0, :, 0:B_Q_RANK]
    ckv = ob_ref[0, :, B_Q_RANK:B_Q_RANK + B_KV_RANK]
    krp = ob_ref[0, :, B_Q_RANK + B_KV_RANK:B_Q_RANK + B_KV_RANK + B_HP]
    krs = ob_ref[0, :, B_Q_RANK + B_KV_RANK + B_HP:B_Q_RANK + B_KV_RANK + 2 * B_HP]
    cos = cos_ref[...]
    sin = sin_ref[...]
    cqn = (cq * lax.rsqrt(jnp.mean(cq * cq, axis=-1, keepdims=True) + EPS) * qnw_ref[...]).astype(BF16)
    ckn = (ckv * lax.rsqrt(jnp.mean(ckv * ckv, axis=-1, keepdims=True) + EPS) * kvnw_ref[...]).astype(BF16)
    q_raw = _dot(cqn, wq_ref[...])
    q_swp = _dot(cqn, wqs_ref[...])
    k_raw = _dot(ckn, wk_ref[...])
    v_all = _dot(ckn, wv_ref[...])
    qw, qws, kw, kws = hw_ref[0:1, :], hw_ref[1:2, :], hw_ref[2:3, :], hw_ref[3:4, :]
    k_rot_sw = krs * kws * sin
    for h in range(B_HEADS):
        sl = slice(h * B_HP, (h + 1) * B_HP)
        qh = q_raw[:, sl]
        rq = lax.rsqrt(jnp.sum(qh * qh, axis=-1, keepdims=True) * (1.0 / B_QK) + EPS)
        q_ref[0, h] = (rq * (qh * qw * cos + q_swp[:, sl] * qws * sin)).astype(BF16)
        kh = k_raw[:, sl] + krp
        rk = lax.rsqrt(jnp.sum(kh * kh, axis=-1, keepdims=True) * (1.0 / B_QK) + EPS)
        k_ref[0, h] = (rk * (kh * kw * cos + k_rot_sw)).astype(BF16)
        v_ref[0, h] = v_all[:, h * B_V:(h + 1) * B_V].astype(BF16)


def _mla_prep(ob, cos_t, sin_t, qnw, kvnw, wq, wqs, wk, wv, hw):
    bsz, n, nb = ob.shape
    full = lambda *s: pl.BlockSpec(s, lambda b, j: (0,) * len(s))
    return pl.pallas_call(
        _mla_prep_kernel,
        grid=(bsz, n // TB),
        in_specs=[
            pl.BlockSpec((1, TB, nb), lambda b, j: (b, j, 0)),
            pl.BlockSpec((TB, B_HP), lambda b, j: (j, 0)),
            pl.BlockSpec((TB, B_HP), lambda b, j: (j, 0)),
            full(1, B_Q_RANK), full(1, B_KV_RANK),
            full(B_Q_RANK, B_HEADS * B_HP), full(B_Q_RANK, B_HEADS * B_HP),
            full(B_KV_RANK, B_HEADS * B_HP), full(B_KV_RANK, B_HEADS * B_V),
            full(8, B_HP),
        ],
        out_specs=[
            pl.BlockSpec((1, B_HEADS, TB, B_HP), lambda b, j: (b, 0, j, 0)),
            pl.BlockSpec((1, B_HEADS, TB, B_HP), lambda b, j: (b, 0, j, 0)),
            pl.BlockSpec((1, B_HEADS, TB, B_V), lambda b, j: (b, 0, j, 0)),
        ],
        out_shape=[
            jax.ShapeDtypeStruct((bsz, B_HEADS, n, B_HP), BF16),
            jax.ShapeDtypeStruct((bsz, B_HEADS, n, B_HP), BF16),
            jax.ShapeDtypeStruct((bsz, B_HEADS, n, B_V), BF16),
        ],
        compiler_params=_cparams(("parallel", "parallel")),
        name="mla_prep",
    )(ob, cos_t, sin_t, qnw, kvnw, wq, wqs, wk, wv, hw)


def _attn_kernel(*refs, nqb):
    q_refs, (k_ref, v_ref, o_ref, o_scr) = refs[:nqb], refs[nqb:]
    scale = (B_QK ** -0.5) * math.log2(math.e)
    for h in range(B_HEADS):
        q = q_refs[0][0, h] if nqb == 1 else jnp.concatenate([r[0, h] for r in q_refs], axis=0)
        s = _dot_nt(q, k_ref[0, h])
        m = jnp.max(s, axis=-1, keepdims=True)
        p = jnp.exp2((s - m) * scale)
        l = jnp.sum(p, axis=-1, keepdims=True)
        o = _dot(p.astype(BF16), v_ref[0, h])
        o_scr[:, h * B_V:(h + 1) * B_V] = o / l
    o_ref[0] = o_scr[...].astype(BF16)


def _attention(q, k, v, first_qblock, n_qblocks, n_keys, out_rows, nqb):
    bsz = q.shape[0]
    qspec = lambda u: pl.BlockSpec((1, B_HEADS, TB, B_HP), lambda b, j: (b, 0, j * nqb + u + first_qblock, 0))
    return pl.pallas_call(
        functools.partial(_attn_kernel, nqb=nqb),
        grid=(bsz, n_qblocks // nqb),
        in_specs=[qspec(u) for u in range(nqb)] + [
            pl.BlockSpec((1, B_HEADS, n_keys, B_HP), lambda b, j: (b, 0, 0, 0)),
            pl.BlockSpec((1, B_HEADS, n_keys, B_V), lambda b, j: (b, 0, 0, 0)),
        ],
        out_specs=pl.BlockSpec((1, nqb * TB, B_HEADS * B_V), lambda b, j: (b, j, 0)),
        out_shape=jax.ShapeDtypeStruct((bsz, out_rows, B_HEADS * B_V), BF16),
        scratch_shapes=[pltpu.VMEM((nqb * TB, B_HEADS * B_V), F32)],
        compiler_params=_cparams(("parallel", "arbitrary")),
        name="mla_attention",
    )(*([q] * nqb), k, v)


def _hgrn_chunk(blk, zcol, lbc, st_ref, rev):
    cc, w = C_CHUNK, C_W
    q = blk[:, 0:w] * (C_DK ** -0.5)
    z = blk[:, zcol * w:(zcol + 1) * w]
    v = blk[:, 3 * w:4 * w]
    log_lb, log1m_lb, one_m_lb = lbc[0:1, :], lbc[1:2, :], lbc[2:3, :]
    az = jnp.abs(z)
    sp = jnp.log1p(jnp.exp(-az))
    lsig = jnp.minimum(z, 0.0) - sp
    t2 = log1m_lb + lsig
    mx = jnp.maximum(log_lb, t2)
    mn = jnp.minimum(log_lb, t2)
    logf = mx + jnp.log1p(jnp.exp(mn - mx))
    kk = one_m_lb * _sigmoid(-z)

    ti = lax.broadcasted_iota(jnp.int32, (cc, cc), 0)
    ui = lax.broadcasted_iota(jnp.int32, (cc, cc), 1)
    tri = ((ui >= ti) if rev else (ui <= ti)).astype(F32)
    b = jnp.dot(tri, logf, precision=HIGHEST, preferred_element_type=F32)
    b_tot = b[0:1, :] if rev else b[cc - 1:cc, :]

    row = lax.broadcasted_iota(jnp.int32, (cc, w), 0)
    lane_head = lax.broadcasted_iota(jnp.int32, (cc, w), 1) // C_DK
    nsub = cc // C_SUB
    row_blk = row // C_SUB

    beta_rows = []
    for i in range(nsub):
        if rev:
            src = None if i == nsub - 1 else b[(i + 1) * C_SUB:(i + 1) * C_SUB + 1, :]
        else:
            src = None if i == 0 else b[i * C_SUB - 1:i * C_SUB, :]
        beta_rows.append(src)
    beta_full = jnp.concatenate(
        [jnp.broadcast_to(b[i * C_SUB:i * C_SUB + 1, :] if r is None else r, (C_SUB, w))
         for i, r in enumerate(beta_rows)], axis=0)
    has_prev = (row_blk < nsub - 1) if rev else (row_blk > 0)
    qs = jnp.where(has_prev, q * jnp.exp(b - beta_full), 0.0)

    q_stack = jnp.concatenate([jnp.where(lane_head == h, qs, 0.0) for h in range(C_HEADS)], axis=0).astype(BF16)
    qblocks = [i for i in range(nsub) if beta_rows[i] is not None]
    ks_parts = []
    for i in qblocks:
        prev = (row_blk > i) if rev else (row_blk < i)
        ks_parts.append(jnp.where(prev, kk * jnp.exp(beta_rows[i] - b), 0.0))
    ks_all = jnp.concatenate(ks_parts, axis=0).astype(BF16)
    a_all = _dot_nt(q_stack, ks_all)
    ar = lax.broadcasted_iota(jnp.int32, a_all.shape, 0)
    ac = lax.broadcasted_iota(jnp.int32, a_all.shape, 1)
    r_blk = (ar % cc) // C_SUB
    c_blk = ac // cc + (0 if rev else 1)
    a_all = jnp.where(r_blk == c_blk, a_all, 0.0).astype(BF16)
    v_bf = v.astype(BF16)
    r_all = _dot(a_all, jnp.concatenate([v_bf] * len(qblocks), axis=0))
    o = jnp.zeros((cc, w), F32)
    for h in range(C_HEADS):
        o = o + jnp.where(lane_head == h, r_all[h * cc:(h + 1) * cc, :], 0.0)

    ones_bd = _block_ones(w, C_DK, BF16)
    tsub = lax.broadcasted_iota(jnp.int32, (C_SUB, w), 0)
    diag_parts = []
    for i in range(nsub):
        r0 = i * C_SUB
        bb = b[r0:r0 + C_SUB, :]
        qq = q[r0:r0 + C_SUB, :]
        ps = []
        for s in range(C_SUB):
            keep = (tsub <= s) if rev else (tsub >= s)
            e = jnp.where(keep, jnp.exp(bb - b[r0 + s:r0 + s + 1, :]), 0.0)
            ps.append(qq * e * kk[r0 + s:r0 + s + 1, :])
        red = _dot(jnp.concatenate(ps, axis=0).astype(BF16), ones_bd)
        od = jnp.zeros((C_SUB, w), F32)
        for s in range(C_SUB):
            od = od + red[s * C_SUB:(s + 1) * C_SUB, :] * v[r0 + s:r0 + s + 1, :]
        diag_parts.append(od)
    o = o + jnp.concatenate(diag_parts, axis=0)

    st = st_ref[...]
    o = o + _dot_nt((q * jnp.exp(b)).astype(BF16), st.astype(BF16))
    kd = (kk * jnp.exp(b_tot - b)).astype(BF16)
    upd = _dot_tn(v_bf, kd)
    st_ref[...] = st * jnp.exp(b_tot) + upd * _block_ones(w, C_DK, F32)
    return o


def _hgrn_kernel(cf_ref, cb_ref, lbc_ref, of_ref, ob_ref, sf_ref, sb_ref):
    @pl.when(pl.program_id(1) == 0)
    def _():
        sf_ref[...] = jnp.zeros_like(sf_ref)
        sb_ref[...] = jnp.zeros_like(sb_ref)

    of_ref[0] = _hgrn_chunk(cf_ref[0], 1, lbc_ref[0], sf_ref, rev=False)
    ob_ref[0] = _hgrn_chunk(cb_ref[0], 2, lbc_ref[1], sb_ref, rev=True)


def _hgrn(oc, lbc, n_ctx):
    bsz, n, wc = oc.shape
    nch = n // C_CHUNK
    nctx = n_ctx // C_CHUNK

    def bwd_idx(c):
        return jnp.where(c < nctx, nctx - 1 - c, nch + nctx - 1 - c)

    return pl.pallas_call(
        _hgrn_kernel,
        grid=(bsz, nch),
        in_specs=[
            pl.BlockSpec((1, C_CHUNK, wc), lambda b, c: (b, c, 0)),
            pl.BlockSpec((1, C_CHUNK, wc), lambda b, c: (b, bwd_idx(c), 0)),
            pl.BlockSpec((2, 8, C_W), lambda b, c: (0, 0, 0)),
        ],
        out_specs=[
            pl.BlockSpec((1, C_CHUNK, C_W), lambda b, c: (b, c, 0)),
            pl.BlockSpec((1, C_CHUNK, C_W), lambda b, c: (b, bwd_idx(c), 0)),
        ],
        out_shape=[jax.ShapeDtypeStruct((bsz, n, C_W), F32)] * 2,
        scratch_shapes=[pltpu.VMEM((C_W, C_W), F32)] * 2,
        compiler_params=_cparams(("parallel", "arbitrary")),
        name="hgrn2_scan",
    )(oc, oc, lbc)


def _outproj_kernel(x_ref, a_ref, b_ref, of_ref, ob_ref, g_ref, cnw_ref, wa_ref, wb_ref, wc_ref,
                    mod_ref, ln2_ref, xo_ref, h2_ref, *, d):
    o = of_ref[0] + ob_ref[0]
    ssq = jnp.dot(o * o, _block_ones(C_W, C_DV, F32), precision=HIGHEST, preferred_element_type=F32)
    g = g_ref[0]
    c_out = o * lax.rsqrt(ssq * (1.0 / C_DV) + EPS) * cnw_ref[...] * (g * _sigmoid(g))
    mix = _dot(a_ref[0], wa_ref[...]) + _dot(b_ref[0], wb_ref[...]) + _dot(c_out.astype(BF16), wc_ref[...])
    gate1 = mod_ref[0, 0, :, 2 * d:3 * d]
    shift2 = mod_ref[0, 0, :, 3 * d:4 * d]
    scale2 = mod_ref[0, 0, :, 4 * d:5 * d]
    x = x_ref[0] + gate1 * mix
    xo_ref[0] = x
    y = x * lax.rsqrt(jnp.mean(x * x, axis=-1, keepdims=True) + EPS) * ln2_ref[...]
    h2_ref[0] = (y * (1.0 + scale2) + shift2).astype(BF16)


def _outproj(x, a_out, b_out, o_f, o_b, og, cnw, wa, wb, wc, modsel, ln2, first_block):
    bsz, n, d = x.shape
    nblk = n // TB - first_block
    full = lambda *s: pl.BlockSpec(s, lambda b, j: (0,) * len(s))
    tok = lambda w: pl.BlockSpec((1, TB, w), lambda b, j: (b, j + first_block, 0))
    return pl.pallas_call(
        functools.partial(_outproj_kernel, d=d),
        grid=(bsz, nblk),
        in_specs=[
            tok(d), tok(A_W),
            pl.BlockSpec((1, TB, B_HEADS * B_V), lambda b, j: (b, j, 0)) if first_block else tok(B_HEADS * B_V),
            tok(C_W), tok(C_W), tok(C_W),
            full(1, C_W), full(A_W, d), full(B_HEADS * B_V, d), full(C_W, d),
            pl.BlockSpec((1, 1, 1, modsel.shape[-1]), lambda b, j: (b, jnp.minimum(j + first_block, 1), 0, 0)),
            full(1, d),
        ],
        out_specs=[
            pl.BlockSpec((1, TB, d), lambda b, j: (b, j, 0)),
            pl.BlockSpec((1, TB, d), lambda b, j: (b, j, 0)),
        ],
        out_shape=[
            jax.ShapeDtypeStruct((bsz, nblk * TB, d), F32),
            jax.ShapeDtypeStruct((bsz, nblk * TB, d), BF16),
        ],
        compiler_params=_cparams(("parallel", "parallel")),
        name="out_proj",
    )(x, a_out, b_out, o_f, o_b, og, cnw, wa, wb, wc, modsel, ln2)


def _top16_exact(s):
    nrows = s.shape[0]
    iota = lax.broadcasted_iota(jnp.int32, s.shape, 0).astype(F32)
    rank = jnp.full(s.shape, P_TOPK, jnp.int32)
    vals = []
    for r in range(P_TOPK):
        m = jnp.max(s, axis=0, keepdims=True)
        idx = jnp.min(jnp.where(s == m, iota, float(nrows)), axis=0, keepdims=True)
        hit = iota == idx
        rank = jnp.where(hit, r, rank)
        s = jnp.where(hit, -jnp.inf, s)
        vals.append(m)
    return jnp.concatenate(vals, axis=0), rank


_MARK0 = int(np.array(0xFF7FFFFF, np.uint32).view(np.int32))


def _top16_marked(s):
    vals = []
    for r in range(P_TOPK):
        m = jnp.max(s, axis=0, keepdims=True)
        mark = float(np.array(_MARK0 - r, np.int32).view(np.float32))
        s = jnp.where(s == m, mark, s)
        vals.append(m)
    rr = _MARK0 - pltpu.bitcast(s, jnp.int32)
    rank = jnp.where(rr < 0, P_TOPK, jnp.where(rr > P_TOPK - 1, P_TOPK, rr))
    slack = jnp.sum(P_TOPK - rank, axis=0, keepdims=True) - (P_TOPK * (P_TOPK + 1)) // 2
    return jnp.concatenate(vals, axis=0), rank, slack


def _int_bits(x):
    return lax.shift_right_logical(pltpu.bitcast(x, jnp.int32), 16)


def _gate_bits(x):
    return lax.shift_right_logical(pltpu.bitcast(x, jnp.int32) + 0x8000, 16)


def _dup(b):
    return b | lax.shift_left(b, 16)


def _pair_rows(b, scr):
    half = b.shape[0] // 2
    scr[...] = b
    return scr[pl.ds(0, half, stride=2), :] | lax.shift_left(scr[pl.ds(1, half, stride=2), :], 16)


def _unpack_row(row):
    n = row.shape[-1]
    tile = pltpu.bitcast(jnp.broadcast_to(row, (8, n)), BF16)
    return jnp.concatenate([tile] * (P_N_KEYS // 16), axis=0)


def _route_kernel(h_ref, wq_ref, sk_ref, r2_ref, e2_ref, n1_ref, e1_ref,
                  q_scr, s_scr, v_scr, r1_scr, e1_scr, pk_scr):
    nl = P_N_KEYS
    q = _dot(h_ref[...], wq_ref[...]).astype(BF16)
    for l in range(2 * P_HEADS):
        q_scr[l] = q[:, l * P_HALF:(l + 1) * P_HALF]

    def lane_tile(lt, carry):
        row0 = pl.multiple_of(lt * nl, nl)

        def put_rank(h, p, rank, scr):
            if p == 0:
                r1_scr[h] = rank
            else:
                r2_ref[lt, h] = _pair_rows(_int_bits(rank.astype(F32)), scr)

        def heads(hp, c):
            bad = jnp.zeros((1, nl), jnp.int32)
            for u in range(2 * ROUTE_HPT):
                h, p = hp * ROUTE_HPT + u // 2, u % 2
                s = _dot_nt(sk_ref[h, p], q_scr[2 * h + p, pl.ds(row0, nl), :])
                s_scr[u] = s
                v, rank, slack = _top16_marked(s)
                v_scr[p, h] = v
                put_rank(h, p, rank, pk_scr.at[u])
                e = jnp.exp(s - v[0:1, :])
                if p == 0:
                    e1_scr[h] = e
                else:
                    e2_ref[lt, h] = _pair_rows(_gate_bits(e), pk_scr.at[u - 1])
                bad = bad + slack

            @pl.when(jnp.max(bad) > 0)
            def _():
                for u in range(2 * ROUTE_HPT):
                    h, p = hp * ROUTE_HPT + u // 2, u % 2
                    ve, re = _top16_exact(s_scr[u])
                    v_scr[p, h] = ve
                    put_rank(h, p, re, pk_scr.at[u])

            return c

        lax.fori_loop(0, P_HEADS // ROUTE_HPT, heads, 0)

        v1 = v_scr[0]
        v2 = v_scr[1]
        ia = lax.broadcasted_iota(jnp.int32, v1.shape, 1).astype(F32)
        n = jnp.zeros(v1.shape, F32)
        g = jnp.broadcast_to(v2[:, 0:1, :], v1.shape)
        cmax = v1[:, 0:1, :] + v2[:, 0:1, :]
        z = jnp.zeros(cmax.shape, F32)
        for _ in range(P_TOPK):
            f = v1 + g
            m = jnp.max(f, axis=1, keepdims=True)
            a_star = jnp.min(jnp.where(f == m, ia, float(P_TOPK)), axis=1, keepdims=True)
            hit = ia == a_star
            n = n + jnp.where(hit, 1.0, 0.0)
            nsel = jnp.sum(jnp.where(hit, n, 0.0), axis=1, keepdims=True)
            nxt = jnp.sum(jnp.where(ia == nsel, v2, 0.0), axis=1, keepdims=True)
            nxt = jnp.where(nsel > P_TOPK - 0.5, -jnp.inf, nxt)
            g = jnp.where(hit, nxt, g)
            z = z + jnp.exp(m - cmax)
        zinv = 1.0 / z
        nw = _dup(_int_bits(n))
        for h in range(P_HEADS):
            rank1 = r1_scr[h]
            n1w = jnp.zeros(rank1.shape, jnp.int32)
            for a in range(P_TOPK):
                n1w = jnp.where(rank1 == a, nw[h, a:a + 1, :], n1w)
            n1_ref[lt, h] = n1w
            e1_ref[lt, h] = _dup(_gate_bits(e1_scr[h] * zinv[h]))
        return carry

    lax.fori_loop(0, h_ref.shape[0] // nl, lane_tile, 0)


def _route(h2, wq, sk):
    t, d = h2.shape
    tr = PEER_TT
    nl = P_N_KEYS
    oshape = (t // nl, P_HEADS, P_N_KEYS, nl)
    hshape = (t // nl, P_HEADS, P_N_KEYS // 2, nl)
    ospec = pl.BlockSpec((tr // nl, P_HEADS, P_N_KEYS, nl), lambda i: (i, 0, 0, 0))
    hspec = pl.BlockSpec((tr // nl, P_HEADS, P_N_KEYS // 2, nl), lambda i: (i, 0, 0, 0))
    return pl.pallas_call(
        _route_kernel,
        grid=(t // tr,),
        in_specs=[
            pl.BlockSpec((tr, d), lambda i: (i, 0)),
            pl.BlockSpec((d, P_HEADS * P_KEY_DIM), lambda i: (0, 0)),
            pl.BlockSpec((P_HEADS, 2, P_N_KEYS, P_HALF), lambda i: (0, 0, 0, 0)),
        ],
        out_specs=[hspec, hspec, ospec, ospec],
        out_shape=[jax.ShapeDtypeStruct(hshape, jnp.int32), jax.ShapeDtypeStruct(hshape, jnp.int32),
                   jax.ShapeDtypeStruct(oshape, jnp.int32), jax.ShapeDtypeStruct(oshape, jnp.int32)],
        scratch_shapes=[
            pltpu.VMEM((2 * P_HEADS, tr, P_HALF), BF16),
            pltpu.VMEM((2 * ROUTE_HPT, P_N_KEYS, nl), F32),
            pltpu.VMEM((2, P_HEADS, P_TOPK, nl), F32),
            pltpu.VMEM((P_HEADS, P_N_KEYS, nl), jnp.int32),
            pltpu.VMEM((P_HEADS, P_N_KEYS, nl), F32),
            pltpu.VMEM((2 * ROUTE_HPT, P_N_KEYS, nl), jnp.int32),
        ],
        compiler_params=_cparams(("parallel",)),
        name="peer_route",
    )(h2, wq, sk)


def _peer_kernel(h_ref, u_ref, vt_ref, r2_ref, e2_ref, n1_ref, e1_ref, x_ref, g5_ref, o_ref,
                 acc_ref, w_ref, a_scr, *, blocks_per_batch, ctx_blocks):
    i = pl.program_id(0)
    k = pl.program_id(1)

    @pl.when(k == 0)
    def _():
        acc_ref[...] = jnp.zeros_like(acc_ref)

    nl = P_N_KEYS
    zero = jnp.zeros((P_N_KEYS, nl), BF16)
    ltc = PEER_TC // nl
    njg = PEER_I1 // PEER_JG

    def gate_block(lt, j0):
        g = [zero] * PEER_JG
        for h in range(P_HEADS):
            r2 = pltpu.bitcast(r2_ref[lt, h], BF16)
            e2 = pltpu.bitcast(e2_ref[lt, h], BF16)
            for jj in range(PEER_JG):
                n1row = _unpack_row(n1_ref[lt, h, pl.ds(j0 + jj, 1), :])
                e1row = _unpack_row(e1_ref[lt, h, pl.ds(j0 + jj, 1), :])
                g[jj] = g[jj] + jnp.where(r2 < n1row, e2, zero) * e1row
        for jj in range(PEER_JG):
            rows = pl.ds(pl.multiple_of((j0 + jj) * P_N_KEYS, P_N_KEYS), P_N_KEYS)
            a = a_scr[lt, rows, :]
            half_cdf = (0.5 * lax.erf(a * (2.0 ** -0.5))).astype(BF16) + 0.5
            w_ref[lt, rows, :] = a.astype(BF16) * half_cdf * g[jj]

    for c in range(PEER_TT // PEER_TC):
        tok = slice(c * PEER_TC, (c + 1) * PEER_TC)
        a_t = _dot_nt(u_ref[...], h_ref[tok, :])
        for lc in range(ltc):
            a_scr[c * ltc + lc] = a_t[:, lc * nl:(lc + 1) * nl]

        def body(it, carry, c=c):
            gate_block(c * ltc + it // njg, (it % njg) * PEER_JG)
            return carry

        lax.fori_loop(0, ltc * njg, body, 0)
        w_t = jnp.concatenate([w_ref[c * ltc + lc] for lc in range(ltc)], axis=1)
        acc_ref[:, tok] += _dot(vt_ref[...], w_t)

    @pl.when(k == pl.num_programs(1) - 1)
    def _():
        y = acc_ref[...].T
        for u in range(PEER_TT // TB):
            sblk = i * (PEER_TT // TB) + u
            bidx = sblk // blocks_per_batch
            is_lat = (sblk - bidx * blocks_per_batch) >= ctx_blocks
            gate = g5_ref[2 * bidx + is_lat.astype(jnp.int32)]
            rs = slice(u * TB, (u + 1) * TB)
            o_ref[rs, :] = x_ref[rs, :] + gate * y[rs, :]


def _peer(h2, u_bf, vt_bf, r2, e2, n1, e1, x, g5, blocks_per_batch, ctx_blocks):
    t, d = h2.shape
    ne = u_bf.shape[0]
    nl = P_N_KEYS
    rspec = pl.BlockSpec((PEER_TT // nl, P_HEADS, P_N_KEYS // 2, nl), lambda i, k: (i, 0, 0, 0))
    nspec = pl.BlockSpec((PEER_TT // nl, P_HEADS, PEER_I1, nl), lambda i, k: (i, 0, k, 0))
    return pl.pallas_call(
        functools.partial(_peer_kernel, blocks_per_batch=blocks_per_batch, ctx_blocks=ctx_blocks),
        grid=(t // PEER_TT, ne // PEER_ET),
        in_specs=[
            pl.BlockSpec((PEER_TT, d), lambda i, k: (i, 0)),
            pl.BlockSpec((PEER_ET, d), lambda i, k: (k, 0)),
            pl.BlockSpec((d, PEER_ET), lambda i, k: (0, k)),
            rspec, rspec, nspec, nspec,
            pl.BlockSpec((PEER_TT, d), lambda i, k: (i, 0)),
            pl.BlockSpec(g5.shape, lambda i, k: (0, 0, 0)),
        ],
        out_specs=pl.BlockSpec((PEER_TT, d), lambda i, k: (i, 0)),
        out_shape=jax.ShapeDtypeStruct((t, d), F32),
        scratch_shapes=[pltpu.VMEM((d, PEER_TT), F32),
                        pltpu.VMEM((PEER_TT // nl, PEER_ET, nl), BF16),
                        pltpu.VMEM((PEER_TT // nl, PEER_ET, nl), F32)],
        compiler_params=_cparams(("parallel", "arbitrary")),
        name="peer_experts",
    )(h2, u_bf, vt_bf, r2, e2, n1, e1, x, g5)


def _transpose_cast_kernel(x_ref, o_ref):
    o_ref[...] = x_ref[0].T.astype(BF16)


def _transpose_cast(v_all, layer):
    _, ne, d = v_all.shape
    te = 512
    return pl.pallas_call(
        _transpose_cast_kernel,
        grid=(ne // te,),
        in_specs=[pl.BlockSpec((1, te, d), lambda e: (layer, e, 0))],
        out_specs=pl.BlockSpec((d, te), lambda e: (0, e)),
        out_shape=jax.ShapeDtypeStruct((d, ne), BF16),
        compiler_params=_cparams(("parallel",)),
        name="expert_value_transpose",
    )(v_all)


def _rope_tables(n_ctx, n_lat):
    n_freq = B_ROPE // 4
    pos = np.arange(n_lat)
    inv_freq = ROPE_THETA ** (-np.arange(n_freq, dtype=np.float32) / n_freq)
    inv_freq = jnp.asarray(inv_freq, F32)
    rowp = jnp.asarray(pos // GRID_W, F32)
    colp = jnp.asarray(pos % GRID_W, F32)
    ang = jnp.stack([rowp[:, None] * inv_freq, colp[:, None] * inv_freq], axis=1)
    cos, sin = jnp.cos(ang), jnp.sin(ang)
    cos32 = jnp.concatenate([cos, cos], axis=2).reshape(n_lat, B_ROPE)
    sin32 = jnp.concatenate([-sin, sin], axis=2).reshape(n_lat, B_ROPE)
    pad_l = jnp.ones((n_lat, B_NOPE), F32)
    pad_r = jnp.ones((n_lat, B_HP - B_QK), F32)
    cos_l = jnp.concatenate([pad_l, cos32, pad_r], axis=1)
    sin_l = jnp.concatenate([0 * pad_l, sin32, 0 * pad_r], axis=1)
    cos_t = jnp.concatenate([jnp.ones((n_ctx, B_HP), F32), cos_l], axis=0)
    sin_t = jnp.concatenate([jnp.zeros((n_ctx, B_HP), F32), sin_l], axis=0)
    return cos_t, sin_t


_SWAP32 = np.arange(B_ROPE) ^ (B_ROPE // 4)


def _pad_head(nope, rope):
    z = jnp.zeros(nope.shape[:-1] + (B_HP - B_QK,), nope.dtype)
    out = jnp.concatenate([nope, rope, z], axis=-1)
    return out.reshape(out.shape[:-2] + (out.shape[-2] * B_HP,))


def _layer_weights(layer, w_in, w_out, a_norm_w, a_w_s, a_b_s, b_q_norm_w, b_w_uq, b_kv_norm_w, b_w_ukv,
                   b_q_head_norm_w, b_k_head_norm_w, c_out_norm_w, p_w_q, p_sub_keys):
    d = w_in.shape[1]
    wi = w_in[layer]
    offs = np.cumsum([0, A_W, A_W, B_Q_RANK, B_KV_RANK, B_ROPE, C_W, C_W, C_W, C_W, C_W])
    col = lambda i: wi[:, offs[i]:offs[i + 1]]
    w_kr = col(4)
    zl = jnp.zeros((d, B_NOPE), F32)
    zr = jnp.zeros((d, B_HP - B_QK), F32)
    kr_placed = jnp.concatenate([zl, w_kr, zr], axis=1)
    kr_swapped = jnp.concatenate([zl, w_kr[:, _SWAP32], zr], axis=1)
    w_all = jnp.concatenate(
        [col(0), col(1), col(2), col(3), kr_placed, kr_swapped, col(5), col(6), col(7), col(8), col(9)],
        axis=1).astype(BF16)
    widths = (2 * A_W, B_Q_RANK + B_KV_RANK + 2 * B_HP, 4 * C_W, C_W)

    wuq = b_w_uq[layer].reshape(B_Q_RANK, B_HEADS, B_QK)
    wq_p = _pad_head(wuq[..., :B_NOPE], wuq[..., B_NOPE:]).astype(BF16)
    wq_s = _pad_head(0 * wuq[..., :B_NOPE], wuq[..., B_NOPE:][..., _SWAP32]).astype(BF16)
    wukv = b_w_ukv[layer].reshape(B_KV_RANK, B_HEADS, B_NOPE + B_V)
    wk_p = _pad_head(wukv[..., :B_NOPE], jnp.zeros((B_KV_RANK, B_HEADS, B_ROPE), F32)).astype(BF16)
    wv = wukv[..., B_NOPE:].reshape(B_KV_RANK, B_HEADS * B_V).astype(BF16)
    qn, kn = b_q_head_norm_w[layer], b_k_head_norm_w[layer]
    zpad = jnp.zeros((B_HP - B_QK,), F32)
    z64 = jnp.zeros((B_NOPE,), F32)
    hw = jnp.stack([
        jnp.concatenate([qn, zpad]),
        jnp.concatenate([z64, qn[B_NOPE:][_SWAP32], zpad]),
        jnp.concatenate([kn, zpad]),
        jnp.concatenate([z64, kn[B_NOPE:][_SWAP32], zpad]),
    ] + [jnp.zeros((B_HP,), F32)] * 4, axis=0)

    wo = w_out[layer].astype(BF16)
    return dict(
        w_all=w_all, widths=widths,
        a_nw=a_norm_w[layer].reshape(1, A_W),
        a_ws=a_w_s[layer].astype(BF16),
        a_bias=jnp.repeat(a_b_s[layer].T, A_HD, axis=1),
        qnw=b_q_norm_w[layer].reshape(1, B_Q_RANK), kvnw=b_kv_norm_w[layer].reshape(1, B_KV_RANK),
        wq_p=wq_p, wq_s=wq_s, wk_p=wk_p, wv=wv, hw=hw,
        cnw=jnp.tile(c_out_norm_w[layer], C_HEADS).reshape(1, C_W),
        wo_a=wo[:A_W], wo_b=wo[A_W:A_W + B_HEADS * B_V], wo_c=wo[A_W + B_HEADS * B_V:],
        p_wq=p_w_q[layer].astype(BF16), p_sk=p_sub_keys[layer].astype(BF16),
    )


def kernel(x, c, ctx, c_ctx, ln1_w, ln2_w, w_mod, b_mod, w_in, w_out, a_norm_w, a_w_s, a_b_s, b_q_norm_w,
           b_w_uq, b_kv_norm_w, b_w_ukv, b_q_head_norm_w, b_k_head_norm_w, c_lb_logits, c_out_norm_w,
           p_w_q, p_sub_keys, p_u, p_v):
    bsz, n_lat, d = x.shape
    n_ctx = ctx.shape[1]
    depth = w_in.shape[0]
    n = n_ctx + n_lat
    ctx_blocks = n_ctx // TB
    blocks = n // TB

    mrows = -(-(bsz + 1) // 8) * 8
    cvec = jnp.concatenate([c, c_ctx[None, :], jnp.zeros((mrows - bsz - 1, d), F32)], axis=0)
    mod_all = _modulation(cvec, w_mod, b_mod)

    lb = jnp.cumsum(jax.nn.softmax(c_lb_logits.astype(F32), axis=0), axis=0)
    lb = lb - lb[0:1]
    lbc_all = jnp.stack([jnp.log(lb), jnp.log1p(-lb), 1.0 - lb] + [jnp.zeros_like(lb)] * 5, axis=2)

    cos_t, sin_t = _rope_tables(n_ctx, n_lat)
    xc = jnp.concatenate([ctx, x], axis=1)

    for layer in range(depth):
        last = layer == depth - 1
        w = _layer_weights(layer, w_in, w_out, a_norm_w, a_w_s, a_b_s, b_q_norm_w, b_w_uq, b_kv_norm_w,
                           b_w_ukv, b_q_head_norm_w, b_k_head_norm_w, c_out_norm_w, p_w_q, p_sub_keys)
        mod_b = mod_all[layer, :bsz]
        mod_c = jnp.broadcast_to(mod_all[layer, bsz][None, :], mod_b.shape)
        modsel = jnp.stack([mod_c, mod_b], axis=1)[:, :, None, :]
        first_block = ctx_blocks if last else 0

        oa, ob, oc, og = _inproj(xc, ln1_w[layer].reshape(1, d), modsel, w["w_all"], w["widths"])
        a_out = _amix(oa, w["a_nw"], w["a_ws"], w["a_bias"], first_block * (TB // A_CHUNK))
        q, k, v = _mla_prep(ob, cos_t, sin_t, w["qnw"], w["kvnw"], w["wq_p"], w["wq_s"], w["wk_p"], w["wv"], w["hw"])
        b_lat = _attention(q, k, v, ctx_blocks, blocks - ctx_blocks, n, n_lat, 2)
        if last:
            b_out = b_lat
        else:
            b_ctx = _attention(q, k, v, 0, ctx_blocks, n_ctx, n_ctx, 1)
            b_out = jnp.concatenate([b_ctx, b_lat], axis=1)
        o_f, o_b = _hgrn(oc, lbc_all[layer], n_ctx)
        x_new, h2 = _outproj(xc, a_out, b_out, o_f, o_b, og, w["cnw"], w["wo_a"], w["wo_b"], w["wo_c"],
                             modsel, ln2_w[layer].reshape(1, d), first_block)

        t = x_new.shape[0] * x_new.shape[1]
        h2f = h2.reshape(t, d)
        r2, e2, n1, e1 = _route(h2f, w["p_wq"], w["p_sk"])
        g5 = modsel[:, :, :, 5 * d:6 * d].reshape(2 * bsz, 1, d)
        u_bf = p_u[layer].astype(BF16)
        vt_bf = _transpose_cast(p_v, layer)
        out = _peer(h2f, u_bf, vt_bf, r2, e2, n1, e1, x_new.reshape(t, d), g5,
                    blocks - first_block, ctx_blocks - first_block)
        xc = out.reshape(bsz, t // bsz, d)
    return xc
```

```python
import functools
import math

import jax
import jax.numpy as jnp
import numpy as np
from jax import lax
from jax.experimental import pallas as pl
from jax.experimental.pallas import tpu as pltpu

F32 = jnp.float32
BF16 = jnp.bfloat16
HIGHEST = lax.Precision.HIGHEST

EPS = 1e-6
GRID_W = 64
ROPE_THETA = 10000.0

A_HEADS, A_HD, A_CHUNK = 4, 64, 128
A_W = A_HEADS * A_HD
B_HEADS, B_NOPE, B_ROPE, B_V = 8, 64, 32, 64
B_QK = B_NOPE + B_ROPE
B_HP = 128
B_Q_RANK, B_KV_RANK = 256, 128
C_HEADS, C_DK, C_DV, C_CHUNK = 4, 64, 64, 64
C_W = C_HEADS * C_DK
C_SUB = 16
P_HEADS, P_KEY_DIM, P_N_KEYS, P_TOPK = 8, 256, 128, 16
P_HALF = P_KEY_DIM // 2

TB = 256
PEER_TT = 1024
PEER_TC = 512
PEER_ET = 1024
PEER_I1 = PEER_ET // P_N_KEYS
PEER_JG = 4
ROUTE_HPT = 8
VMEM_LIMIT = 56 * 1024 * 1024


def _cparams(sem, flags=None):
    return pltpu.CompilerParams(dimension_semantics=sem, vmem_limit_bytes=VMEM_LIMIT, flags=flags)


def _dot_nt(a, b):
    return lax.dot_general(a, b, (((1,), (1,)), ((), ())), preferred_element_type=F32)


def _dot_tn(a, b):
    return lax.dot_general(a, b, (((0,), (0,)), ((), ())), preferred_element_type=F32)


def _dot(a, b):
    return jnp.dot(a, b, preferred_element_type=F32)


def _sigmoid(x):
    return 1.0 / (1.0 + jnp.exp(-x))


def _block_ones(n, blk, dtype):
    r = lax.broadcasted_iota(jnp.int32, (n, n), 0) // blk
    c = lax.broadcasted_iota(jnp.int32, (n, n), 1) // blk
    return (r == c).astype(dtype)


def _mod_kernel(c_ref, w_ref, b_ref, o_ref):
    c = c_ref[...]
    sc = c * _sigmoid(c)
    o_ref[0] = _dot(sc.astype(BF16), w_ref[0].astype(BF16)) + b_ref[0]


def _modulation(cvec, w_mod, b_mod):
    depth, d, n6 = w_mod.shape
    rows = cvec.shape[0]
    tn = 1024
    return pl.pallas_call(
        _mod_kernel,
        grid=(depth, n6 // tn),
        in_specs=[
            pl.BlockSpec((rows, d), lambda l, n: (0, 0)),
            pl.BlockSpec((1, d, tn), lambda l, n: (l, 0, n)),
            pl.BlockSpec((1, 1, tn), lambda l, n: (l, 0, n)),
        ],
        out_specs=pl.BlockSpec((1, rows, tn), lambda l, n: (l, 0, n)),
        out_shape=jax.ShapeDtypeStruct((depth, rows, n6), F32),
        compiler_params=_cparams(("parallel", "parallel")),
        name="adaln_mod",
    )(cvec, w_mod, b_mod.reshape(depth, 1, n6))


def _inproj_kernel(x_ref, lnw_ref, mod_ref, w_ref, oa_ref, ob_ref, oc_ref, og_ref, *, d):
    x = x_ref[0]
    ms = jnp.mean(x * x, axis=-1, keepdims=True)
    y = x * lax.rsqrt(ms + EPS) * lnw_ref[...]
    shift = mod_ref[0, 0, :, 0:d]
    scale = mod_ref[0, 0, :, d:2 * d]
    h = (y * (1.0 + scale) + shift).astype(BF16)
    p = _dot(h, w_ref[...])
    na = oa_ref.shape[-1]
    nb = ob_ref.shape[-1]
    nc = oc_ref.shape[-1]
    oa_ref[0] = p[:, 0:na]
    ob_ref[0] = p[:, na:na + nb]
    oc_ref[0] = p[:, na + nb:na + nb + nc]
    og_ref[0] = p[:, na + nb + nc:]


def _inproj(x, lnw, modsel, w_all, widths):
    bsz, n, d = x.shape
    na, nb, nc, ng = widths
    nout = w_all.shape[1]
    return pl.pallas_call(
        functools.partial(_inproj_kernel, d=d),
        grid=(bsz, n // TB),
        in_specs=[
            pl.BlockSpec((1, TB, d), lambda b, j: (b, j, 0)),
            pl.BlockSpec((1, d), lambda b, j: (0, 0)),
            pl.BlockSpec((1, 1, 1, modsel.shape[-1]), lambda b, j: (b, jnp.minimum(j, 1), 0, 0)),
            pl.BlockSpec((d, nout), lambda b, j: (0, 0)),
        ],
        out_specs=[
            pl.BlockSpec((1, TB, na), lambda b, j: (b, j, 0)),
            pl.BlockSpec((1, TB, nb), lambda b, j: (b, j, 0)),
            pl.BlockSpec((1, TB, nc), lambda b, j: (b, j, 0)),
            pl.BlockSpec((1, TB, ng), lambda b, j: (b, j, 0)),
        ],
        out_shape=[
            jax.ShapeDtypeStruct((bsz, n, na), F32),
            jax.ShapeDtypeStruct((bsz, n, nb), F32),
            jax.ShapeDtypeStruct((bsz, n, nc), F32),
            jax.ShapeDtypeStruct((bsz, n, ng), F32),
        ],
        compiler_params=_cparams(("parallel", "parallel")),
        name="in_proj",
    )(x, lnw, modsel, w_all)


def _amix_kernel(a_ref, nw_ref, ws_ref, bias_ref, o_ref):
    u = a_ref[0, :, 0:A_W]
    v = a_ref[0, :, A_W:2 * A_W]
    ssq = jnp.dot(v * v, _block_ones(A_W, A_HD, F32), precision=HIGHEST, preferred_element_type=F32)
    vn = v * lax.rsqrt(ssq * (1.0 / A_HD) + EPS) * nw_ref[...]
    lane_head = lax.broadcasted_iota(jnp.int32, vn.shape, 1) // A_HD
    acc = bias_ref[...]
    for h in range(A_HEADS):
        vm = jnp.where(lane_head == h, vn, 0.0).astype(BF16)
        acc = acc + _dot(ws_ref[h], vm)
    o_ref[0] = (u * acc).astype(BF16)


def _amix(oa, nw, ws, bias, first_chunk):
    bsz, n, _ = oa.shape
    nchunk = n // A_CHUNK - first_chunk
    return pl.pallas_call(
        _amix_kernel,
        grid=(bsz, nchunk),
        in_specs=[
            pl.BlockSpec((1, A_CHUNK, 2 * A_W), lambda b, j: (b, j + first_chunk, 0)),
            pl.BlockSpec((1, A_W), lambda b, j: (0, 0)),
            pl.BlockSpec((A_HEADS, A_CHUNK, A_CHUNK), lambda b, j: (0, 0, 0)),
            pl.BlockSpec((A_CHUNK, A_W), lambda b, j: (0, 0)),
        ],
        out_specs=pl.BlockSpec((1, A_CHUNK, A_W), lambda b, j: (b, j + first_chunk, 0)),
        out_shape=jax.ShapeDtypeStruct((bsz, n, A_W), BF16),
        compiler_params=_cparams(("parallel", "parallel")),
        name="mixer_a",
    )(oa, nw, ws, bias)


def _mla_prep_kernel(ob_ref, cos_ref, sin_ref, qnw_ref, kvnw_ref, wq_ref, wqs_ref, wk_ref, wv_ref,
                     hw_ref, q_ref, k_ref, v_ref):
    cq = ob_ref[0, :, 0:B_Q_RANK]
    ckv = ob_ref[0, :, B_Q_RANK:B_Q_RANK + B_KV_RANK]
    krp = ob_ref[0, :, B_Q_RANK + B_KV_RANK:B_Q_RANK + B_KV_RANK + B_HP]
    krs = ob_ref[0, :, B_Q_RANK + B_KV_RANK + B_HP:B_Q_RANK + B_KV_RANK + 2 * B_HP]
    cos = cos_ref[...]
    sin = sin_ref[...]
    cqn = (cq * lax.rsqrt(jnp.mean(cq * cq, axis=-1, keepdims=True) + EPS) * qnw_ref[...]).astype(BF16)
    ckn = (ckv * lax.rsqrt(jnp.mean(ckv * ckv, axis=-1, keepdims=True) + EPS) * kvnw_ref[...]).astype(BF16)
    q_raw = _dot(cqn, wq_ref[...])
    q_swp = _dot(cqn, wqs_ref[...])
    k_raw = _dot(ckn, wk_ref[...])
    v_all = _dot(ckn, wv_ref[...])
    qw, qws, kw, kws = hw_ref[0:1, :], hw_ref[1:2, :], hw_ref[2:3, :], hw_ref[3:4, :]
    k_rot_sw = krs * kws * sin
    for h in range(B_HEADS):
        sl = slice(h * B_HP, (h + 1) * B_HP)
        qh = q_raw[:, sl]
        rq = lax.rsqrt(jnp.sum(qh * qh, axis=-1, keepdims=True) * (1.0 / B_QK) + EPS)
        q_ref[0, h] = (rq * (qh * qw * cos + q_swp[:, sl] * qws * sin)).astype(BF16)
        kh = k_raw[:, sl] + krp
        rk = lax.rsqrt(jnp.sum(kh * kh, axis=-1, keepdims=True) * (1.0 / B_QK) + EPS)
        k_ref[0, h] = (rk * (kh * kw * cos + k_rot_sw)).astype(BF16)
        v_ref[0, h] = v_all[:, h * B_V:(h + 1) * B_V].astype(BF16)


def _mla_prep(ob, cos_t, sin_t, qnw, kvnw, wq, wqs, wk, wv, hw):
    bsz, n, nb = ob.shape
    full = lambda *s: pl.BlockSpec(s, lambda b, j: (0,) * len(s))
    return pl.pallas_call(
        _mla_prep_kernel,
        grid=(bsz, n // TB),
        in_specs=[
            pl.BlockSpec((1, TB, nb), lambda b, j: (b, j, 0)),
            pl.BlockSpec((TB, B_HP), lambda b, j: (j, 0)),
            pl.BlockSpec((TB, B_HP), lambda b, j: (j, 0)),
            full(1, B_Q_RANK), full(1, B_KV_RANK),
            full(B_Q_RANK, B_HEADS * B_HP), full(B_Q_RANK, B_HEADS * B_HP),
            full(B_KV_RANK, B_HEADS * B_HP), full(B_KV_RANK, B_HEADS * B_V),
            full(8, B_HP),
        ],
        out_specs=[
            pl.BlockSpec((1, B_HEADS, TB, B_HP), lambda b, j: (b, 0, j, 0)),
            pl.BlockSpec((1, B_HEADS, TB, B_HP), lambda b, j: (b, 0, j, 0)),
            pl.BlockSpec((1, B_HEADS, TB, B_V), lambda b, j: (b, 0, j, 0)),
        ],
        out_shape=[
            jax.ShapeDtypeStruct((bsz, B_HEADS, n, B_HP), BF16),
            jax.ShapeDtypeStruct((bsz, B_HEADS, n, B_HP), BF16),
            jax.ShapeDtypeStruct((bsz, B_HEADS, n, B_V), BF16),
        ],
        compiler_params=_cparams(("parallel", "parallel")),
        name="mla_prep",
    )(ob, cos_t, sin_t, qnw, kvnw, wq, wqs, wk, wv, hw)


def _attn_kernel(*refs, nqb):
    q_refs, (k_ref, v_ref, o_ref, o_scr) = refs[:nqb], refs[nqb:]
    scale = (B_QK ** -0.5) * math.log2(math.e)
    for h in range(B_HEADS):
        q = q_refs[0][0, h] if nqb == 1 else jnp.concatenate([r[0, h] for r in q_refs], axis=0)
        s = _dot_nt(q, k_ref[0, h])
        m = jnp.max(s, axis=-1, keepdims=True)
        p = jnp.exp2((s - m) * scale)
        l = jnp.sum(p, axis=-1, keepdims=True)
        o = _dot(p.astype(BF16), v_ref[0, h])
        o_scr[:, h * B_V:(h + 1) * B_V] = o / l
    o_ref[0] = o_scr[...].astype(BF16)


def _attention(q, k, v, first_qblock, n_qblocks, n_keys, out_rows, nqb):
    bsz = q.shape[0]
    qspec = lambda u: pl.BlockSpec((1, B_HEADS, TB, B_HP), lambda b, j: (b, 0, j * nqb + u + first_qblock, 0))
    return pl.pallas_call(
        functools.partial(_attn_kernel, nqb=nqb),
        grid=(bsz, n_qblocks // nqb),
        in_specs=[qspec(u) for u in range(nqb)] + [
            pl.BlockSpec((1, B_HEADS, n_keys, B_HP), lambda b, j: (b, 0, 0, 0)),
            pl.BlockSpec((1, B_HEADS, n_keys, B_V), lambda b, j: (b, 0, 0, 0)),
        ],
        out_specs=pl.BlockSpec((1, nqb * TB, B_HEADS * B_V), lambda b, j: (b, j, 0)),
        out_shape=jax.ShapeDtypeStruct((bsz, out_rows, B_HEADS * B_V), BF16),
        scratch_shapes=[pltpu.VMEM((nqb * TB, B_HEADS * B_V), F32)],
        compiler_params=_cparams(("parallel", "arbitrary")),
        name="mla_attention",
    )(*([q] * nqb), k, v)


def _hgrn_chunk(blk, zcol, lbc, st_ref, rev):
    cc, w = C_CHUNK, C_W
    q = blk[:, 0:w] * (C_DK ** -0.5)
    z = blk[:, zcol * w:(zcol + 1) * w]
    v = blk[:, 3 * w:4 * w]
    log_lb, log1m_lb, one_m_lb = lbc[0:1, :], lbc[1:2, :], lbc[2:3, :]
    az = jnp.abs(z)
    sp = jnp.log1p(jnp.exp(-az))
    lsig = jnp.minimum(z, 0.0) - sp
    t2 = log1m_lb + lsig
    mx = jnp.maximum(log_lb, t2)
    mn = jnp.minimum(log_lb, t2)
    logf = mx + jnp.log1p(jnp.exp(mn - mx))
    kk = one_m_lb * _sigmoid(-z)

    ti = lax.broadcasted_iota(jnp.int32, (cc, cc), 0)
    ui = lax.broadcasted_iota(jnp.int32, (cc, cc), 1)
    tri = ((ui >= ti) if rev else (ui <= ti)).astype(F32)
    b = jnp.dot(tri, logf, precision=HIGHEST, preferred_element_type=F32)
    b_tot = b[0:1, :] if rev else b[cc - 1:cc, :]

    row = lax.broadcasted_iota(jnp.int32, (cc, w), 0)
    lane_head = lax.broadcasted_iota(jnp.int32, (cc, w), 1) // C_DK
    nsub = cc // C_SUB
    row_blk = row // C_SUB

    beta_rows = []
    for i in range(nsub):
        if rev:
            src = None if i == nsub - 1 else b[(i + 1) * C_SUB:(i + 1) * C_SUB + 1, :]
        else:
            src = None if i == 0 else b[i * C_SUB - 1:i * C_SUB, :]
        beta_rows.append(src)
    beta_full = jnp.concatenate(
        [jnp.broadcast_to(b[i * C_SUB:i * C_SUB + 1, :] if r is None else r, (C_SUB, w))
         for i, r in enumerate(beta_rows)], axis=0)
    has_prev = (row_blk < nsub - 1) if rev else (row_blk > 0)
    qs = jnp.where(has_prev, q * jnp.exp(b - beta_full), 0.0)

    q_stack = jnp.concatenate([jnp.where(lane_head == h, qs, 0.0) for h in range(C_HEADS)], axis=0).astype(BF16)
    qblocks = [i for i in range(nsub) if beta_rows[i] is not None]
    ks_parts = []
    for i in qblocks:
        prev = (row_blk > i) if rev else (row_blk < i)
        ks_parts.append(jnp.where(prev, kk * jnp.exp(beta_rows[i] - b), 0.0))
    ks_all = jnp.concatenate(ks_parts, axis=0).astype(BF16)
    a_all = _dot_nt(q_stack, ks_all)
    ar = lax.broadcasted_iota(jnp.int32, a_all.shape, 0)
    ac = lax.broadcasted_iota(jnp.int32, a_all.shape, 1)
    r_blk = (ar % cc) // C_SUB
    c_blk = ac // cc + (0 if rev else 1)
    a_all = jnp.where(r_blk == c_blk, a_all, 0.0).astype(BF16)
    v_bf = v.astype(BF16)
    r_all = _dot(a_all, jnp.concatenate([v_bf] * len(qblocks), axis=0))
    o = jnp.zeros((cc, w), F32)
    for h in range(C_HEADS):
        o = o + jnp.where(lane_head == h, r_all[h * cc:(h + 1) * cc, :], 0.0)

    ones_bd = _block_ones(w, C_DK, BF16)
    tsub = lax.broadcasted_iota(jnp.int32, (C_SUB, w), 0)
    diag_parts = []
    for i in range(nsub):
        r0 = i * C_SUB
        bb = b[r0:r0 + C_SUB, :]
        qq = q[r0:r0 + C_SUB, :]
        ps = []
        for s in range(C_SUB):
            keep = (tsub <= s) if rev else (tsub >= s)
            e = jnp.where(keep, jnp.exp(bb - b[r0 + s:r0 + s + 1, :]), 0.0)
            ps.append(qq * e * kk[r0 + s:r0 + s + 1, :])
        red = _dot(jnp.concatenate(ps, axis=0).astype(BF16), ones_bd)
        od = jnp.zeros((C_SUB, w), F32)
        for s in range(C_SUB):
            od = od + red[s * C_SUB:(s + 1) * C_SUB, :] * v[r0 + s:r0 + s + 1, :]
        diag_parts.append(od)
    o = o + jnp.concatenate(diag_parts, axis=0)

    st = st_ref[...]
    o = o + _dot_nt((q * jnp.exp(b)).astype(BF16), st.astype(BF16))
    kd = (kk * jnp.exp(b_tot - b)).astype(BF16)
    upd = _dot_tn(v_bf, kd)
    st_ref[...] = st * jnp.exp(b_tot) + upd * _block_ones(w, C_DK, F32)
    return o


def _hgrn_kernel(cf_ref, cb_ref, lbc_ref, of_ref, ob_ref, sf_ref, sb_ref):
    @pl.when(pl.program_id(1) == 0)
    def _():
        sf_ref[...] = jnp.zeros_like(sf_ref)
        sb_ref[...] = jnp.zeros_like(sb_ref)

    of_ref[0] = _hgrn_chunk(cf_ref[0], 1, lbc_ref[0], sf_ref, rev=False)
    ob_ref[0] = _hgrn_chunk(cb_ref[0], 2, lbc_ref[1], sb_ref, rev=True)


def _hgrn(oc, lbc, n_ctx):
    bsz, n, wc = oc.shape
    nch = n // C_CHUNK
    nctx = n_ctx // C_CHUNK

    def bwd_idx(c):
        return jnp.where(c < nctx, nctx - 1 - c, nch + nctx - 1 - c)

    return pl.pallas_call(
        _hgrn_kernel,
        grid=(bsz, nch),
        in_specs=[
            pl.BlockSpec((1, C_CHUNK, wc), lambda b, c: (b, c, 0)),
            pl.BlockSpec((1, C_CHUNK, wc), lambda b, c: (b, bwd_idx(c), 0)),
            pl.BlockSpec((2, 8, C_W), lambda b, c: (0, 0, 0)),
        ],
        out_specs=[
            pl.BlockSpec((1, C_CHUNK, C_W), lambda b, c: (b, c, 0)),
            pl.BlockSpec((1, C_CHUNK, C_W), lambda b, c: (b, bwd_idx(c), 0)),
        ],
        out_shape=[jax.ShapeDtypeStruct((bsz, n, C_W), F32)] * 2,
        scratch_shapes=[pltpu.VMEM((C_W, C_W), F32)] * 2,
        compiler_params=_cparams(("parallel", "arbitrary")),
        name="hgrn2_scan",
    )(oc, oc, lbc)


def _outproj_kernel(x_ref, a_ref, b_ref, of_ref, ob_ref, g_ref, cnw_ref, wa_ref, wb_ref, wc_ref,
                    mod_ref, ln2_ref, xo_ref, h2_ref, *, d):
    o = of_ref[0] + ob_ref[0]
    ssq = jnp.dot(o * o, _block_ones(C_W, C_DV, F32), precision=HIGHEST, preferred_element_type=F32)
    g = g_ref[0]
    c_out = o * lax.rsqrt(ssq * (1.0 / C_DV) + EPS) * cnw_ref[...] * (g * _sigmoid(g))
    mix = _dot(a_ref[0], wa_ref[...]) + _dot(b_ref[0], wb_ref[...]) + _dot(c_out.astype(BF16), wc_ref[...])
    gate1 = mod_ref[0, 0, :, 2 * d:3 * d]
    shift2 = mod_ref[0, 0, :, 3 * d:4 * d]
    scale2 = mod_ref[0, 0, :, 4 * d:5 * d]
    x = x_ref[0] + gate1 * mix
    xo_ref[0] = x
    y = x * lax.rsqrt(jnp.mean(x * x, axis=-1, keepdims=True) + EPS) * ln2_ref[...]
    h2_ref[0] = (y * (1.0 + scale2) + shift2).astype(BF16)


def _outproj(x, a_out, b_out, o_f, o_b, og, cnw, wa, wb, wc, modsel, ln2, first_block):
    bsz, n, d = x.shape
    nblk = n // TB - first_block
    full = lambda *s: pl.BlockSpec(s, lambda b, j: (0,) * len(s))
    tok = lambda w: pl.BlockSpec((1, TB, w), lambda b, j: (b, j + first_block, 0))
    return pl.pallas_call(
        functools.partial(_outproj_kernel, d=d),
        grid=(bsz, nblk),
        in_specs=[
            tok(d), tok(A_W),
            pl.BlockSpec((1, TB, B_HEADS * B_V), lambda b, j: (b, j, 0)) if first_block else tok(B_HEADS * B_V),
            tok(C_W), tok(C_W), tok(C_W),
            full(1, C_W), full(A_W, d), full(B_HEADS * B_V, d), full(C_W, d),
            pl.BlockSpec((1, 1, 1, modsel.shape[-1]), lambda b, j: (b, jnp.minimum(j + first_block, 1), 0, 0)),
            full(1, d),
        ],
        out_specs=[
            pl.BlockSpec((1, TB, d), lambda b, j: (b, j, 0)),
            pl.BlockSpec((1, TB, d), lambda b, j: (b, j, 0)),
        ],
        out_shape=[
            jax.ShapeDtypeStruct((bsz, nblk * TB, d), F32),
            jax.ShapeDtypeStruct((bsz, nblk * TB, d), BF16),
        ],
        compiler_params=_cparams(("parallel", "parallel")),
        name="out_proj",
    )(x, a_out, b_out, o_f, o_b, og, cnw, wa, wb, wc, modsel, ln2)


def _top16_exact(s):
    nrows = s.shape[0]
    iota = lax.broadcasted_iota(jnp.int32, s.shape, 0).astype(F32)
    rank = jnp.full(s.shape, P_TOPK, jnp.int32)
    vals = []
    for r in range(P_TOPK):
        m = jnp.max(s, axis=0, keepdims=True)
        idx = jnp.min(jnp.where(s == m, iota, float(nrows)), axis=0, keepdims=True)
        hit = iota == idx
        rank = jnp.where(hit, r, rank)
        s = jnp.where(hit, -jnp.inf, s)
        vals.append(m)
    return jnp.concatenate(vals, axis=0), rank


_MARK0 = int(np.array(0xFF7FFFFF, np.uint32).view(np.int32))


def _top16_marked(s):
    vals = []
    for r in range(P_TOPK):
        m = jnp.max(s, axis=0, keepdims=True)
        mark = float(np.array(_MARK0 - r, np.int32).view(np.float32))
        s = jnp.where(s == m, mark, s)
        vals.append(m)
    rr = _MARK0 - pltpu.bitcast(s, jnp.int32)
    rank = jnp.where(rr < 0, P_TOPK, jnp.where(rr > P_TOPK - 1, P_TOPK, rr))
    slack = jnp.sum(P_TOPK - rank, axis=0, keepdims=True) - (P_TOPK * (P_TOPK + 1)) // 2
    return jnp.concatenate(vals, axis=0), rank, slack


def _int_bits(x):
    return lax.shift_right_logical(pltpu.bitcast(x, jnp.int32), 16)


def _gate_bits(x):
    return lax.shift_right_logical(pltpu.bitcast(x, jnp.int32) + 0x8000, 16)


def _dup(b):
    return b | lax.shift_left(b, 16)


def _pair_rows(b, scr):
    half = b.shape[0] // 2
    scr[...] = b
    return scr[pl.ds(0, half, stride=2), :] | lax.shift_left(scr[pl.ds(1, half, stride=2), :], 16)


def _unpack_row(row):
    n = row.shape[-1]
    tile = pltpu.bitcast(jnp.broadcast_to(row, (8, n)), BF16)
    return jnp.concatenate([tile] * (P_N_KEYS // 16), axis=0)


def _route_kernel(h_ref, wq_ref, sk_ref, r2_ref, e2_ref, n1_ref, e1_ref,
                  q_scr, s_scr, v_scr, r1_scr, e1_scr, pk_scr):
    nl = P_N_KEYS
    q = _dot(h_ref[...], wq_ref[...]).astype(BF16)
    for l in range(2 * P_HEADS):
        q_scr[l] = q[:, l * P_HALF:(l + 1) * P_HALF]

    def lane_tile(lt, carry):
        row0 = pl.multiple_of(lt * nl, nl)

        def put_rank(h, p, rank, scr):
            if p == 0:
                r1_scr[h] = rank
            else:
                r2_ref[lt, h] = _pair_rows(_int_bits(rank.astype(F32)), scr)

        def heads(hp, c):
            bad = jnp.zeros((1, nl), jnp.int32)
            for u in range(2 * ROUTE_HPT):
                h, p = hp * ROUTE_HPT + u // 2, u % 2
                s = _dot_nt(sk_ref[h, p], q_scr[2 * h + p, pl.ds(row0, nl), :])
                s_scr[u] = s
                v, rank, slack = _top16_marked(s)
                v_scr[p, h] = v
                put_rank(h, p, rank, pk_scr.at[u])
                e = jnp.exp(s - v[0:1, :])
                if p == 0:
                    e1_scr[h] = e
                else:
                    e2_ref[lt, h] = _pair_rows(_gate_bits(e), pk_scr.at[u - 1])
                bad = bad + slack

            @pl.when(jnp.max(bad) > 0)
            def _():
                for u in range(2 * ROUTE_HPT):
                    h, p = hp * ROUTE_HPT + u // 2, u % 2
                    ve, re = _top16_exact(s_scr[u])
                    v_scr[p, h] = ve
                    put_rank(h, p, re, pk_scr.at[u])

            return c

        lax.fori_loop(0, P_HEADS // ROUTE_HPT, heads, 0)

        v1 = v_scr[0]
        v2 = v_scr[1]
        ia = lax.broadcasted_iota(jnp.int32, v1.shape, 1).astype(F32)
        n = jnp.zeros(v1.shape, F32)
        g = jnp.broadcast_to(v2[:, 0:1, :], v1.shape)
        cmax = v1[:, 0:1, :] + v2[:, 0:1, :]
        z = jnp.zeros(cmax.shape, F32)
        for _ in range(P_TOPK):
            f = v1 + g
            m = jnp.max(f, axis=1, keepdims=True)
            a_star = jnp.min(jnp.where(f == m, ia, float(P_TOPK)), axis=1, keepdims=True)
            hit = ia == a_star
            n = n + jnp.where(hit, 1.0, 0.0)
            nsel = jnp.sum(jnp.where(hit, n, 0.0), axis=1, keepdims=True)
            nxt = jnp.sum(jnp.where(ia == nsel, v2, 0.0), axis=1, keepdims=True)
            nxt = jnp.where(nsel > P_TOPK - 0.5, -jnp.inf, nxt)
            g = jnp.where(hit, nxt, g)
            z = z + jnp.exp(m - cmax)
        zinv = 1.0 / z
        nw = _dup(_int_bits(n))
        for h in range(P_HEADS):
            rank1 = r1_scr[h]
            n1w = jnp.zeros(rank1.shape, jnp.int32)
            for a in range(P_TOPK):
                n1w = jnp.where(rank1 == a, nw[h, a:a + 1, :], n1w)
            n1_ref[lt, h] = n1w
            e1_ref[lt, h] = _dup(_gate_bits(e1_scr[h] * zinv[h]))
        return carry

    lax.fori_loop(0, h_ref.shape[0] // nl, lane_tile, 0)


def _route(h2, wq, sk):
    t, d = h2.shape
    tr = PEER_TT
    nl = P_N_KEYS
    oshape = (t // nl, P_HEADS, P_N_KEYS, nl)
    hshape = (t // nl, P_HEADS, P_N_KEYS // 2, nl)
    ospec = pl.BlockSpec((tr // nl, P_HEADS, P_N_KEYS, nl), lambda i: (i, 0, 0, 0))
    hspec = pl.BlockSpec((tr // nl, P_HEADS, P_N_KEYS // 2, nl), lambda i: (i, 0, 0, 0))
    return pl.pallas_call(
        _route_kernel,
        grid=(t // tr,),
        in_specs=[
            pl.BlockSpec((tr, d), lambda i: (i, 0)),
            pl.BlockSpec((d, P_HEADS * P_KEY_DIM), lambda i: (0, 0)),
            pl.BlockSpec((P_HEADS, 2, P_N_KEYS, P_HALF), lambda i: (0, 0, 0, 0)),
        ],
        out_specs=[hspec, hspec, ospec, ospec],
        out_shape=[jax.ShapeDtypeStruct(hshape, jnp.int32), jax.ShapeDtypeStruct(hshape, jnp.int32),
                   jax.ShapeDtypeStruct(oshape, jnp.int32), jax.ShapeDtypeStruct(oshape, jnp.int32)],
        scratch_shapes=[
            pltpu.VMEM((2 * P_HEADS, tr, P_HALF), BF16),
            pltpu.VMEM((2 * ROUTE_HPT, P_N_KEYS, nl), F32),
            pltpu.VMEM((2, P_HEADS, P_TOPK, nl), F32),
            pltpu.VMEM((P_HEADS, P_N_KEYS, nl), jnp.int32),
            pltpu.VMEM((P_HEADS, P_N_KEYS, nl), F32),
            pltpu.VMEM((2 * ROUTE_HPT, P_N_KEYS, nl), jnp.int32),
        ],
        compiler_params=_cparams(("parallel",)),
        name="peer_route",
    )(h2, wq, sk)


def _peer_kernel(h_ref, u_ref, vt_ref, r2_ref, e2_ref, n1_ref, e1_ref, x_ref, g5_ref, o_ref,
                 acc_ref, w_ref, a_scr, *, blocks_per_batch, ctx_blocks):
    i = pl.program_id(0)
    k = pl.program_id(1)

    @pl.when(k == 0)
    def _():
        acc_ref[...] = jnp.zeros_like(acc_ref)

    nl = P_N_KEYS
    zero = jnp.zeros((P_N_KEYS, nl), BF16)
    ltc = PEER_TC // nl
    njg = PEER_I1 // PEER_JG

    def gate_block(lt, j0):
        g = [zero] * PEER_JG
        for h in range(P_HEADS):
            r2 = pltpu.bitcast(r2_ref[lt, h], BF16)
            e2 = pltpu.bitcast(e2_ref[lt, h], BF16)
            for jj in range(PEER_JG):
                n1row = _unpack_row(n1_ref[lt, h, pl.ds(j0 + jj, 1), :])
                e1row = _unpack_row(e1_ref[lt, h, pl.ds(j0 + jj, 1), :])
                g[jj] = g[jj] + jnp.where(r2 < n1row, e2, zero) * e1row
        for jj in range(PEER_JG):
            rows = pl.ds(pl.multiple_of((j0 + jj) * P_N_KEYS, P_N_KEYS), P_N_KEYS)
            a = a_scr[lt, rows, :]
            half_cdf = (0.5 * lax.erf(a * (2.0 ** -0.5))).astype(BF16) + 0.5
            w_ref[lt, rows, :] = a.astype(BF16) * half_cdf * g[jj]

    for c in range(PEER_TT // PEER_TC):
        tok = slice(c * PEER_TC, (c + 1) * PEER_TC)
        a_t = _dot_nt(u_ref[...], h_ref[tok, :])
        for lc in range(ltc):
            a_scr[c * ltc + lc] = a_t[:, lc * nl:(lc + 1) * nl]

        def body(it, carry, c=c):
            gate_block(c * ltc + it // njg, (it % njg) * PEER_JG)
            return carry

        lax.fori_loop(0, ltc * njg, body, 0)
        w_t = jnp.concatenate([w_ref[c * ltc + lc] for lc in range(ltc)], axis=1)
        acc_ref[:, tok] += _dot(vt_ref[...], w_t)

    @pl.when(k == pl.num_programs(1) - 1)
    def _():
        y = acc_ref[...].T
        for u in range(PEER_TT // TB):
            sblk = i * (PEER_TT // TB) + u
            bidx = sblk // blocks_per_batch
            is_lat = (sblk - bidx * blocks_per_batch) >= ctx_blocks
            gate = g5_ref[2 * bidx + is_lat.astype(jnp.int32)]
            rs = slice(u * TB, (u + 1) * TB)
            o_ref[rs, :] = x_ref[rs, :] + gate * y[rs, :]


def _peer(h2, u_bf, vt_bf, r2, e2, n1, e1, x, g5, blocks_per_batch, ctx_blocks):
    t, d = h2.shape
    ne = u_bf.shape[0]
    nl = P_N_KEYS
    rspec = pl.BlockSpec((PEER_TT // nl, P_HEADS, P_N_KEYS // 2, nl), lambda i, k: (i, 0, 0, 0))
    nspec = pl.BlockSpec((PEER_TT // nl, P_HEADS, PEER_I1, nl), lambda i, k: (i, 0, k, 0))
    return pl.pallas_call(
        functools.partial(_peer_kernel, blocks_per_batch=blocks_per_batch, ctx_blocks=ctx_blocks),
        grid=(t // PEER_TT, ne // PEER_ET),
        in_specs=[
            pl.BlockSpec((PEER_TT, d), lambda i, k: (i, 0)),
            pl.BlockSpec((PEER_ET, d), lambda i, k: (k, 0)),
            pl.BlockSpec((d, PEER_ET), lambda i, k: (0, k)),
            rspec, rspec, nspec, nspec,
            pl.BlockSpec((PEER_TT, d), lambda i, k: (i, 0)),
            pl.BlockSpec(g5.shape, lambda i, k: (0, 0, 0)),
        ],
        out_specs=pl.BlockSpec((PEER_TT, d), lambda i, k: (i, 0)),
        out_shape=jax.ShapeDtypeStruct((t, d), F32),
        scratch_shapes=[pltpu.VMEM((d, PEER_TT), F32),
                        pltpu.VMEM((PEER_TT // nl, PEER_ET, nl), BF16),
                        pltpu.VMEM((PEER_TT // nl, PEER_ET, nl), F32)],
        compiler_params=_cparams(("parallel", "arbitrary")),
        name="peer_experts",
    )(h2, u_bf, vt_bf, r2, e2, n1, e1, x, g5)


def _transpose_cast_kernel(x_ref, o_ref):
    o_ref[...] = x_ref[0].T.astype(BF16)


def _transpose_cast(v_all, layer):
    _, ne, d = v_all.shape
    te = 512
    return pl.pallas_call(
        _transpose_cast_kernel,
        grid=(ne // te,),
        in_specs=[pl.BlockSpec((1, te, d), lambda e: (layer, e, 0))],
        out_specs=pl.BlockSpec((d, te), lambda e: (0, e)),
        out_shape=jax.ShapeDtypeStruct((d, ne), BF16),
        compiler_params=_cparams(("parallel",)),
        name="expert_value_transpose",
    )(v_all)


def _rope_tables(n_ctx, n_lat):
    n_freq = B_ROPE // 4
    pos = np.arange(n_lat)
    inv_freq = ROPE_THETA ** (-np.arange(n_freq, dtype=np.float32) / n_freq)
    inv_freq = jnp.asarray(inv_freq, F32)
    rowp = jnp.asarray(pos // GRID_W, F32)
    colp = jnp.asarray(pos % GRID_W, F32)
    ang = jnp.stack([rowp[:, None] * inv_freq, colp[:, None] * inv_freq], axis=1)
    cos, sin = jnp.cos(ang), jnp.sin(ang)
    cos32 = jnp.concatenate([cos, cos], axis=2).reshape(n_lat, B_ROPE)
    sin32 = jnp.concatenate([-sin, sin], axis=2).reshape(n_lat, B_ROPE)
    pad_l = jnp.ones((n_lat, B_NOPE), F32)
    pad_r = jnp.ones((n_lat, B_HP - B_QK), F32)
    cos_l = jnp.concatenate([pad_l, cos32, pad_r], axis=1)
    sin_l = jnp.concatenate([0 * pad_l, sin32, 0 * pad_r], axis=1)
    cos_t = jnp.concatenate([jnp.ones((n_ctx, B_HP), F32), cos_l], axis=0)
    sin_t = jnp.concatenate([jnp.zeros((n_ctx, B_HP), F32), sin_l], axis=0)
    return cos_t, sin_t


_SWAP32 = np.arange(B_ROPE) ^ (B_ROPE // 4)


def _pad_head(nope, rope):
    z = jnp.zeros(nope.shape[:-1] + (B_HP - B_QK,), nope.dtype)
    out = jnp.concatenate([nope, rope, z], axis=-1)
    return out.reshape(out.shape[:-2] + (out.shape[-2] * B_HP,))


def _layer_weights(layer, w_in, w_out, a_norm_w, a_w_s, a_b_s, b_q_norm_w, b_w_uq, b_kv_norm_w, b_w_ukv,
                   b_q_head_norm_w, b_k_head_norm_w, c_out_norm_w, p_w_q, p_sub_keys):
    d = w_in.shape[1]
    wi = w_in[layer]
    offs = np.cumsum([0, A_W, A_W, B_Q_RANK, B_KV_RANK, B_ROPE, C_W, C_W, C_W, C_W, C_W])
    col = lambda i: wi[:, offs[i]:offs[i + 1]]
    w_kr = col(4)
    zl = jnp.zeros((d, B_NOPE), F32)
    zr = jnp.zeros((d, B_HP - B_QK), F32)
    kr_placed = jnp.concatenate([zl, w_kr, zr], axis=1)
    kr_swapped = jnp.concatenate([zl, w_kr[:, _SWAP32], zr], axis=1)
    w_all = jnp.concatenate(
        [col(0), col(1), col(2), col(3), kr_placed, kr_swapped, col(5), col(6), col(7), col(8), col(9)],
        axis=1).astype(BF16)
    widths = (2 * A_W, B_Q_RANK + B_KV_RANK + 2 * B_HP, 4 * C_W, C_W)

    wuq = b_w_uq[layer].reshape(B_Q_RANK, B_HEADS, B_QK)
    wq_p = _pad_head(wuq[..., :B_NOPE], wuq[..., B_NOPE:]).astype(BF16)
    wq_s = _pad_head(0 * wuq[..., :B_NOPE], wuq[..., B_NOPE:][..., _SWAP32]).astype(BF16)
    wukv = b_w_ukv[layer].reshape(B_KV_RANK, B_HEADS, B_NOPE + B_V)
    wk_p = _pad_head(wukv[..., :B_NOPE], jnp.zeros((B_KV_RANK, B_HEADS, B_ROPE), F32)).astype(BF16)
    wv = wukv[..., B_NOPE:].reshape(B_KV_RANK, B_HEADS * B_V).astype(BF16)
    qn, kn = b_q_head_norm_w[layer], b_k_head_norm_w[layer]
    zpad = jnp.zeros((B_HP - B_QK,), F32)
    z64 = jnp.zeros((B_NOPE,), F32)
    hw = jnp.stack([
        jnp.concatenate([qn, zpad]),
        jnp.concatenate([z64, qn[B_NOPE:][_SWAP32], zpad]),
        jnp.concatenate([kn, zpad]),
        jnp.concatenate([z64, kn[B_NOPE:][_SWAP32], zpad]),
    ] + [jnp.zeros((B_HP,), F32)] * 4, axis=0)

    wo = w_out[layer].astype(BF16)
    return dict(
        w_all=w_all, widths=widths,
        a_nw=a_norm_w[layer].reshape(1, A_W),
        a_ws=a_w_s[layer].astype(BF16),
        a_bias=jnp.repeat(a_b_s[layer].T, A_HD, axis=1),
        qnw=b_q_norm_w[layer].reshape(1, B_Q_RANK), kvnw=b_kv_norm_w[layer].reshape(1, B_KV_RANK),
        wq_p=wq_p, wq_s=wq_s, wk_p=wk_p, wv=wv, hw=hw,
        cnw=jnp.tile(c_out_norm_w[layer], C_HEADS).reshape(1, C_W),
        wo_a=wo[:A_W], wo_b=wo[A_W:A_W + B_HEADS * B_V], wo_c=wo[A_W + B_HEADS * B_V:],
        p_wq=p_w_q[layer].astype(BF16), p_sk=p_sub_keys[layer].astype(BF16),
    )


def kernel(x, c, ctx, c_ctx, ln1_w, ln2_w, w_mod, b_mod, w_in, w_out, a_norm_w, a_w_s, a_b_s, b_q_norm_w,
           b_w_uq, b_kv_norm_w, b_w_ukv, b_q_head_norm_w, b_k_head_norm_w, c_lb_logits, c_out_norm_w,
           p_w_q, p_sub_keys, p_u, p_v):
    bsz, n_lat, d = x.shape
    n_ctx = ctx.shape[1]
    depth = w_in.shape[0]
    n = n_ctx + n_lat
    ctx_blocks = n_ctx // TB
    blocks = n // TB

    mrows = -(-(bsz + 1) // 8) * 8
    cvec = jnp.concatenate([c, c_ctx[None, :], jnp.zeros((mrows - bsz - 1, d), F32)], axis=0)
    mod_all = _modulation(cvec, w_mod, b_mod)

    lb = jnp.cumsum(jax.nn.softmax(c_lb_logits.astype(F32), axis=0), axis=0)
    lb = lb - lb[0:1]
    lbc_all = jnp.stack([jnp.log(lb), jnp.log1p(-lb), 1.0 - lb] + [jnp.zeros_like(lb)] * 5, axis=2)

    cos_t, sin_t = _rope_tables(n_ctx, n_lat)
    xc = jnp.concatenate([ctx, x], axis=1)

    for layer in range(depth):
        last = layer == depth - 1
        w = _layer_weights(layer, w_in, w_out, a_norm_w, a_w_s, a_b_s, b_q_norm_w, b_w_uq, b_kv_norm_w,
                           b_w_ukv, b_q_head_norm_w, b_k_head_norm_w, c_out_norm_w, p_w_q, p_sub_keys)
        mod_b = mod_all[layer, :bsz]
        mod_c = jnp.broadcast_to(mod_all[layer, bsz][None, :], mod_b.shape)
        modsel = jnp.stack([mod_c, mod_b], axis=1)[:, :, None, :]
        first_block = ctx_blocks if last else 0

        oa, ob, oc, og = _inproj(xc, ln1_w[layer].reshape(1, d), modsel, w["w_all"], w["widths"])
        a_out = _amix(oa, w["a_nw"], w["a_ws"], w["a_bias"], first_block * (TB // A_CHUNK))
        q, k, v = _mla_prep(ob, cos_t, sin_t, w["qnw"], w["kvnw"], w["wq_p"], w["wq_s"], w["wk_p"], w["wv"], w["hw"])
        b_lat = _attention(q, k, v, ctx_blocks, blocks - ctx_blocks, n, n_lat, 2)
        if last:
            b_out = b_lat
        else:
            b_ctx = _attention(q, k, v, 0, ctx_blocks, n_ctx, n_ctx, 1)
            b_out = jnp.concatenate([b_ctx, b_lat], axis=1)
        o_f, o_b = _hgrn(oc, lbc_all[layer], n_ctx)
        x_new, h2 = _outproj(xc, a_out, b_out, o_f, o_b, og, w["cnw"], w["wo_a"], w["wo_b"], w["wo_c"],
                             modsel, ln2_w[layer].reshape(1, d), first_block)

        t = x_new.shape[0] * x_new.shape[1]
        h2f = h2.reshape(t, d)
        r2, e2, n1, e1 = _route(h2f, w["p_wq"], w["p_sk"])
        g5 = modsel[:, :, :, 5 * d:6 * d].reshape(2 * bsz, 1, d)
        u_bf = p_u[layer].astype(BF16)
        vt_bf = _transpose_cast(p_v, layer)
        out = _peer(h2f, u_bf, vt_bf, r2, e2, n1, e1, x_new.reshape(t, d), g5,
                    blocks - first_block, ctx_blocks - first_block)
        xc = out.reshape(bsz, t // bsz, d)
    return xc
```

```python
import functools
import math

import jax
import jax.numpy as jnp
import numpy as np
from jax import lax
from jax.experimental import pallas as pl
from jax.experimental.pallas import tpu as pltpu

F32 = jnp.float32
BF16 = jnp.bfloat16
HIGHEST = lax.Precision.HIGHEST

EPS = 1e-6
GRID_W = 64
ROPE_THETA = 10000.0

A_HEADS, A_HD, A_CHUNK = 4, 64, 128
A_W = A_HEADS * A_HD
B_HEADS, B_NOPE, B_ROPE, B_V = 8, 64, 32, 64
B_QK = B_NOPE + B_ROPE
B_HP = 128
B_Q_RANK, B_KV_RANK = 256, 128
C_HEADS, C_DK, C_DV, C_CHUNK = 4, 64, 64, 64
C_W = C_HEADS * C_DK
C_SUB = 16
P_HEADS, P_KEY_DIM, P_N_KEYS, P_TOPK = 8, 256, 128, 16
P_HALF = P_KEY_DIM // 2

TB = 256
PEER_TT = 1024
PEER_TC = 512
PEER_ET = 1024
PEER_I1 = PEER_ET // P_N_KEYS
PEER_JG = 4
ROUTE_HPT = 8
VMEM_LIMIT = 56 * 1024 * 1024


def _cparams(sem, flags=None):
    return pltpu.CompilerParams(dimension_semantics=sem, vmem_limit_bytes=VMEM_LIMIT, flags=flags)


def _dot_nt(a, b):
    return lax.dot_general(a, b, (((1,), (1,)), ((), ())), preferred_element_type=F32)


def _dot_tn(a, b):
    return lax.dot_general(a, b, (((0,), (0,)), ((), ())), preferred_element_type=F32)


def _dot(a, b):
    return jnp.dot(a, b, preferred_element_type=F32)


def _sigmoid(x):
    return 1.0 / (1.0 + jnp.exp(-x))


def _block_ones(n, blk, dtype):
    r = lax.broadcasted_iota(jnp.int32, (n, n), 0) // blk
    c = lax.broadcasted_iota(jnp.int32, (n, n), 1) // blk
    return (r == c).astype(dtype)


def _mod_kernel(c_ref, w_ref, b_ref, o_ref):
    c = c_ref[...]
    sc = c * _sigmoid(c)
    o_ref[0] = _dot(sc.astype(BF16), w_ref[0].astype(BF16)) + b_ref[0]


def _modulation(cvec, w_mod, b_mod):
    depth, d, n6 = w_mod.shape
    rows = cvec.shape[0]
    tn = 1024
    return pl.pallas_call(
        _mod_kernel,
        grid=(depth, n6 // tn),
        in_specs=[
            pl.BlockSpec((rows, d), lambda l, n: (0, 0)),
            pl.BlockSpec((1, d, tn), lambda l, n: (l, 0, n)),
            pl.BlockSpec((1, 1, tn), lambda l, n: (l, 0, n)),
        ],
        out_specs=pl.BlockSpec((1, rows, tn), lambda l, n: (l, 0, n)),
        out_shape=jax.ShapeDtypeStruct((depth, rows, n6), F32),
        compiler_params=_cparams(("parallel", "parallel")),
        name="adaln_mod",
    )(cvec, w_mod, b_mod.reshape(depth, 1, n6))


def _inproj_kernel(x_ref, lnw_ref, mod_ref, w_ref, oa_ref, ob_ref, oc_ref, og_ref, *, d):
    x = x_ref[0]
    ms = jnp.mean(x * x, axis=-1, keepdims=True)
    y = x * lax.rsqrt(ms + EPS) * lnw_ref[...]
    shift = mod_ref[0, 0, :, 0:d]
    scale = mod_ref[0, 0, :, d:2 * d]
    h = (y * (1.0 + scale) + shift).astype(BF16)
    p = _dot(h, w_ref[...])
    na = oa_ref.shape[-1]
    nb = ob_ref.shape[-1]
    nc = oc_ref.shape[-1]
    oa_ref[0] = p[:, 0:na]
    ob_ref[0] = p[:, na:na + nb]
    oc_ref[0] = p[:, na + nb:na + nb + nc]
    og_ref[0] = p[:, na + nb + nc:]


def _inproj(x, lnw, modsel, w_all, widths):
    bsz, n, d = x.shape
    na, nb, nc, ng = widths
    nout = w_all.shape[1]
    return pl.pallas_call(
        functools.partial(_inproj_kernel, d=d),
        grid=(bsz, n // TB),
        in_specs=[
            pl.BlockSpec((1, TB, d), lambda b, j: (b, j, 0)),
            pl.BlockSpec((1, d), lambda b, j: (0, 0)),
            pl.BlockSpec((1, 1, 1, modsel.shape[-1]), lambda b, j: (b, jnp.minimum(j, 1), 0, 0)),
            pl.BlockSpec((d, nout), lambda b, j: (0, 0)),
        ],
        out_specs=[
            pl.BlockSpec((1, TB, na), lambda b, j: (b, j, 0)),
            pl.BlockSpec((1, TB, nb), lambda b, j: (b, j, 0)),
            pl.BlockSpec((1, TB, nc), lambda b, j: (b, j, 0)),
            pl.BlockSpec((1, TB, ng), lambda b, j: (b, j, 0)),
        ],
        out_shape=[
            jax.ShapeDtypeStruct((bsz, n, na), F32),
            jax.ShapeDtypeStruct((bsz, n, nb), F32),
            jax.ShapeDtypeStruct((bsz, n, nc), F32),
            jax.ShapeDtypeStruct((bsz, n, ng), F32),
        ],
        compiler_params=_cparams(("parallel", "parallel")),
        name="in_proj",
    )(x, lnw, modsel, w_all)


def _amix_kernel(a_ref, nw_ref, ws_ref, bias_ref, o_ref):
    u = a_ref[0, :, 0:A_W]
    v = a_ref[0, :, A_W:2 * A_W]
    ssq = jnp.dot(v * v, _block_ones(A_W, A_HD, F32), precision=HIGHEST, preferred_element_type=F32)
    vn = v * lax.rsqrt(ssq * (1.0 / A_HD) + EPS) * nw_ref[...]
    lane_head = lax.broadcasted_iota(jnp.int32, vn.shape, 1) // A_HD
    acc = bias_ref[...]
    for h in range(A_HEADS):
        vm = jnp.where(lane_head == h, vn, 0.0).astype(BF16)
        acc = acc + _dot(ws_ref[h], vm)
    o_ref[0] = (u * acc).astype(BF16)


def _amix(oa, nw, ws, bias, first_chunk):
    bsz, n, _ = oa.shape
    nchunk = n // A_CHUNK - first_chunk
    return pl.pallas_call(
        _amix_kernel,
        grid=(bsz, nchunk),
        in_specs=[
            pl.BlockSpec((1, A_CHUNK, 2 * A_W), lambda b, j: (b, j + first_chunk, 0)),
            pl.BlockSpec((1, A_W), lambda b, j: (0, 0)),
            pl.BlockSpec((A_HEADS, A_CHUNK, A_CHUNK), lambda b, j: (0, 0, 0)),
            pl.BlockSpec((A_CHUNK, A_W), lambda b, j: (0, 0)),
        ],
        out_specs=pl.BlockSpec((1, A_CHUNK, A_W), lambda b, j: (b, j + first_chunk, 0)),
        out_shape=jax.ShapeDtypeStruct((bsz, n, A_W), BF16),
        compiler_params=_cparams(("parallel", "parallel")),
        name="mixer_a",
    )(oa, nw, ws, bias)


def _mla_prep_kernel(ob_ref, cos_ref, sin_ref, qnw_ref, kvnw_ref, wq_ref, wqs_ref, wk_ref, wv_ref,
                     hw_ref, q_ref, k_ref, v_ref):
    cq = ob_ref[0, :, 0:B_Q_RANK]
    ckv = ob_ref[0, :, B_Q_RANK:B_Q_RANK + B_KV_RANK]
    krp = ob_ref[0, :, B_Q_RANK + B_KV_RANK:B_Q_RANK + B_KV_RANK + B_HP]
    krs = ob_ref[0, :, B_Q_RANK + B_KV_RANK + B_HP:B_Q_RANK + B_KV_RANK + 2 * B_HP]
    cos = cos_ref[...]
    sin = sin_ref[...]
    cqn = (cq * lax.rsqrt(jnp.mean(cq * cq, axis=-1, keepdims=True) + EPS) * qnw_ref[...]).astype(BF16)
    ckn = (ckv * lax.rsqrt(jnp.mean(ckv * ckv, axis=-1, keepdims=True) + EPS) * kvnw_ref[...]).astype(BF16)
    q_raw = _dot(cqn, wq_ref[...])
    q_swp = _dot(cqn, wqs_ref[...])
    k_raw = _dot(ckn, wk_ref[...])
    v_all = _dot(ckn, wv_ref[...])
    qw, qws, kw, kws = hw_ref[0:1, :], hw_ref[1:2, :], hw_ref[2:3, :], hw_ref[3:4, :]
    k_rot_sw = krs * kws * sin
    for h in range(B_HEADS):
        sl = slice(h * B_HP, (h + 1) * B_HP)
        qh = q_raw[:, sl]
        rq = lax.rsqrt(jnp.sum(qh * qh, axis=-1, keepdims=True) * (1.0 / B_QK) + EPS)
        q_ref[0, h] = (rq * (qh * qw * cos + q_swp[:, sl] * qws * sin)).astype(BF16)
        kh = k_raw[:, sl] + krp
        rk = lax.rsqrt(jnp.sum(kh * kh, axis=-1, keepdims=True) * (1.0 / B_QK) + EPS)
        k_ref[0, h] = (rk * (kh * kw * cos + k_rot_sw)).astype(BF16)
        v_ref[0, h] = v_all[:, h * B_V:(h + 1) * B_V].astype(BF16)


def _mla_prep(ob, cos_t, sin_t, qnw, kvnw, wq, wqs, wk, wv, hw):
    bsz, n, nb = ob.shape
    full = lambda *s: pl.BlockSpec(s, lambda b, j: (0,) * len(s))
    return pl.pallas_call(
        _mla_prep_kernel,
        grid=(bsz, n // TB),
        in_specs=[
            pl.BlockSpec((1, TB, nb), lambda b, j: (b, j, 0)),
            pl.BlockSpec((TB, B_HP), lambda b, j: (j, 0)),
            pl.BlockSpec((TB, B_HP), lambda b, j: (j, 0)),
            full(1, B_Q_RANK), full(1, B_KV_RANK),
            full(B_Q_RANK, B_HEADS * B_HP), full(B_Q_RANK, B_HEADS * B_HP),
            full(B_KV_RANK, B_HEADS * B_HP), full(B_KV_RANK, B_HEADS * B_V),
            full(8, B_HP),
        ],
        out_specs=[
            pl.BlockSpec((1, B_HEADS, TB, B_HP), lambda b, j: (b, 0, j, 0)),
            pl.BlockSpec((1, B_HEADS, TB, B_HP), lambda b, j: (b, 0, j, 0)),
            pl.BlockSpec((1, B_HEADS, TB, B_V), lambda b, j: (b, 0, j, 0)),
        ],
        out_shape=[
            jax.ShapeDtypeStruct((bsz, B_HEADS, n, B_HP), BF16),
            jax.ShapeDtypeStruct((bsz, B_HEADS, n, B_HP), BF16),
            jax.ShapeDtypeStruct((bsz, B_HEADS, n, B_V), BF16),
        ],
        compiler_params=_cparams(("parallel", "parallel")),
        name="mla_prep",
    )(ob, cos_t, sin_t, qnw, kvnw, wq, wqs, wk, wv, hw)


def _attn_kernel(*refs, nqb):
    q_refs, (k_ref, v_ref, o_ref, o_scr) = refs[:nqb], refs[nqb:]
    scale = (B_QK ** -0.5) * math.log2(math.e)
    for h in range(B_HEADS):
        q = q_refs[0][0, h] if nqb == 1 else jnp.concatenate([r[0, h] for r in q_refs], axis=0)
        s = _dot_nt(q, k_ref[0, h])
        m = jnp.max(s, axis=-1, keepdims=True)
        p = jnp.exp2((s - m) * scale)
        l = jnp.sum(p, axis=-1, keepdims=True)
        o = _dot(p.astype(BF16), v_ref[0, h])
        o_scr[:, h * B_V:(h + 1) * B_V] = o / l
    o_ref[0] = o_scr[...].astype(BF16)


def _attention(q, k, v, first_qblock, n_qblocks, n_keys, out_rows, nqb):
    bsz = q.shape[0]
    qspec = lambda u: pl.BlockSpec((1, B_HEADS, TB, B_HP), lambda b, j: (b, 0, j * nqb + u + first_qblock, 0))
    return pl.pallas_call(
        functools.partial(_attn_kernel, nqb=nqb),
        grid=(bsz, n_qblocks // nqb),
        in_specs=[qspec(u) for u in range(nqb)] + [
            pl.BlockSpec((1, B_HEADS, n_keys, B_HP), lambda b, j: (b, 0, 0, 0)),
            pl.BlockSpec((1, B_HEADS, n_keys, B_V), lambda b, j: (b, 0, 0, 0)),
        ],
        out_specs=pl.BlockSpec((1, nqb * TB, B_HEADS * B_V), lambda b, j: (b, j, 0)),
        out_shape=jax.ShapeDtypeStruct((bsz, out_rows, B_HEADS * B_V), BF16),
        scratch_shapes=[pltpu.VMEM((nqb * TB, B_HEADS * B_V), F32)],
        compiler_params=_cparams(("parallel", "arbitrary")),
        name="mla_attention",
    )(*([q] * nqb), k, v)


def _hgrn_chunk(blk, zcol, lbc, st_ref, rev):
    cc, w = C_CHUNK, C_W
    q = blk[:, 0:w] * (C_DK ** -0.5)
    z = blk[:, zcol * w:(zcol + 1) * w]
    v = blk[:, 3 * w:4 * w]
    log_lb, log1m_lb, one_m_lb = lbc[0:1, :], lbc[1:2, :], lbc[2:3, :]
    az = jnp.abs(z)
    sp = jnp.log1p(jnp.exp(-az))
    lsig = jnp.minimum(z, 0.0) - sp
    t2 = log1m_lb + lsig
    mx = jnp.maximum(log_lb, t2)
    mn = jnp.minimum(log_lb, t2)
    logf = mx + jnp.log1p(jnp.exp(mn - mx))
    kk = one_m_lb * _sigmoid(-z)

    ti = lax.broadcasted_iota(jnp.int32, (cc, cc), 0)
    ui = lax.broadcasted_iota(jnp.int32, (cc, cc), 1)
    tri = ((ui >= ti) if rev else (ui <= ti)).astype(F32)
    b = jnp.dot(tri, logf, precision=HIGHEST, preferred_element_type=F32)
    b_tot = b[0:1, :] if rev else b[cc - 1:cc, :]

    row = lax.broadcasted_iota(jnp.int32, (cc, w), 0)
    lane_head = lax.broadcasted_iota(jnp.int32, (cc, w), 1) // C_DK
    nsub = cc // C_SUB
    row_blk = row // C_SUB

    beta_rows = []
    for i in range(nsub):
        if rev:
            src = None if i == nsub - 1 else b[(i + 1) * C_SUB:(i + 1) * C_SUB + 1, :]
        else:
            src = None if i == 0 else b[i * C_SUB - 1:i * C_SUB, :]
        beta_rows.append(src)
    beta_full = jnp.concatenate(
        [jnp.broadcast_to(b[i * C_SUB:i * C_SUB + 1, :] if r is None else r, (C_SUB, w))
         for i, r in enumerate(beta_rows)], axis=0)
    has_prev = (row_blk < nsub - 1) if rev else (row_blk > 0)
    qs = jnp.where(has_prev, q * jnp.exp(b - beta_full), 0.0)

    q_stack = jnp.concatenate([jnp.where(lane_head == h, qs, 0.0) for h in range(C_HEADS)], axis=0).astype(BF16)
    qblocks = [i for i in range(nsub) if beta_rows[i] is not None]
    ks_parts = []
    for i in qblocks:
        prev = (row_blk > i) if rev else (row_blk < i)
        ks_parts.append(jnp.where(prev, kk * jnp.exp(beta_rows[i] - b), 0.0))
    ks_all = jnp.concatenate(ks_parts, axis=0).astype(BF16)
    a_all = _dot_nt(q_stack, ks_all)
    ar = lax.broadcasted_iota(jnp.int32, a_all.shape, 0)
    ac = lax.broadcasted_iota(jnp.int32, a_all.shape, 1)
    r_blk = (ar % cc) // C_SUB
    c_blk = ac // cc + (0 if rev else 1)
    a_all = jnp.where(r_blk == c_blk, a_all, 0.0).astype(BF16)
    v_bf = v.astype(BF16)
    r_all = _dot(a_all, jnp.concatenate([v_bf] * len(qblocks), axis=0))
    o = jnp.zeros((cc, w), F32)
    for h in range(C_HEADS):
        o = o + jnp.where(lane_head == h, r_all[h * cc:(h + 1) * cc, :], 0.0)

    ones_bd = _block_ones(w, C_DK, BF16)
    tsub = lax.broadcasted_iota(jnp.int32, (C_SUB, w), 0)
    diag_parts = []
    for i in range(nsub):
        r0 = i * C_SUB
        bb = b[r0:r0 + C_SUB, :]
        qq = q[r0:r0 + C_SUB, :]
        ps = []
        for s in range(C_SUB):
            keep = (tsub <= s) if rev else (tsub >= s)
            e = jnp.where(keep, jnp.exp(bb - b[r0 + s:r0 + s + 1, :]), 0.0)
            ps.append(qq * e * kk[r0 + s:r0 + s + 1, :])
        red = _dot(jnp.concatenate(ps, axis=0).astype(BF16), ones_bd)
        od = jnp.zeros((C_SUB, w), F32)
        for s in range(C_SUB):
            od = od + red[s * C_SUB:(s + 1) * C_SUB, :] * v[r0 + s:r0 + s + 1, :]
        diag_parts.append(od)
    o = o + jnp.concatenate(diag_parts, axis=0)

    st = st_ref[...]
    o = o + _dot_nt((q * jnp.exp(b)).astype(BF16), st.astype(BF16))
    kd = (kk * jnp.exp(b_tot - b)).astype(BF16)
    upd = _dot_tn(v_bf, kd)
    st_ref[...] = st * jnp.exp(b_tot) + upd * _block_ones(w, C_DK, F32)
    return o


def _hgrn_kernel(cf_ref, cb_ref, lbc_ref, of_ref, ob_ref, sf_ref, sb_ref):
    @pl.when(pl.program_id(1) == 0)
    def _():
        sf_ref[...] = jnp.zeros_like(sf_ref)
        sb_ref[...] = jnp.zeros_like(sb_ref)

    of_ref[0] = _hgrn_chunk(cf_ref[0], 1, lbc_ref[0], sf_ref, rev=False)
    ob_ref[0] = _hgrn_chunk(cb_ref[0], 2, lbc_ref[1], sb_ref, rev=True)


def _hgrn(oc, lbc, n_ctx):
    bsz, n, wc = oc.shape
    nch = n // C_CHUNK
    nctx = n_ctx // C_CHUNK

    def bwd_idx(c):
        return jnp.where(c < nctx, nctx - 1 - c, nch + nctx - 1 - c)

    return pl.pallas_call(
        _hgrn_kernel,
        grid=(bsz, nch),
        in_specs=[
            pl.BlockSpec((1, C_CHUNK, wc), lambda b, c: (b, c, 0)),
            pl.BlockSpec((1, C_CHUNK, wc), lambda b, c: (b, bwd_idx(c), 0)),
            pl.BlockSpec((2, 8, C_W), lambda b, c: (0, 0, 0)),
        ],
        out_specs=[
            pl.BlockSpec((1, C_CHUNK, C_W), lambda b, c: (b, c, 0)),
            pl.BlockSpec((1, C_CHUNK, C_W), lambda b, c: (b, bwd_idx(c), 0)),
        ],
        out_shape=[jax.ShapeDtypeStruct((bsz, n, C_W), F32)] * 2,
        scratch_shapes=[pltpu.VMEM((C_W, C_W), F32)] * 2,
        compiler_params=_cparams(("parallel", "arbitrary")),
        name="hgrn2_scan",
    )(oc, oc, lbc)


def _outproj_kernel(x_ref, a_ref, b_ref, of_ref, ob_ref, g_ref, cnw_ref, wa_ref, wb_ref, wc_ref,
                    mod_ref, ln2_ref, xo_ref, h2_ref, *, d):
    o = of_ref[0] + ob_ref[0]
    ssq = jnp.dot(o * o, _block_ones(C_W, C_DV, F32), precision=HIGHEST, preferred_element_type=F32)
    g = g_ref[0]
    c_out = o * lax.rsqrt(ssq * (1.0 / C_DV) + EPS) * cnw_ref[...] * (g * _sigmoid(g))
    mix = _dot(a_ref[0], wa_ref[...]) + _dot(b_ref[0], wb_ref[...]) + _dot(c_out.astype(BF16), wc_ref[...])
    gate1 = mod_ref[0, 0, :, 2 * d:3 * d]
    shift2 = mod_ref[0, 0, :, 3 * d:4 * d]
    scale2 = mod_ref[0, 0, :, 4 * d:5 * d]
    x = x_ref[0] + gate1 * mix
    xo_ref[0] = x
    y = x * lax.rsqrt(jnp.mean(x * x, axis=-1, keepdims=True) + EPS) * ln2_ref[...]
    h2_ref[0] = (y * (1.0 + scale2) + shift2).astype(BF16)


def _outproj(x, a_out, b_out, o_f, o_b, og, cnw, wa, wb, wc, modsel, ln2, first_block):
    bsz, n, d = x.shape
    nblk = n // TB - first_block
    full = lambda *s: pl.BlockSpec(s, lambda b, j: (0,) * len(s))
    tok = lambda w: pl.BlockSpec((1, TB, w), lambda b, j: (b, j + first_block, 0))
    return pl.pallas_call(
        functools.partial(_outproj_kernel, d=d),
        grid=(bsz, nblk),
        in_specs=[
            tok(d), tok(A_W),
            pl.BlockSpec((1, TB, B_HEADS * B_V), lambda b, j: (b, j, 0)) if first_block else tok(B_HEADS * B_V),
            tok(C_W), tok(C_W), tok(C_W),
            full(1, C_W), full(A_W, d), full(B_HEADS * B_V, d), full(C_W, d),
            pl.BlockSpec((1, 1, 1, modsel.shape[-1]), lambda b, j: (b, jnp.minimum(j + first_block, 1), 0, 0)),
            full(1, d),
        ],
        out_specs=[
            pl.BlockSpec((1, TB, d), lambda b, j: (b, j, 0)),
            pl.BlockSpec((1, TB, d), lambda b, j: (b, j, 0)),
        ],
        out_shape=[
            jax.ShapeDtypeStruct((bsz, nblk * TB, d), F32),
            jax.ShapeDtypeStruct((bsz, nblk * TB, d), BF16),
        ],
        compiler_params=_cparams(("parallel", "parallel")),
        name="out_proj",
    )(x, a_out, b_out, o_f, o_b, og, cnw, wa, wb, wc, modsel, ln2)


def _top16_exact(s):
    nrows = s.shape[0]
    iota = lax.broadcasted_iota(jnp.int32, s.shape, 0).astype(F32)
    rank = jnp.full(s.shape, P_TOPK, jnp.int32)
    vals = []
    for r in range(P_TOPK):
        m = jnp.max(s, axis=0, keepdims=True)
        idx = jnp.min(jnp.where(s == m, iota, float(nrows)), axis=0, keepdims=True)
        hit = iota == idx
        rank = jnp.where(hit, r, rank)
        s = jnp.where(hit, -jnp.inf, s)
        vals.append(m)
    return jnp.concatenate(vals, axis=0), rank


_MARK0 = int(np.array(0xFF7FFFFF, np.uint32).view(np.int32))


def _top16_marked(s):
    vals = []
    for r in range(P_TOPK):
        m = jnp.max(s, axis=0, keepdims=True)
        mark = float(np.array(_MARK0 - r, np.int32).view(np.float32))
        s = jnp.where(s == m, mark, s)
        vals.append(m)
    rr = _MARK0 - pltpu.bitcast(s, jnp.int32)
    rank = jnp.where(rr < 0, P_TOPK, jnp.where(rr > P_TOPK - 1, P_TOPK, rr))
    slack = jnp.sum(P_TOPK - rank, axis=0, keepdims=True) - (P_TOPK * (P_TOPK + 1)) // 2
    return jnp.concatenate(vals, axis=0), rank, slack


def _int_bits(x):
    return lax.shift_right_logical(pltpu.bitcast(x, jnp.int32), 16)


def _gate_bits(x):
    return lax.shift_right_logical(pltpu.bitcast(x, jnp.int32) + 0x8000, 16)


def _dup(b):
    return b | lax.shift_left(b, 16)


def _pair_rows(b, scr):
    half = b.shape[0] // 2
    scr[...] = b
    return scr[pl.ds(0, half, stride=2), :] | lax.shift_left(scr[pl.ds(1, half, stride=2), :], 16)


def _unpack_row(row):
    n = row.shape[-1]
    tile = pltpu.bitcast(jnp.broadcast_to(row, (8, n)), BF16)
    return jnp.concatenate([tile] * (P_N_KEYS // 16), axis=0)


def _route_kernel(h_ref, wq_ref, sk_ref, r2_ref, e2_ref, n1_ref, e1_ref,
                  q_scr, s_scr, v_scr, r1_scr, e1_scr, pk_scr):
    nl = P_N_KEYS
    q = _dot(h_ref[...], wq_ref[...]).astype(BF16)
    for l in range(2 * P_HEADS):
        q_scr[l] = q[:, l * P_HALF:(l + 1) * P_HALF]

    def lane_tile(lt, carry):
        row0 = pl.multiple_of(lt * nl, nl)

        def put_rank(h, p, rank, scr):
            if p == 0:
                r1_scr[h] = rank
            else:
                r2_ref[lt, h] = _pair_rows(_int_bits(rank.astype(F32)), scr)

        def heads(hp, c):
            bad = jnp.zeros((1, nl), jnp.int32)
            for u in range(2 * ROUTE_HPT):
                h, p = hp * ROUTE_HPT + u // 2, u % 2
                s = _dot_nt(sk_ref[h, p], q_scr[2 * h + p, pl.ds(row0, nl), :])
                s_scr[u] = s
                v, rank, slack = _top16_marked(s)
                v_scr[p, h] = v
                put_rank(h, p, rank, pk_scr.at[u])
                e = jnp.exp(s - v[0:1, :])
                if p == 0:
                    e1_scr[h] = e
                else:
                    e2_ref[lt, h] = _pair_rows(_gate_bits(e), pk_scr.at[u - 1])
                bad = bad + slack

            @pl.when(jnp.max(bad) > 0)
            def _():
                for u in range(2 * ROUTE_HPT):
                    h, p = hp * ROUTE_HPT + u // 2, u % 2
                    ve, re = _top16_exact(s_scr[u])
                    v_scr[p, h] = ve
                    put_rank(h, p, re, pk_scr.at[u])

            return c

        lax.fori_loop(0, P_HEADS // ROUTE_HPT, heads, 0)

        v1 = v_scr[0]
        v2 = v_scr[1]
        ia = lax.broadcasted_iota(jnp.int32, v1.shape, 1).astype(F32)
        n = jnp.zeros(v1.shape, F32)
        g = jnp.broadcast_to(v2[:, 0:1, :], v1.shape)
        cmax = v1[:, 0:1, :] + v2[:, 0:1, :]
        z = jnp.zeros(cmax.shape, F32)
        for _ in range(P_TOPK):
            f = v1 + g
            m = jnp.max(f, axis=1, keepdims=True)
            a_star = jnp.min(jnp.where(f == m, ia, float(P_TOPK)), axis=1, keepdims=True)
            hit = ia == a_star
            n = n + jnp.where(hit, 1.0, 0.0)
            nsel = jnp.sum(jnp.where(hit, n, 0.0), axis=1, keepdims=True)
            nxt = jnp.sum(jnp.where(ia == nsel, v2, 0.0), axis=1, keepdims=True)
            nxt = jnp.where(nsel > P_TOPK - 0.5, -jnp.inf, nxt)
            g = jnp.where(hit, nxt, g)
            z = z + jnp.exp(m - cmax)
        zinv = 1.0 / z
        nw = _dup(_int_bits(n))
        for h in range(P_HEADS):
            rank1 = r1_scr[h]
            n1w = jnp.zeros(rank1.shape, jnp.int32)
            for a in range(P_TOPK):
                n1w = jnp.where(rank1 == a, nw[h, a:a + 1, :], n1w)
            n1_ref[lt, h] = n1w
            e1_ref[lt, h] = _dup(_gate_bits(e1_scr[h] * zinv[h]))
        return carry

    lax.fori_loop(0, h_ref.shape[0] // nl, lane_tile, 0)


def _route(h2, wq, sk):
    t, d = h2.shape
    tr = PEER_TT
    nl = P_N_KEYS
    oshape = (t // nl, P_HEADS, P_N_KEYS, nl)
    hshape = (t // nl, P_HEADS, P_N_KEYS // 2, nl)
    ospec = pl.BlockSpec((tr // nl, P_HEADS, P_N_KEYS, nl), lambda i: (i, 0, 0, 0))
    hspec = pl.BlockSpec((tr // nl, P_HEADS, P_N_KEYS // 2, nl), lambda i: (i, 0, 0, 0))
    return pl.pallas_call(
        _route_kernel,
        grid=(t // tr,),
        in_specs=[
            pl.BlockSpec((tr, d), lambda i: (i, 0)),
            pl.BlockSpec((d, P_HEADS * P_KEY_DIM), lambda i: (0, 0)),
            pl.BlockSpec((P_HEADS, 2, P_N_KEYS, P_HALF), lambda i: (0, 0, 0, 0)),
        ],
        out_specs=[hspec, hspec, ospec, ospec],
        out_shape=[jax.ShapeDtypeStruct(hshape, jnp.int32), jax.ShapeDtypeStruct(hshape, jnp.int32),
                   jax.ShapeDtypeStruct(oshape, jnp.int32), jax.ShapeDtypeStruct(oshape, jnp.int32)],
        scratch_shapes=[
            pltpu.VMEM((2 * P_HEADS, tr, P_HALF), BF16),
            pltpu.VMEM((2 * ROUTE_HPT, P_N_KEYS, nl), F32),
            pltpu.VMEM((2, P_HEADS, P_TOPK, nl), F32),
            pltpu.VMEM((P_HEADS, P_N_KEYS, nl), jnp.int32),
            pltpu.VMEM((P_HEADS, P_N_KEYS, nl), F32),
            pltpu.VMEM((2 * ROUTE_HPT, P_N_KEYS, nl), jnp.int32),
        ],
        compiler_params=_cparams(("parallel",)),
        name="peer_route",
    )(h2, wq, sk)


def _peer_kernel(h_ref, u_ref, vt_ref, r2_ref, e2_ref, n1_ref, e1_ref, x_ref, g5_ref, o_ref,
                 acc_ref, w_ref, a_scr, *, blocks_per_batch, ctx_blocks):
    i = pl.program_id(0)
    k = pl.program_id(1)

    @pl.when(k == 0)
    def _():
        acc_ref[...] = jnp.zeros_like(acc_ref)

    nl = P_N_KEYS
    zero = jnp.zeros((P_N_KEYS, nl), BF16)
    ltc = PEER_TC // nl
    njg = PEER_I1 // PEER_JG

    def gate_block(lt, j0):
        g = [zero] * PEER_JG
        for h in range(P_HEADS):
            r2 = pltpu.bitcast(r2_ref[lt, h], BF16)
            e2 = pltpu.bitcast(e2_ref[lt, h], BF16)
            for jj in range(PEER_JG):
                n1row = _unpack_row(n1_ref[lt, h, pl.ds(j0 + jj, 1), :])
                e1row = _unpack_row(e1_ref[lt, h, pl.ds(j0 + jj, 1), :])
                g[jj] = g[jj] + jnp.where(r2 < n1row, e2, zero) * e1row
        for jj in range(PEER_JG):
            rows = pl.ds(pl.multiple_of((j0 + jj) * P_N_KEYS, P_N_KEYS), P_N_KEYS)
            w_ref[lt, rows, :] = a_scr[lt, rows, :] * g[jj]

    for c in range(PEER_TT // PEER_TC):
        tok = slice(c * PEER_TC, (c + 1) * PEER_TC)
        a_t = _dot_nt(u_ref[...], h_ref[tok, :])
        half_cdf = (0.5 * lax.erf(a_t * (2.0 ** -0.5))).astype(BF16) + 0.5
        act_t = a_t.astype(BF16) * half_cdf
        for lc in range(ltc):
            a_scr[c * ltc + lc] = act_t[:, lc * nl:(lc + 1) * nl]

        def body(it, carry, c=c):
            gate_block(c * ltc + it // njg, (it % njg) * PEER_JG)
            return carry

        lax.fori_loop(0, ltc * njg, body, 0)
        w_t = jnp.concatenate([w_ref[c * ltc + lc] for lc in range(ltc)], axis=1)
        acc_ref[:, tok] += _dot(vt_ref[...], w_t)

    @pl.when(k == pl.num_programs(1) - 1)
    def _():
        y = acc_ref[...].T
        for u in range(PEER_TT // TB):
            sblk = i * (PEER_TT // TB) + u
            bidx = sblk // blocks_per_batch
            is_lat = (sblk - bidx * blocks_per_batch) >= ctx_blocks
            gate = g5_ref[2 * bidx + is_lat.astype(jnp.int32)]
            rs = slice(u * TB, (u + 1) * TB)
            o_ref[rs, :] = x_ref[rs, :] + gate * y[rs, :]


def _peer(h2, u_bf, vt_bf, r2, e2, n1, e1, x, g5, blocks_per_batch, ctx_blocks):
    t, d = h2.shape
    ne = u_bf.shape[0]
    nl = P_N_KEYS
    rspec = pl.BlockSpec((PEER_TT // nl, P_HEADS, P_N_KEYS // 2, nl), lambda i, k: (i, 0, 0, 0))
    nspec = pl.BlockSpec((PEER_TT // nl, P_HEADS, PEER_I1, nl), lambda i, k: (i, 0, k, 0))
    return pl.pallas_call(
        functools.partial(_peer_kernel, blocks_per_batch=blocks_per_batch, ctx_blocks=ctx_blocks),
        grid=(t // PEER_TT, ne // PEER_ET),
        in_specs=[
            pl.BlockSpec((PEER_TT, d), lambda i, k: (i, 0)),
            pl.BlockSpec((PEER_ET, d), lambda i, k: (k, 0)),
            pl.BlockSpec((d, PEER_ET), lambda i, k: (0, k)),
            rspec, rspec, nspec, nspec,
            pl.BlockSpec((PEER_TT, d), lambda i, k: (i, 0)),
            pl.BlockSpec(g5.shape, lambda i, k: (0, 0, 0)),
        ],
        out_specs=pl.BlockSpec((PEER_TT, d), lambda i, k: (i, 0)),
        out_shape=jax.ShapeDtypeStruct((t, d), F32),
        scratch_shapes=[pltpu.VMEM((d, PEER_TT), F32),
                        pltpu.VMEM((PEER_TT // nl, PEER_ET, nl), BF16),
                        pltpu.VMEM((PEER_TT // nl, PEER_ET, nl), BF16)],
        compiler_params=_cparams(("parallel", "arbitrary")),
        name="peer_experts",
    )(h2, u_bf, vt_bf, r2, e2, n1, e1, x, g5)


def _transpose_cast_kernel(x_ref, o_ref):
    o_ref[...] = x_ref[0].T.astype(BF16)


def _transpose_cast(v_all, layer):
    _, ne, d = v_all.shape
    te = 512
    return pl.pallas_call(
        _transpose_cast_kernel,
        grid=(ne // te,),
        in_specs=[pl.BlockSpec((1, te, d), lambda e: (layer, e, 0))],
        out_specs=pl.BlockSpec((d, te), lambda e: (0, e)),
        out_shape=jax.ShapeDtypeStruct((d, ne), BF16),
        compiler_params=_cparams(("parallel",)),
        name="expert_value_transpose",
    )(v_all)


def _rope_tables(n_ctx, n_lat):
    n_freq = B_ROPE // 4
    pos = np.arange(n_lat)
    inv_freq = ROPE_THETA ** (-np.arange(n_freq, dtype=np.float32) / n_freq)
    inv_freq = jnp.asarray(inv_freq, F32)
    rowp = jnp.asarray(pos // GRID_W, F32)
    colp = jnp.asarray(pos % GRID_W, F32)
    ang = jnp.stack([rowp[:, None] * inv_freq, colp[:, None] * inv_freq], axis=1)
    cos, sin = jnp.cos(ang), jnp.sin(ang)
    cos32 = jnp.concatenate([cos, cos], axis=2).reshape(n_lat, B_ROPE)
    sin32 = jnp.concatenate([-sin, sin], axis=2).reshape(n_lat, B_ROPE)
    pad_l = jnp.ones((n_lat, B_NOPE), F32)
    pad_r = jnp.ones((n_lat, B_HP - B_QK), F32)
    cos_l = jnp.concatenate([pad_l, cos32, pad_r], axis=1)
    sin_l = jnp.concatenate([0 * pad_l, sin32, 0 * pad_r], axis=1)
    cos_t = jnp.concatenate([jnp.ones((n_ctx, B_HP), F32), cos_l], axis=0)
    sin_t = jnp.concatenate([jnp.zeros((n_ctx, B_HP), F32), sin_l], axis=0)
    return cos_t, sin_t


_SWAP32 = np.arange(B_ROPE) ^ (B_ROPE // 4)


def _pad_head(nope, rope):
    z = jnp.zeros(nope.shape[:-1] + (B_HP - B_QK,), nope.dtype)
    out = jnp.concatenate([nope, rope, z], axis=-1)
    return out.reshape(out.shape[:-2] + (out.shape[-2] * B_HP,))


def _layer_weights(layer, w_in, w_out, a_norm_w, a_w_s, a_b_s, b_q_norm_w, b_w_uq, b_kv_norm_w, b_w_ukv,
                   b_q_head_norm_w, b_k_head_norm_w, c_out_norm_w, p_w_q, p_sub_keys):
    d = w_in.shape[1]
    wi = w_in[layer]
    offs = np.cumsum([0, A_W, A_W, B_Q_RANK, B_KV_RANK, B_ROPE, C_W, C_W, C_W, C_W, C_W])
    col = lambda i: wi[:, offs[i]:offs[i + 1]]
    w_kr = col(4)
    zl = jnp.zeros((d, B_NOPE), F32)
    zr = jnp.zeros((d, B_HP - B_QK), F32)
    kr_placed = jnp.concatenate([zl, w_kr, zr], axis=1)
    kr_swapped = jnp.concatenate([zl, w_kr[:, _SWAP32], zr], axis=1)
    w_all = jnp.concatenate(
        [col(0), col(1), col(2), col(3), kr_placed, kr_swapped, col(5), col(6), col(7), col(8), col(9)],
        axis=1).astype(BF16)
    widths = (2 * A_W, B_Q_RANK + B_KV_RANK + 2 * B_HP, 4 * C_W, C_W)

    wuq = b_w_uq[layer].reshape(B_Q_RANK, B_HEADS, B_QK)
    wq_p = _pad_head(wuq[..., :B_NOPE], wuq[..., B_NOPE:]).astype(BF16)
    wq_s = _pad_head(0 * wuq[..., :B_NOPE], wuq[..., B_NOPE:][..., _SWAP32]).astype(BF16)
    wukv = b_w_ukv[layer].reshape(B_KV_RANK, B_HEADS, B_NOPE + B_V)
    wk_p = _pad_head(wukv[..., :B_NOPE], jnp.zeros((B_KV_RANK, B_HEADS, B_ROPE), F32)).astype(BF16)
    wv = wukv[..., B_NOPE:].reshape(B_KV_RANK, B_HEADS * B_V).astype(BF16)
    qn, kn = b_q_head_norm_w[layer], b_k_head_norm_w[layer]
    zpad = jnp.zeros((B_HP - B_QK,), F32)
    z64 = jnp.zeros((B_NOPE,), F32)
    hw = jnp.stack([
        jnp.concatenate([qn, zpad]),
        jnp.concatenate([z64, qn[B_NOPE:][_SWAP32], zpad]),
        jnp.concatenate([kn, zpad]),
        jnp.concatenate([z64, kn[B_NOPE:][_SWAP32], zpad]),
    ] + [jnp.zeros((B_HP,), F32)] * 4, axis=0)

    wo = w_out[layer].astype(BF16)
    return dict(
        w_all=w_all, widths=widths,
        a_nw=a_norm_w[layer].reshape(1, A_W),
        a_ws=a_w_s[layer].astype(BF16),
        a_bias=jnp.repeat(a_b_s[layer].T, A_HD, axis=1),
        qnw=b_q_norm_w[layer].reshape(1, B_Q_RANK), kvnw=b_kv_norm_w[layer].reshape(1, B_KV_RANK),
        wq_p=wq_p, wq_s=wq_s, wk_p=wk_p, wv=wv, hw=hw,
        cnw=jnp.tile(c_out_norm_w[layer], C_HEADS).reshape(1, C_W),
        wo_a=wo[:A_W], wo_b=wo[A_W:A_W + B_HEADS * B_V], wo_c=wo[A_W + B_HEADS * B_V:],
        p_wq=p_w_q[layer].astype(BF16), p_sk=p_sub_keys[layer].astype(BF16),
    )


def kernel(x, c, ctx, c_ctx, ln1_w, ln2_w, w_mod, b_mod, w_in, w_out, a_norm_w, a_w_s, a_b_s, b_q_norm_w,
           b_w_uq, b_kv_norm_w, b_w_ukv, b_q_head_norm_w, b_k_head_norm_w, c_lb_logits, c_out_norm_w,
           p_w_q, p_sub_keys, p_u, p_v):
    bsz, n_lat, d = x.shape
    n_ctx = ctx.shape[1]
    depth = w_in.shape[0]
    n = n_ctx + n_lat
    ctx_blocks = n_ctx // TB
    blocks = n // TB

    mrows = -(-(bsz + 1) // 8) * 8
    cvec = jnp.concatenate([c, c_ctx[None, :], jnp.zeros((mrows - bsz - 1, d), F32)], axis=0)
    mod_all = _modulation(cvec, w_mod, b_mod)

    lb = jnp.cumsum(jax.nn.softmax(c_lb_logits.astype(F32), axis=0), axis=0)
    lb = lb - lb[0:1]
    lbc_all = jnp.stack([jnp.log(lb), jnp.log1p(-lb), 1.0 - lb] + [jnp.zeros_like(lb)] * 5, axis=2)

    cos_t, sin_t = _rope_tables(n_ctx, n_lat)
    xc = jnp.concatenate([ctx, x], axis=1)

    for layer in range(depth):
        last = layer == depth - 1
        w = _layer_weights(layer, w_in, w_out, a_norm_w, a_w_s, a_b_s, b_q_norm_w, b_w_uq, b_kv_norm_w,
                           b_w_ukv, b_q_head_norm_w, b_k_head_norm_w, c_out_norm_w, p_w_q, p_sub_keys)
        mod_b = mod_all[layer, :bsz]
        mod_c = jnp.broadcast_to(mod_all[layer, bsz][None, :], mod_b.shape)
        modsel = jnp.stack([mod_c, mod_b], axis=1)[:, :, None, :]
        first_block = ctx_blocks if last else 0

        oa, ob, oc, og = _inproj(xc, ln1_w[layer].reshape(1, d), modsel, w["w_all"], w["widths"])
        a_out = _amix(oa, w["a_nw"], w["a_ws"], w["a_bias"], first_block * (TB // A_CHUNK))
        q, k, v = _mla_prep(ob, cos_t, sin_t, w["qnw"], w["kvnw"], w["wq_p"], w["wq_s"], w["wk_p"], w["wv"], w["hw"])
        b_lat = _attention(q, k, v, ctx_blocks, blocks - ctx_blocks, n, n_lat, 2)
        if last:
            b_out = b_lat
        else:
            b_ctx = _attention(q, k, v, 0, ctx_blocks, n_ctx, n_ctx, 1)
            b_out = jnp.concatenate([b_ctx, b_lat], axis=1)
        o_f, o_b = _hgrn(oc, lbc_all[layer], n_ctx)
        x_new, h2 = _outproj(xc, a_out, b_out, o_f, o_b, og, w["cnw"], w["wo_a"], w["wo_b"], w["wo_c"],
                             modsel, ln2_w[layer].reshape(1, d), first_block)

        t = x_new.shape[0] * x_new.shape[1]
        h2f = h2.reshape(t, d)
        r2, e2, n1, e1 = _route(h2f, w["p_wq"], w["p_sk"])
        g5 = modsel[:, :, :, 5 * d:6 * d].reshape(2 * bsz, 1, d)
        u_bf = p_u[layer].astype(BF16)
        vt_bf = _transpose_cast(p_v, layer)
        out = _peer(h2f, u_bf, vt_bf, r2, e2, n1, e1, x_new.reshape(t, d), g5,
                    blocks - first_block, ctx_blocks - first_block)
        xc = out.reshape(bsz, t // bsz, d)
    return xc
```

```python
import functools
import math

import jax
import jax.numpy as jnp
import numpy as np
from jax import lax
from jax.experimental import pallas as pl
from jax.experimental.pallas import tpu as pltpu

F32 = jnp.float32
BF16 = jnp.bfloat16
HIGHEST = lax.Precision.HIGHEST

EPS = 1e-6
GRID_W = 64
ROPE_THETA = 10000.0

A_HEADS, A_HD, A_CHUNK = 4, 64, 128
A_W = A_HEADS * A_HD
B_HEADS, B_NOPE, B_ROPE, B_V = 8, 64, 32, 64
B_QK = B_NOPE + B_ROPE
B_HP = 128
B_Q_RANK, B_KV_RANK = 256, 128
C_HEADS, C_DK, C_DV, C_CHUNK = 4, 64, 64, 64
C_W = C_HEADS * C_DK
C_SUB = 16
P_HEADS, P_KEY_DIM, P_N_KEYS, P_TOPK = 8, 256, 128, 16
P_HALF = P_KEY_DIM // 2

TB = 256
PEER_TT = 1024
PEER_TC = 512
PEER_ET = 1024
PEER_I1 = PEER_ET // P_N_KEYS
PEER_JG = 4
ROUTE_HPT = 8
VMEM_LIMIT = 56 * 1024 * 1024


def _cparams(sem, flags=None):
    return pltpu.CompilerParams(dimension_semantics=sem, vmem_limit_bytes=VMEM_LIMIT, flags=flags)


def _dot_nt(a, b):
    return lax.dot_general(a, b, (((1,), (1,)), ((), ())), preferred_element_type=F32)


def _dot_tn(a, b):
    return lax.dot_general(a, b, (((0,), (0,)), ((), ())), preferred_element_type=F32)


def _dot(a, b):
    return jnp.dot(a, b, preferred_element_type=F32)


def _sigmoid(x):
    return 1.0 / (1.0 + jnp.exp(-x))


def _block_ones(n, blk, dtype):
    r = lax.broadcasted_iota(jnp.int32, (n, n), 0) // blk
    c = lax.broadcasted_iota(jnp.int32, (n, n), 1) // blk
    return (r == c).astype(dtype)


def _mod_kernel(c_ref, w_ref, b_ref, o_ref):
    c = c_ref[...]
    sc = c * _sigmoid(c)
    o_ref[0] = _dot(sc.astype(BF16), w_ref[0].astype(BF16)) + b_ref[0]


def _modulation(cvec, w_mod, b_mod):
    depth, d, n6 = w_mod.shape
    rows = cvec.shape[0]
    tn = 1024
    return pl.pallas_call(
        _mod_kernel,
        grid=(depth, n6 // tn),
        in_specs=[
            pl.BlockSpec((rows, d), lambda l, n: (0, 0)),
            pl.BlockSpec((1, d, tn), lambda l, n: (l, 0, n)),
            pl.BlockSpec((1, 1, tn), lambda l, n: (l, 0, n)),
        ],
        out_specs=pl.BlockSpec((1, rows, tn), lambda l, n: (l, 0, n)),
        out_shape=jax.ShapeDtypeStruct((depth, rows, n6), F32),
        compiler_params=_cparams(("parallel", "parallel")),
        name="adaln_mod",
    )(cvec, w_mod, b_mod.reshape(depth, 1, n6))


def _inproj_kernel(x_ref, lnw_ref, mod_ref, w_ref, oa_ref, ob_ref, oc_ref, og_ref, *, d):
    x = x_ref[0]
    ms = jnp.mean(x * x, axis=-1, keepdims=True)
    y = x * lax.rsqrt(ms + EPS) * lnw_ref[...]
    shift = mod_ref[0, 0, :, 0:d]
    scale = mod_ref[0, 0, :, d:2 * d]
    h = (y * (1.0 + scale) + shift).astype(BF16)
    p = _dot(h, w_ref[...])
    na = oa_ref.shape[-1]
    nb = ob_ref.shape[-1]
    nc = oc_ref.shape[-1]
    oa_ref[0] = p[:, 0:na]
    ob_ref[0] = p[:, na:na + nb]
    oc_ref[0] = p[:, na + nb:na + nb + nc]
    og_ref[0] = p[:, na + nb + nc:]


def _inproj(x, lnw, modsel, w_all, widths):
    bsz, n, d = x.shape
    na, nb, nc, ng = widths
    nout = w_all.shape[1]
    return pl.pallas_call(
        functools.partial(_inproj_kernel, d=d),
        grid=(bsz, n // TB),
        in_specs=[
            pl.BlockSpec((1, TB, d), lambda b, j: (b, j, 0)),
            pl.BlockSpec((1, d), lambda b, j: (0, 0)),
            pl.BlockSpec((1, 1, 1, modsel.shape[-1]), lambda b, j: (b, jnp.minimum(j, 1), 0, 0)),
            pl.BlockSpec((d, nout), lambda b, j: (0, 0)),
        ],
        out_specs=[
            pl.BlockSpec((1, TB, na), lambda b, j: (b, j, 0)),
            pl.BlockSpec((1, TB, nb), lambda b, j: (b, j, 0)),
            pl.BlockSpec((1, TB, nc), lambda b, j: (b, j, 0)),
            pl.BlockSpec((1, TB, ng), lambda b, j: (b, j, 0)),
        ],
        out_shape=[
            jax.ShapeDtypeStruct((bsz, n, na), F32),
            jax.ShapeDtypeStruct((bsz, n, nb), F32),
            jax.ShapeDtypeStruct((bsz, n, nc), F32),
            jax.ShapeDtypeStruct((bsz, n, ng), F32),
        ],
        compiler_params=_cparams(("parallel", "parallel")),
        name="in_proj",
    )(x, lnw, modsel, w_all)


def _amix_kernel(a_ref, nw_ref, ws_ref, bias_ref, o_ref):
    u = a_ref[0, :, 0:A_W]
    v = a_ref[0, :, A_W:2 * A_W]
    ssq = jnp.dot(v * v, _block_ones(A_W, A_HD, F32), precision=HIGHEST, preferred_element_type=F32)
    vn = v * lax.rsqrt(ssq * (1.0 / A_HD) + EPS) * nw_ref[...]
    lane_head = lax.broadcasted_iota(jnp.int32, (A_CHUNK, A_W), 1) // A_HD
    for c in range(TB // A_CHUNK):
        rows = slice(c * A_CHUNK, (c + 1) * A_CHUNK)
        acc = bias_ref[...]
        for h in range(A_HEADS):
            vm = jnp.where(lane_head == h, vn[rows], 0.0).astype(BF16)
            acc = acc + _dot(ws_ref[h], vm)
        o_ref[0, rows, :] = (u[rows] * acc).astype(BF16)


def _amix(oa, nw, ws, bias, first_block):
    bsz, n, _ = oa.shape
    nblk = n // TB - first_block
    return pl.pallas_call(
        _amix_kernel,
        grid=(bsz, nblk),
        in_specs=[
            pl.BlockSpec((1, TB, 2 * A_W), lambda b, j: (b, j + first_block, 0)),
            pl.BlockSpec((1, A_W), lambda b, j: (0, 0)),
            pl.BlockSpec((A_HEADS, A_CHUNK, A_CHUNK), lambda b, j: (0, 0, 0)),
            pl.BlockSpec((A_CHUNK, A_W), lambda b, j: (0, 0)),
        ],
        out_specs=pl.BlockSpec((1, TB, A_W), lambda b, j: (b, j + first_block, 0)),
        out_shape=jax.ShapeDtypeStruct((bsz, n, A_W), BF16),
        compiler_params=_cparams(("parallel", "parallel")),
        name="mixer_a",
    )(oa, nw, ws, bias)


def _mla_prep_kernel(ob_ref, cos_ref, sin_ref, qnw_ref, kvnw_ref, wq_ref, wqs_ref, wk_ref, wv_ref,
                     hw_ref, q_ref, k_ref, v_ref):
    cq = ob_ref[0, :, 0:B_Q_RANK]
    ckv = ob_ref[0, :, B_Q_RANK:B_Q_RANK + B_KV_RANK]
    krp = ob_ref[0, :, B_Q_RANK + B_KV_RANK:B_Q_RANK + B_KV_RANK + B_HP]
    krs = ob_ref[0, :, B_Q_RANK + B_KV_RANK + B_HP:B_Q_RANK + B_KV_RANK + 2 * B_HP]
    cos = cos_ref[...]
    sin = sin_ref[...]
    cqn = (cq * lax.rsqrt(jnp.mean(cq * cq, axis=-1, keepdims=True) + EPS) * qnw_ref[...]).astype(BF16)
    ckn = (ckv * lax.rsqrt(jnp.mean(ckv * ckv, axis=-1, keepdims=True) + EPS) * kvnw_ref[...]).astype(BF16)
    q_raw = _dot(cqn, wq_ref[...])
    q_swp = _dot(cqn, wqs_ref[...])
    k_raw = _dot(ckn, wk_ref[...])
    v_all = _dot(ckn, wv_ref[...])
    qw, qws, kw, kws = hw_ref[0:1, :], hw_ref[1:2, :], hw_ref[2:3, :], hw_ref[3:4, :]
    k_rot_sw = krs * kws * sin
    for h in range(B_HEADS):
        sl = slice(h * B_HP, (h + 1) * B_HP)
        qh = q_raw[:, sl]
        rq = lax.rsqrt(jnp.sum(qh * qh, axis=-1, keepdims=True) * (1.0 / B_QK) + EPS)
        q_ref[0, h] = (rq * (qh * qw * cos + q_swp[:, sl] * qws * sin)).astype(BF16)
        kh = k_raw[:, sl] + krp
        rk = lax.rsqrt(jnp.sum(kh * kh, axis=-1, keepdims=True) * (1.0 / B_QK) + EPS)
        k_ref[0, h] = (rk * (kh * kw * cos + k_rot_sw)).astype(BF16)
        v_ref[0, h] = v_all[:, h * B_V:(h + 1) * B_V].astype(BF16)


def _mla_prep(ob, cos_t, sin_t, qnw, kvnw, wq, wqs, wk, wv, hw):
    bsz, n, nb = ob.shape
    full = lambda *s: pl.BlockSpec(s, lambda b, j: (0,) * len(s))
    return pl.pallas_call(
        _mla_prep_kernel,
        grid=(bsz, n // TB),
        in_specs=[
            pl.BlockSpec((1, TB, nb), lambda b, j: (b, j, 0)),
            pl.BlockSpec((TB, B_HP), lambda b, j: (j, 0)),
            pl.BlockSpec((TB, B_HP), lambda b, j: (j, 0)),
            full(1, B_Q_RANK), full(1, B_KV_RANK),
            full(B_Q_RANK, B_HEADS * B_HP), full(B_Q_RANK, B_HEADS * B_HP),
            full(B_KV_RANK, B_HEADS * B_HP), full(B_KV_RANK, B_HEADS * B_V),
            full(8, B_HP),
        ],
        out_specs=[
            pl.BlockSpec((1, B_HEADS, TB, B_HP), lambda b, j: (b, 0, j, 0)),
            pl.BlockSpec((1, B_HEADS, TB, B_HP), lambda b, j: (b, 0, j, 0)),
            pl.BlockSpec((1, B_HEADS, TB, B_V), lambda b, j: (b, 0, j, 0)),
        ],
        out_shape=[
            jax.ShapeDtypeStruct((bsz, B_HEADS, n, B_HP), BF16),
            jax.ShapeDtypeStruct((bsz, B_HEADS, n, B_HP), BF16),
            jax.ShapeDtypeStruct((bsz, B_HEADS, n, B_V), BF16),
        ],
        compiler_params=_cparams(("parallel", "parallel")),
        name="mla_prep",
    )(ob, cos_t, sin_t, qnw, kvnw, wq, wqs, wk, wv, hw)


def _attn_kernel(*refs, nqb):
    q_refs, (k_ref, v_ref, o_ref, o_scr) = refs[:nqb], refs[nqb:]
    scale = (B_QK ** -0.5) * math.log2(math.e)
    for h in range(B_HEADS):
        q = q_refs[0][0, h] if nqb == 1 else jnp.concatenate([r[0, h] for r in q_refs], axis=0)
        s = _dot_nt(q, k_ref[0, h])
        m = jnp.max(s, axis=-1, keepdims=True)
        p = jnp.exp2((s - m) * scale)
        l = jnp.sum(p, axis=-1, keepdims=True)
        o = _dot(p.astype(BF16), v_ref[0, h])
        o_scr[:, h * B_V:(h + 1) * B_V] = o / l
    o_ref[0] = o_scr[...].astype(BF16)


def _attention(q, k, v, first_qblock, n_qblocks, n_keys, out_rows, nqb):
    bsz = q.shape[0]
    qspec = lambda u: pl.BlockSpec((1, B_HEADS, TB, B_HP), lambda b, j: (b, 0, j * nqb + u + first_qblock, 0))
    return pl.pallas_call(
        functools.partial(_attn_kernel, nqb=nqb),
        grid=(bsz, n_qblocks // nqb),
        in_specs=[qspec(u) for u in range(nqb)] + [
            pl.BlockSpec((1, B_HEADS, n_keys, B_HP), lambda b, j: (b, 0, 0, 0)),
            pl.BlockSpec((1, B_HEADS, n_keys, B_V), lambda b, j: (b, 0, 0, 0)),
        ],
        out_specs=pl.BlockSpec((1, nqb * TB, B_HEADS * B_V), lambda b, j: (b, j, 0)),
        out_shape=jax.ShapeDtypeStruct((bsz, out_rows, B_HEADS * B_V), BF16),
        scratch_shapes=[pltpu.VMEM((nqb * TB, B_HEADS * B_V), F32)],
        compiler_params=_cparams(("parallel", "arbitrary")),
        name="mla_attention",
    )(*([q] * nqb), k, v)


def _hgrn_chunk(blk, zcol, lbc, st_ref, rev):
    cc, w = C_CHUNK, C_W
    q = blk[:, 0:w] * (C_DK ** -0.5)
    z = blk[:, zcol * w:(zcol + 1) * w]
    v = blk[:, 3 * w:4 * w]
    log_lb, log1m_lb, one_m_lb = lbc[0:1, :], lbc[1:2, :], lbc[2:3, :]
    az = jnp.abs(z)
    sp = jnp.log1p(jnp.exp(-az))
    lsig = jnp.minimum(z, 0.0) - sp
    t2 = log1m_lb + lsig
    mx = jnp.maximum(log_lb, t2)
    mn = jnp.minimum(log_lb, t2)
    logf = mx + jnp.log1p(jnp.exp(mn - mx))
    kk = one_m_lb * _sigmoid(-z)

    ti = lax.broadcasted_iota(jnp.int32, (cc, cc), 0)
    ui = lax.broadcasted_iota(jnp.int32, (cc, cc), 1)
    tri = ((ui >= ti) if rev else (ui <= ti)).astype(F32)
    b = jnp.dot(tri, logf, precision=HIGHEST, preferred_element_type=F32)
    b_tot = b[0:1, :] if rev else b[cc - 1:cc, :]

    row = lax.broadcasted_iota(jnp.int32, (cc, w), 0)
    lane_head = lax.broadcasted_iota(jnp.int32, (cc, w), 1) // C_DK
    nsub = cc // C_SUB
    row_blk = row // C_SUB

    beta_rows = []
    for i in range(nsub):
        if rev:
            src = None if i == nsub - 1 else b[(i + 1) * C_SUB:(i + 1) * C_SUB + 1, :]
        else:
            src = None if i == 0 else b[i * C_SUB - 1:i * C_SUB, :]
        beta_rows.append(src)
    beta_full = jnp.concatenate(
        [jnp.broadcast_to(b[i * C_SUB:i * C_SUB + 1, :] if r is None else r, (C_SUB, w))
         for i, r in enumerate(beta_rows)], axis=0)
    has_prev = (row_blk < nsub - 1) if rev else (row_blk > 0)
    qs = jnp.where(has_prev, q * jnp.exp(b - beta_full), 0.0)

    q_stack = jnp.concatenate([jnp.where(lane_head == h, qs, 0.0) for h in range(C_HEADS)], axis=0).astype(BF16)
    qblocks = [i for i in range(nsub) if beta_rows[i] is not None]
    ks_parts = []
    for i in qblocks:
        prev = (row_blk > i) if rev else (row_blk < i)
        ks_parts.append(jnp.where(prev, kk * jnp.exp(beta_rows[i] - b), 0.0))
    ks_all = jnp.concatenate(ks_parts, axis=0).astype(BF16)
    a_all = _dot_nt(q_stack, ks_all)
    ar = lax.broadcasted_iota(jnp.int32, a_all.shape, 0)
    ac = lax.broadcasted_iota(jnp.int32, a_all.shape, 1)
    r_blk = (ar % cc) // C_SUB
    c_blk = ac // cc + (0 if rev else 1)
    a_all = jnp.where(r_blk == c_blk, a_all, 0.0).astype(BF16)
    v_bf = v.astype(BF16)
    r_all = _dot(a_all, jnp.concatenate([v_bf] * len(qblocks), axis=0))
    o = jnp.zeros((cc, w), F32)
    for h in range(C_HEADS):
        o = o + jnp.where(lane_head == h, r_all[h * cc:(h + 1) * cc, :], 0.0)

    ones_bd = _block_ones(w, C_DK, BF16)
    tsub = lax.broadcasted_iota(jnp.int32, (C_SUB, w), 0)
    diag_parts = []
    for i in range(nsub):
        r0 = i * C_SUB
        bb = b[r0:r0 + C_SUB, :]
        qq = q[r0:r0 + C_SUB, :]
        ps = []
        for s in range(C_SUB):
            keep = (tsub <= s) if rev else (tsub >= s)
            e = jnp.where(keep, jnp.exp(bb - b[r0 + s:r0 + s + 1, :]), 0.0)
            ps.append(qq * e * kk[r0 + s:r0 + s + 1, :])
        red = _dot(jnp.concatenate(ps, axis=0).astype(BF16), ones_bd)
        od = jnp.zeros((C_SUB, w), F32)
        for s in range(C_SUB):
            od = od + red[s * C_SUB:(s + 1) * C_SUB, :] * v[r0 + s:r0 + s + 1, :]
        diag_parts.append(od)
    o = o + jnp.concatenate(diag_parts, axis=0)

    st = st_ref[...]
    o = o + _dot_nt((q * jnp.exp(b)).astype(BF16), st.astype(BF16))
    kd = (kk * jnp.exp(b_tot - b)).astype(BF16)
    upd = _dot_tn(v_bf, kd)
    st_ref[...] = st * jnp.exp(b_tot) + upd * _block_ones(w, C_DK, F32)
    return o


def _hgrn_kernel(cf_ref, cb_ref, lbc_ref, of_ref, ob_ref, sf_ref, sb_ref):
    @pl.when(pl.program_id(1) == 0)
    def _():
        sf_ref[...] = jnp.zeros_like(sf_ref)
        sb_ref[...] = jnp.zeros_like(sb_ref)

    of_ref[0] = _hgrn_chunk(cf_ref[0], 1, lbc_ref[0], sf_ref, rev=False)
    ob_ref[0] = _hgrn_chunk(cb_ref[0], 2, lbc_ref[1], sb_ref, rev=True)


def _hgrn(oc, lbc, n_ctx):
    bsz, n, wc = oc.shape
    nch = n // C_CHUNK
    nctx = n_ctx // C_CHUNK

    def bwd_idx(c):
        return jnp.where(c < nctx, nctx - 1 - c, nch + nctx - 1 - c)

    return pl.pallas_call(
        _hgrn_kernel,
        grid=(bsz, nch),
        in_specs=[
            pl.BlockSpec((1, C_CHUNK, wc), lambda b, c: (b, c, 0)),
            pl.BlockSpec((1, C_CHUNK, wc), lambda b, c: (b, bwd_idx(c), 0)),
            pl.BlockSpec((2, 8, C_W), lambda b, c: (0, 0, 0)),
        ],
        out_specs=[
            pl.BlockSpec((1, C_CHUNK, C_W), lambda b, c: (b, c, 0)),
            pl.BlockSpec((1, C_CHUNK, C_W), lambda b, c: (b, bwd_idx(c), 0)),
        ],
        out_shape=[jax.ShapeDtypeStruct((bsz, n, C_W), F32)] * 2,
        scratch_shapes=[pltpu.VMEM((C_W, C_W), F32)] * 2,
        compiler_params=_cparams(("parallel", "arbitrary")),
        name="hgrn2_scan",
    )(oc, oc, lbc)


def _outproj_kernel(x_ref, a_ref, b_ref, of_ref, ob_ref, g_ref, cnw_ref, wa_ref, wb_ref, wc_ref,
                    mod_ref, ln2_ref, xo_ref, h2_ref, *, d):
    o = of_ref[0] + ob_ref[0]
    ssq = jnp.dot(o * o, _block_ones(C_W, C_DV, F32), precision=HIGHEST, preferred_element_type=F32)
    g = g_ref[0]
    c_out = o * lax.rsqrt(ssq * (1.0 / C_DV) + EPS) * cnw_ref[...] * (g * _sigmoid(g))
    mix = _dot(a_ref[0], wa_ref[...]) + _dot(b_ref[0], wb_ref[...]) + _dot(c_out.astype(BF16), wc_ref[...])
    gate1 = mod_ref[0, 0, :, 2 * d:3 * d]
    shift2 = mod_ref[0, 0, :, 3 * d:4 * d]
    scale2 = mod_ref[0, 0, :, 4 * d:5 * d]
    x = x_ref[0] + gate1 * mix
    xo_ref[0] = x
    y = x * lax.rsqrt(jnp.mean(x * x, axis=-1, keepdims=True) + EPS) * ln2_ref[...]
    h2_ref[0] = (y * (1.0 + scale2) + shift2).astype(BF16)


def _outproj(x, a_out, b_out, o_f, o_b, og, cnw, wa, wb, wc, modsel, ln2, first_block):
    bsz, n, d = x.shape
    nblk = n // TB - first_block
    full = lambda *s: pl.BlockSpec(s, lambda b, j: (0,) * len(s))
    tok = lambda w: pl.BlockSpec((1, TB, w), lambda b, j: (b, j + first_block, 0))
    return pl.pallas_call(
        functools.partial(_outproj_kernel, d=d),
        grid=(bsz, nblk),
        in_specs=[
            tok(d), tok(A_W),
            pl.BlockSpec((1, TB, B_HEADS * B_V), lambda b, j: (b, j, 0)) if first_block else tok(B_HEADS * B_V),
            tok(C_W), tok(C_W), tok(C_W),
            full(1, C_W), full(A_W, d), full(B_HEADS * B_V, d), full(C_W, d),
            pl.BlockSpec((1, 1, 1, modsel.shape[-1]), lambda b, j: (b, jnp.minimum(j + first_block, 1), 0, 0)),
            full(1, d),
        ],
        out_specs=[
            pl.BlockSpec((1, TB, d), lambda b, j: (b, j, 0)),
            pl.BlockSpec((1, TB, d), lambda b, j: (b, j, 0)),
        ],
        out_shape=[
            jax.ShapeDtypeStruct((bsz, nblk * TB, d), F32),
            jax.ShapeDtypeStruct((bsz, nblk * TB, d), BF16),
        ],
        compiler_params=_cparams(("parallel", "parallel")),
        name="out_proj",
    )(x, a_out, b_out, o_f, o_b, og, cnw, wa, wb, wc, modsel, ln2)


def _top16_exact(s):
    nrows = s.shape[0]
    iota = lax.broadcasted_iota(jnp.int32, s.shape, 0).astype(F32)
    rank = jnp.full(s.shape, P_TOPK, jnp.int32)
    vals = []
    for r in range(P_TOPK):
        m = jnp.max(s, axis=0, keepdims=True)
        idx = jnp.min(jnp.where(s == m, iota, float(nrows)), axis=0, keepdims=True)
        hit = iota == idx
        rank = jnp.where(hit, r, rank)
        s = jnp.where(hit, -jnp.inf, s)
        vals.append(m)
    return jnp.concatenate(vals, axis=0), rank


_MARK0 = int(np.array(0xFF7FFFFF, np.uint32).view(np.int32))


def _top16_marked(s):
    vals = []
    for r in range(P_TOPK):
        m = jnp.max(s, axis=0, keepdims=True)
        mark = float(np.array(_MARK0 - r, np.int32).view(np.float32))
        s = jnp.where(s == m, mark, s)
        vals.append(m)
    rr = _MARK0 - pltpu.bitcast(s, jnp.int32)
    rank = jnp.where(rr < 0, P_TOPK, jnp.where(rr > P_TOPK - 1, P_TOPK, rr))
    slack = jnp.sum(P_TOPK - rank, axis=0, keepdims=True) - (P_TOPK * (P_TOPK + 1)) // 2
    return jnp.concatenate(vals, axis=0), rank, slack


def _as_words(x):
    return pltpu.bitcast(x.astype(BF16), jnp.int32)


def _row_plane(row):
    n = row.shape[-1]
    tile = jnp.broadcast_to(row, (16, n)).astype(BF16)
    return jnp.concatenate([tile] * (P_N_KEYS // 16), axis=0)


def _route_kernel(h_ref, wq_ref, sk_ref, r2_ref, e2_ref, n1_ref, e1_ref,
                  q_scr, s_scr, v_scr, r1_scr, e1_scr):
    nl = P_N_KEYS
    q = _dot(h_ref[...], wq_ref[...]).astype(BF16)
    for l in range(2 * P_HEADS):
        q_scr[l] = q[:, l * P_HALF:(l + 1) * P_HALF]

    def lane_tile(lt, carry):
        row0 = pl.multiple_of(lt * nl, nl)

        def put_rank(h, p, rank):
            if p == 0:
                r1_scr[h] = rank
            else:
                r2_ref[lt, h] = _as_words(rank.astype(F32))

        def heads(hp, c):
            bad = jnp.zeros((1, nl), jnp.int32)
            for u in range(2 * ROUTE_HPT):
                h, p = hp * ROUTE_HPT + u // 2, u % 2
                s = _dot_nt(sk_ref[h, p], q_scr[2 * h + p, pl.ds(row0, nl), :])
                s_scr[u] = s
                v, rank, slack = _top16_marked(s)
                v_scr[p, h] = v
                put_rank(h, p, rank)
                e = jnp.exp(s - v[0:1, :])
                if p == 0:
                    e1_scr[h] = e
                else:
                    e2_ref[lt, h] = _as_words(e)
                bad = bad + slack

            @pl.when(jnp.max(bad) > 0)
            def _():
                for u in range(2 * ROUTE_HPT):
                    h, p = hp * ROUTE_HPT + u // 2, u % 2
                    ve, re = _top16_exact(s_scr[u])
                    v_scr[p, h] = ve
                    put_rank(h, p, re)

            return c

        lax.fori_loop(0, P_HEADS // ROUTE_HPT, heads, 0)

        v1 = v_scr[0]
        v2 = v_scr[1]
        ia = lax.broadcasted_iota(jnp.int32, v1.shape, 1).astype(F32)
        n = jnp.zeros(v1.shape, F32)
        g = jnp.broadcast_to(v2[:, 0:1, :], v1.shape)
        cmax = v1[:, 0:1, :] + v2[:, 0:1, :]
        z = jnp.zeros(cmax.shape, F32)
        for _ in range(P_TOPK):
            f = v1 + g
            m = jnp.max(f, axis=1, keepdims=True)
            a_star = jnp.min(jnp.where(f == m, ia, float(P_TOPK)), axis=1, keepdims=True)
            hit = ia == a_star
            n = n + jnp.where(hit, 1.0, 0.0)
            nsel = jnp.sum(jnp.where(hit, n, 0.0), axis=1, keepdims=True)
            nxt = jnp.sum(jnp.where(ia == nsel, v2, 0.0), axis=1, keepdims=True)
            nxt = jnp.where(nsel > P_TOPK - 0.5, -jnp.inf, nxt)
            g = jnp.where(hit, nxt, g)
            z = z + jnp.exp(m - cmax)
        zinv = 1.0 / z
        for h in range(P_HEADS):
            rank1 = r1_scr[h]
            n1 = jnp.zeros(rank1.shape, F32)
            for a in range(P_TOPK):
                n1 = jnp.where(rank1 == a, n[h, a:a + 1, :], n1)
            n1_ref[lt, h] = n1
            e1_ref[lt, h] = e1_scr[h] * zinv[h]
        return carry

    lax.fori_loop(0, h_ref.shape[0] // nl, lane_tile, 0)


def _route(h2, wq, sk):
    t, d = h2.shape
    tr = PEER_TT
    nl = P_N_KEYS
    oshape = (t // nl, P_HEADS, P_N_KEYS, nl)
    hshape = (t // nl, P_HEADS, P_N_KEYS // 2, nl)
    ospec = pl.BlockSpec((tr // nl, P_HEADS, P_N_KEYS, nl), lambda i: (i, 0, 0, 0))
    hspec = pl.BlockSpec((tr // nl, P_HEADS, P_N_KEYS // 2, nl), lambda i: (i, 0, 0, 0))
    return pl.pallas_call(
        _route_kernel,
        grid=(t // tr,),
        in_specs=[
            pl.BlockSpec((tr, d), lambda i: (i, 0)),
            pl.BlockSpec((d, P_HEADS * P_KEY_DIM), lambda i: (0, 0)),
            pl.BlockSpec((P_HEADS, 2, P_N_KEYS, P_HALF), lambda i: (0, 0, 0, 0)),
        ],
        out_specs=[hspec, hspec, ospec, ospec],
        out_shape=[jax.ShapeDtypeStruct(hshape, jnp.int32), jax.ShapeDtypeStruct(hshape, jnp.int32),
                   jax.ShapeDtypeStruct(oshape, F32), jax.ShapeDtypeStruct(oshape, F32)],
        scratch_shapes=[
            pltpu.VMEM((2 * P_HEADS, tr, P_HALF), BF16),
            pltpu.VMEM((2 * ROUTE_HPT, P_N_KEYS, nl), F32),
            pltpu.VMEM((2, P_HEADS, P_TOPK, nl), F32),
            pltpu.VMEM((P_HEADS, P_N_KEYS, nl), jnp.int32),
            pltpu.VMEM((P_HEADS, P_N_KEYS, nl), F32),
        ],
        compiler_params=_cparams(("parallel",)),
        name="peer_route",
    )(h2, wq, sk)


def _peer_kernel(h_ref, u_ref, vt_ref, r2_ref, e2_ref, n1_ref, e1_ref, x_ref, g5_ref, o_ref,
                 acc_ref, w_ref, a_scr, *, blocks_per_batch, ctx_blocks):
    i = pl.program_id(0)
    k = pl.program_id(1)

    @pl.when(k == 0)
    def _():
        acc_ref[...] = jnp.zeros_like(acc_ref)

    nl = P_N_KEYS
    zero = jnp.zeros((P_N_KEYS, nl), BF16)
    ltc = PEER_TC // nl
    njg = PEER_I1 // PEER_JG

    def gate_block(lt, j0):
        g = [None] * PEER_JG
        for h in range(P_HEADS):
            r2 = pltpu.bitcast(r2_ref[lt, h], BF16)
            e2 = pltpu.bitcast(e2_ref[lt, h], BF16)
            for jj in range(PEER_JG):
                n1row = _row_plane(n1_ref[lt, h, pl.ds(j0 + jj, 1), :])
                e1row = _row_plane(e1_ref[lt, h, pl.ds(j0 + jj, 1), :])
                term = jnp.where(r2 < n1row, e2, zero) * e1row
                g[jj] = term if h == 0 else g[jj] + term
        for jj in range(PEER_JG):
            rows = pl.ds(pl.multiple_of((j0 + jj) * P_N_KEYS, P_N_KEYS), P_N_KEYS)
            w_ref[lt, rows, :] = a_scr[lt, rows, :] * g[jj]

    for c in range(PEER_TT // PEER_TC):
        tok = slice(c * PEER_TC, (c + 1) * PEER_TC)
        a_t = _dot_nt(u_ref[...], h_ref[tok, :])
        half_cdf = (0.5 * lax.erf(a_t * (2.0 ** -0.5))).astype(BF16) + 0.5
        act_t = a_t.astype(BF16) * half_cdf
        for lc in range(ltc):
            a_scr[c * ltc + lc] = act_t[:, lc * nl:(lc + 1) * nl]

        def body(it, carry, c=c):
            gate_block(c * ltc + it // njg, (it % njg) * PEER_JG)
            return carry

        lax.fori_loop(0, ltc * njg, body, 0)
        w_t = jnp.concatenate([w_ref[c * ltc + lc] for lc in range(ltc)], axis=1)
        acc_ref[:, tok] += _dot(vt_ref[...], w_t)

    @pl.when(k == pl.num_programs(1) - 1)
    def _():
        y = acc_ref[...].T
        for u in range(PEER_TT // TB):
            sblk = i * (PEER_TT // TB) + u
            bidx = sblk // blocks_per_batch
            is_lat = (sblk - bidx * blocks_per_batch) >= ctx_blocks
            gate = g5_ref[2 * bidx + is_lat.astype(jnp.int32)]
            rs = slice(u * TB, (u + 1) * TB)
            o_ref[rs, :] = x_ref[rs, :] + gate * y[rs, :]


def _peer(h2, u_bf, vt_bf, r2, e2, n1, e1, x, g5, blocks_per_batch, ctx_blocks):
    t, d = h2.shape
    ne = u_bf.shape[0]
    nl = P_N_KEYS
    rspec = pl.BlockSpec((PEER_TT // nl, P_HEADS, P_N_KEYS // 2, nl), lambda i, k: (i, 0, 0, 0))
    nspec = pl.BlockSpec((PEER_TT // nl, P_HEADS, PEER_I1, nl), lambda i, k: (i, 0, k, 0))
    return pl.pallas_call(
        functools.partial(_peer_kernel, blocks_per_batch=blocks_per_batch, ctx_blocks=ctx_blocks),
        grid=(t // PEER_TT, ne // PEER_ET),
        in_specs=[
            pl.BlockSpec((PEER_TT, d), lambda i, k: (i, 0)),
            pl.BlockSpec((PEER_ET, d), lambda i, k: (k, 0)),
            pl.BlockSpec((d, PEER_ET), lambda i, k: (0, k)),
            rspec, rspec, nspec, nspec,
            pl.BlockSpec((PEER_TT, d), lambda i, k: (i, 0)),
            pl.BlockSpec(g5.shape, lambda i, k: (0, 0, 0)),
        ],
        out_specs=pl.BlockSpec((PEER_TT, d), lambda i, k: (i, 0)),
        out_shape=jax.ShapeDtypeStruct((t, d), F32),
        scratch_shapes=[pltpu.VMEM((d, PEER_TT), F32),
                        pltpu.VMEM((PEER_TT // nl, PEER_ET, nl), BF16),
                        pltpu.VMEM((PEER_TT // nl, PEER_ET, nl), BF16)],
        compiler_params=_cparams(("parallel", "arbitrary")),
        name="peer_experts",
    )(h2, u_bf, vt_bf, r2, e2, n1, e1, x, g5)


def _transpose_cast_kernel(x_ref, o_ref):
    o_ref[...] = x_ref[0].T.astype(BF16)


def _transpose_cast(v_all, layer):
    _, ne, d = v_all.shape
    te = 512
    return pl.pallas_call(
        _transpose_cast_kernel,
        grid=(ne // te,),
        in_specs=[pl.BlockSpec((1, te, d), lambda e: (layer, e, 0))],
        out_specs=pl.BlockSpec((d, te), lambda e: (0, e)),
        out_shape=jax.ShapeDtypeStruct((d, ne), BF16),
        compiler_params=_cparams(("parallel",)),
        name="expert_value_transpose",
    )(v_all)


def _rope_tables(n_ctx, n_lat):
    n_freq = B_ROPE // 4
    pos = np.arange(n_lat)
    inv_freq = ROPE_THETA ** (-np.arange(n_freq, dtype=np.float32) / n_freq)
    inv_freq = jnp.asarray(inv_freq, F32)
    rowp = jnp.asarray(pos // GRID_W, F32)
    colp = jnp.asarray(pos % GRID_W, F32)
    ang = jnp.stack([rowp[:, None] * inv_freq, colp[:, None] * inv_freq], axis=1)
    cos, sin = jnp.cos(ang), jnp.sin(ang)
    cos32 = jnp.concatenate([cos, cos], axis=2).reshape(n_lat, B_ROPE)
    sin32 = jnp.concatenate([-sin, sin], axis=2).reshape(n_lat, B_ROPE)
    pad_l = jnp.ones((n_lat, B_NOPE), F32)
    pad_r = jnp.ones((n_lat, B_HP - B_QK), F32)
    cos_l = jnp.concatenate([pad_l, cos32, pad_r], axis=1)
    sin_l = jnp.concatenate([0 * pad_l, sin32, 0 * pad_r], axis=1)
    cos_t = jnp.concatenate([jnp.ones((n_ctx, B_HP), F32), cos_l], axis=0)
    sin_t = jnp.concatenate([jnp.zeros((n_ctx, B_HP), F32), sin_l], axis=0)
    return cos_t, sin_t


_SWAP32 = np.arange(B_ROPE) ^ (B_ROPE // 4)


def _pad_head(nope, rope):
    z = jnp.zeros(nope.shape[:-1] + (B_HP - B_QK,), nope.dtype)
    out = jnp.concatenate([nope, rope, z], axis=-1)
    return out.reshape(out.shape[:-2] + (out.shape[-2] * B_HP,))


def _layer_weights(layer, w_in, w_out, a_norm_w, a_w_s, a_b_s, b_q_norm_w, b_w_uq, b_kv_norm_w, b_w_ukv,
                   b_q_head_norm_w, b_k_head_norm_w, c_out_norm_w, p_w_q, p_sub_keys):
    d = w_in.shape[1]
    wi = w_in[layer]
    offs = np.cumsum([0, A_W, A_W, B_Q_RANK, B_KV_RANK, B_ROPE, C_W, C_W, C_W, C_W, C_W])
    col = lambda i: wi[:, offs[i]:offs[i + 1]]
    w_kr = col(4)
    zl = jnp.zeros((d, B_NOPE), F32)
    zr = jnp.zeros((d, B_HP - B_QK), F32)
    kr_placed = jnp.concatenate([zl, w_kr, zr], axis=1)
    kr_swapped = jnp.concatenate([zl, w_kr[:, _SWAP32], zr], axis=1)
    w_all = jnp.concatenate(
        [col(0), col(1), col(2), col(3), kr_placed, kr_swapped, col(5), col(6), col(7), col(8), col(9)],
        axis=1).astype(BF16)
    widths = (2 * A_W, B_Q_RANK + B_KV_RANK + 2 * B_HP, 4 * C_W, C_W)

    wuq = b_w_uq[layer].reshape(B_Q_RANK, B_HEADS, B_QK)
    wq_p = _pad_head(wuq[..., :B_NOPE], wuq[..., B_NOPE:]).astype(BF16)
    wq_s = _pad_head(0 * wuq[..., :B_NOPE], wuq[..., B_NOPE:][..., _SWAP32]).astype(BF16)
    wukv = b_w_ukv[layer].reshape(B_KV_RANK, B_HEADS, B_NOPE + B_V)
    wk_p = _pad_head(wukv[..., :B_NOPE], jnp.zeros((B_KV_RANK, B_HEADS, B_ROPE), F32)).astype(BF16)
    wv = wukv[..., B_NOPE:].reshape(B_KV_RANK, B_HEADS * B_V).astype(BF16)
    qn, kn = b_q_head_norm_w[layer], b_k_head_norm_w[layer]
    zpad = jnp.zeros((B_HP - B_QK,), F32)
    z64 = jnp.zeros((B_NOPE,), F32)
    hw = jnp.stack([
        jnp.concatenate([qn, zpad]),
        jnp.concatenate([z64, qn[B_NOPE:][_SWAP32], zpad]),
        jnp.concatenate([kn, zpad]),
        jnp.concatenate([z64, kn[B_NOPE:][_SWAP32], zpad]),
    ] + [jnp.zeros((B_HP,), F32)] * 4, axis=0)

    wo = w_out[layer].astype(BF16)
    return dict(
        w_all=w_all, widths=widths,
        a_nw=a_norm_w[layer].reshape(1, A_W),
        a_ws=a_w_s[layer].astype(BF16),
        a_bias=jnp.repeat(a_b_s[layer].T, A_HD, axis=1),
        qnw=b_q_norm_w[layer].reshape(1, B_Q_RANK), kvnw=b_kv_norm_w[layer].reshape(1, B_KV_RANK),
        wq_p=wq_p, wq_s=wq_s, wk_p=wk_p, wv=wv, hw=hw,
        cnw=jnp.tile(c_out_norm_w[layer], C_HEADS).reshape(1, C_W),
        wo_a=wo[:A_W], wo_b=wo[A_W:A_W + B_HEADS * B_V], wo_c=wo[A_W + B_HEADS * B_V:],
        p_wq=p_w_q[layer].astype(BF16), p_sk=p_sub_keys[layer].astype(BF16),
    )


def kernel(x, c, ctx, c_ctx, ln1_w, ln2_w, w_mod, b_mod, w_in, w_out, a_norm_w, a_w_s, a_b_s, b_q_norm_w,
           b_w_uq, b_kv_norm_w, b_w_ukv, b_q_head_norm_w, b_k_head_norm_w, c_lb_logits, c_out_norm_w,
           p_w_q, p_sub_keys, p_u, p_v):
    bsz, n_lat, d = x.shape
    n_ctx = ctx.shape[1]
    depth = w_in.shape[0]
    n = n_ctx + n_lat
    ctx_blocks = n_ctx // TB
    blocks = n // TB

    mrows = -(-(bsz + 1) // 8) * 8
    cvec = jnp.concatenate([c, c_ctx[None, :], jnp.zeros((mrows - bsz - 1, d), F32)], axis=0)
    mod_all = _modulation(cvec, w_mod, b_mod)

    lb = jnp.cumsum(jax.nn.softmax(c_lb_logits.astype(F32), axis=0), axis=0)
    lb = lb - lb[0:1]
    lbc_all = jnp.stack([jnp.log(lb), jnp.log1p(-lb), 1.0 - lb] + [jnp.zeros_like(lb)] * 5, axis=2)

    cos_t, sin_t = _rope_tables(n_ctx, n_lat)
    xc = jnp.concatenate([ctx, x], axis=1)

    for layer in range(depth):
        last = layer == depth - 1
        w = _layer_weights(layer, w_in, w_out, a_norm_w, a_w_s, a_b_s, b_q_norm_w, b_w_uq, b_kv_norm_w,
                           b_w_ukv, b_q_head_norm_w, b_k_head_norm_w, c_out_norm_w, p_w_q, p_sub_keys)
        mod_b = mod_all[layer, :bsz]
        mod_c = jnp.broadcast_to(mod_all[layer, bsz][None, :], mod_b.shape)
        modsel = jnp.stack([mod_c, mod_b], axis=1)[:, :, None, :]
        first_block = ctx_blocks if last else 0

        oa, ob, oc, og = _inproj(xc, ln1_w[layer].reshape(1, d), modsel, w["w_all"], w["widths"])
        a_out = _amix(oa, w["a_nw"], w["a_ws"], w["a_bias"], first_block)
        q, k, v = _mla_prep(ob, cos_t, sin_t, w["qnw"], w["kvnw"], w["wq_p"], w["wq_s"], w["wk_p"], w["wv"], w["hw"])
        b_lat = _attention(q, k, v, ctx_blocks, blocks - ctx_blocks, n, n_lat, 2)
        if last:
            b_out = b_lat
        else:
            b_ctx = _attention(q, k, v, 0, ctx_blocks, n_ctx, n_ctx, 1)
            b_out = jnp.concatenate([b_ctx, b_lat], axis=1)
        o_f, o_b = _hgrn(oc, lbc_all[layer], n_ctx)
        x_new, h2 = _outproj(xc, a_out, b_out, o_f, o_b, og, w["cnw"], w["wo_a"], w["wo_b"], w["wo_c"],
                             modsel, ln2_w[layer].reshape(1, d), first_block)

        t = x_new.shape[0] * x_new.shape[1]
        h2f = h2.reshape(t, d)
        r2, e2, n1, e1 = _route(h2f, w["p_wq"], w["p_sk"])
        g5 = modsel[:, :, :, 5 * d:6 * d].reshape(2 * bsz, 1, d)
        u_bf = p_u[layer].astype(BF16)
        vt_bf = _transpose_cast(p_v, layer)
        out = _peer(h2f, u_bf, vt_bf, r2, e2, n1, e1, x_new.reshape(t, d), g5,
                    blocks - first_block, ctx_blocks - first_block)
        xc = out.reshape(bsz, t // bsz, d)
    return xc
```

```python
import functools
import math

import jax
import jax.numpy as jnp
import numpy as np
from jax import lax
from jax.experimental import pallas as pl
from jax.experimental.pallas import tpu as pltpu

F32 = jnp.float32
BF16 = jnp.bfloat16
HIGHEST = lax.Precision.HIGHEST

EPS = 1e-6
GRID_W = 64
ROPE_THETA = 10000.0

A_HEADS, A_HD, A_CHUNK = 4, 64, 128
A_W = A_HEADS * A_HD
B_HEADS, B_NOPE, B_ROPE, B_V = 8, 64, 32, 64
B_QK = B_NOPE + B_ROPE
B_HP = 128
B_Q_RANK, B_KV_RANK = 256, 128
C_HEADS, C_DK, C_DV, C_CHUNK = 4, 64, 64, 64
C_W = C_HEADS * C_DK
C_SUB = 16
P_HEADS, P_KEY_DIM, P_N_KEYS, P_TOPK = 8, 256, 128, 16
P_HALF = P_KEY_DIM // 2

TB = 256
PEER_TT = 1024
PEER_TC = 1024
PEER_ET = 1024
PEER_I1 = PEER_ET // P_N_KEYS
PEER_JG = 4
ROUTE_HPT = 8
VMEM_LIMIT = 56 * 1024 * 1024


def _cparams(sem, flags=None):
    return pltpu.CompilerParams(dimension_semantics=sem, vmem_limit_bytes=VMEM_LIMIT, flags=flags)


def _dot_nt(a, b):
    return lax.dot_general(a, b, (((1,), (1,)), ((), ())), preferred_element_type=F32)


def _dot_tn(a, b):
    return lax.dot_general(a, b, (((0,), (0,)), ((), ())), preferred_element_type=F32)


def _dot(a, b):
    return jnp.dot(a, b, preferred_element_type=F32)


def _sigmoid(x):
    return 1.0 / (1.0 + jnp.exp(-x))


def _block_ones(n, blk, dtype):
    r = lax.broadcasted_iota(jnp.int32, (n, n), 0) // blk
    c = lax.broadcasted_iota(jnp.int32, (n, n), 1) // blk
    return (r == c).astype(dtype)


def _mod_kernel(c_ref, w_ref, b_ref, o_ref):
    c = c_ref[...]
    sc = c * _sigmoid(c)
    o_ref[0] = _dot(sc.astype(BF16), w_ref[0].astype(BF16)) + b_ref[0]


def _modulation(cvec, w_mod, b_mod):
    depth, d, n6 = w_mod.shape
    rows = cvec.shape[0]
    tn = 1024
    return pl.pallas_call(
        _mod_kernel,
        grid=(depth, n6 // tn),
        in_specs=[
            pl.BlockSpec((rows, d), lambda l, n: (0, 0)),
            pl.BlockSpec((1, d, tn), lambda l, n: (l, 0, n)),
            pl.BlockSpec((1, 1, tn), lambda l, n: (l, 0, n)),
        ],
        out_specs=pl.BlockSpec((1, rows, tn), lambda l, n: (l, 0, n)),
        out_shape=jax.ShapeDtypeStruct((depth, rows, n6), F32),
        compiler_params=_cparams(("parallel", "parallel")),
        name="adaln_mod",
    )(cvec, w_mod, b_mod.reshape(depth, 1, n6))


def _amix_body(ua, nw_ref, ws_ref, bias_ref):
    u = ua[:, 0:A_W]
    v = ua[:, A_W:2 * A_W]
    ssq = jnp.dot(v * v, _block_ones(A_W, A_HD, F32), precision=HIGHEST, preferred_element_type=F32)
    vn = v * lax.rsqrt(ssq * (1.0 / A_HD) + EPS) * nw_ref[...]
    lane_head = lax.broadcasted_iota(jnp.int32, (A_CHUNK, A_W), 1) // A_HD
    outs = []
    for c in range(TB // A_CHUNK):
        rows = slice(c * A_CHUNK, (c + 1) * A_CHUNK)
        acc = bias_ref[...]
        for h in range(A_HEADS):
            vm = jnp.where(lane_head == h, vn[rows], 0.0).astype(BF16)
            acc = acc + _dot(ws_ref[h], vm)
        outs.append((u[rows] * acc).astype(BF16))
    return jnp.concatenate(outs, axis=0)


def _mla_body(ob, cos, sin, qnw_ref, kvnw_ref, wq_ref, wqs_ref, wk_ref, wv_ref, hw_ref, q_ref, k_ref, v_ref):
    cq = ob[:, 0:B_Q_RANK]
    ckv = ob[:, B_Q_RANK:B_Q_RANK + B_KV_RANK]
    krp = ob[:, B_Q_RANK + B_KV_RANK:B_Q_RANK + B_KV_RANK + B_HP]
    krs = ob[:, B_Q_RANK + B_KV_RANK + B_HP:B_Q_RANK + B_KV_RANK + 2 * B_HP]
    cqn = (cq * lax.rsqrt(jnp.mean(cq * cq, axis=-1, keepdims=True) + EPS) * qnw_ref[...]).astype(BF16)
    ckn = (ckv * lax.rsqrt(jnp.mean(ckv * ckv, axis=-1, keepdims=True) + EPS) * kvnw_ref[...]).astype(BF16)
    q_raw = _dot(cqn, wq_ref[...])
    q_swp = _dot(cqn, wqs_ref[...])
    k_raw = _dot(ckn, wk_ref[...])
    v_all = _dot(ckn, wv_ref[...])
    qw, qws, kw, kws = hw_ref[0:1, :], hw_ref[1:2, :], hw_ref[2:3, :], hw_ref[3:4, :]
    k_rot_sw = krs * kws * sin
    for h in range(B_HEADS):
        sl = slice(h * B_HP, (h + 1) * B_HP)
        qh = q_raw[:, sl]
        rq = lax.rsqrt(jnp.sum(qh * qh, axis=-1, keepdims=True) * (1.0 / B_QK) + EPS)
        q_ref[0, h] = (rq * (qh * qw * cos + q_swp[:, sl] * qws * sin)).astype(BF16)
        kh = k_raw[:, sl] + krp
        rk = lax.rsqrt(jnp.sum(kh * kh, axis=-1, keepdims=True) * (1.0 / B_QK) + EPS)
        k_ref[0, h] = (rk * (kh * kw * cos + k_rot_sw)).astype(BF16)
        v_ref[0, h] = v_all[:, h * B_V:(h + 1) * B_V].astype(BF16)


def _front_kernel(x_ref, lnw_ref, mod_ref, w_ref, anw_ref, aws_ref, abias_ref, cos_ref, sin_ref,
                  qnw_ref, kvnw_ref, wq_ref, wqs_ref, wk_ref, wv_ref, hw_ref,
                  a_ref, q_ref, k_ref, v_ref, oc_ref, og_ref, *, d, widths):
    x = x_ref[0]
    ms = jnp.mean(x * x, axis=-1, keepdims=True)
    y = x * lax.rsqrt(ms + EPS) * lnw_ref[...]
    shift = mod_ref[0, 0, :, 0:d]
    scale = mod_ref[0, 0, :, d:2 * d]
    h = (y * (1.0 + scale) + shift).astype(BF16)
    p = _dot(h, w_ref[...])
    na, nb, nc, _ = widths
    oc_ref[0] = p[:, na + nb:na + nb + nc]
    og_ref[0] = p[:, na + nb + nc:]
    a_ref[0] = _amix_body(p[:, 0:na], anw_ref, aws_ref, abias_ref)
    _mla_body(p[:, na:na + nb], cos_ref[...], sin_ref[...], qnw_ref, kvnw_ref, wq_ref, wqs_ref, wk_ref,
              wv_ref, hw_ref, q_ref, k_ref, v_ref)


def _front(x, lnw, modsel, w, cos_t, sin_t):
    bsz, n, d = x.shape
    na, nb, nc, ng = w["widths"]
    full = lambda *s: pl.BlockSpec(s, lambda b, j: (0,) * len(s))
    tok = lambda width: pl.BlockSpec((1, TB, width), lambda b, j: (b, j, 0))
    head = lambda width: pl.BlockSpec((1, B_HEADS, TB, width), lambda b, j: (b, 0, j, 0))
    return pl.pallas_call(
        functools.partial(_front_kernel, d=d, widths=w["widths"]),
        grid=(bsz, n // TB),
        in_specs=[
            tok(d), full(1, d),
            pl.BlockSpec((1, 1, 1, modsel.shape[-1]), lambda b, j: (b, jnp.minimum(j, 1), 0, 0)),
            full(d, w["w_all"].shape[1]),
            full(1, A_W), full(A_HEADS, A_CHUNK, A_CHUNK), full(A_CHUNK, A_W),
            pl.BlockSpec((TB, B_HP), lambda b, j: (j, 0)),
            pl.BlockSpec((TB, B_HP), lambda b, j: (j, 0)),
            full(1, B_Q_RANK), full(1, B_KV_RANK),
            full(B_Q_RANK, B_HEADS * B_HP), full(B_Q_RANK, B_HEADS * B_HP),
            full(B_KV_RANK, B_HEADS * B_HP), full(B_KV_RANK, B_HEADS * B_V),
            full(8, B_HP),
        ],
        out_specs=[tok(A_W), head(B_HP), head(B_HP), head(B_V), tok(nc), tok(ng)],
        out_shape=[
            jax.ShapeDtypeStruct((bsz, n, A_W), BF16),
            jax.ShapeDtypeStruct((bsz, B_HEADS, n, B_HP), BF16),
            jax.ShapeDtypeStruct((bsz, B_HEADS, n, B_HP), BF16),
            jax.ShapeDtypeStruct((bsz, B_HEADS, n, B_V), BF16),
            jax.ShapeDtypeStruct((bsz, n, nc), F32),
            jax.ShapeDtypeStruct((bsz, n, ng), F32),
        ],
        compiler_params=_cparams(("parallel", "parallel")),
        name="in_proj_mix",
    )(x, lnw, modsel, w["w_all"], w["a_nw"], w["a_ws"], w["a_bias"], cos_t, sin_t, w["qnw"], w["kvnw"],
      w["wq_p"], w["wq_s"], w["wk_p"], w["wv"], w["hw"])


def _attn_kernel(*refs, nqb):
    q_refs, (k_ref, v_ref, o_ref, o_scr) = refs[:nqb], refs[nqb:]
    scale = (B_QK ** -0.5) * math.log2(math.e)
    for h in range(B_HEADS):
        q = q_refs[0][0, h] if nqb == 1 else jnp.concatenate([r[0, h] for r in q_refs], axis=0)
        s = _dot_nt(q, k_ref[0, h])
        m = jnp.max(s, axis=-1, keepdims=True)
        p = jnp.exp2((s - m) * scale)
        l = jnp.sum(p, axis=-1, keepdims=True)
        o = _dot(p.astype(BF16), v_ref[0, h])
        o_scr[:, h * B_V:(h + 1) * B_V] = o / l
    o_ref[0] = o_scr[...].astype(BF16)


def _attention(q, k, v, first_qblock, n_qblocks, n_keys, out_rows, nqb):
    bsz = q.shape[0]
    qspec = lambda u: pl.BlockSpec((1, B_HEADS, TB, B_HP), lambda b, j: (b, 0, j * nqb + u + first_qblock, 0))
    return pl.pallas_call(
        functools.partial(_attn_kernel, nqb=nqb),
        grid=(bsz, n_qblocks // nqb),
        in_specs=[qspec(u) for u in range(nqb)] + [
            pl.BlockSpec((1, B_HEADS, n_keys, B_HP), lambda b, j: (b, 0, 0, 0)),
            pl.BlockSpec((1, B_HEADS, n_keys, B_V), lambda b, j: (b, 0, 0, 0)),
        ],
        out_specs=pl.BlockSpec((1, nqb * TB, B_HEADS * B_V), lambda b, j: (b, j, 0)),
        out_shape=jax.ShapeDtypeStruct((bsz, out_rows, B_HEADS * B_V), BF16),
        scratch_shapes=[pltpu.VMEM((nqb * TB, B_HEADS * B_V), F32)],
        compiler_params=_cparams(("parallel", "arbitrary")),
        name="mla_attention",
    )(*([q] * nqb), k, v)


def _hgrn_chunk(blk, zcol, lbc, st_ref, rev):
    cc, w = C_CHUNK, C_W
    q = blk[:, 0:w] * (C_DK ** -0.5)
    z = blk[:, zcol * w:(zcol + 1) * w]
    v = blk[:, 3 * w:4 * w]
    log_lb, log1m_lb, one_m_lb = lbc[0:1, :], lbc[1:2, :], lbc[2:3, :]
    az = jnp.abs(z)
    sp = jnp.log1p(jnp.exp(-az))
    lsig = jnp.minimum(z, 0.0) - sp
    t2 = log1m_lb + lsig
    mx = jnp.maximum(log_lb, t2)
    mn = jnp.minimum(log_lb, t2)
    logf = mx + jnp.log1p(jnp.exp(mn - mx))
    kk = one_m_lb * _sigmoid(-z)

    ti = lax.broadcasted_iota(jnp.int32, (cc, cc), 0)
    ui = lax.broadcasted_iota(jnp.int32, (cc, cc), 1)
    tri = ((ui >= ti) if rev else (ui <= ti)).astype(F32)
    b = jnp.dot(tri, logf, precision=HIGHEST, preferred_element_type=F32)
    b_tot = b[0:1, :] if rev else b[cc - 1:cc, :]

    row = lax.broadcasted_iota(jnp.int32, (cc, w), 0)
    lane_head = lax.broadcasted_iota(jnp.int32, (cc, w), 1) // C_DK
    nsub = cc // C_SUB
    row_blk = row // C_SUB

    beta_rows = []
    for i in range(nsub):
        if rev:
            src = None if i == nsub - 1 else b[(i + 1) * C_SUB:(i + 1) * C_SUB + 1, :]
        else:
            src = None if i == 0 else b[i * C_SUB - 1:i * C_SUB, :]
        beta_rows.append(src)
    beta_full = jnp.concatenate(
        [jnp.broadcast_to(b[i * C_SUB:i * C_SUB + 1, :] if r is None else r, (C_SUB, w))
         for i, r in enumerate(beta_rows)], axis=0)
    has_prev = (row_blk < nsub - 1) if rev else (row_blk > 0)
    qs = jnp.where(has_prev, q * jnp.exp(b - beta_full), 0.0)

    q_stack = jnp.concatenate([jnp.where(lane_head == h, qs, 0.0) for h in range(C_HEADS)], axis=0).astype(BF16)
    qblocks = [i for i in range(nsub) if beta_rows[i] is not None]
    ks_parts = []
    for i in qblocks:
        prev = (row_blk > i) if rev else (row_blk < i)
        ks_parts.append(jnp.where(prev, kk * jnp.exp(beta_rows[i] - b), 0.0))
    ks_all = jnp.concatenate(ks_parts, axis=0).astype(BF16)
    a_all = _dot_nt(q_stack, ks_all)
    ar = lax.broadcasted_iota(jnp.int32, a_all.shape, 0)
    ac = lax.broadcasted_iota(jnp.int32, a_all.shape, 1)
    r_blk = (ar % cc) // C_SUB
    c_blk = ac // cc + (0 if rev else 1)
    a_all = jnp.where(r_blk == c_blk, a_all, 0.0).astype(BF16)
    v_bf = v.astype(BF16)
    r_all = _dot(a_all, jnp.concatenate([v_bf] * len(qblocks), axis=0))
    o = jnp.zeros((cc, w), F32)
    for h in range(C_HEADS):
        o = o + jnp.where(lane_head == h, r_all[h * cc:(h + 1) * cc, :], 0.0)

    ones_bd = _block_ones(w, C_DK, BF16)
    tsub = lax.broadcasted_iota(jnp.int32, (C_SUB, w), 0)
    diag_parts = []
    for i in range(nsub):
        r0 = i * C_SUB
        bb = b[r0:r0 + C_SUB, :]
        qq = q[r0:r0 + C_SUB, :]
        ps = []
        for s in range(C_SUB):
            keep = (tsub <= s) if rev else (tsub >= s)
            e = jnp.where(keep, jnp.exp(bb - b[r0 + s:r0 + s + 1, :]), 0.0)
            ps.append(qq * e * kk[r0 + s:r0 + s + 1, :])
        red = _dot(jnp.concatenate(ps, axis=0).astype(BF16), ones_bd)
        od = jnp.zeros((C_SUB, w), F32)
        for s in range(C_SUB):
            od = od + red[s * C_SUB:(s + 1) * C_SUB, :] * v[r0 + s:r0 + s + 1, :]
        diag_parts.append(od)
    o = o + jnp.concatenate(diag_parts, axis=0)

    st = st_ref[...]
    o = o + _dot_nt((q * jnp.exp(b)).astype(BF16), st.astype(BF16))
    kd = (kk * jnp.exp(b_tot - b)).astype(BF16)
    upd = _dot_tn(v_bf, kd)
    st_ref[...] = st * jnp.exp(b_tot) + upd * _block_ones(w, C_DK, F32)
    return o


def _hgrn_kernel(cf_ref, cb_ref, lbc_ref, of_ref, ob_ref, sf_ref, sb_ref):
    @pl.when(pl.program_id(1) == 0)
    def _():
        sf_ref[...] = jnp.zeros_like(sf_ref)
        sb_ref[...] = jnp.zeros_like(sb_ref)

    of_ref[0] = _hgrn_chunk(cf_ref[0], 1, lbc_ref[0], sf_ref, rev=False)
    ob_ref[0] = _hgrn_chunk(cb_ref[0], 2, lbc_ref[1], sb_ref, rev=True)


def _hgrn(oc, lbc, n_ctx):
    bsz, n, wc = oc.shape
    nch = n // C_CHUNK
    nctx = n_ctx // C_CHUNK

    def bwd_idx(c):
        return jnp.where(c < nctx, nctx - 1 - c, nch + nctx - 1 - c)

    return pl.pallas_call(
        _hgrn_kernel,
        grid=(bsz, nch),
        in_specs=[
            pl.BlockSpec((1, C_CHUNK, wc), lambda b, c: (b, c, 0)),
            pl.BlockSpec((1, C_CHUNK, wc), lambda b, c: (b, bwd_idx(c), 0)),
            pl.BlockSpec((2, 8, C_W), lambda b, c: (0, 0, 0)),
        ],
        out_specs=[
            pl.BlockSpec((1, C_CHUNK, C_W), lambda b, c: (b, c, 0)),
            pl.BlockSpec((1, C_CHUNK, C_W), lambda b, c: (b, bwd_idx(c), 0)),
        ],
        out_shape=[jax.ShapeDtypeStruct((bsz, n, C_W), F32)] * 2,
        scratch_shapes=[pltpu.VMEM((C_W, C_W), F32)] * 2,
        compiler_params=_cparams(("parallel", "arbitrary")),
        name="hgrn2_scan",
    )(oc, oc, lbc)


def _outproj_kernel(x_ref, a_ref, b_ref, of_ref, ob_ref, g_ref, cnw_ref, wa_ref, wb_ref, wc_ref,
                    mod_ref, ln2_ref, xo_ref, h2_ref, *, d):
    o = of_ref[0] + ob_ref[0]
    ssq = jnp.dot(o * o, _block_ones(C_W, C_DV, F32), precision=HIGHEST, preferred_element_type=F32)
    g = g_ref[0]
    c_out = o * lax.rsqrt(ssq * (1.0 / C_DV) + EPS) * cnw_ref[...] * (g * _sigmoid(g))
    mix = _dot(a_ref[0], wa_ref[...]) + _dot(b_ref[0], wb_ref[...]) + _dot(c_out.astype(BF16), wc_ref[...])
    gate1 = mod_ref[0, 0, :, 2 * d:3 * d]
    shift2 = mod_ref[0, 0, :, 3 * d:4 * d]
    scale2 = mod_ref[0, 0, :, 4 * d:5 * d]
    x = x_ref[0] + gate1 * mix
    xo_ref[0] = x
    y = x * lax.rsqrt(jnp.mean(x * x, axis=-1, keepdims=True) + EPS) * ln2_ref[...]
    h2_ref[0] = (y * (1.0 + scale2) + shift2).astype(BF16)


def _outproj(x, a_out, b_out, o_f, o_b, og, cnw, wa, wb, wc, modsel, ln2, first_block):
    bsz, n, d = x.shape
    nblk = n // TB - first_block
    full = lambda *s: pl.BlockSpec(s, lambda b, j: (0,) * len(s))
    tok = lambda w: pl.BlockSpec((1, TB, w), lambda b, j: (b, j + first_block, 0))
    return pl.pallas_call(
        functools.partial(_outproj_kernel, d=d),
        grid=(bsz, nblk),
        in_specs=[
            tok(d), tok(A_W),
            pl.BlockSpec((1, TB, B_HEADS * B_V), lambda b, j: (b, j, 0)) if first_block else tok(B_HEADS * B_V),
            tok(C_W), tok(C_W), tok(C_W),
            full(1, C_W), full(A_W, d), full(B_HEADS * B_V, d), full(C_W, d),
            pl.BlockSpec((1, 1, 1, modsel.shape[-1]), lambda b, j: (b, jnp.minimum(j + first_block, 1), 0, 0)),
            full(1, d),
        ],
        out_specs=[
            pl.BlockSpec((1, TB, d), lambda b, j: (b, j, 0)),
            pl.BlockSpec((1, TB, d), lambda b, j: (b, j, 0)),
        ],
        out_shape=[
            jax.ShapeDtypeStruct((bsz, nblk * TB, d), F32),
            jax.ShapeDtypeStruct((bsz, nblk * TB, d), BF16),
        ],
        compiler_params=_cparams(("parallel", "parallel")),
        name="out_proj",
    )(x, a_out, b_out, o_f, o_b, og, cnw, wa, wb, wc, modsel, ln2)


def _top16_exact(s):
    nrows = s.shape[0]
    iota = lax.broadcasted_iota(jnp.int32, s.shape, 0).astype(F32)
    rank = jnp.full(s.shape, P_TOPK, jnp.int32)
    vals = []
    for r in range(P_TOPK):
        m = jnp.max(s, axis=0, keepdims=True)
        idx = jnp.min(jnp.where(s == m, iota, float(nrows)), axis=0, keepdims=True)
        hit = iota == idx
        rank = jnp.where(hit, r, rank)
        s = jnp.where(hit, -jnp.inf, s)
        vals.append(m)
    return jnp.concatenate(vals, axis=0), rank


_MARK0 = int(np.array(0xFF7FFFFF, np.uint32).view(np.int32))


def _top16_marked(s):
    vals = []
    for r in range(P_TOPK):
        m = jnp.max(s, axis=0, keepdims=True)
        mark = float(np.array(_MARK0 - r, np.int32).view(np.float32))
        s = jnp.where(s == m, mark, s)
        vals.append(m)
    rr = _MARK0 - pltpu.bitcast(s, jnp.int32)
    rank = jnp.where(rr < 0, P_TOPK, jnp.where(rr > P_TOPK - 1, P_TOPK, rr))
    slack = jnp.sum(P_TOPK - rank, axis=0, keepdims=True) - (P_TOPK * (P_TOPK + 1)) // 2
    return jnp.concatenate(vals, axis=0), rank, slack


def _as_words(x):
    return pltpu.bitcast(x.astype(BF16), jnp.int32)


def _row_plane(row):
    n = row.shape[-1]
    tile = jnp.broadcast_to(row, (16, n)).astype(BF16)
    return jnp.concatenate([tile] * (P_N_KEYS // 16), axis=0)


def _route_kernel(h_ref, wq_ref, sk_ref, r2_ref, e2_ref, n1_ref, e1_ref,
                  q_scr, s_scr, v_scr, r1_scr, e1_scr):
    nl = P_N_KEYS
    q = _dot(h_ref[...], wq_ref[...]).astype(BF16)
    for l in range(2 * P_HEADS):
        q_scr[l] = q[:, l * P_HALF:(l + 1) * P_HALF]

    def lane_tile(lt, carry):
        row0 = pl.multiple_of(lt * nl, nl)

        def put_rank(h, p, rank):
            if p == 0:
                r1_scr[h] = rank
            else:
                r2_ref[lt, h] = _as_words(rank.astype(F32))

        def heads(hp, c):
            bad = jnp.zeros((1, nl), jnp.int32)
            for u in range(2 * ROUTE_HPT):
                h, p = hp * ROUTE_HPT + u // 2, u % 2
                s = _dot_nt(sk_ref[h, p], q_scr[2 * h + p, pl.ds(row0, nl), :])
                s_scr[u] = s
                v, rank, slack = _top16_marked(s)
                v_scr[p, h] = v
                put_rank(h, p, rank)
                e = jnp.exp(s - v[0:1, :])
                if p == 0:
                    e1_scr[h] = e
                else:
                    e2_ref[lt, h] = _as_words(e)
                bad = bad + slack

            @pl.when(jnp.max(bad) > 0)
            def _():
                for u in range(2 * ROUTE_HPT):
                    h, p = hp * ROUTE_HPT + u // 2, u % 2
                    ve, re = _top16_exact(s_scr[u])
                    v_scr[p, h] = ve
                    put_rank(h, p, re)

            return c

        lax.fori_loop(0, P_HEADS // ROUTE_HPT, heads, 0)

        v1 = v_scr[0]
        v2 = v_scr[1]
        ia = lax.broadcasted_iota(jnp.int32, v1.shape, 1).astype(F32)
        n = jnp.zeros(v1.shape, F32)
        g = jnp.broadcast_to(v2[:, 0:1, :], v1.shape)
        cmax = v1[:, 0:1, :] + v2[:, 0:1, :]
        z = jnp.zeros(cmax.shape, F32)
        for _ in range(P_TOPK):
            f = v1 + g
            m = jnp.max(f, axis=1, keepdims=True)
            a_star = jnp.min(jnp.where(f == m, ia, float(P_TOPK)), axis=1, keepdims=True)
            hit = ia == a_star
            n = n + jnp.where(hit, 1.0, 0.0)
            nsel = jnp.sum(jnp.where(hit, n, 0.0), axis=1, keepdims=True)
            nxt = jnp.sum(jnp.where(ia == nsel, v2, 0.0), axis=1, keepdims=True)
            nxt = jnp.where(nsel > P_TOPK - 0.5, -jnp.inf, nxt)
            g = jnp.where(hit, nxt, g)
            z = z + jnp.exp(m - cmax)
        zinv = 1.0 / z
        for h in range(P_HEADS):
            rank1 = r1_scr[h]
            n1 = jnp.zeros(rank1.shape, F32)
            for a in range(P_TOPK):
                n1 = jnp.where(rank1 == a, n[h, a:a + 1, :], n1)
            n1_ref[lt, h] = n1
            e1_ref[lt, h] = e1_scr[h] * zinv[h]
        return carry

    lax.fori_loop(0, h_ref.shape[0] // nl, lane_tile, 0)


def _route(h2, wq, sk):
    t, d = h2.shape
    tr = PEER_TT
    nl = P_N_KEYS
    oshape = (t // nl, P_HEADS, P_N_KEYS, nl)
    hshape = (t // nl, P_HEADS, P_N_KEYS // 2, nl)
    ospec = pl.BlockSpec((tr // nl, P_HEADS, P_N_KEYS, nl), lambda i: (i, 0, 0, 0))
    hspec = pl.BlockSpec((tr // nl, P_HEADS, P_N_KEYS // 2, nl), lambda i: (i, 0, 0, 0))
    return pl.pallas_call(
        _route_kernel,
        grid=(t // tr,),
        in_specs=[
            pl.BlockSpec((tr, d), lambda i: (i, 0)),
            pl.BlockSpec((d, P_HEADS * P_KEY_DIM), lambda i: (0, 0)),
            pl.BlockSpec((P_HEADS, 2, P_N_KEYS, P_HALF), lambda i: (0, 0, 0, 0)),
        ],
        out_specs=[hspec, hspec, ospec, ospec],
        out_shape=[jax.ShapeDtypeStruct(hshape, jnp.int32), jax.ShapeDtypeStruct(hshape, jnp.int32),
                   jax.ShapeDtypeStruct(oshape, F32), jax.ShapeDtypeStruct(oshape, F32)],
        scratch_shapes=[
            pltpu.VMEM((2 * P_HEADS, tr, P_HALF), BF16),
            pltpu.VMEM((2 * ROUTE_HPT, P_N_KEYS, nl), F32),
            pltpu.VMEM((2, P_HEADS, P_TOPK, nl), F32),
            pltpu.VMEM((P_HEADS, P_N_KEYS, nl), jnp.int32),
            pltpu.VMEM((P_HEADS, P_N_KEYS, nl), F32),
        ],
        compiler_params=_cparams(("parallel",)),
        name="peer_route",
    )(h2, wq, sk)


def _peer_kernel(h_ref, u_ref, vt_ref, r2_ref, e2_ref, n1_ref, e1_ref, x_ref, g5_ref, o_ref,
                 acc_ref, w_ref, a_scr, *, blocks_per_batch, ctx_blocks):
    i = pl.program_id(0)
    k = pl.program_id(1)

    @pl.when(k == 0)
    def _():
        acc_ref[...] = jnp.zeros_like(acc_ref)

    nl = P_N_KEYS
    zero = jnp.zeros((P_N_KEYS, nl), BF16)
    ltc = PEER_TC // nl
    njg = PEER_I1 // PEER_JG

    def gate_block(lt, j0):
        g = [None] * PEER_JG
        for h in range(P_HEADS):
            r2 = pltpu.bitcast(r2_ref[lt, h], BF16)
            e2 = pltpu.bitcast(e2_ref[lt, h], BF16)
            for jj in range(PEER_JG):
                n1row = _row_plane(n1_ref[lt, h, pl.ds(j0 + jj, 1), :])
                e1row = _row_plane(e1_ref[lt, h, pl.ds(j0 + jj, 1), :])
                term = jnp.where(r2 < n1row, e2, zero) * e1row
                g[jj] = term if h == 0 else g[jj] + term
        for jj in range(PEER_JG):
            rows = pl.ds(pl.multiple_of((j0 + jj) * P_N_KEYS, P_N_KEYS), P_N_KEYS)
            w_ref[lt, rows, :] = a_scr[lt, rows, :] * g[jj]

    for c in range(PEER_TT // PEER_TC):
        tok = slice(c * PEER_TC, (c + 1) * PEER_TC)
        a_t = _dot_nt(u_ref[...], h_ref[tok, :])
        half_cdf = (0.5 * lax.erf(a_t * (2.0 ** -0.5))).astype(BF16) + 0.5
        act_t = a_t.astype(BF16) * half_cdf
        for lc in range(ltc):
            a_scr[c * ltc + lc] = act_t[:, lc * nl:(lc + 1) * nl]

        def body(it, carry, c=c):
            gate_block(c * ltc + it // njg, (it % njg) * PEER_JG)
            return carry

        lax.fori_loop(0, ltc * njg, body, 0)
        w_t = jnp.concatenate([w_ref[c * ltc + lc] for lc in range(ltc)], axis=1)
        acc_ref[:, tok] += _dot(vt_ref[...], w_t)

    @pl.when(k == pl.num_programs(1) - 1)
    def _():
        y = acc_ref[...].T
        for u in range(PEER_TT // TB):
            sblk = i * (PEER_TT // TB) + u
            bidx = sblk // blocks_per_batch
            is_lat = (sblk - bidx * blocks_per_batch) >= ctx_blocks
            gate = g5_ref[2 * bidx + is_lat.astype(jnp.int32)]
            rs = slice(u * TB, (u + 1) * TB)
            o_ref[rs, :] = x_ref[rs, :] + gate * y[rs, :]


def _peer(h2, u_bf, vt_bf, r2, e2, n1, e1, x, g5, blocks_per_batch, ctx_blocks):
    t, d = h2.shape
    ne = u_bf.shape[0]
    nl = P_N_KEYS
    rspec = pl.BlockSpec((PEER_TT // nl, P_HEADS, P_N_KEYS // 2, nl), lambda i, k: (i, 0, 0, 0))
    nspec = pl.BlockSpec((PEER_TT // nl, P_HEADS, PEER_I1, nl), lambda i, k: (i, 0, k, 0))
    return pl.pallas_call(
        functools.partial(_peer_kernel, blocks_per_batch=blocks_per_batch, ctx_blocks=ctx_blocks),
        grid=(t // PEER_TT, ne // PEER_ET),
        in_specs=[
            pl.BlockSpec((PEER_TT, d), lambda i, k: (i, 0)),
            pl.BlockSpec((PEER_ET, d), lambda i, k: (k, 0)),
            pl.BlockSpec((d, PEER_ET), lambda i, k: (0, k)),
            rspec, rspec, nspec, nspec,
            pl.BlockSpec((PEER_TT, d), lambda i, k: (i, 0)),
            pl.BlockSpec(g5.shape, lambda i, k: (0, 0, 0)),
        ],
        out_specs=pl.BlockSpec((PEER_TT, d), lambda i, k: (i, 0)),
        out_shape=jax.ShapeDtypeStruct((t, d), F32),
        scratch_shapes=[pltpu.VMEM((d, PEER_TT), F32),
                        pltpu.VMEM((PEER_TT // nl, PEER_ET, nl), BF16),
                        pltpu.VMEM((PEER_TT // nl, PEER_ET, nl), BF16)],
        compiler_params=_cparams(("parallel", "arbitrary")),
        name="peer_experts",
    )(h2, u_bf, vt_bf, r2, e2, n1, e1, x, g5)


def _transpose_cast_kernel(x_ref, o_ref):
    o_ref[...] = x_ref[0].T.astype(BF16)


def _transpose_cast(v_all, layer):
    _, ne, d = v_all.shape
    te = 512
    return pl.pallas_call(
        _transpose_cast_kernel,
        grid=(ne // te,),
        in_specs=[pl.BlockSpec((1, te, d), lambda e: (layer, e, 0))],
        out_specs=pl.BlockSpec((d, te), lambda e: (0, e)),
        out_shape=jax.ShapeDtypeStruct((d, ne), BF16),
        compiler_params=_cparams(("parallel",)),
        name="expert_value_transpose",
    )(v_all)


def _rope_tables(n_ctx, n_lat):
    n_freq = B_ROPE // 4
    pos = np.arange(n_lat)
    inv_freq = ROPE_THETA ** (-np.arange(n_freq, dtype=np.float32) / n_freq)
    inv_freq = jnp.asarray(inv_freq, F32)
    rowp = jnp.asarray(pos // GRID_W, F32)
    colp = jnp.asarray(pos % GRID_W, F32)
    ang = jnp.stack([rowp[:, None] * inv_freq, colp[:, None] * inv_freq], axis=1)
    cos, sin = jnp.cos(ang), jnp.sin(ang)
    cos32 = jnp.concatenate([cos, cos], axis=2).reshape(n_lat, B_ROPE)
    sin32 = jnp.concatenate([-sin, sin], axis=2).reshape(n_lat, B_ROPE)
    pad_l = jnp.ones((n_lat, B_NOPE), F32)
    pad_r = jnp.ones((n_lat, B_HP - B_QK), F32)
    cos_l = jnp.concatenate([pad_l, cos32, pad_r], axis=1)
    sin_l = jnp.concatenate([0 * pad_l, sin32, 0 * pad_r], axis=1)
    cos_t = jnp.concatenate([jnp.ones((n_ctx, B_HP), F32), cos_l], axis=0)
    sin_t = jnp.concatenate([jnp.zeros((n_ctx, B_HP), F32), sin_l], axis=0)
    return cos_t, sin_t


_SWAP32 = np.arange(B_ROPE) ^ (B_ROPE // 4)


def _pad_head(nope, rope):
    z = jnp.zeros(nope.shape[:-1] + (B_HP - B_QK,), nope.dtype)
    out = jnp.concatenate([nope, rope, z], axis=-1)
    return out.reshape(out.shape[:-2] + (out.shape[-2] * B_HP,))


def _layer_weights(layer, w_in, w_out, a_norm_w, a_w_s, a_b_s, b_q_norm_w, b_w_uq, b_kv_norm_w, b_w_ukv,
                   b_q_head_norm_w, b_k_head_norm_w, c_out_norm_w, p_w_q, p_sub_keys):
    d = w_in.shape[1]
    wi = w_in[layer]
    offs = np.cumsum([0, A_W, A_W, B_Q_RANK, B_KV_RANK, B_ROPE, C_W, C_W, C_W, C_W, C_W])
    col = lambda i: wi[:, offs[i]:offs[i + 1]]
    w_kr = col(4)
    zl = jnp.zeros((d, B_NOPE), F32)
    zr = jnp.zeros((d, B_HP - B_QK), F32)
    kr_placed = jnp.concatenate([zl, w_kr, zr], axis=1)
    kr_swapped = jnp.concatenate([zl, w_kr[:, _SWAP32], zr], axis=1)
    w_all = jnp.concatenate(
        [col(0), col(1), col(2), col(3), kr_placed, kr_swapped, col(5), col(6), col(7), col(8), col(9)],
        axis=1).astype(BF16)
    widths = (2 * A_W, B_Q_RANK + B_KV_RANK + 2 * B_HP, 4 * C_W, C_W)

    wuq = b_w_uq[layer].reshape(B_Q_RANK, B_HEADS, B_QK)
    wq_p = _pad_head(wuq[..., :B_NOPE], wuq[..., B_NOPE:]).astype(BF16)
    wq_s = _pad_head(0 * wuq[..., :B_NOPE], wuq[..., B_NOPE:][..., _SWAP32]).astype(BF16)
    wukv = b_w_ukv[layer].reshape(B_KV_RANK, B_HEADS, B_NOPE + B_V)
    wk_p = _pad_head(wukv[..., :B_NOPE], jnp.zeros((B_KV_RANK, B_HEADS, B_ROPE), F32)).astype(BF16)
    wv = wukv[..., B_NOPE:].reshape(B_KV_RANK, B_HEADS * B_V).astype(BF16)
    qn, kn = b_q_head_norm_w[layer], b_k_head_norm_w[layer]
    zpad = jnp.zeros((B_HP - B_QK,), F32)
    z64 = jnp.zeros((B_NOPE,), F32)
    hw = jnp.stack([
        jnp.concatenate([qn, zpad]),
        jnp.concatenate([z64, qn[B_NOPE:][_SWAP32], zpad]),
        jnp.concatenate([kn, zpad]),
        jnp.concatenate([z64, kn[B_NOPE:][_SWAP32], zpad]),
    ] + [jnp.zeros((B_HP,), F32)] * 4, axis=0)

    wo = w_out[layer].astype(BF16)
    return dict(
        w_all=w_all, widths=widths,
        a_nw=a_norm_w[layer].reshape(1, A_W),
        a_ws=a_w_s[layer].astype(BF16),
        a_bias=jnp.repeat(a_b_s[layer].T, A_HD, axis=1),
        qnw=b_q_norm_w[layer].reshape(1, B_Q_RANK), kvnw=b_kv_norm_w[layer].reshape(1, B_KV_RANK),
        wq_p=wq_p, wq_s=wq_s, wk_p=wk_p, wv=wv, hw=hw,
        cnw=jnp.tile(c_out_norm_w[layer], C_HEADS).reshape(1, C_W),
        wo_a=wo[:A_W], wo_b=wo[A_W:A_W + B_HEADS * B_V], wo_c=wo[A_W + B_HEADS * B_V:],
        p_wq=p_w_q[layer].astype(BF16), p_sk=p_sub_keys[layer].astype(BF16),
    )


def kernel(x, c, ctx, c_ctx, ln1_w, ln2_w, w_mod, b_mod, w_in, w_out, a_norm_w, a_w_s, a_b_s, b_q_norm_w,
           b_w_uq, b_kv_norm_w, b_w_ukv, b_q_head_norm_w, b_k_head_norm_w, c_lb_logits, c_out_norm_w,
           p_w_q, p_sub_keys, p_u, p_v):
    bsz, n_lat, d = x.shape
    n_ctx = ctx.shape[1]
    depth = w_in.shape[0]
    n = n_ctx + n_lat
    ctx_blocks = n_ctx // TB
    blocks = n // TB

    mrows = -(-(bsz + 1) // 8) * 8
    cvec = jnp.concatenate([c, c_ctx[None, :], jnp.zeros((mrows - bsz - 1, d), F32)], axis=0)
    mod_all = _modulation(cvec, w_mod, b_mod)

    lb = jnp.cumsum(jax.nn.softmax(c_lb_logits.astype(F32), axis=0), axis=0)
    lb = lb - lb[0:1]
    lbc_all = jnp.stack([jnp.log(lb), jnp.log1p(-lb), 1.0 - lb] + [jnp.zeros_like(lb)] * 5, axis=2)

    cos_t, sin_t = _rope_tables(n_ctx, n_lat)
    xc = jnp.concatenate([ctx, x], axis=1)

    for layer in range(depth):
        last = layer == depth - 1
        w = _layer_weights(layer, w_in, w_out, a_norm_w, a_w_s, a_b_s, b_q_norm_w, b_w_uq, b_kv_norm_w,
                           b_w_ukv, b_q_head_norm_w, b_k_head_norm_w, c_out_norm_w, p_w_q, p_sub_keys)
        mod_b = mod_all[layer, :bsz]
        mod_c = jnp.broadcast_to(mod_all[layer, bsz][None, :], mod_b.shape)
        modsel = jnp.stack([mod_c, mod_b], axis=1)[:, :, None, :]
        first_block = ctx_blocks if last else 0

        a_out, q, k, v, oc, og = _front(xc, ln1_w[layer].reshape(1, d), modsel, w, cos_t, sin_t)
        b_lat = _attention(q, k, v, ctx_blocks, blocks - ctx_blocks, n, n_lat, 2)
        if last:
            b_out = b_lat
        else:
            b_ctx = _attention(q, k, v, 0, ctx_blocks, n_ctx, n_ctx, 1)
            b_out = jnp.concatenate([b_ctx, b_lat], axis=1)
        o_f, o_b = _hgrn(oc, lbc_all[layer], n_ctx)
        x_new, h2 = _outproj(xc, a_out, b_out, o_f, o_b, og, w["cnw"], w["wo_a"], w["wo_b"], w["wo_c"],
                             modsel, ln2_w[layer].reshape(1, d), first_block)

        t = x_new.shape[0] * x_new.shape[1]
        h2f = h2.reshape(t, d)
        r2, e2, n1, e1 = _route(h2f, w["p_wq"], w["p_sk"])
        g5 = modsel[:, :, :, 5 * d:6 * d].reshape(2 * bsz, 1, d)
        u_bf = p_u[layer].astype(BF16)
        vt_bf = _transpose_cast(p_v, layer)
        out = _peer(h2f, u_bf, vt_bf, r2, e2, n1, e1, x_new.reshape(t, d), g5,
                    blocks - first_block, ctx_blocks - first_block)
        xc = out.reshape(bsz, t // bsz, d)
    return xc
```

```python
import functools
import math

import jax
import jax.numpy as jnp
import numpy as np
from jax import lax
from jax.experimental import pallas as pl
from jax.experimental.pallas import tpu as pltpu

F32 = jnp.float32
BF16 = jnp.bfloat16
HIGHEST = lax.Precision.HIGHEST

EPS = 1e-6
GRID_W = 64
ROPE_THETA = 10000.0

A_HEADS, A_HD, A_CHUNK = 4, 64, 128
A_W = A_HEADS * A_HD
B_HEADS, B_NOPE, B_ROPE, B_V = 8, 64, 32, 64
B_QK = B_NOPE + B_ROPE
B_HP = 128
B_Q_RANK, B_KV_RANK = 256, 128
C_HEADS, C_DK, C_DV, C_CHUNK = 4, 64, 64, 64
C_W = C_HEADS * C_DK
C_SUB = 16
P_HEADS, P_KEY_DIM, P_N_KEYS, P_TOPK = 8, 256, 128, 16
P_HALF = P_KEY_DIM // 2

TB = 256
PEER_TT = 1024
PEER_TC = 1024
PEER_ET = 1024
PEER_I1 = PEER_ET // P_N_KEYS
PEER_JG = 4
ROUTE_HPT = 8
VMEM_LIMIT = 56 * 1024 * 1024


def _cparams(sem, flags=None):
    return pltpu.CompilerParams(dimension_semantics=sem, vmem_limit_bytes=VMEM_LIMIT, flags=flags)


def _dot_nt(a, b):
    return lax.dot_general(a, b, (((1,), (1,)), ((), ())), preferred_element_type=F32)


def _dot_tn(a, b):
    return lax.dot_general(a, b, (((0,), (0,)), ((), ())), preferred_element_type=F32)


def _dot(a, b):
    return jnp.dot(a, b, preferred_element_type=F32)


def _sigmoid(x):
    return 1.0 / (1.0 + jnp.exp(-x))


def _block_ones(n, blk, dtype):
    r = lax.broadcasted_iota(jnp.int32, (n, n), 0) // blk
    c = lax.broadcasted_iota(jnp.int32, (n, n), 1) // blk
    return (r == c).astype(dtype)


def _mod_kernel(c_ref, w_ref, b_ref, o_ref):
    c = c_ref[...]
    sc = c * _sigmoid(c)
    o_ref[0] = _dot(sc.astype(BF16), w_ref[0].astype(BF16)) + b_ref[0]


def _modulation(cvec, w_mod, b_mod):
    depth, d, n6 = w_mod.shape
    rows = cvec.shape[0]
    tn = 1024
    return pl.pallas_call(
        _mod_kernel,
        grid=(depth, n6 // tn),
        in_specs=[
            pl.BlockSpec((rows, d), lambda l, n: (0, 0)),
            pl.BlockSpec((1, d, tn), lambda l, n: (l, 0, n)),
            pl.BlockSpec((1, 1, tn), lambda l, n: (l, 0, n)),
        ],
        out_specs=pl.BlockSpec((1, rows, tn), lambda l, n: (l, 0, n)),
        out_shape=jax.ShapeDtypeStruct((depth, rows, n6), F32),
        compiler_params=_cparams(("parallel", "parallel")),
        name="adaln_mod",
    )(cvec, w_mod, b_mod.reshape(depth, 1, n6))


def _amix_body(ua, nw_ref, ws_ref, bias_ref):
    u = ua[:, 0:A_W]
    v = ua[:, A_W:2 * A_W]
    ssq = jnp.dot(v * v, _block_ones(A_W, A_HD, F32), precision=HIGHEST, preferred_element_type=F32)
    vn = v * lax.rsqrt(ssq * (1.0 / A_HD) + EPS) * nw_ref[...]
    lane_head = lax.broadcasted_iota(jnp.int32, (A_CHUNK, A_W), 1) // A_HD
    outs = []
    for c in range(TB // A_CHUNK):
        rows = slice(c * A_CHUNK, (c + 1) * A_CHUNK)
        acc = bias_ref[...]
        for h in range(A_HEADS):
            vm = jnp.where(lane_head == h, vn[rows], 0.0).astype(BF16)
            acc = acc + _dot(ws_ref[h], vm)
        outs.append((u[rows] * acc).astype(BF16))
    return jnp.concatenate(outs, axis=0)


def _mla_body(ob, cos, sin, qnw_ref, kvnw_ref, wq_ref, wqs_ref, wk_ref, wv_ref, hw_ref, q_ref, k_ref, v_ref):
    cq = ob[:, 0:B_Q_RANK]
    ckv = ob[:, B_Q_RANK:B_Q_RANK + B_KV_RANK]
    krp = ob[:, B_Q_RANK + B_KV_RANK:B_Q_RANK + B_KV_RANK + B_HP]
    krs = ob[:, B_Q_RANK + B_KV_RANK + B_HP:B_Q_RANK + B_KV_RANK + 2 * B_HP]
    cqn = (cq * lax.rsqrt(jnp.mean(cq * cq, axis=-1, keepdims=True) + EPS) * qnw_ref[...]).astype(BF16)
    ckn = (ckv * lax.rsqrt(jnp.mean(ckv * ckv, axis=-1, keepdims=True) + EPS) * kvnw_ref[...]).astype(BF16)
    q_raw = _dot(cqn, wq_ref[...])
    q_swp = _dot(cqn, wqs_ref[...])
    k_raw = _dot(ckn, wk_ref[...])
    v_all = _dot(ckn, wv_ref[...])
    qw, qws, kw, kws = hw_ref[0:1, :], hw_ref[1:2, :], hw_ref[2:3, :], hw_ref[3:4, :]
    k_rot_sw = krs * kws * sin
    for h in range(B_HEADS):
        sl = slice(h * B_HP, (h + 1) * B_HP)
        qh = q_raw[:, sl]
        rq = lax.rsqrt(jnp.sum(qh * qh, axis=-1, keepdims=True) * (1.0 / B_QK) + EPS)
        q_ref[0, h] = (rq * (qh * qw * cos + q_swp[:, sl] * qws * sin)).astype(BF16)
        kh = k_raw[:, sl] + krp
        rk = lax.rsqrt(jnp.sum(kh * kh, axis=-1, keepdims=True) * (1.0 / B_QK) + EPS)
        k_ref[0, h] = (rk * (kh * kw * cos + k_rot_sw)).astype(BF16)
        v_ref[0, h] = v_all[:, h * B_V:(h + 1) * B_V].astype(BF16)


def _front_kernel(x_ref, lnw_ref, mod_ref, w_ref, anw_ref, aws_ref, abias_ref, cos_ref, sin_ref,
                  qnw_ref, kvnw_ref, wq_ref, wqs_ref, wk_ref, wv_ref, hw_ref,
                  a_ref, q_ref, k_ref, v_ref, oc_ref, og_ref, *, d, widths):
    x = x_ref[0]
    ms = jnp.mean(x * x, axis=-1, keepdims=True)
    y = x * lax.rsqrt(ms + EPS) * lnw_ref[...]
    shift = mod_ref[0, 0, :, 0:d]
    scale = mod_ref[0, 0, :, d:2 * d]
    h = (y * (1.0 + scale) + shift).astype(BF16)
    p = _dot(h, w_ref[...])
    na, nb, nc, _ = widths
    oc_ref[0] = p[:, na + nb:na + nb + nc]
    og_ref[0] = p[:, na + nb + nc:]
    a_ref[0] = _amix_body(p[:, 0:na], anw_ref, aws_ref, abias_ref)
    _mla_body(p[:, na:na + nb], cos_ref[...], sin_ref[...], qnw_ref, kvnw_ref, wq_ref, wqs_ref, wk_ref,
              wv_ref, hw_ref, q_ref, k_ref, v_ref)


def _front(x, lnw, modsel, w, cos_t, sin_t):
    bsz, n, d = x.shape
    na, nb, nc, ng = w["widths"]
    full = lambda *s: pl.BlockSpec(s, lambda b, j: (0,) * len(s))
    tok = lambda width: pl.BlockSpec((1, TB, width), lambda b, j: (b, j, 0))
    head = lambda width: pl.BlockSpec((1, B_HEADS, TB, width), lambda b, j: (b, 0, j, 0))
    return pl.pallas_call(
        functools.partial(_front_kernel, d=d, widths=w["widths"]),
        grid=(bsz, n // TB),
        in_specs=[
            tok(d), full(1, d),
            pl.BlockSpec((1, 1, 1, modsel.shape[-1]), lambda b, j: (b, jnp.minimum(j, 1), 0, 0)),
            full(d, w["w_all"].shape[1]),
            full(1, A_W), full(A_HEADS, A_CHUNK, A_CHUNK), full(A_CHUNK, A_W),
            pl.BlockSpec((TB, B_HP), lambda b, j: (j, 0)),
            pl.BlockSpec((TB, B_HP), lambda b, j: (j, 0)),
            full(1, B_Q_RANK), full(1, B_KV_RANK),
            full(B_Q_RANK, B_HEADS * B_HP), full(B_Q_RANK, B_HEADS * B_HP),
            full(B_KV_RANK, B_HEADS * B_HP), full(B_KV_RANK, B_HEADS * B_V),
            full(8, B_HP),
        ],
        out_specs=[tok(A_W), head(B_HP), head(B_HP), head(B_V), tok(nc), tok(ng)],
        out_shape=[
            jax.ShapeDtypeStruct((bsz, n, A_W), BF16),
            jax.ShapeDtypeStruct((bsz, B_HEADS, n, B_HP), BF16),
            jax.ShapeDtypeStruct((bsz, B_HEADS, n, B_HP), BF16),
            jax.ShapeDtypeStruct((bsz, B_HEADS, n, B_V), BF16),
            jax.ShapeDtypeStruct((bsz, n, nc), F32),
            jax.ShapeDtypeStruct((bsz, n, ng), F32),
        ],
        compiler_params=_cparams(("parallel", "parallel")),
        name="in_proj_mix",
    )(x, lnw, modsel, w["w_all"], w["a_nw"], w["a_ws"], w["a_bias"], cos_t, sin_t, w["qnw"], w["kvnw"],
      w["wq_p"], w["wq_s"], w["wk_p"], w["wv"], w["hw"])


def _attn_kernel(*refs, nqb):
    q_refs, (k_ref, v_ref, o_ref, o_scr) = refs[:nqb], refs[nqb:]
    scale = (B_QK ** -0.5) * math.log2(math.e)
    for h in range(B_HEADS):
        q = q_refs[0][0, h] if nqb == 1 else jnp.concatenate([r[0, h] for r in q_refs], axis=0)
        s = _dot_nt(q, k_ref[0, h])
        m = jnp.max(s, axis=-1, keepdims=True)
        p = jnp.exp2((s - m) * scale)
        l = jnp.sum(p, axis=-1, keepdims=True)
        o = _dot(p.astype(BF16), v_ref[0, h])
        o_scr[:, h * B_V:(h + 1) * B_V] = o / l
    o_ref[0] = o_scr[...].astype(BF16)


def _attention(q, k, v, first_qblock, n_qblocks, n_keys, out_rows, nqb):
    bsz = q.shape[0]
    qspec = lambda u: pl.BlockSpec((1, B_HEADS, TB, B_HP), lambda b, j: (b, 0, j * nqb + u + first_qblock, 0))
    return pl.pallas_call(
        functools.partial(_attn_kernel, nqb=nqb),
        grid=(bsz, n_qblocks // nqb),
        in_specs=[qspec(u) for u in range(nqb)] + [
            pl.BlockSpec((1, B_HEADS, n_keys, B_HP), lambda b, j: (b, 0, 0, 0)),
            pl.BlockSpec((1, B_HEADS, n_keys, B_V), lambda b, j: (b, 0, 0, 0)),
        ],
        out_specs=pl.BlockSpec((1, nqb * TB, B_HEADS * B_V), lambda b, j: (b, j, 0)),
        out_shape=jax.ShapeDtypeStruct((bsz, out_rows, B_HEADS * B_V), BF16),
        scratch_shapes=[pltpu.VMEM((nqb * TB, B_HEADS * B_V), F32)],
        compiler_params=_cparams(("parallel", "arbitrary")),
        name="mla_attention",
    )(*([q] * nqb), k, v)


def _hgrn_chunk(blk, zcol, lbc, st_ref, rev):
    cc, w = C_CHUNK, C_W
    q = blk[:, 0:w] * (C_DK ** -0.5)
    z = blk[:, zcol * w:(zcol + 1) * w]
    v = blk[:, 3 * w:4 * w]
    log_lb, log1m_lb, one_m_lb = lbc[0:1, :], lbc[1:2, :], lbc[2:3, :]
    az = jnp.abs(z)
    sp = jnp.log1p(jnp.exp(-az))
    lsig = jnp.minimum(z, 0.0) - sp
    t2 = log1m_lb + lsig
    mx = jnp.maximum(log_lb, t2)
    mn = jnp.minimum(log_lb, t2)
    logf = mx + jnp.log1p(jnp.exp(mn - mx))
    kk = one_m_lb * _sigmoid(-z)

    ti = lax.broadcasted_iota(jnp.int32, (cc, cc), 0)
    ui = lax.broadcasted_iota(jnp.int32, (cc, cc), 1)
    tri = ((ui >= ti) if rev else (ui <= ti)).astype(F32)
    b = jnp.dot(tri, logf, precision=HIGHEST, preferred_element_type=F32)
    b_tot = b[0:1, :] if rev else b[cc - 1:cc, :]

    row = lax.broadcasted_iota(jnp.int32, (cc, w), 0)
    lane_head = lax.broadcasted_iota(jnp.int32, (cc, w), 1) // C_DK
    nsub = cc // C_SUB
    row_blk = row // C_SUB

    beta_rows = []
    for i in range(nsub):
        if rev:
            src = None if i == nsub - 1 else b[(i + 1) * C_SUB:(i + 1) * C_SUB + 1, :]
        else:
            src = None if i == 0 else b[i * C_SUB - 1:i * C_SUB, :]
        beta_rows.append(src)
    beta_full = jnp.concatenate(
        [jnp.broadcast_to(b[i * C_SUB:i * C_SUB + 1, :] if r is None else r, (C_SUB, w))
         for i, r in enumerate(beta_rows)], axis=0)
    has_prev = (row_blk < nsub - 1) if rev else (row_blk > 0)
    qs = jnp.where(has_prev, q * jnp.exp(b - beta_full), 0.0)

    q_stack = jnp.concatenate([jnp.where(lane_head == h, qs, 0.0) for h in range(C_HEADS)], axis=0).astype(BF16)
    qblocks = [i for i in range(nsub) if beta_rows[i] is not None]
    ks_parts = []
    for i in qblocks:
        prev = (row_blk > i) if rev else (row_blk < i)
        ks_parts.append(jnp.where(prev, kk * jnp.exp(beta_rows[i] - b), 0.0))
    ks_all = jnp.concatenate(ks_parts, axis=0).astype(BF16)
    a_all = _dot_nt(q_stack, ks_all)
    ar = lax.broadcasted_iota(jnp.int32, a_all.shape, 0)
    ac = lax.broadcasted_iota(jnp.int32, a_all.shape, 1)
    r_blk = (ar % cc) // C_SUB
    c_blk = ac // cc + (0 if rev else 1)
    a_all = jnp.where(r_blk == c_blk, a_all, 0.0).astype(BF16)
    v_bf = v.astype(BF16)
    r_all = _dot(a_all, jnp.concatenate([v_bf] * len(qblocks), axis=0))
    o = jnp.zeros((cc, w), F32)
    for h in range(C_HEADS):
        o = o + jnp.where(lane_head == h, r_all[h * cc:(h + 1) * cc, :], 0.0)

    ones_bd = _block_ones(w, C_DK, BF16)
    tsub = lax.broadcasted_iota(jnp.int32, (C_SUB, w), 0)
    diag_parts = []
    for i in range(nsub):
        r0 = i * C_SUB
        bb = b[r0:r0 + C_SUB, :]
        qq = q[r0:r0 + C_SUB, :]
        ps = []
        for s in range(C_SUB):
            keep = (tsub <= s) if rev else (tsub >= s)
            e = jnp.where(keep, jnp.exp(bb - b[r0 + s:r0 + s + 1, :]), 0.0)
            ps.append(qq * e * kk[r0 + s:r0 + s + 1, :])
        red = _dot(jnp.concatenate(ps, axis=0).astype(BF16), ones_bd)
        od = jnp.zeros((C_SUB, w), F32)
        for s in range(C_SUB):
            od = od + red[s * C_SUB:(s + 1) * C_SUB, :] * v[r0 + s:r0 + s + 1, :]
        diag_parts.append(od)
    o = o + jnp.concatenate(diag_parts, axis=0)

    st = st_ref[...]
    o = o + _dot_nt((q * jnp.exp(b)).astype(BF16), st.astype(BF16))
    kd = (kk * jnp.exp(b_tot - b)).astype(BF16)
    upd = _dot_tn(v_bf, kd)
    st_ref[...] = st * jnp.exp(b_tot) + upd * _block_ones(w, C_DK, F32)
    return o


def _hgrn_kernel(cf_ref, cb_ref, lbc_ref, of_ref, ob_ref, sf_ref, sb_ref):
    @pl.when(pl.program_id(1) == 0)
    def _():
        sf_ref[...] = jnp.zeros_like(sf_ref)
        sb_ref[...] = jnp.zeros_like(sb_ref)

    of_ref[0] = _hgrn_chunk(cf_ref[0], 1, lbc_ref[0], sf_ref, rev=False)
    ob_ref[0] = _hgrn_chunk(cb_ref[0], 2, lbc_ref[1], sb_ref, rev=True)


def _hgrn(oc, lbc, n_ctx):
    bsz, n, wc = oc.shape
    nch = n // C_CHUNK
    nctx = n_ctx // C_CHUNK

    def bwd_idx(c):
        return jnp.where(c < nctx, nctx - 1 - c, nch + nctx - 1 - c)

    return pl.pallas_call(
        _hgrn_kernel,
        grid=(bsz, nch),
        in_specs=[
            pl.BlockSpec((1, C_CHUNK, wc), lambda b, c: (b, c, 0)),
            pl.BlockSpec((1, C_CHUNK, wc), lambda b, c: (b, bwd_idx(c), 0)),
            pl.BlockSpec((2, 8, C_W), lambda b, c: (0, 0, 0)),
        ],
        out_specs=[
            pl.BlockSpec((1, C_CHUNK, C_W), lambda b, c: (b, c, 0)),
            pl.BlockSpec((1, C_CHUNK, C_W), lambda b, c: (b, bwd_idx(c), 0)),
        ],
        out_shape=[jax.ShapeDtypeStruct((bsz, n, C_W), F32)] * 2,
        scratch_shapes=[pltpu.VMEM((C_W, C_W), F32)] * 2,
        compiler_params=_cparams(("parallel", "arbitrary")),
        name="hgrn2_scan",
    )(oc, oc, lbc)


def _outproj_kernel(x_ref, a_ref, b_ref, of_ref, ob_ref, g_ref, cnw_ref, wa_ref, wb_ref, wc_ref,
                    mod_ref, ln2_ref, xo_ref, h2_ref, *, d):
    o = of_ref[0] + ob_ref[0]
    ssq = jnp.dot(o * o, _block_ones(C_W, C_DV, F32), precision=HIGHEST, preferred_element_type=F32)
    g = g_ref[0]
    c_out = o * lax.rsqrt(ssq * (1.0 / C_DV) + EPS) * cnw_ref[...] * (g * _sigmoid(g))
    mix = _dot(a_ref[0], wa_ref[...]) + _dot(b_ref[0], wb_ref[...]) + _dot(c_out.astype(BF16), wc_ref[...])
    gate1 = mod_ref[0, 0, :, 2 * d:3 * d]
    shift2 = mod_ref[0, 0, :, 3 * d:4 * d]
    scale2 = mod_ref[0, 0, :, 4 * d:5 * d]
    x = x_ref[0] + gate1 * mix
    xo_ref[0] = x
    y = x * lax.rsqrt(jnp.mean(x * x, axis=-1, keepdims=True) + EPS) * ln2_ref[...]
    h2_ref[0] = (y * (1.0 + scale2) + shift2).astype(BF16)


def _outproj(x, a_out, b_out, o_f, o_b, og, cnw, wa, wb, wc, modsel, ln2, first_block):
    bsz, n, d = x.shape
    nblk = n // TB - first_block
    full = lambda *s: pl.BlockSpec(s, lambda b, j: (0,) * len(s))
    tok = lambda w: pl.BlockSpec((1, TB, w), lambda b, j: (b, j + first_block, 0))
    return pl.pallas_call(
        functools.partial(_outproj_kernel, d=d),
        grid=(bsz, nblk),
        in_specs=[
            tok(d), tok(A_W),
            pl.BlockSpec((1, TB, B_HEADS * B_V), lambda b, j: (b, j, 0)) if first_block else tok(B_HEADS * B_V),
            tok(C_W), tok(C_W), tok(C_W),
            full(1, C_W), full(A_W, d), full(B_HEADS * B_V, d), full(C_W, d),
            pl.BlockSpec((1, 1, 1, modsel.shape[-1]), lambda b, j: (b, jnp.minimum(j + first_block, 1), 0, 0)),
            full(1, d),
        ],
        out_specs=[
            pl.BlockSpec((1, TB, d), lambda b, j: (b, j, 0)),
            pl.BlockSpec((1, TB, d), lambda b, j: (b, j, 0)),
        ],
        out_shape=[
            jax.ShapeDtypeStruct((bsz, nblk * TB, d), F32),
            jax.ShapeDtypeStruct((bsz, nblk * TB, d), BF16),
        ],
        compiler_params=_cparams(("parallel", "parallel")),
        name="out_proj",
    )(x, a_out, b_out, o_f, o_b, og, cnw, wa, wb, wc, modsel, ln2)


def _top16_exact(s):
    nrows = s.shape[0]
    iota = lax.broadcasted_iota(jnp.int32, s.shape, 0).astype(F32)
    rank = jnp.full(s.shape, P_TOPK, jnp.int32)
    vals = []
    for r in range(P_TOPK):
        m = jnp.max(s, axis=0, keepdims=True)
        idx = jnp.min(jnp.where(s == m, iota, float(nrows)), axis=0, keepdims=True)
        hit = iota == idx
        rank = jnp.where(hit, r, rank)
        s = jnp.where(hit, -jnp.inf, s)
        vals.append(m)
    return jnp.concatenate(vals, axis=0), rank


_MARK0 = int(np.array(0xFF7FFFFF, np.uint32).view(np.int32))


def _top16_marked(s):
    vals = []
    for r in range(P_TOPK):
        m = jnp.max(s, axis=0, keepdims=True)
        mark = float(np.array(_MARK0 - r, np.int32).view(np.float32))
        s = jnp.where(s == m, mark, s)
        vals.append(m)
    rr = _MARK0 - pltpu.bitcast(s, jnp.int32)
    rank = jnp.where(rr < 0, P_TOPK, jnp.where(rr > P_TOPK - 1, P_TOPK, rr))
    slack = jnp.sum(P_TOPK - rank, axis=0, keepdims=True) - (P_TOPK * (P_TOPK + 1)) // 2
    return jnp.concatenate(vals, axis=0), rank, slack


def _as_words(x):
    return pltpu.bitcast(x.astype(BF16), jnp.int32)


def _row_plane(row):
    n = row.shape[-1]
    tile = jnp.broadcast_to(row, (16, n)).astype(BF16)
    return jnp.concatenate([tile] * (P_N_KEYS // 16), axis=0)


def _route_kernel(h_ref, wq_ref, sk_ref, r2_ref, e2_ref, n1_ref, e1_ref,
                  q_scr, s_scr, v_scr, r1_scr, e1_scr):
    nl = P_N_KEYS
    q = _dot(h_ref[...], wq_ref[...]).astype(BF16)
    for l in range(2 * P_HEADS):
        q_scr[l] = q[:, l * P_HALF:(l + 1) * P_HALF]

    def lane_tile(lt, carry):
        row0 = pl.multiple_of(lt * nl, nl)

        def put_rank(h, p, rank):
            if p == 0:
                r1_scr[h] = rank
            else:
                r2_ref[lt, h] = _as_words(rank.astype(F32))

        def heads(hp, c):
            bad = jnp.zeros((1, nl), jnp.int32)
            for u in range(2 * ROUTE_HPT):
                h, p = hp * ROUTE_HPT + u // 2, u % 2
                s = _dot_nt(sk_ref[h, p], q_scr[2 * h + p, pl.ds(row0, nl), :])
                s_scr[u] = s
                v, rank, slack = _top16_marked(s)
                v_scr[p, h] = v
                put_rank(h, p, rank)
                e = jnp.exp(s - v[0:1, :])
                if p == 0:
                    e1_scr[h] = e
                else:
                    e2_ref[lt, h] = _as_words(e)
                bad = bad + slack

            @pl.when(jnp.max(bad) > 0)
            def _():
                for u in range(2 * ROUTE_HPT):
                    h, p = hp * ROUTE_HPT + u // 2, u % 2
                    ve, re = _top16_exact(s_scr[u])
                    v_scr[p, h] = ve
                    put_rank(h, p, re)

            return c

        lax.fori_loop(0, P_HEADS // ROUTE_HPT, heads, 0)

        v1 = v_scr[0]
        v2 = v_scr[1]
        ia = lax.broadcasted_iota(jnp.int32, v1.shape, 1).astype(F32)
        n = jnp.zeros(v1.shape, F32)
        g = jnp.broadcast_to(v2[:, 0:1, :], v1.shape)
        cmax = v1[:, 0:1, :] + v2[:, 0:1, :]
        z = jnp.zeros(cmax.shape, F32)
        for _ in range(P_TOPK):
            f = v1 + g
            m = jnp.max(f, axis=1, keepdims=True)
            a_star = jnp.min(jnp.where(f == m, ia, float(P_TOPK)), axis=1, keepdims=True)
            hit = ia == a_star
            n = n + jnp.where(hit, 1.0, 0.0)
            nsel = jnp.sum(jnp.where(hit, n, 0.0), axis=1, keepdims=True)
            nxt = jnp.sum(jnp.where(ia == nsel, v2, 0.0), axis=1, keepdims=True)
            nxt = jnp.where(nsel > P_TOPK - 0.5, -jnp.inf, nxt)
            g = jnp.where(hit, nxt, g)
            z = z + jnp.exp(m - cmax)
        zinv = 1.0 / z
        for h in range(P_HEADS):
            rank1 = r1_scr[h]
            n1 = jnp.zeros(rank1.shape, F32)
            for a in range(P_TOPK):
                n1 = jnp.where(rank1 == a, n[h, a:a + 1, :], n1)
            n1_ref[lt, h] = n1
            e1_ref[lt, h] = e1_scr[h] * zinv[h]
        return carry

    lax.fori_loop(0, h_ref.shape[0] // nl, lane_tile, 0)


def _route(h2, wq, sk):
    t, d = h2.shape
    tr = PEER_TT
    nl = P_N_KEYS
    oshape = (t // nl, P_HEADS, P_N_KEYS, nl)
    hshape = (t // nl, P_HEADS, P_N_KEYS // 2, nl)
    ospec = pl.BlockSpec((tr // nl, P_HEADS, P_N_KEYS, nl), lambda i: (i, 0, 0, 0))
    hspec = pl.BlockSpec((tr // nl, P_HEADS, P_N_KEYS // 2, nl), lambda i: (i, 0, 0, 0))
    return pl.pallas_call(
        _route_kernel,
        grid=(t // tr,),
        in_specs=[
            pl.BlockSpec((tr, d), lambda i: (i, 0)),
            pl.BlockSpec((d, P_HEADS * P_KEY_DIM), lambda i: (0, 0)),
            pl.BlockSpec((P_HEADS, 2, P_N_KEYS, P_HALF), lambda i: (0, 0, 0, 0)),
        ],
        out_specs=[hspec, hspec, ospec, ospec],
        out_shape=[jax.ShapeDtypeStruct(hshape, jnp.int32), jax.ShapeDtypeStruct(hshape, jnp.int32),
                   jax.ShapeDtypeStruct(oshape, F32), jax.ShapeDtypeStruct(oshape, F32)],
        scratch_shapes=[
            pltpu.VMEM((2 * P_HEADS, tr, P_HALF), BF16),
            pltpu.VMEM((2 * ROUTE_HPT, P_N_KEYS, nl), F32),
            pltpu.VMEM((2, P_HEADS, P_TOPK, nl), F32),
            pltpu.VMEM((P_HEADS, P_N_KEYS, nl), jnp.int32),
            pltpu.VMEM((P_HEADS, P_N_KEYS, nl), F32),
        ],
        compiler_params=_cparams(("parallel",)),
        name="peer_route",
    )(h2, wq, sk)


def _peer_kernel(h_ref, u_ref, v_ref, r2_ref, e2_ref, n1_ref, e1_ref, x_ref, g5_ref, o_ref,
                 acc_ref, w_ref, a_scr, *, blocks_per_batch, ctx_blocks):
    i = pl.program_id(0)
    k = pl.program_id(1)

    @pl.when(k == 0)
    def _():
        acc_ref[...] = jnp.zeros_like(acc_ref)

    nl = P_N_KEYS
    zero = jnp.zeros((P_N_KEYS, nl), BF16)
    ltc = PEER_TC // nl
    njg = PEER_I1 // PEER_JG

    def gate_block(lt, j0):
        g = [None] * PEER_JG
        for h in range(P_HEADS):
            r2 = pltpu.bitcast(r2_ref[lt, h], BF16)
            e2 = pltpu.bitcast(e2_ref[lt, h], BF16)
            for jj in range(PEER_JG):
                n1row = _row_plane(n1_ref[lt, h, pl.ds(j0 + jj, 1), :])
                e1row = _row_plane(e1_ref[lt, h, pl.ds(j0 + jj, 1), :])
                term = jnp.where(r2 < n1row, e2, zero) * e1row
                g[jj] = term if h == 0 else g[jj] + term
        for jj in range(PEER_JG):
            rows = pl.ds(pl.multiple_of((j0 + jj) * P_N_KEYS, P_N_KEYS), P_N_KEYS)
            w_ref[lt, rows, :] = a_scr[lt, rows, :] * g[jj]

    for c in range(PEER_TT // PEER_TC):
        tok = slice(c * PEER_TC, (c + 1) * PEER_TC)
        a_t = _dot_nt(u_ref[0].astype(BF16), h_ref[tok, :])
        half_cdf = (0.5 * lax.erf(a_t * (2.0 ** -0.5))).astype(BF16) + 0.5
        act_t = a_t.astype(BF16) * half_cdf
        for lc in range(ltc):
            a_scr[c * ltc + lc] = act_t[:, lc * nl:(lc + 1) * nl]

        def body(it, carry, c=c):
            gate_block(c * ltc + it // njg, (it % njg) * PEER_JG)
            return carry

        lax.fori_loop(0, ltc * njg, body, 0)
        w_t = jnp.concatenate([w_ref[c * ltc + lc] for lc in range(ltc)], axis=1)
        acc_ref[:, tok] += _dot_tn(v_ref[0].astype(BF16), w_t)

    @pl.when(k == pl.num_programs(1) - 1)
    def _():
        y = acc_ref[...].T
        for u in range(PEER_TT // TB):
            sblk = i * (PEER_TT // TB) + u
            bidx = sblk // blocks_per_batch
            is_lat = (sblk - bidx * blocks_per_batch) >= ctx_blocks
            gate = g5_ref[2 * bidx + is_lat.astype(jnp.int32)]
            rs = slice(u * TB, (u + 1) * TB)
            o_ref[rs, :] = x_ref[rs, :] + gate * y[rs, :]


def _peer(h2, u_all, v_all, layer, r2, e2, n1, e1, x, g5, blocks_per_batch, ctx_blocks):
    t, d = h2.shape
    ne = u_all.shape[1]
    nl = P_N_KEYS
    rspec = pl.BlockSpec((PEER_TT // nl, P_HEADS, P_N_KEYS // 2, nl), lambda i, k: (i, 0, 0, 0))
    nspec = pl.BlockSpec((PEER_TT // nl, P_HEADS, PEER_I1, nl), lambda i, k: (i, 0, k, 0))
    return pl.pallas_call(
        functools.partial(_peer_kernel, blocks_per_batch=blocks_per_batch, ctx_blocks=ctx_blocks),
        grid=(t // PEER_TT, ne // PEER_ET),
        in_specs=[
            pl.BlockSpec((PEER_TT, d), lambda i, k: (i, 0)),
            pl.BlockSpec((1, PEER_ET, d), lambda i, k: (layer, k, 0)),
            pl.BlockSpec((1, PEER_ET, d), lambda i, k: (layer, k, 0)),
            rspec, rspec, nspec, nspec,
            pl.BlockSpec((PEER_TT, d), lambda i, k: (i, 0)),
            pl.BlockSpec(g5.shape, lambda i, k: (0, 0, 0)),
        ],
        out_specs=pl.BlockSpec((PEER_TT, d), lambda i, k: (i, 0)),
        out_shape=jax.ShapeDtypeStruct((t, d), F32),
        scratch_shapes=[pltpu.VMEM((d, PEER_TT), F32),
                        pltpu.VMEM((PEER_TT // nl, PEER_ET, nl), BF16),
                        pltpu.VMEM((PEER_TT // nl, PEER_ET, nl), BF16)],
        compiler_params=_cparams(("parallel", "arbitrary")),
        name="peer_experts",
    )(h2, u_all, v_all, r2, e2, n1, e1, x, g5)


def _rope_tables(n_ctx, n_lat):
    n_freq = B_ROPE // 4
    pos = np.arange(n_lat)
    inv_freq = ROPE_THETA ** (-np.arange(n_freq, dtype=np.float32) / n_freq)
    inv_freq = jnp.asarray(inv_freq, F32)
    rowp = jnp.asarray(pos // GRID_W, F32)
    colp = jnp.asarray(pos % GRID_W, F32)
    ang = jnp.stack([rowp[:, None] * inv_freq, colp[:, None] * inv_freq], axis=1)
    cos, sin = jnp.cos(ang), jnp.sin(ang)
    cos32 = jnp.concatenate([cos, cos], axis=2).reshape(n_lat, B_ROPE)
    sin32 = jnp.concatenate([-sin, sin], axis=2).reshape(n_lat, B_ROPE)
    pad_l = jnp.ones((n_lat, B_NOPE), F32)
    pad_r = jnp.ones((n_lat, B_HP - B_QK), F32)
    cos_l = jnp.concatenate([pad_l, cos32, pad_r], axis=1)
    sin_l = jnp.concatenate([0 * pad_l, sin32, 0 * pad_r], axis=1)
    cos_t = jnp.concatenate([jnp.ones((n_ctx, B_HP), F32), cos_l], axis=0)
    sin_t = jnp.concatenate([jnp.zeros((n_ctx, B_HP), F32), sin_l], axis=0)
    return cos_t, sin_t


_SWAP32 = np.arange(B_ROPE) ^ (B_ROPE // 4)


def _pad_head(nope, rope):
    z = jnp.zeros(nope.shape[:-1] + (B_HP - B_QK,), nope.dtype)
    out = jnp.concatenate([nope, rope, z], axis=-1)
    return out.reshape(out.shape[:-2] + (out.shape[-2] * B_HP,))


def _layer_weights(layer, w_in, w_out, a_norm_w, a_w_s, a_b_s, b_q_norm_w, b_w_uq, b_kv_norm_w, b_w_ukv,
                   b_q_head_norm_w, b_k_head_norm_w, c_out_norm_w, p_w_q, p_sub_keys):
    d = w_in.shape[1]
    wi = w_in[layer]
    offs = np.cumsum([0, A_W, A_W, B_Q_RANK, B_KV_RANK, B_ROPE, C_W, C_W, C_W, C_W, C_W])
    col = lambda i: wi[:, offs[i]:offs[i + 1]]
    w_kr = col(4)
    zl = jnp.zeros((d, B_NOPE), F32)
    zr = jnp.zeros((d, B_HP - B_QK), F32)
    kr_placed = jnp.concatenate([zl, w_kr, zr], axis=1)
    kr_swapped = jnp.concatenate([zl, w_kr[:, _SWAP32], zr], axis=1)
    w_all = jnp.concatenate(
        [col(0), col(1), col(2), col(3), kr_placed, kr_swapped, col(5), col(6), col(7), col(8), col(9)],
        axis=1).astype(BF16)
    widths = (2 * A_W, B_Q_RANK + B_KV_RANK + 2 * B_HP, 4 * C_W, C_W)

    wuq = b_w_uq[layer].reshape(B_Q_RANK, B_HEADS, B_QK)
    wq_p = _pad_head(wuq[..., :B_NOPE], wuq[..., B_NOPE:]).astype(BF16)
    wq_s = _pad_head(0 * wuq[..., :B_NOPE], wuq[..., B_NOPE:][..., _SWAP32]).astype(BF16)
    wukv = b_w_ukv[layer].reshape(B_KV_RANK, B_HEADS, B_NOPE + B_V)
    wk_p = _pad_head(wukv[..., :B_NOPE], jnp.zeros((B_KV_RANK, B_HEADS, B_ROPE), F32)).astype(BF16)
    wv = wukv[..., B_NOPE:].reshape(B_KV_RANK, B_HEADS * B_V).astype(BF16)
    qn, kn = b_q_head_norm_w[layer], b_k_head_norm_w[layer]
    zpad = jnp.zeros((B_HP - B_QK,), F32)
    z64 = jnp.zeros((B_NOPE,), F32)
    hw = jnp.stack([
        jnp.concatenate([qn, zpad]),
        jnp.concatenate([z64, qn[B_NOPE:][_SWAP32], zpad]),
        jnp.concatenate([kn, zpad]),
        jnp.concatenate([z64, kn[B_NOPE:][_SWAP32], zpad]),
    ] + [jnp.zeros((B_HP,), F32)] * 4, axis=0)

    wo = w_out[layer].astype(BF16)
    return dict(
        w_all=w_all, widths=widths,
        a_nw=a_norm_w[layer].reshape(1, A_W),
        a_ws=a_w_s[layer].astype(BF16),
        a_bias=jnp.repeat(a_b_s[layer].T, A_HD, axis=1),
        qnw=b_q_norm_w[layer].reshape(1, B_Q_RANK), kvnw=b_kv_norm_w[layer].reshape(1, B_KV_RANK),
        wq_p=wq_p, wq_s=wq_s, wk_p=wk_p, wv=wv, hw=hw,
        cnw=jnp.tile(c_out_norm_w[layer], C_HEADS).reshape(1, C_W),
        wo_a=wo[:A_W], wo_b=wo[A_W:A_W + B_HEADS * B_V], wo_c=wo[A_W + B_HEADS * B_V:],
        p_wq=p_w_q[layer].astype(BF16), p_sk=p_sub_keys[layer].astype(BF16),
    )


def kernel(x, c, ctx, c_ctx, ln1_w, ln2_w, w_mod, b_mod, w_in, w_out, a_norm_w, a_w_s, a_b_s, b_q_norm_w,
           b_w_uq, b_kv_norm_w, b_w_ukv, b_q_head_norm_w, b_k_head_norm_w, c_lb_logits, c_out_norm_w,
           p_w_q, p_sub_keys, p_u, p_v):
    bsz, n_lat, d = x.shape
    n_ctx = ctx.shape[1]
    depth = w_in.shape[0]
    n = n_ctx + n_lat
    ctx_blocks = n_ctx // TB
    blocks = n // TB

    mrows = -(-(bsz + 1) // 8) * 8
    cvec = jnp.concatenate([c, c_ctx[None, :], jnp.zeros((mrows - bsz - 1, d), F32)], axis=0)
    mod_all = _modulation(cvec, w_mod, b_mod)

    lb = jnp.cumsum(jax.nn.softmax(c_lb_logits.astype(F32), axis=0), axis=0)
    lb = lb - lb[0:1]
    lbc_all = jnp.stack([jnp.log(lb), jnp.log1p(-lb), 1.0 - lb] + [jnp.zeros_like(lb)] * 5, axis=2)

    cos_t, sin_t = _rope_tables(n_ctx, n_lat)
    xc = jnp.concatenate([ctx, x], axis=1)

    for layer in range(depth):
        last = layer == depth - 1
        w = _layer_weights(layer, w_in, w_out, a_norm_w, a_w_s, a_b_s, b_q_norm_w, b_w_uq, b_kv_norm_w,
                           b_w_ukv, b_q_head_norm_w, b_k_head_norm_w, c_out_norm_w, p_w_q, p_sub_keys)
        mod_b = mod_all[layer, :bsz]
        mod_c = jnp.broadcast_to(mod_all[layer, bsz][None, :], mod_b.shape)
        modsel = jnp.stack([mod_c, mod_b], axis=1)[:, :, None, :]
        first_block = ctx_blocks if last else 0

        a_out, q, k, v, oc, og = _front(xc, ln1_w[layer].reshape(1, d), modsel, w, cos_t, sin_t)
        b_lat = _attention(q, k, v, ctx_blocks, blocks - ctx_blocks, n, n_lat, 2)
        if last:
            b_out = b_lat
        else:
            b_ctx = _attention(q, k, v, 0, ctx_blocks, n_ctx, n_ctx, 1)
            b_out = jnp.concatenate([b_ctx, b_lat], axis=1)
        o_f, o_b = _hgrn(oc, lbc_all[layer], n_ctx)
        x_new, h2 = _outproj(xc, a_out, b_out, o_f, o_b, og, w["cnw"], w["wo_a"], w["wo_b"], w["wo_c"],
                             modsel, ln2_w[layer].reshape(1, d), first_block)

        t = x_new.shape[0] * x_new.shape[1]
        h2f = h2.reshape(t, d)
        r2, e2, n1, e1 = _route(h2f, w["p_wq"], w["p_sk"])
        g5 = modsel[:, :, :, 5 * d:6 * d].reshape(2 * bsz, 1, d)
        out = _peer(h2f, p_u, p_v, layer, r2, e2, n1, e1, x_new.reshape(t, d), g5,
                    blocks - first_block, ctx_blocks - first_block)
        xc = out.reshape(bsz, t // bsz, d)
    return xc
```

```python
import functools
import math

import jax
import jax.numpy as jnp
import numpy as np
from jax import lax
from jax.experimental import pallas as pl
from jax.experimental.pallas import tpu as pltpu

F32 = jnp.float32
BF16 = jnp.bfloat16
HIGHEST = lax.Precision.HIGHEST

EPS = 1e-6
GRID_W = 64
ROPE_THETA = 10000.0

A_HEADS, A_HD, A_CHUNK = 4, 64, 128
A_W = A_HEADS * A_HD
B_HEADS, B_NOPE, B_ROPE, B_V = 8, 64, 32, 64
B_QK = B_NOPE + B_ROPE
B_HP = 128
B_Q_RANK, B_KV_RANK = 256, 128
C_HEADS, C_DK, C_DV, C_CHUNK = 4, 64, 64, 64
C_W = C_HEADS * C_DK
C_SUB = 16
HGRN_BPS = 4
P_HEADS, P_KEY_DIM, P_N_KEYS, P_TOPK = 8, 256, 128, 16
P_HALF = P_KEY_DIM // 2

TB = 256
PEER_TT = 1024
PEER_TC = 1024
PEER_ET = 1024
PEER_I1 = PEER_ET // P_N_KEYS
PEER_JG = 4
ROUTE_HPT = 8
VMEM_LIMIT = 56 * 1024 * 1024


def _cparams(sem, flags=None):
    return pltpu.CompilerParams(dimension_semantics=sem, vmem_limit_bytes=VMEM_LIMIT, flags=flags)


def _dot_nt(a, b):
    return lax.dot_general(a, b, (((1,), (1,)), ((), ())), preferred_element_type=F32)


def _dot_tn(a, b):
    return lax.dot_general(a, b, (((0,), (0,)), ((), ())), preferred_element_type=F32)


def _dot(a, b):
    return jnp.dot(a, b, preferred_element_type=F32)


def _sigmoid(x):
    return 1.0 / (1.0 + jnp.exp(-x))


def _block_ones(n, blk, dtype):
    r = lax.broadcasted_iota(jnp.int32, (n, n), 0) // blk
    c = lax.broadcasted_iota(jnp.int32, (n, n), 1) // blk
    return (r == c).astype(dtype)


def _mod_kernel(c_ref, w_ref, b_ref, o_ref):
    c = c_ref[...]
    sc = c * _sigmoid(c)
    o_ref[0] = _dot(sc.astype(BF16), w_ref[0].astype(BF16)) + b_ref[0]


def _modulation(cvec, w_mod, b_mod):
    depth, d, n6 = w_mod.shape
    rows = cvec.shape[0]
    tn = 1024
    return pl.pallas_call(
        _mod_kernel,
        grid=(depth, n6 // tn),
        in_specs=[
            pl.BlockSpec((rows, d), lambda l, n: (0, 0)),
            pl.BlockSpec((1, d, tn), lambda l, n: (l, 0, n)),
            pl.BlockSpec((1, 1, tn), lambda l, n: (l, 0, n)),
        ],
        out_specs=pl.BlockSpec((1, rows, tn), lambda l, n: (l, 0, n)),
        out_shape=jax.ShapeDtypeStruct((depth, rows, n6), F32),
        compiler_params=_cparams(("parallel", "parallel")),
        name="adaln_mod",
    )(cvec, w_mod, b_mod.reshape(depth, 1, n6))


def _amix_body(ua, nw_ref, ws_ref, bias_ref):
    u = ua[:, 0:A_W]
    v = ua[:, A_W:2 * A_W]
    ssq = jnp.dot(v * v, _block_ones(A_W, A_HD, F32), precision=HIGHEST, preferred_element_type=F32)
    vn = v * lax.rsqrt(ssq * (1.0 / A_HD) + EPS) * nw_ref[...]
    lane_head = lax.broadcasted_iota(jnp.int32, (A_CHUNK, A_W), 1) // A_HD
    outs = []
    for c in range(TB // A_CHUNK):
        rows = slice(c * A_CHUNK, (c + 1) * A_CHUNK)
        acc = bias_ref[...]
        for h in range(A_HEADS):
            vm = jnp.where(lane_head == h, vn[rows], 0.0).astype(BF16)
            acc = acc + _dot(ws_ref[h], vm)
        outs.append((u[rows] * acc).astype(BF16))
    return jnp.concatenate(outs, axis=0)


def _mla_body(ob, cos, sin, qnw_ref, kvnw_ref, wq_ref, wqs_ref, wk_ref, wv_ref, hw_ref, q_ref, k_ref, v_ref):
    cq = ob[:, 0:B_Q_RANK]
    ckv = ob[:, B_Q_RANK:B_Q_RANK + B_KV_RANK]
    krp = ob[:, B_Q_RANK + B_KV_RANK:B_Q_RANK + B_KV_RANK + B_HP]
    krs = ob[:, B_Q_RANK + B_KV_RANK + B_HP:B_Q_RANK + B_KV_RANK + 2 * B_HP]
    cqn = (cq * lax.rsqrt(jnp.mean(cq * cq, axis=-1, keepdims=True) + EPS) * qnw_ref[...]).astype(BF16)
    ckn = (ckv * lax.rsqrt(jnp.mean(ckv * ckv, axis=-1, keepdims=True) + EPS) * kvnw_ref[...]).astype(BF16)
    q_raw = _dot(cqn, wq_ref[...])
    q_swp = _dot(cqn, wqs_ref[...])
    k_raw = _dot(ckn, wk_ref[...])
    v_all = _dot(ckn, wv_ref[...])
    qw, qws, kw, kws = hw_ref[0:1, :], hw_ref[1:2, :], hw_ref[2:3, :], hw_ref[3:4, :]
    k_rot_sw = krs * kws * sin
    for h in range(B_HEADS):
        sl = slice(h * B_HP, (h + 1) * B_HP)
        qh = q_raw[:, sl]
        rq = lax.rsqrt(jnp.sum(qh * qh, axis=-1, keepdims=True) * (1.0 / B_QK) + EPS)
        q_ref[0, h] = (rq * (qh * qw * cos + q_swp[:, sl] * qws * sin)).astype(BF16)
        kh = k_raw[:, sl] + krp
        rk = lax.rsqrt(jnp.sum(kh * kh, axis=-1, keepdims=True) * (1.0 / B_QK) + EPS)
        k_ref[0, h] = (rk * (kh * kw * cos + k_rot_sw)).astype(BF16)
        v_ref[0, h] = v_all[:, h * B_V:(h + 1) * B_V].astype(BF16)


def _front_kernel(x_ref, lnw_ref, mod_ref, w_ref, anw_ref, aws_ref, abias_ref, cos_ref, sin_ref,
                  qnw_ref, kvnw_ref, wq_ref, wqs_ref, wk_ref, wv_ref, hw_ref,
                  a_ref, q_ref, k_ref, v_ref, oc_ref, og_ref, *, d, widths):
    x = x_ref[0]
    ms = jnp.mean(x * x, axis=-1, keepdims=True)
    y = x * lax.rsqrt(ms + EPS) * lnw_ref[...]
    shift = mod_ref[0, 0, :, 0:d]
    scale = mod_ref[0, 0, :, d:2 * d]
    h = (y * (1.0 + scale) + shift).astype(BF16)
    p = _dot(h, w_ref[...])
    na, nb, nc, _ = widths
    oc_ref[0] = p[:, na + nb:na + nb + nc]
    og_ref[0] = p[:, na + nb + nc:]
    a_ref[0] = _amix_body(p[:, 0:na], anw_ref, aws_ref, abias_ref)
    _mla_body(p[:, na:na + nb], cos_ref[...], sin_ref[...], qnw_ref, kvnw_ref, wq_ref, wqs_ref, wk_ref,
              wv_ref, hw_ref, q_ref, k_ref, v_ref)


def _front(x, lnw, modsel, w, cos_t, sin_t):
    bsz, n, d = x.shape
    na, nb, nc, ng = w["widths"]
    full = lambda *s: pl.BlockSpec(s, lambda b, j: (0,) * len(s))
    tok = lambda width: pl.BlockSpec((1, TB, width), lambda b, j: (b, j, 0))
    head = lambda width: pl.BlockSpec((1, B_HEADS, TB, width), lambda b, j: (b, 0, j, 0))
    return pl.pallas_call(
        functools.partial(_front_kernel, d=d, widths=w["widths"]),
        grid=(bsz, n // TB),
        in_specs=[
            tok(d), full(1, d),
            pl.BlockSpec((1, 1, 1, modsel.shape[-1]), lambda b, j: (b, jnp.minimum(j, 1), 0, 0)),
            full(d, w["w_all"].shape[1]),
            full(1, A_W), full(A_HEADS, A_CHUNK, A_CHUNK), full(A_CHUNK, A_W),
            pl.BlockSpec((TB, B_HP), lambda b, j: (j, 0)),
            pl.BlockSpec((TB, B_HP), lambda b, j: (j, 0)),
            full(1, B_Q_RANK), full(1, B_KV_RANK),
            full(B_Q_RANK, B_HEADS * B_HP), full(B_Q_RANK, B_HEADS * B_HP),
            full(B_KV_RANK, B_HEADS * B_HP), full(B_KV_RANK, B_HEADS * B_V),
            full(8, B_HP),
        ],
        out_specs=[tok(A_W), head(B_HP), head(B_HP), head(B_V), tok(nc), tok(ng)],
        out_shape=[
            jax.ShapeDtypeStruct((bsz, n, A_W), BF16),
            jax.ShapeDtypeStruct((bsz, B_HEADS, n, B_HP), BF16),
            jax.ShapeDtypeStruct((bsz, B_HEADS, n, B_HP), BF16),
            jax.ShapeDtypeStruct((bsz, B_HEADS, n, B_V), BF16),
            jax.ShapeDtypeStruct((bsz, n, nc), F32),
            jax.ShapeDtypeStruct((bsz, n, ng), F32),
        ],
        compiler_params=_cparams(("parallel", "parallel")),
        name="in_proj_mix",
    )(x, lnw, modsel, w["w_all"], w["a_nw"], w["a_ws"], w["a_bias"], cos_t, sin_t, w["qnw"], w["kvnw"],
      w["wq_p"], w["wq_s"], w["wk_p"], w["wv"], w["hw"])


def _attn_kernel(*refs, nqb):
    q_refs, (k_ref, v_ref, o_ref, o_scr) = refs[:nqb], refs[nqb:]
    scale = (B_QK ** -0.5) * math.log2(math.e)
    for h in range(B_HEADS):
        q = q_refs[0][0, h] if nqb == 1 else jnp.concatenate([r[0, h] for r in q_refs], axis=0)
        s = _dot_nt(q, k_ref[0, h])
        m = jnp.max(s, axis=-1, keepdims=True)
        p = jnp.exp2((s - m) * scale)
        l = jnp.sum(p, axis=-1, keepdims=True)
        o = _dot(p.astype(BF16), v_ref[0, h])
        o_scr[:, h * B_V:(h + 1) * B_V] = o / l
    o_ref[0] = o_scr[...].astype(BF16)


def _attention(q, k, v, first_qblock, n_qblocks, n_keys, out_rows, nqb):
    bsz = q.shape[0]
    qspec = lambda u: pl.BlockSpec((1, B_HEADS, TB, B_HP), lambda b, j: (b, 0, j * nqb + u + first_qblock, 0))
    return pl.pallas_call(
        functools.partial(_attn_kernel, nqb=nqb),
        grid=(bsz, n_qblocks // nqb),
        in_specs=[qspec(u) for u in range(nqb)] + [
            pl.BlockSpec((1, B_HEADS, n_keys, B_HP), lambda b, j: (b, 0, 0, 0)),
            pl.BlockSpec((1, B_HEADS, n_keys, B_V), lambda b, j: (b, 0, 0, 0)),
        ],
        out_specs=pl.BlockSpec((1, nqb * TB, B_HEADS * B_V), lambda b, j: (b, j, 0)),
        out_shape=jax.ShapeDtypeStruct((bsz, out_rows, B_HEADS * B_V), BF16),
        scratch_shapes=[pltpu.VMEM((nqb * TB, B_HEADS * B_V), F32)],
        compiler_params=_cparams(("parallel", "arbitrary")),
        name="mla_attention",
    )(*([q] * nqb), k, v)


def _hgrn_chunk(blk, zcol, lbc, st_ref, rev):
    cc, w = C_CHUNK, C_W
    q = blk[:, 0:w] * (C_DK ** -0.5)
    z = blk[:, zcol * w:(zcol + 1) * w]
    v = blk[:, 3 * w:4 * w]
    log_lb, log1m_lb, one_m_lb = lbc[0:1, :], lbc[1:2, :], lbc[2:3, :]
    az = jnp.abs(z)
    sp = jnp.log1p(jnp.exp(-az))
    lsig = jnp.minimum(z, 0.0) - sp
    t2 = log1m_lb + lsig
    mx = jnp.maximum(log_lb, t2)
    mn = jnp.minimum(log_lb, t2)
    logf = mx + jnp.log1p(jnp.exp(mn - mx))
    kk = one_m_lb * _sigmoid(-z)

    ti = lax.broadcasted_iota(jnp.int32, (cc, cc), 0)
    ui = lax.broadcasted_iota(jnp.int32, (cc, cc), 1)
    tri = ((ui >= ti) if rev else (ui <= ti)).astype(F32)
    b = jnp.dot(tri, logf, precision=HIGHEST, preferred_element_type=F32)
    b_tot = b[0:1, :] if rev else b[cc - 1:cc, :]

    row = lax.broadcasted_iota(jnp.int32, (cc, w), 0)
    lane_head = lax.broadcasted_iota(jnp.int32, (cc, w), 1) // C_DK
    nsub = cc // C_SUB
    row_blk = row // C_SUB

    beta_rows = []
    for i in range(nsub):
        if rev:
            src = None if i == nsub - 1 else b[(i + 1) * C_SUB:(i + 1) * C_SUB + 1, :]
        else:
            src = None if i == 0 else b[i * C_SUB - 1:i * C_SUB, :]
        beta_rows.append(src)
    beta_full = jnp.concatenate(
        [jnp.broadcast_to(b[i * C_SUB:i * C_SUB + 1, :] if r is None else r, (C_SUB, w))
         for i, r in enumerate(beta_rows)], axis=0)
    has_prev = (row_blk < nsub - 1) if rev else (row_blk > 0)
    qs = jnp.where(has_prev, q * jnp.exp(b - beta_full), 0.0)

    q_stack = jnp.concatenate([jnp.where(lane_head == h, qs, 0.0) for h in range(C_HEADS)], axis=0).astype(BF16)
    qblocks = [i for i in range(nsub) if beta_rows[i] is not None]
    ks_parts = []
    for i in qblocks:
        prev = (row_blk > i) if rev else (row_blk < i)
        ks_parts.append(jnp.where(prev, kk * jnp.exp(beta_rows[i] - b), 0.0))
    ks_all = jnp.concatenate(ks_parts, axis=0).astype(BF16)
    a_all = _dot_nt(q_stack, ks_all)
    ar = lax.broadcasted_iota(jnp.int32, a_all.shape, 0)
    ac = lax.broadcasted_iota(jnp.int32, a_all.shape, 1)
    r_blk = (ar % cc) // C_SUB
    c_blk = ac // cc + (0 if rev else 1)
    a_all = jnp.where(r_blk == c_blk, a_all, 0.0).astype(BF16)
    v_bf = v.astype(BF16)
    r_all = _dot(a_all, jnp.concatenate([v_bf] * len(qblocks), axis=0))
    o = jnp.zeros((cc, w), F32)
    for h in range(C_HEADS):
        o = o + jnp.where(lane_head == h, r_all[h * cc:(h + 1) * cc, :], 0.0)

    ones_bd = _block_ones(w, C_DK, BF16)
    tsub = lax.broadcasted_iota(jnp.int32, (C_SUB, w), 0)
    diag_parts = []
    for i in range(nsub):
        r0 = i * C_SUB
        bb = b[r0:r0 + C_SUB, :]
        qq = q[r0:r0 + C_SUB, :]
        ps = []
        for s in range(C_SUB):
            keep = (tsub <= s) if rev else (tsub >= s)
            e = jnp.where(keep, jnp.exp(bb - b[r0 + s:r0 + s + 1, :]), 0.0)
            ps.append(qq * e * kk[r0 + s:r0 + s + 1, :])
        red = _dot(jnp.concatenate(ps, axis=0).astype(BF16), ones_bd)
        od = jnp.zeros((C_SUB, w), F32)
        for s in range(C_SUB):
            od = od + red[s * C_SUB:(s + 1) * C_SUB, :] * v[r0 + s:r0 + s + 1, :]
        diag_parts.append(od)
    o = o + jnp.concatenate(diag_parts, axis=0)

    st = st_ref[...]
    o = o + _dot_nt((q * jnp.exp(b)).astype(BF16), st.astype(BF16))
    kd = (kk * jnp.exp(b_tot - b)).astype(BF16)
    upd = _dot_tn(v_bf, kd)
    st_ref[...] = st * jnp.exp(b_tot) + upd * _block_ones(w, C_DK, F32)
    return o


def _hgrn_kernel(cf_ref, cb_ref, lbc_ref, of_ref, ob_ref, sf_ref, sb_ref):
    @pl.when(pl.program_id(1) == 0)
    def _():
        sf_ref[...] = jnp.zeros_like(sf_ref)
        sb_ref[...] = jnp.zeros_like(sb_ref)

    for bb in range(HGRN_BPS):
        of_ref[bb] = _hgrn_chunk(cf_ref[bb], 1, lbc_ref[0], sf_ref.at[bb], rev=False)
        ob_ref[bb] = _hgrn_chunk(cb_ref[bb], 2, lbc_ref[1], sb_ref.at[bb], rev=True)


def _hgrn(oc, lbc, n_ctx):
    bsz, n, wc = oc.shape
    nch = n // C_CHUNK
    nctx = n_ctx // C_CHUNK
    nb = HGRN_BPS

    def bwd_idx(c):
        return jnp.where(c < nctx, nctx - 1 - c, nch + nctx - 1 - c)

    return pl.pallas_call(
        _hgrn_kernel,
        grid=(bsz // nb, nch),
        in_specs=[
            pl.BlockSpec((nb, C_CHUNK, wc), lambda b, c: (b, c, 0)),
            pl.BlockSpec((nb, C_CHUNK, wc), lambda b, c: (b, bwd_idx(c), 0)),
            pl.BlockSpec((2, 8, C_W), lambda b, c: (0, 0, 0)),
        ],
        out_specs=[
            pl.BlockSpec((nb, C_CHUNK, C_W), lambda b, c: (b, c, 0)),
            pl.BlockSpec((nb, C_CHUNK, C_W), lambda b, c: (b, bwd_idx(c), 0)),
        ],
        out_shape=[jax.ShapeDtypeStruct((bsz, n, C_W), F32)] * 2,
        scratch_shapes=[pltpu.VMEM((nb, C_W, C_W), F32)] * 2,
        compiler_params=_cparams(("parallel", "arbitrary")),
        name="hgrn2_scan",
    )(oc, oc, lbc)


def _outproj_kernel(x_ref, a_ref, b_ref, of_ref, ob_ref, g_ref, cnw_ref, wa_ref, wb_ref, wc_ref,
                    mod_ref, ln2_ref, xo_ref, h2_ref, *, d):
    o = of_ref[0] + ob_ref[0]
    ssq = jnp.dot(o * o, _block_ones(C_W, C_DV, F32), precision=HIGHEST, preferred_element_type=F32)
    g = g_ref[0]
    c_out = o * lax.rsqrt(ssq * (1.0 / C_DV) + EPS) * cnw_ref[...] * (g * _sigmoid(g))
    mix = _dot(a_ref[0], wa_ref[...]) + _dot(b_ref[0], wb_ref[...]) + _dot(c_out.astype(BF16), wc_ref[...])
    gate1 = mod_ref[0, 0, :, 2 * d:3 * d]
    shift2 = mod_ref[0, 0, :, 3 * d:4 * d]
    scale2 = mod_ref[0, 0, :, 4 * d:5 * d]
    x = x_ref[0] + gate1 * mix
    xo_ref[0] = x
    y = x * lax.rsqrt(jnp.mean(x * x, axis=-1, keepdims=True) + EPS) * ln2_ref[...]
    h2_ref[0] = (y * (1.0 + scale2) + shift2).astype(BF16)


def _outproj(x, a_out, b_out, o_f, o_b, og, cnw, wa, wb, wc, modsel, ln2, first_block):
    bsz, n, d = x.shape
    nblk = n // TB - first_block
    full = lambda *s: pl.BlockSpec(s, lambda b, j: (0,) * len(s))
    tok = lambda w: pl.BlockSpec((1, TB, w), lambda b, j: (b, j + first_block, 0))
    return pl.pallas_call(
        functools.partial(_outproj_kernel, d=d),
        grid=(bsz, nblk),
        in_specs=[
            tok(d), tok(A_W),
            pl.BlockSpec((1, TB, B_HEADS * B_V), lambda b, j: (b, j, 0)) if first_block else tok(B_HEADS * B_V),
            tok(C_W), tok(C_W), tok(C_W),
            full(1, C_W), full(A_W, d), full(B_HEADS * B_V, d), full(C_W, d),
            pl.BlockSpec((1, 1, 1, modsel.shape[-1]), lambda b, j: (b, jnp.minimum(j + first_block, 1), 0, 0)),
            full(1, d),
        ],
        out_specs=[
            pl.BlockSpec((1, TB, d), lambda b, j: (b, j, 0)),
            pl.BlockSpec((1, TB, d), lambda b, j: (b, j, 0)),
        ],
        out_shape=[
            jax.ShapeDtypeStruct((bsz, nblk * TB, d), F32),
            jax.ShapeDtypeStruct((bsz, nblk * TB, d), BF16),
        ],
        compiler_params=_cparams(("parallel", "parallel")),
        name="out_proj",
    )(x, a_out, b_out, o_f, o_b, og, cnw, wa, wb, wc, modsel, ln2)


def _top16_exact(s):
    nrows = s.shape[0]
    iota = lax.broadcasted_iota(jnp.int32, s.shape, 0).astype(F32)
    rank = jnp.full(s.shape, P_TOPK, jnp.int32)
    vals = []
    for r in range(P_TOPK):
        m = jnp.max(s, axis=0, keepdims=True)
        idx = jnp.min(jnp.where(s == m, iota, float(nrows)), axis=0, keepdims=True)
        hit = iota == idx
        rank = jnp.where(hit, r, rank)
        s = jnp.where(hit, -jnp.inf, s)
        vals.append(m)
    return jnp.concatenate(vals, axis=0), rank


_MARK0 = int(np.array(0xFF7FFFFF, np.uint32).view(np.int32))


def _top16_marked(s):
    vals = []
    for r in range(P_TOPK):
        m = jnp.max(s, axis=0, keepdims=True)
        mark = float(np.array(_MARK0 - r, np.int32).view(np.float32))
        s = jnp.where(s == m, mark, s)
        vals.append(m)
    rr = _MARK0 - pltpu.bitcast(s, jnp.int32)
    rank = jnp.where(rr < 0, P_TOPK, jnp.where(rr > P_TOPK - 1, P_TOPK, rr))
    slack = jnp.sum(P_TOPK - rank, axis=0, keepdims=True) - (P_TOPK * (P_TOPK + 1)) // 2
    return jnp.concatenate(vals, axis=0), rank, slack


def _as_words(x):
    return pltpu.bitcast(x.astype(BF16), jnp.int32)


def _row_plane(row):
    n = row.shape[-1]
    tile = jnp.broadcast_to(row, (16, n)).astype(BF16)
    return jnp.concatenate([tile] * (P_N_KEYS // 16), axis=0)


def _route_kernel(h_ref, wq_ref, sk_ref, r2_ref, e2_ref, n1_ref, e1_ref,
                  q_scr, s_scr, v_scr, r1_scr, e1_scr):
    nl = P_N_KEYS
    q = _dot(h_ref[...], wq_ref[...]).astype(BF16)
    for l in range(2 * P_HEADS):
        q_scr[l] = q[:, l * P_HALF:(l + 1) * P_HALF]

    def lane_tile(lt, carry):
        row0 = pl.multiple_of(lt * nl, nl)

        def put_rank(h, p, rank):
            if p == 0:
                r1_scr[h] = rank
            else:
                r2_ref[lt, h] = _as_words(rank.astype(F32))

        def heads(hp, c):
            bad = jnp.zeros((1, nl), jnp.int32)
            for u in range(2 * ROUTE_HPT):
                h, p = hp * ROUTE_HPT + u // 2, u % 2
                s = _dot_nt(sk_ref[h, p], q_scr[2 * h + p, pl.ds(row0, nl), :])
                s_scr[u] = s
                v, rank, slack = _top16_marked(s)
                v_scr[p, h] = v
                put_rank(h, p, rank)
                e = jnp.exp(s - v[0:1, :])
                if p == 0:
                    e1_scr[h] = e
                else:
                    e2_ref[lt, h] = _as_words(e)
                bad = bad + slack

            @pl.when(jnp.max(bad) > 0)
            def _():
                for u in range(2 * ROUTE_HPT):
                    h, p = hp * ROUTE_HPT + u // 2, u % 2
                    ve, re = _top16_exact(s_scr[u])
                    v_scr[p, h] = ve
                    put_rank(h, p, re)

            return c

        lax.fori_loop(0, P_HEADS // ROUTE_HPT, heads, 0)

        v1 = v_scr[0]
        v2 = v_scr[1]
        ia = lax.broadcasted_iota(jnp.int32, v1.shape, 1).astype(F32)
        n = jnp.zeros(v1.shape, F32)
        g = jnp.broadcast_to(v2[:, 0:1, :], v1.shape)
        cmax = v1[:, 0:1, :] + v2[:, 0:1, :]
        z = jnp.zeros(cmax.shape, F32)
        for _ in range(P_TOPK):
            f = v1 + g
            m = jnp.max(f, axis=1, keepdims=True)
            a_star = jnp.min(jnp.where(f == m, ia, float(P_TOPK)), axis=1, keepdims=True)
            hit = ia == a_star
            n = n + jnp.where(hit, 1.0, 0.0)
            nsel = jnp.sum(jnp.where(hit, n, 0.0), axis=1, keepdims=True)
            nxt = jnp.sum(jnp.where(ia == nsel, v2, 0.0), axis=1, keepdims=True)
            nxt = jnp.where(nsel > P_TOPK - 0.5, -jnp.inf, nxt)
            g = jnp.where(hit, nxt, g)
            z = z + jnp.exp(m - cmax)
        zinv = 1.0 / z
        for h in range(P_HEADS):
            rank1 = r1_scr[h]
            n1 = jnp.zeros(rank1.shape, F32)
            for a in range(P_TOPK):
                n1 = jnp.where(rank1 == a, n[h, a:a + 1, :], n1)
            n1_ref[lt, h] = n1
            e1_ref[lt, h] = e1_scr[h] * zinv[h]
        return carry

    lax.fori_loop(0, h_ref.shape[0] // nl, lane_tile, 0)


def _route(h2, wq, sk):
    t, d = h2.shape
    tr = PEER_TT
    nl = P_N_KEYS
    oshape = (t // nl, P_HEADS, P_N_KEYS, nl)
    hshape = (t // nl, P_HEADS, P_N_KEYS // 2, nl)
    ospec = pl.BlockSpec((tr // nl, P_HEADS, P_N_KEYS, nl), lambda i: (i, 0, 0, 0))
    hspec = pl.BlockSpec((tr // nl, P_HEADS, P_N_KEYS // 2, nl), lambda i: (i, 0, 0, 0))
    return pl.pallas_call(
        _route_kernel,
        grid=(t // tr,),
        in_specs=[
            pl.BlockSpec((tr, d), lambda i: (i, 0)),
            pl.BlockSpec((d, P_HEADS * P_KEY_DIM), lambda i: (0, 0)),
            pl.BlockSpec((P_HEADS, 2, P_N_KEYS, P_HALF), lambda i: (0, 0, 0, 0)),
        ],
        out_specs=[hspec, hspec, ospec, ospec],
        out_shape=[jax.ShapeDtypeStruct(hshape, jnp.int32), jax.ShapeDtypeStruct(hshape, jnp.int32),
                   jax.ShapeDtypeStruct(oshape, F32), jax.ShapeDtypeStruct(oshape, F32)],
        scratch_shapes=[
            pltpu.VMEM((2 * P_HEADS, tr, P_HALF), BF16),
            pltpu.VMEM((2 * ROUTE_HPT, P_N_KEYS, nl), F32),
            pltpu.VMEM((2, P_HEADS, P_TOPK, nl), F32),
            pltpu.VMEM((P_HEADS, P_N_KEYS, nl), jnp.int32),
            pltpu.VMEM((P_HEADS, P_N_KEYS, nl), F32),
        ],
        compiler_params=_cparams(("parallel",)),
        name="peer_route",
    )(h2, wq, sk)


def _peer_kernel(h_ref, u_ref, v_ref, r2_ref, e2_ref, n1_ref, e1_ref, x_ref, g5_ref, o_ref,
                 acc_ref, w_ref, a_scr, *, blocks_per_batch, ctx_blocks):
    i = pl.program_id(0)
    k = pl.program_id(1)

    @pl.when(k == 0)
    def _():
        acc_ref[...] = jnp.zeros_like(acc_ref)

    nl = P_N_KEYS
    zero = jnp.zeros((P_N_KEYS, nl), BF16)
    ltc = PEER_TC // nl
    njg = PEER_I1 // PEER_JG

    def gate_block(lt, j0):
        g = [None] * PEER_JG
        for h in range(P_HEADS):
            r2 = pltpu.bitcast(r2_ref[lt, h], BF16)
            e2 = pltpu.bitcast(e2_ref[lt, h], BF16)
            for jj in range(PEER_JG):
                n1row = _row_plane(n1_ref[lt, h, pl.ds(j0 + jj, 1), :])
                e1row = _row_plane(e1_ref[lt, h, pl.ds(j0 + jj, 1), :])
                term = jnp.where(r2 < n1row, e2, zero) * e1row
                g[jj] = term if h == 0 else g[jj] + term
        for jj in range(PEER_JG):
            rows = pl.ds(pl.multiple_of((j0 + jj) * P_N_KEYS, P_N_KEYS), P_N_KEYS)
            w_ref[lt, rows, :] = a_scr[lt, rows, :] * g[jj]

    for c in range(PEER_TT // PEER_TC):
        tok = slice(c * PEER_TC, (c + 1) * PEER_TC)
        a_t = _dot_nt(u_ref[0].astype(BF16), h_ref[tok, :])
        half_cdf = (0.5 * lax.erf(a_t * (2.0 ** -0.5))).astype(BF16) + 0.5
        act_t = a_t.astype(BF16) * half_cdf
        for lc in range(ltc):
            a_scr[c * ltc + lc] = act_t[:, lc * nl:(lc + 1) * nl]

        def body(it, carry, c=c):
            gate_block(c * ltc + it // njg, (it % njg) * PEER_JG)
            return carry

        lax.fori_loop(0, ltc * njg, body, 0)
        w_t = jnp.concatenate([w_ref[c * ltc + lc] for lc in range(ltc)], axis=1)
        acc_ref[:, tok] += _dot_tn(v_ref[0].astype(BF16), w_t)

    @pl.when(k == pl.num_programs(1) - 1)
    def _():
        y = acc_ref[...].T
        for u in range(PEER_TT // TB):
            sblk = i * (PEER_TT // TB) + u
            bidx = sblk // blocks_per_batch
            is_lat = (sblk - bidx * blocks_per_batch) >= ctx_blocks
            gate = g5_ref[2 * bidx + is_lat.astype(jnp.int32)]
            rs = slice(u * TB, (u + 1) * TB)
            o_ref[rs, :] = x_ref[rs, :] + gate * y[rs, :]


def _peer(h2, u_all, v_all, layer, r2, e2, n1, e1, x, g5, blocks_per_batch, ctx_blocks):
    t, d = h2.shape
    ne = u_all.shape[1]
    nl = P_N_KEYS
    rspec = pl.BlockSpec((PEER_TT // nl, P_HEADS, P_N_KEYS // 2, nl), lambda i, k: (i, 0, 0, 0))
    nspec = pl.BlockSpec((PEER_TT // nl, P_HEADS, PEER_I1, nl), lambda i, k: (i, 0, k, 0))
    return pl.pallas_call(
        functools.partial(_peer_kernel, blocks_per_batch=blocks_per_batch, ctx_blocks=ctx_blocks),
        grid=(t // PEER_TT, ne // PEER_ET),
        in_specs=[
            pl.BlockSpec((PEER_TT, d), lambda i, k: (i, 0)),
            pl.BlockSpec((1, PEER_ET, d), lambda i, k: (layer, k, 0)),
            pl.BlockSpec((1, PEER_ET, d), lambda i, k: (layer, k, 0)),
            rspec, rspec, nspec, nspec,
            pl.BlockSpec((PEER_TT, d), lambda i, k: (i, 0)),
            pl.BlockSpec(g5.shape, lambda i, k: (0, 0, 0)),
        ],
        out_specs=pl.BlockSpec((PEER_TT, d), lambda i, k: (i, 0)),
        out_shape=jax.ShapeDtypeStruct((t, d), F32),
        scratch_shapes=[pltpu.VMEM((d, PEER_TT), F32),
                        pltpu.VMEM((PEER_TT // nl, PEER_ET, nl), BF16),
                        pltpu.VMEM((PEER_TT // nl, PEER_ET, nl), BF16)],
        compiler_params=_cparams(("parallel", "arbitrary")),
        name="peer_experts",
    )(h2, u_all, v_all, r2, e2, n1, e1, x, g5)


def _rope_tables(n_ctx, n_lat):
    n_freq = B_ROPE // 4
    pos = np.arange(n_lat)
    inv_freq = ROPE_THETA ** (-np.arange(n_freq, dtype=np.float32) / n_freq)
    inv_freq = jnp.asarray(inv_freq, F32)
    rowp = jnp.asarray(pos // GRID_W, F32)
    colp = jnp.asarray(pos % GRID_W, F32)
    ang = jnp.stack([rowp[:, None] * inv_freq, colp[:, None] * inv_freq], axis=1)
    cos, sin = jnp.cos(ang), jnp.sin(ang)
    cos32 = jnp.concatenate([cos, cos], axis=2).reshape(n_lat, B_ROPE)
    sin32 = jnp.concatenate([-sin, sin], axis=2).reshape(n_lat, B_ROPE)
    pad_l = jnp.ones((n_lat, B_NOPE), F32)
    pad_r = jnp.ones((n_lat, B_HP - B_QK), F32)
    cos_l = jnp.concatenate([pad_l, cos32, pad_r], axis=1)
    sin_l = jnp.concatenate([0 * pad_l, sin32, 0 * pad_r], axis=1)
    cos_t = jnp.concatenate([jnp.ones((n_ctx, B_HP), F32), cos_l], axis=0)
    sin_t = jnp.concatenate([jnp.zeros((n_ctx, B_HP), F32), sin_l], axis=0)
    return cos_t, sin_t


_SWAP32 = np.arange(B_ROPE) ^ (B_ROPE // 4)


def _pad_head(nope, rope):
    z = jnp.zeros(nope.shape[:-1] + (B_HP - B_QK,), nope.dtype)
    out = jnp.concatenate([nope, rope, z], axis=-1)
    return out.reshape(out.shape[:-2] + (out.shape[-2] * B_HP,))


def _layer_weights(layer, w_in, w_out, a_norm_w, a_w_s, a_b_s, b_q_norm_w, b_w_uq, b_kv_norm_w, b_w_ukv,
                   b_q_head_norm_w, b_k_head_norm_w, c_out_norm_w, p_w_q, p_sub_keys):
    d = w_in.shape[1]
    wi = w_in[layer]
    offs = np.cumsum([0, A_W, A_W, B_Q_RANK, B_KV_RANK, B_ROPE, C_W, C_W, C_W, C_W, C_W])
    col = lambda i: wi[:, offs[i]:offs[i + 1]]
    w_kr = col(4)
    zl = jnp.zeros((d, B_NOPE), F32)
    zr = jnp.zeros((d, B_HP - B_QK), F32)
    kr_placed = jnp.concatenate([zl, w_kr, zr], axis=1)
    kr_swapped = jnp.concatenate([zl, w_kr[:, _SWAP32], zr], axis=1)
    w_all = jnp.concatenate(
        [col(0), col(1), col(2), col(3), kr_placed, kr_swapped, col(5), col(6), col(7), col(8), col(9)],
        axis=1).astype(BF16)
    widths = (2 * A_W, B_Q_RANK + B_KV_RANK + 2 * B_HP, 4 * C_W, C_W)

    wuq = b_w_uq[layer].reshape(B_Q_RANK, B_HEADS, B_QK)
    wq_p = _pad_head(wuq[..., :B_NOPE], wuq[..., B_NOPE:]).astype(BF16)
    wq_s = _pad_head(0 * wuq[..., :B_NOPE], wuq[..., B_NOPE:][..., _SWAP32]).astype(BF16)
    wukv = b_w_ukv[layer].reshape(B_KV_RANK, B_HEADS, B_NOPE + B_V)
    wk_p = _pad_head(wukv[..., :B_NOPE], jnp.zeros((B_KV_RANK, B_HEADS, B_ROPE), F32)).astype(BF16)
    wv = wukv[..., B_NOPE:].reshape(B_KV_RANK, B_HEADS * B_V).astype(BF16)
    qn, kn = b_q_head_norm_w[layer], b_k_head_norm_w[layer]
    zpad = jnp.zeros((B_HP - B_QK,), F32)
    z64 = jnp.zeros((B_NOPE,), F32)
    hw = jnp.stack([
        jnp.concatenate([qn, zpad]),
        jnp.concatenate([z64, qn[B_NOPE:][_SWAP32], zpad]),
        jnp.concatenate([kn, zpad]),
        jnp.concatenate([z64, kn[B_NOPE:][_SWAP32], zpad]),
    ] + [jnp.zeros((B_HP,), F32)] * 4, axis=0)

    wo = w_out[layer].astype(BF16)
    return dict(
        w_all=w_all, widths=widths,
        a_nw=a_norm_w[layer].reshape(1, A_W),
        a_ws=a_w_s[layer].astype(BF16),
        a_bias=jnp.repeat(a_b_s[layer].T, A_HD, axis=1),
        qnw=b_q_norm_w[layer].reshape(1, B_Q_RANK), kvnw=b_kv_norm_w[layer].reshape(1, B_KV_RANK),
        wq_p=wq_p, wq_s=wq_s, wk_p=wk_p, wv=wv, hw=hw,
        cnw=jnp.tile(c_out_norm_w[layer], C_HEADS).reshape(1, C_W),
        wo_a=wo[:A_W], wo_b=wo[A_W:A_W + B_HEADS * B_V], wo_c=wo[A_W + B_HEADS * B_V:],
        p_wq=p_w_q[layer].astype(BF16), p_sk=p_sub_keys[layer].astype(BF16),
    )


def kernel(x, c, ctx, c_ctx, ln1_w, ln2_w, w_mod, b_mod, w_in, w_out, a_norm_w, a_w_s, a_b_s, b_q_norm_w,
           b_w_uq, b_kv_norm_w, b_w_ukv, b_q_head_norm_w, b_k_head_norm_w, c_lb_logits, c_out_norm_w,
           p_w_q, p_sub_keys, p_u, p_v):
    bsz, n_lat, d = x.shape
    n_ctx = ctx.shape[1]
    depth = w_in.shape[0]
    n = n_ctx + n_lat
    ctx_blocks = n_ctx // TB
    blocks = n // TB

    mrows = -(-(bsz + 1) // 8) * 8
    cvec = jnp.concatenate([c, c_ctx[None, :], jnp.zeros((mrows - bsz - 1, d), F32)], axis=0)
    mod_all = _modulation(cvec, w_mod, b_mod)

    lb = jnp.cumsum(jax.nn.softmax(c_lb_logits.astype(F32), axis=0), axis=0)
    lb = lb - lb[0:1]
    lbc_all = jnp.stack([jnp.log(lb), jnp.log1p(-lb), 1.0 - lb] + [jnp.zeros_like(lb)] * 5, axis=2)

    cos_t, sin_t = _rope_tables(n_ctx, n_lat)
    xc = jnp.concatenate([ctx, x], axis=1)

    for layer in range(depth):
        last = layer == depth - 1
        w = _layer_weights(layer, w_in, w_out, a_norm_w, a_w_s, a_b_s, b_q_norm_w, b_w_uq, b_kv_norm_w,
                           b_w_ukv, b_q_head_norm_w, b_k_head_norm_w, c_out_norm_w, p_w_q, p_sub_keys)
        mod_b = mod_all[layer, :bsz]
        mod_c = jnp.broadcast_to(mod_all[layer, bsz][None, :], mod_b.shape)
        modsel = jnp.stack([mod_c, mod_b], axis=1)[:, :, None, :]
        first_block = ctx_blocks if last else 0

        a_out, q, k, v, oc, og = _front(xc, ln1_w[layer].reshape(1, d), modsel, w, cos_t, sin_t)
        b_lat = _attention(q, k, v, ctx_blocks, blocks - ctx_blocks, n, n_lat, 2)
        if last:
            b_out = b_lat
        else:
            b_ctx = _attention(q, k, v, 0, ctx_blocks, n_ctx, n_ctx, 1)
            b_out = jnp.concatenate([b_ctx, b_lat], axis=1)
        o_f, o_b = _hgrn(oc, lbc_all[layer], n_ctx)
        x_new, h2 = _outproj(xc, a_out, b_out, o_f, o_b, og, w["cnw"], w["wo_a"], w["wo_b"], w["wo_c"],
                             modsel, ln2_w[layer].reshape(1, d), first_block)

        t = x_new.shape[0] * x_new.shape[1]
        h2f = h2.reshape(t, d)
        r2, e2, n1, e1 = _route(h2f, w["p_wq"], w["p_sk"])
        g5 = modsel[:, :, :, 5 * d:6 * d].reshape(2 * bsz, 1, d)
        out = _peer(h2f, p_u, p_v, layer, r2, e2, n1, e1, x_new.reshape(t, d), g5,
                    blocks - first_block, ctx_blocks - first_block)
        xc = out.reshape(bsz, t // bsz, d)
    return xc
```

```python
import functools
import math

import jax
import jax.numpy as jnp
import numpy as np
from jax import lax
from jax.experimental import pallas as pl
from jax.experimental.pallas import tpu as pltpu

F32 = jnp.float32
BF16 = jnp.bfloat16
HIGHEST = lax.Precision.HIGHEST

EPS = 1e-6
GRID_W = 64
ROPE_THETA = 10000.0

A_HEADS, A_HD, A_CHUNK = 4, 64, 128
A_W = A_HEADS * A_HD
B_HEADS, B_NOPE, B_ROPE, B_V = 8, 64, 32, 64
B_QK = B_NOPE + B_ROPE
B_HP = 128
B_Q_RANK, B_KV_RANK = 256, 128
C_HEADS, C_DK, C_DV, C_CHUNK = 4, 64, 64, 64
C_W = C_HEADS * C_DK
C_SUB = 16
HGRN_BPS = 8
P_HEADS, P_KEY_DIM, P_N_KEYS, P_TOPK = 8, 256, 128, 16
P_HALF = P_KEY_DIM // 2

TB = 256
PEER_TT = 1024
PEER_TC = 1024
PEER_ET = 1024
PEER_I1 = PEER_ET // P_N_KEYS
PEER_JG = 4
ROUTE_HPT = 8
VMEM_LIMIT = 56 * 1024 * 1024


def _cparams(sem, flags=None):
    return pltpu.CompilerParams(dimension_semantics=sem, vmem_limit_bytes=VMEM_LIMIT, flags=flags)


def _dot_nt(a, b):
    return lax.dot_general(a, b, (((1,), (1,)), ((), ())), preferred_element_type=F32)


def _dot_tn(a, b):
    return lax.dot_general(a, b, (((0,), (0,)), ((), ())), preferred_element_type=F32)


def _dot(a, b):
    return jnp.dot(a, b, preferred_element_type=F32)


def _sigmoid(x):
    return 1.0 / (1.0 + jnp.exp(-x))


def _block_ones(n, blk, dtype):
    r = lax.broadcasted_iota(jnp.int32, (n, n), 0) // blk
    c = lax.broadcasted_iota(jnp.int32, (n, n), 1) // blk
    return (r == c).astype(dtype)


def _mod_kernel(c_ref, w_ref, b_ref, o_ref):
    c = c_ref[...]
    sc = c * _sigmoid(c)
    o_ref[0] = _dot(sc.astype(BF16), w_ref[0].astype(BF16)) + b_ref[0]


def _modulation(cvec, w_mod, b_mod):
    depth, d, n6 = w_mod.shape
    rows = cvec.shape[0]
    tn = 1024
    return pl.pallas_call(
        _mod_kernel,
        grid=(depth, n6 // tn),
        in_specs=[
            pl.BlockSpec((rows, d), lambda l, n: (0, 0)),
            pl.BlockSpec((1, d, tn), lambda l, n: (l, 0, n)),
            pl.BlockSpec((1, 1, tn), lambda l, n: (l, 0, n)),
        ],
        out_specs=pl.BlockSpec((1, rows, tn), lambda l, n: (l, 0, n)),
        out_shape=jax.ShapeDtypeStruct((depth, rows, n6), F32),
        compiler_params=_cparams(("parallel", "parallel")),
        name="adaln_mod",
    )(cvec, w_mod, b_mod.reshape(depth, 1, n6))


def _amix_body(ua, nw_ref, ws_ref, bias_ref):
    u = ua[:, 0:A_W]
    v = ua[:, A_W:2 * A_W]
    ssq = jnp.dot(v * v, _block_ones(A_W, A_HD, F32), precision=HIGHEST, preferred_element_type=F32)
    vn = v * lax.rsqrt(ssq * (1.0 / A_HD) + EPS) * nw_ref[...]
    lane_head = lax.broadcasted_iota(jnp.int32, (A_CHUNK, A_W), 1) // A_HD
    outs = []
    for c in range(TB // A_CHUNK):
        rows = slice(c * A_CHUNK, (c + 1) * A_CHUNK)
        acc = bias_ref[...]
        for h in range(A_HEADS):
            vm = jnp.where(lane_head == h, vn[rows], 0.0).astype(BF16)
            acc = acc + _dot(ws_ref[h], vm)
        outs.append((u[rows] * acc).astype(BF16))
    return jnp.concatenate(outs, axis=0)


def _mla_body(ob, cos, sin, qnw_ref, kvnw_ref, wq_ref, wqs_ref, wk_ref, wv_ref, hw_ref, q_ref, k_ref, v_ref):
    cq = ob[:, 0:B_Q_RANK]
    ckv = ob[:, B_Q_RANK:B_Q_RANK + B_KV_RANK]
    krp = ob[:, B_Q_RANK + B_KV_RANK:B_Q_RANK + B_KV_RANK + B_HP]
    krs = ob[:, B_Q_RANK + B_KV_RANK + B_HP:B_Q_RANK + B_KV_RANK + 2 * B_HP]
    cqn = (cq * lax.rsqrt(jnp.mean(cq * cq, axis=-1, keepdims=True) + EPS) * qnw_ref[...]).astype(BF16)
    ckn = (ckv * lax.rsqrt(jnp.mean(ckv * ckv, axis=-1, keepdims=True) + EPS) * kvnw_ref[...]).astype(BF16)
    q_raw = _dot(cqn, wq_ref[...])
    q_swp = _dot(cqn, wqs_ref[...])
    k_raw = _dot(ckn, wk_ref[...])
    v_all = _dot(ckn, wv_ref[...])
    qw, qws, kw, kws = hw_ref[0:1, :], hw_ref[1:2, :], hw_ref[2:3, :], hw_ref[3:4, :]
    k_rot_sw = krs * kws * sin
    for h in range(B_HEADS):
        sl = slice(h * B_HP, (h + 1) * B_HP)
        qh = q_raw[:, sl]
        rq = lax.rsqrt(jnp.sum(qh * qh, axis=-1, keepdims=True) * (1.0 / B_QK) + EPS)
        q_ref[0, h] = (rq * (qh * qw * cos + q_swp[:, sl] * qws * sin)).astype(BF16)
        kh = k_raw[:, sl] + krp
        rk = lax.rsqrt(jnp.sum(kh * kh, axis=-1, keepdims=True) * (1.0 / B_QK) + EPS)
        k_ref[0, h] = (rk * (kh * kw * cos + k_rot_sw)).astype(BF16)
        v_ref[0, h] = v_all[:, h * B_V:(h + 1) * B_V].astype(BF16)


def _front_kernel(x_ref, lnw_ref, mod_ref, w_ref, anw_ref, aws_ref, abias_ref, cos_ref, sin_ref,
                  qnw_ref, kvnw_ref, wq_ref, wqs_ref, wk_ref, wv_ref, hw_ref,
                  a_ref, q_ref, k_ref, v_ref, oc_ref, og_ref, *, d, widths):
    x = x_ref[0]
    ms = jnp.mean(x * x, axis=-1, keepdims=True)
    y = x * lax.rsqrt(ms + EPS) * lnw_ref[...]
    shift = mod_ref[0, 0, :, 0:d]
    scale = mod_ref[0, 0, :, d:2 * d]
    h = (y * (1.0 + scale) + shift).astype(BF16)
    p = _dot(h, w_ref[...])
    na, nb, nc, _ = widths
    oc_ref[0] = p[:, na + nb:na + nb + nc]
    og_ref[0] = p[:, na + nb + nc:]
    a_ref[0] = _amix_body(p[:, 0:na], anw_ref, aws_ref, abias_ref)
    _mla_body(p[:, na:na + nb], cos_ref[...], sin_ref[...], qnw_ref, kvnw_ref, wq_ref, wqs_ref, wk_ref,
              wv_ref, hw_ref, q_ref, k_ref, v_ref)


def _front(x, lnw, modsel, w, cos_t, sin_t):
    bsz, n, d = x.shape
    na, nb, nc, ng = w["widths"]
    full = lambda *s: pl.BlockSpec(s, lambda b, j: (0,) * len(s))
    tok = lambda width: pl.BlockSpec((1, TB, width), lambda b, j: (b, j, 0))
    head = lambda width: pl.BlockSpec((1, B_HEADS, TB, width), lambda b, j: (b, 0, j, 0))
    return pl.pallas_call(
        functools.partial(_front_kernel, d=d, widths=w["widths"]),
        grid=(bsz, n // TB),
        in_specs=[
            tok(d), full(1, d),
            pl.BlockSpec((1, 1, 1, modsel.shape[-1]), lambda b, j: (b, jnp.minimum(j, 1), 0, 0)),
            full(d, w["w_all"].shape[1]),
            full(1, A_W), full(A_HEADS, A_CHUNK, A_CHUNK), full(A_CHUNK, A_W),
            pl.BlockSpec((TB, B_HP), lambda b, j: (j, 0)),
            pl.BlockSpec((TB, B_HP), lambda b, j: (j, 0)),
            full(1, B_Q_RANK), full(1, B_KV_RANK),
            full(B_Q_RANK, B_HEADS * B_HP), full(B_Q_RANK, B_HEADS * B_HP),
            full(B_KV_RANK, B_HEADS * B_HP), full(B_KV_RANK, B_HEADS * B_V),
            full(8, B_HP),
        ],
        out_specs=[tok(A_W), head(B_HP), head(B_HP), head(B_V), tok(nc), tok(ng)],
        out_shape=[
            jax.ShapeDtypeStruct((bsz, n, A_W), BF16),
            jax.ShapeDtypeStruct((bsz, B_HEADS, n, B_HP), BF16),
            jax.ShapeDtypeStruct((bsz, B_HEADS, n, B_HP), BF16),
            jax.ShapeDtypeStruct((bsz, B_HEADS, n, B_V), BF16),
            jax.ShapeDtypeStruct((bsz, n, nc), F32),
            jax.ShapeDtypeStruct((bsz, n, ng), F32),
        ],
        compiler_params=_cparams(("parallel", "parallel")),
        name="in_proj_mix",
    )(x, lnw, modsel, w["w_all"], w["a_nw"], w["a_ws"], w["a_bias"], cos_t, sin_t, w["qnw"], w["kvnw"],
      w["wq_p"], w["wq_s"], w["wk_p"], w["wv"], w["hw"])


def _attn_kernel(*refs, nqb):
    q_refs, (k_ref, v_ref, o_ref, o_scr) = refs[:nqb], refs[nqb:]
    scale = (B_QK ** -0.5) * math.log2(math.e)
    for h in range(B_HEADS):
        q = q_refs[0][0, h] if nqb == 1 else jnp.concatenate([r[0, h] for r in q_refs], axis=0)
        s = _dot_nt(q, k_ref[0, h])
        m = jnp.max(s, axis=-1, keepdims=True)
        p = jnp.exp2((s - m) * scale)
        l = jnp.sum(p, axis=-1, keepdims=True)
        o = _dot(p.astype(BF16), v_ref[0, h])
        o_scr[:, h * B_V:(h + 1) * B_V] = o / l
    o_ref[0] = o_scr[...].astype(BF16)


def _attention(q, k, v, first_qblock, n_qblocks, n_keys, out_rows, nqb):
    bsz = q.shape[0]
    qspec = lambda u: pl.BlockSpec((1, B_HEADS, TB, B_HP), lambda b, j: (b, 0, j * nqb + u + first_qblock, 0))
    return pl.pallas_call(
        functools.partial(_attn_kernel, nqb=nqb),
        grid=(bsz, n_qblocks // nqb),
        in_specs=[qspec(u) for u in range(nqb)] + [
            pl.BlockSpec((1, B_HEADS, n_keys, B_HP), lambda b, j: (b, 0, 0, 0)),
            pl.BlockSpec((1, B_HEADS, n_keys, B_V), lambda b, j: (b, 0, 0, 0)),
        ],
        out_specs=pl.BlockSpec((1, nqb * TB, B_HEADS * B_V), lambda b, j: (b, j, 0)),
        out_shape=jax.ShapeDtypeStruct((bsz, out_rows, B_HEADS * B_V), BF16),
        scratch_shapes=[pltpu.VMEM((nqb * TB, B_HEADS * B_V), F32)],
        compiler_params=_cparams(("parallel", "arbitrary")),
        name="mla_attention",
    )(*([q] * nqb), k, v)


def _hgrn_chunk(blk, zcol, lbc, st_ref, rev):
    cc, w = C_CHUNK, C_W
    q = blk[:, 0:w] * (C_DK ** -0.5)
    z = blk[:, zcol * w:(zcol + 1) * w]
    v = blk[:, 3 * w:4 * w]
    log_lb, log1m_lb, one_m_lb = lbc[0:1, :], lbc[1:2, :], lbc[2:3, :]
    az = jnp.abs(z)
    sp = jnp.log1p(jnp.exp(-az))
    lsig = jnp.minimum(z, 0.0) - sp
    t2 = log1m_lb + lsig
    mx = jnp.maximum(log_lb, t2)
    mn = jnp.minimum(log_lb, t2)
    logf = mx + jnp.log1p(jnp.exp(mn - mx))
    kk = one_m_lb * _sigmoid(-z)

    ti = lax.broadcasted_iota(jnp.int32, (cc, cc), 0)
    ui = lax.broadcasted_iota(jnp.int32, (cc, cc), 1)
    tri = ((ui >= ti) if rev else (ui <= ti)).astype(F32)
    b = jnp.dot(tri, logf, precision=HIGHEST, preferred_element_type=F32)
    b_tot = b[0:1, :] if rev else b[cc - 1:cc, :]

    row = lax.broadcasted_iota(jnp.int32, (cc, w), 0)
    lane_head = lax.broadcasted_iota(jnp.int32, (cc, w), 1) // C_DK
    nsub = cc // C_SUB
    row_blk = row // C_SUB

    beta_rows = []
    for i in range(nsub):
        if rev:
            src = None if i == nsub - 1 else b[(i + 1) * C_SUB:(i + 1) * C_SUB + 1, :]
        else:
            src = None if i == 0 else b[i * C_SUB - 1:i * C_SUB, :]
        beta_rows.append(src)
    beta_full = jnp.concatenate(
        [jnp.broadcast_to(b[i * C_SUB:i * C_SUB + 1, :] if r is None else r, (C_SUB, w))
         for i, r in enumerate(beta_rows)], axis=0)
    has_prev = (row_blk < nsub - 1) if rev else (row_blk > 0)
    qs = jnp.where(has_prev, q * jnp.exp(b - beta_full), 0.0)

    q_stack = jnp.concatenate([jnp.where(lane_head == h, qs, 0.0) for h in range(C_HEADS)], axis=0).astype(BF16)
    qblocks = [i for i in range(nsub) if beta_rows[i] is not None]
    ks_parts = []
    for i in qblocks:
        prev = (row_blk > i) if rev else (row_blk < i)
        ks_parts.append(jnp.where(prev, kk * jnp.exp(beta_rows[i] - b), 0.0))
    ks_all = jnp.concatenate(ks_parts, axis=0).astype(BF16)
    a_all = _dot_nt(q_stack, ks_all)
    ar = lax.broadcasted_iota(jnp.int32, a_all.shape, 0)
    ac = lax.broadcasted_iota(jnp.int32, a_all.shape, 1)
    r_blk = (ar % cc) // C_SUB
    c_blk = ac // cc + (0 if rev else 1)
    a_all = jnp.where(r_blk == c_blk, a_all, 0.0).astype(BF16)
    v_bf = v.astype(BF16)
    r_all = _dot(a_all, jnp.concatenate([v_bf] * len(qblocks), axis=0))
    o = jnp.zeros((cc, w), F32)
    for h in range(C_HEADS):
        o = o + jnp.where(lane_head == h, r_all[h * cc:(h + 1) * cc, :], 0.0)

    ones_bd = _block_ones(w, C_DK, BF16)
    tsub = lax.broadcasted_iota(jnp.int32, (C_SUB, w), 0)
    diag_parts = []
    for i in range(nsub):
        r0 = i * C_SUB
        bb = b[r0:r0 + C_SUB, :]
        qq = q[r0:r0 + C_SUB, :]
        ps = []
        for s in range(C_SUB):
            keep = (tsub <= s) if rev else (tsub >= s)
            e = jnp.where(keep, jnp.exp(bb - b[r0 + s:r0 + s + 1, :]), 0.0)
            ps.append(qq * e * kk[r0 + s:r0 + s + 1, :])
        red = _dot(jnp.concatenate(ps, axis=0).astype(BF16), ones_bd)
        od = jnp.zeros((C_SUB, w), F32)
        for s in range(C_SUB):
            od = od + red[s * C_SUB:(s + 1) * C_SUB, :] * v[r0 + s:r0 + s + 1, :]
        diag_parts.append(od)
    o = o + jnp.concatenate(diag_parts, axis=0)

    st = st_ref[...]
    o = o + _dot_nt((q * jnp.exp(b)).astype(BF16), st.astype(BF16))
    kd = (kk * jnp.exp(b_tot - b)).astype(BF16)
    upd = _dot_tn(v_bf, kd)
    st_ref[...] = st * jnp.exp(b_tot) + upd * _block_ones(w, C_DK, F32)
    return o


def _hgrn_kernel(cf_ref, cb_ref, lbc_ref, of_ref, ob_ref, sf_ref, sb_ref):
    @pl.when(pl.program_id(1) == 0)
    def _():
        sf_ref[...] = jnp.zeros_like(sf_ref)
        sb_ref[...] = jnp.zeros_like(sb_ref)

    for bb in range(HGRN_BPS):
        of_ref[bb] = _hgrn_chunk(cf_ref[bb], 1, lbc_ref[0], sf_ref.at[bb], rev=False)
        ob_ref[bb] = _hgrn_chunk(cb_ref[bb], 2, lbc_ref[1], sb_ref.at[bb], rev=True)


def _hgrn(oc, lbc, n_ctx):
    bsz, n, wc = oc.shape
    nch = n // C_CHUNK
    nctx = n_ctx // C_CHUNK
    nb = HGRN_BPS

    def bwd_idx(c):
        return jnp.where(c < nctx, nctx - 1 - c, nch + nctx - 1 - c)

    return pl.pallas_call(
        _hgrn_kernel,
        grid=(bsz // nb, nch),
        in_specs=[
            pl.BlockSpec((nb, C_CHUNK, wc), lambda b, c: (b, c, 0)),
            pl.BlockSpec((nb, C_CHUNK, wc), lambda b, c: (b, bwd_idx(c), 0)),
            pl.BlockSpec((2, 8, C_W), lambda b, c: (0, 0, 0)),
        ],
        out_specs=[
            pl.BlockSpec((nb, C_CHUNK, C_W), lambda b, c: (b, c, 0)),
            pl.BlockSpec((nb, C_CHUNK, C_W), lambda b, c: (b, bwd_idx(c), 0)),
        ],
        out_shape=[jax.ShapeDtypeStruct((bsz, n, C_W), F32)] * 2,
        scratch_shapes=[pltpu.VMEM((nb, C_W, C_W), F32)] * 2,
        compiler_params=_cparams(("parallel", "arbitrary")),
        name="hgrn2_scan",
    )(oc, oc, lbc)


def _outproj_kernel(x_ref, a_ref, b_ref, of_ref, ob_ref, g_ref, cnw_ref, wa_ref, wb_ref, wc_ref,
                    mod_ref, ln2_ref, xo_ref, h2_ref, *, d):
    o = of_ref[0] + ob_ref[0]
    ssq = jnp.dot(o * o, _block_ones(C_W, C_DV, F32), precision=HIGHEST, preferred_element_type=F32)
    g = g_ref[0]
    c_out = o * lax.rsqrt(ssq * (1.0 / C_DV) + EPS) * cnw_ref[...] * (g * _sigmoid(g))
    mix = _dot(a_ref[0], wa_ref[...]) + _dot(b_ref[0], wb_ref[...]) + _dot(c_out.astype(BF16), wc_ref[...])
    gate1 = mod_ref[0, 0, :, 2 * d:3 * d]
    shift2 = mod_ref[0, 0, :, 3 * d:4 * d]
    scale2 = mod_ref[0, 0, :, 4 * d:5 * d]
    x = x_ref[0] + gate1 * mix
    xo_ref[0] = x
    y = x * lax.rsqrt(jnp.mean(x * x, axis=-1, keepdims=True) + EPS) * ln2_ref[...]
    h2_ref[0] = (y * (1.0 + scale2) + shift2).astype(BF16)


def _outproj(x, a_out, b_out, o_f, o_b, og, cnw, wa, wb, wc, modsel, ln2, first_block):
    bsz, n, d = x.shape
    nblk = n // TB - first_block
    full = lambda *s: pl.BlockSpec(s, lambda b, j: (0,) * len(s))
    tok = lambda w: pl.BlockSpec((1, TB, w), lambda b, j: (b, j + first_block, 0))
    return pl.pallas_call(
        functools.partial(_outproj_kernel, d=d),
        grid=(bsz, nblk),
        in_specs=[
            tok(d), tok(A_W),
            pl.BlockSpec((1, TB, B_HEADS * B_V), lambda b, j: (b, j, 0)) if first_block else tok(B_HEADS * B_V),
            tok(C_W), tok(C_W), tok(C_W),
            full(1, C_W), full(A_W, d), full(B_HEADS * B_V, d), full(C_W, d),
            pl.BlockSpec((1, 1, 1, modsel.shape[-1]), lambda b, j: (b, jnp.minimum(j + first_block, 1), 0, 0)),
            full(1, d),
        ],
        out_specs=[
            pl.BlockSpec((1, TB, d), lambda b, j: (b, j, 0)),
            pl.BlockSpec((1, TB, d), lambda b, j: (b, j, 0)),
        ],
        out_shape=[
            jax.ShapeDtypeStruct((bsz, nblk * TB, d), F32),
            jax.ShapeDtypeStruct((bsz, nblk * TB, d), BF16),
        ],
        compiler_params=_cparams(("parallel", "parallel")),
        name="out_proj",
    )(x, a_out, b_out, o_f, o_b, og, cnw, wa, wb, wc, modsel, ln2)


def _top16_exact(s):
    nrows = s.shape[0]
    iota = lax.broadcasted_iota(jnp.int32, s.shape, 0).astype(F32)
    rank = jnp.full(s.shape, P_TOPK, jnp.int32)
    vals = []
    for r in range(P_TOPK):
        m = jnp.max(s, axis=0, keepdims=True)
        idx = jnp.min(jnp.where(s == m, iota, float(nrows)), axis=0, keepdims=True)
        hit = iota == idx
        rank = jnp.where(hit, r, rank)
        s = jnp.where(hit, -jnp.inf, s)
        vals.append(m)
    return jnp.concatenate(vals, axis=0), rank


_MARK0 = int(np.array(0xFF7FFFFF, np.uint32).view(np.int32))


def _top16_marked(s):
    vals = []
    for r in range(P_TOPK):
        m = jnp.max(s, axis=0, keepdims=True)
        mark = float(np.array(_MARK0 - r, np.int32).view(np.float32))
        s = jnp.where(s == m, mark, s)
        vals.append(m)
    rr = _MARK0 - pltpu.bitcast(s, jnp.int32)
    rank = jnp.where(rr < 0, P_TOPK, jnp.where(rr > P_TOPK - 1, P_TOPK, rr))
    slack = jnp.sum(P_TOPK - rank, axis=0, keepdims=True) - (P_TOPK * (P_TOPK + 1)) // 2
    return jnp.concatenate(vals, axis=0), rank, slack


def _as_words(x):
    return pltpu.bitcast(x.astype(BF16), jnp.int32)


def _row_plane(row):
    n = row.shape[-1]
    tile = jnp.broadcast_to(row, (16, n)).astype(BF16)
    return jnp.concatenate([tile] * (P_N_KEYS // 16), axis=0)


def _route_kernel(h_ref, wq_ref, sk_ref, r2_ref, e2_ref, n1_ref, e1_ref,
                  q_scr, s_scr, v_scr, r1_scr, e1_scr):
    nl = P_N_KEYS
    q = _dot(h_ref[...], wq_ref[...]).astype(BF16)
    for l in range(2 * P_HEADS):
        q_scr[l] = q[:, l * P_HALF:(l + 1) * P_HALF]

    def lane_tile(lt, carry):
        row0 = pl.multiple_of(lt * nl, nl)

        def put_rank(h, p, rank):
            if p == 0:
                r1_scr[h] = rank
            else:
                r2_ref[lt, h] = _as_words(rank.astype(F32))

        def heads(hp, c):
            bad = jnp.zeros((1, nl), jnp.int32)
            for u in range(2 * ROUTE_HPT):
                h, p = hp * ROUTE_HPT + u // 2, u % 2
                s = _dot_nt(sk_ref[h, p], q_scr[2 * h + p, pl.ds(row0, nl), :])
                s_scr[u] = s
                v, rank, slack = _top16_marked(s)
                v_scr[p, h] = v
                put_rank(h, p, rank)
                e = jnp.exp(s - v[0:1, :])
                if p == 0:
                    e1_scr[h] = e
                else:
                    e2_ref[lt, h] = _as_words(e)
                bad = bad + slack

            @pl.when(jnp.max(bad) > 0)
            def _():
                for u in range(2 * ROUTE_HPT):
                    h, p = hp * ROUTE_HPT + u // 2, u % 2
                    ve, re = _top16_exact(s_scr[u])
                    v_scr[p, h] = ve
                    put_rank(h, p, re)

            return c

        lax.fori_loop(0, P_HEADS // ROUTE_HPT, heads, 0)

        v1 = v_scr[0]
        v2 = v_scr[1]
        ia = lax.broadcasted_iota(jnp.int32, v1.shape, 1).astype(F32)
        n = jnp.zeros(v1.shape, F32)
        g = jnp.broadcast_to(v2[:, 0:1, :], v1.shape)
        cmax = v1[:, 0:1, :] + v2[:, 0:1, :]
        z = jnp.zeros(cmax.shape, F32)
        for _ in range(P_TOPK):
            f = v1 + g
            m = jnp.max(f, axis=1, keepdims=True)
            a_star = jnp.min(jnp.where(f == m, ia, float(P_TOPK)), axis=1, keepdims=True)
            hit = ia == a_star
            n = n + jnp.where(hit, 1.0, 0.0)
            nsel = jnp.sum(jnp.where(hit, n, 0.0), axis=1, keepdims=True)
            nxt = jnp.sum(jnp.where(ia == nsel, v2, 0.0), axis=1, keepdims=True)
            nxt = jnp.where(nsel > P_TOPK - 0.5, -jnp.inf, nxt)
            g = jnp.where(hit, nxt, g)
            z = z + jnp.exp(m - cmax)
        zinv = 1.0 / z
        for h in range(P_HEADS):
            rank1 = r1_scr[h]
            n1 = jnp.zeros(rank1.shape, F32)
            for a in range(P_TOPK):
                n1 = jnp.where(rank1 == a, n[h, a:a + 1, :], n1)
            n1_ref[lt, h] = n1
            e1_ref[lt, h] = e1_scr[h] * zinv[h]
        return carry

    def lane_pair(t2, carry):
        lane_tile(2 * t2, carry)
        return lane_tile(2 * t2 + 1, carry)

    lax.fori_loop(0, h_ref.shape[0] // nl // 2, lane_pair, 0)


def _route(h2, wq, sk):
    t, d = h2.shape
    tr = PEER_TT
    nl = P_N_KEYS
    oshape = (t // nl, P_HEADS, P_N_KEYS, nl)
    hshape = (t // nl, P_HEADS, P_N_KEYS // 2, nl)
    ospec = pl.BlockSpec((tr // nl, P_HEADS, P_N_KEYS, nl), lambda i: (i, 0, 0, 0))
    hspec = pl.BlockSpec((tr // nl, P_HEADS, P_N_KEYS // 2, nl), lambda i: (i, 0, 0, 0))
    return pl.pallas_call(
        _route_kernel,
        grid=(t // tr,),
        in_specs=[
            pl.BlockSpec((tr, d), lambda i: (i, 0)),
            pl.BlockSpec((d, P_HEADS * P_KEY_DIM), lambda i: (0, 0)),
            pl.BlockSpec((P_HEADS, 2, P_N_KEYS, P_HALF), lambda i: (0, 0, 0, 0)),
        ],
        out_specs=[hspec, hspec, ospec, ospec],
        out_shape=[jax.ShapeDtypeStruct(hshape, jnp.int32), jax.ShapeDtypeStruct(hshape, jnp.int32),
                   jax.ShapeDtypeStruct(oshape, F32), jax.ShapeDtypeStruct(oshape, F32)],
        scratch_shapes=[
            pltpu.VMEM((2 * P_HEADS, tr, P_HALF), BF16),
            pltpu.VMEM((2 * ROUTE_HPT, P_N_KEYS, nl), F32),
            pltpu.VMEM((2, P_HEADS, P_TOPK, nl), F32),
            pltpu.VMEM((P_HEADS, P_N_KEYS, nl), jnp.int32),
            pltpu.VMEM((P_HEADS, P_N_KEYS, nl), F32),
        ],
        compiler_params=_cparams(("parallel",)),
        name="peer_route",
    )(h2, wq, sk)


def _peer_kernel(h_ref, u_ref, v_ref, r2_ref, e2_ref, n1_ref, e1_ref, x_ref, g5_ref, o_ref,
                 acc_ref, w_ref, a_scr, *, blocks_per_batch, ctx_blocks):
    i = pl.program_id(0)
    k = pl.program_id(1)

    @pl.when(k == 0)
    def _():
        acc_ref[...] = jnp.zeros_like(acc_ref)

    nl = P_N_KEYS
    zero = jnp.zeros((P_N_KEYS, nl), BF16)
    ltc = PEER_TC // nl
    njg = PEER_I1 // PEER_JG

    def gate_block(lt, j0):
        g = [None] * PEER_JG
        for h in range(P_HEADS):
            r2 = pltpu.bitcast(r2_ref[lt, h], BF16)
            e2 = pltpu.bitcast(e2_ref[lt, h], BF16)
            for jj in range(PEER_JG):
                n1row = _row_plane(n1_ref[lt, h, pl.ds(j0 + jj, 1), :])
                e1row = _row_plane(e1_ref[lt, h, pl.ds(j0 + jj, 1), :])
                term = jnp.where(r2 < n1row, e2, zero) * e1row
                g[jj] = term if h == 0 else g[jj] + term
        for jj in range(PEER_JG):
            rows = pl.ds(pl.multiple_of((j0 + jj) * P_N_KEYS, P_N_KEYS), P_N_KEYS)
            w_ref[lt, rows, :] = a_scr[lt, rows, :] * g[jj]

    for c in range(PEER_TT // PEER_TC):
        tok = slice(c * PEER_TC, (c + 1) * PEER_TC)
        a_t = _dot_nt(u_ref[0].astype(BF16), h_ref[tok, :])
        half_cdf = (0.5 * lax.erf(a_t * (2.0 ** -0.5))).astype(BF16) + 0.5
        act_t = a_t.astype(BF16) * half_cdf
        for lc in range(ltc):
            a_scr[c * ltc + lc] = act_t[:, lc * nl:(lc + 1) * nl]

        def body(it, carry, c=c):
            gate_block(c * ltc + it // njg, (it % njg) * PEER_JG)
            return carry

        lax.fori_loop(0, ltc * njg, body, 0)
        w_t = jnp.concatenate([w_ref[c * ltc + lc] for lc in range(ltc)], axis=1)
        acc_ref[:, tok] += _dot_tn(v_ref[0].astype(BF16), w_t)

    @pl.when(k == pl.num_programs(1) - 1)
    def _():
        y = acc_ref[...].T
        for u in range(PEER_TT // TB):
            sblk = i * (PEER_TT // TB) + u
            bidx = sblk // blocks_per_batch
            is_lat = (sblk - bidx * blocks_per_batch) >= ctx_blocks
            gate = g5_ref[2 * bidx + is_lat.astype(jnp.int32)]
            rs = slice(u * TB, (u + 1) * TB)
            o_ref[rs, :] = x_ref[rs, :] + gate * y[rs, :]


def _peer(h2, u_all, v_all, layer, r2, e2, n1, e1, x, g5, blocks_per_batch, ctx_blocks):
    t, d = h2.shape
    ne = u_all.shape[1]
    nl = P_N_KEYS
    rspec = pl.BlockSpec((PEER_TT // nl, P_HEADS, P_N_KEYS // 2, nl), lambda i, k: (i, 0, 0, 0))
    nspec = pl.BlockSpec((PEER_TT // nl, P_HEADS, PEER_I1, nl), lambda i, k: (i, 0, k, 0))
    return pl.pallas_call(
        functools.partial(_peer_kernel, blocks_per_batch=blocks_per_batch, ctx_blocks=ctx_blocks),
        grid=(t // PEER_TT, ne // PEER_ET),
        in_specs=[
            pl.BlockSpec((PEER_TT, d), lambda i, k: (i, 0)),
            pl.BlockSpec((1, PEER_ET, d), lambda i, k: (layer, k, 0)),
            pl.BlockSpec((1, PEER_ET, d), lambda i, k: (layer, k, 0)),
            rspec, rspec, nspec, nspec,
            pl.BlockSpec((PEER_TT, d), lambda i, k: (i, 0)),
            pl.BlockSpec(g5.shape, lambda i, k: (0, 0, 0)),
        ],
        out_specs=pl.BlockSpec((PEER_TT, d), lambda i, k: (i, 0)),
        out_shape=jax.ShapeDtypeStruct((t, d), F32),
        scratch_shapes=[pltpu.VMEM((d, PEER_TT), F32),
                        pltpu.VMEM((PEER_TT // nl, PEER_ET, nl), BF16),
                        pltpu.VMEM((PEER_TT // nl, PEER_ET, nl), BF16)],
        compiler_params=_cparams(("parallel", "arbitrary")),
        name="peer_experts",
    )(h2, u_all, v_all, r2, e2, n1, e1, x, g5)


def _rope_tables(n_ctx, n_lat):
    n_freq = B_ROPE // 4
    pos = np.arange(n_lat)
    inv_freq = ROPE_THETA ** (-np.arange(n_freq, dtype=np.float32) / n_freq)
    inv_freq = jnp.asarray(inv_freq, F32)
    rowp = jnp.asarray(pos // GRID_W, F32)
    colp = jnp.asarray(pos % GRID_W, F32)
    ang = jnp.stack([rowp[:, None] * inv_freq, colp[:, None] * inv_freq], axis=1)
    cos, sin = jnp.cos(ang), jnp.sin(ang)
    cos32 = jnp.concatenate([cos, cos], axis=2).reshape(n_lat, B_ROPE)
    sin32 = jnp.concatenate([-sin, sin], axis=2).reshape(n_lat, B_ROPE)
    pad_l = jnp.ones((n_lat, B_NOPE), F32)
    pad_r = jnp.ones((n_lat, B_HP - B_QK), F32)
    cos_l = jnp.concatenate([pad_l, cos32, pad_r], axis=1)
    sin_l = jnp.concatenate([0 * pad_l, sin32, 0 * pad_r], axis=1)
    cos_t = jnp.concatenate([jnp.ones((n_ctx, B_HP), F32), cos_l], axis=0)
    sin_t = jnp.concatenate([jnp.zeros((n_ctx, B_HP), F32), sin_l], axis=0)
    return cos_t, sin_t


_SWAP32 = np.arange(B_ROPE) ^ (B_ROPE // 4)


def _pad_head(nope, rope):
    z = jnp.zeros(nope.shape[:-1] + (B_HP - B_QK,), nope.dtype)
    out = jnp.concatenate([nope, rope, z], axis=-1)
    return out.reshape(out.shape[:-2] + (out.shape[-2] * B_HP,))


def _layer_weights(layer, w_in, w_out, a_norm_w, a_w_s, a_b_s, b_q_norm_w, b_w_uq, b_kv_norm_w, b_w_ukv,
                   b_q_head_norm_w, b_k_head_norm_w, c_out_norm_w, p_w_q, p_sub_keys):
    d = w_in.shape[1]
    wi = w_in[layer]
    offs = np.cumsum([0, A_W, A_W, B_Q_RANK, B_KV_RANK, B_ROPE, C_W, C_W, C_W, C_W, C_W])
    col = lambda i: wi[:, offs[i]:offs[i + 1]]
    w_kr = col(4)
    zl = jnp.zeros((d, B_NOPE), F32)
    zr = jnp.zeros((d, B_HP - B_QK), F32)
    kr_placed = jnp.concatenate([zl, w_kr, zr], axis=1)
    kr_swapped = jnp.concatenate([zl, w_kr[:, _SWAP32], zr], axis=1)
    w_all = jnp.concatenate(
        [col(0), col(1), col(2), col(3), kr_placed, kr_swapped, col(5), col(6), col(7), col(8), col(9)],
        axis=1).astype(BF16)
    widths = (2 * A_W, B_Q_RANK + B_KV_RANK + 2 * B_HP, 4 * C_W, C_W)

    wuq = b_w_uq[layer].reshape(B_Q_RANK, B_HEADS, B_QK)
    wq_p = _pad_head(wuq[..., :B_NOPE], wuq[..., B_NOPE:]).astype(BF16)
    wq_s = _pad_head(0 * wuq[..., :B_NOPE], wuq[..., B_NOPE:][..., _SWAP32]).astype(BF16)
    wukv = b_w_ukv[layer].reshape(B_KV_RANK, B_HEADS, B_NOPE + B_V)
    wk_p = _pad_head(wukv[..., :B_NOPE], jnp.zeros((B_KV_RANK, B_HEADS, B_ROPE), F32)).astype(BF16)
    wv = wukv[..., B_NOPE:].reshape(B_KV_RANK, B_HEADS * B_V).astype(BF16)
    qn, kn = b_q_head_norm_w[layer], b_k_head_norm_w[layer]
    zpad = jnp.zeros((B_HP - B_QK,), F32)
    z64 = jnp.zeros((B_NOPE,), F32)
    hw = jnp.stack([
        jnp.concatenate([qn, zpad]),
        jnp.concatenate([z64, qn[B_NOPE:][_SWAP32], zpad]),
        jnp.concatenate([kn, zpad]),
        jnp.concatenate([z64, kn[B_NOPE:][_SWAP32], zpad]),
    ] + [jnp.zeros((B_HP,), F32)] * 4, axis=0)

    wo = w_out[layer].astype(BF16)
    return dict(
        w_all=w_all, widths=widths,
        a_nw=a_norm_w[layer].reshape(1, A_W),
        a_ws=a_w_s[layer].astype(BF16),
        a_bias=jnp.repeat(a_b_s[layer].T, A_HD, axis=1),
        qnw=b_q_norm_w[layer].reshape(1, B_Q_RANK), kvnw=b_kv_norm_w[layer].reshape(1, B_KV_RANK),
        wq_p=wq_p, wq_s=wq_s, wk_p=wk_p, wv=wv, hw=hw,
        cnw=jnp.tile(c_out_norm_w[layer], C_HEADS).reshape(1, C_W),
        wo_a=wo[:A_W], wo_b=wo[A_W:A_W + B_HEADS * B_V], wo_c=wo[A_W + B_HEADS * B_V:],
        p_wq=p_w_q[layer].astype(BF16), p_sk=p_sub_keys[layer].astype(BF16),
    )


def kernel(x, c, ctx, c_ctx, ln1_w, ln2_w, w_mod, b_mod, w_in, w_out, a_norm_w, a_w_s, a_b_s, b_q_norm_w,
           b_w_uq, b_kv_norm_w, b_w_ukv, b_q_head_norm_w, b_k_head_norm_w, c_lb_logits, c_out_norm_w,
           p_w_q, p_sub_keys, p_u, p_v):
    bsz, n_lat, d = x.shape
    n_ctx = ctx.shape[1]
    depth = w_in.shape[0]
    n = n_ctx + n_lat
    ctx_blocks = n_ctx // TB
    blocks = n // TB

    mrows = -(-(bsz + 1) // 8) * 8
    cvec = jnp.concatenate([c, c_ctx[None, :], jnp.zeros((mrows - bsz - 1, d), F32)], axis=0)
    mod_all = _modulation(cvec, w_mod, b_mod)

    lb = jnp.cumsum(jax.nn.softmax(c_lb_logits.astype(F32), axis=0), axis=0)
    lb = lb - lb[0:1]
    lbc_all = jnp.stack([jnp.log(lb), jnp.log1p(-lb), 1.0 - lb] + [jnp.zeros_like(lb)] * 5, axis=2)

    cos_t, sin_t = _rope_tables(n_ctx, n_lat)
    xc = jnp.concatenate([ctx, x], axis=1)

    for layer in range(depth):
        last = layer == depth - 1
        w = _layer_weights(layer, w_in, w_out, a_norm_w, a_w_s, a_b_s, b_q_norm_w, b_w_uq, b_kv_norm_w,
                           b_w_ukv, b_q_head_norm_w, b_k_head_norm_w, c_out_norm_w, p_w_q, p_sub_keys)
        mod_b = mod_all[layer, :bsz]
        mod_c = jnp.broadcast_to(mod_all[layer, bsz][None, :], mod_b.shape)
        modsel = jnp.stack([mod_c, mod_b], axis=1)[:, :, None, :]
        first_block = ctx_blocks if last else 0

        a_out, q, k, v, oc, og = _front(xc, ln1_w[layer].reshape(1, d), modsel, w, cos_t, sin_t)
        b_lat = _attention(q, k, v, ctx_blocks, blocks - ctx_blocks, n, n_lat, 2)
        if last:
            b_out = b_lat
        else:
            b_ctx = _attention(q, k, v, 0, ctx_blocks, n_ctx, n_ctx, 1)
            b_out = jnp.concatenate([b_ctx, b_lat], axis=1)
        o_f, o_b = _hgrn(oc, lbc_all[layer], n_ctx)
        x_new, h2 = _outproj(xc, a_out, b_out, o_f, o_b, og, w["cnw"], w["wo_a"], w["wo_b"], w["wo_c"],
                             modsel, ln2_w[layer].reshape(1, d), first_block)

        t = x_new.shape[0] * x_new.shape[1]
        h2f = h2.reshape(t, d)
        r2, e2, n1, e1 = _route(h2f, w["p_wq"], w["p_sk"])
        g5 = modsel[:, :, :, 5 * d:6 * d].reshape(2 * bsz, 1, d)
        out = _peer(h2f, p_u, p_v, layer, r2, e2, n1, e1, x_new.reshape(t, d), g5,
                    blocks - first_block, ctx_blocks - first_block)
        xc = out.reshape(bsz, t // bsz, d)
    return xc
```

```python
import functools
import math

import jax
import jax.numpy as jnp
import numpy as np
from jax import lax
from jax.experimental import pallas as pl
from jax.experimental.pallas import tpu as pltpu

F32 = jnp.float32
BF16 = jnp.bfloat16
HIGHEST = lax.Precision.HIGHEST

EPS = 1e-6
GRID_W = 64
ROPE_THETA = 10000.0

A_HEADS, A_HD, A_CHUNK = 4, 64, 128
A_W = A_HEADS * A_HD
B_HEADS, B_NOPE, B_ROPE, B_V = 8, 64, 32, 64
B_QK = B_NOPE + B_ROPE
B_HP = 128
B_Q_RANK, B_KV_RANK = 256, 128
C_HEADS, C_DK, C_DV, C_CHUNK = 4, 64, 64, 64
C_W = C_HEADS * C_DK
C_SUB = 16
HGRN_BPS = 8
P_HEADS, P_KEY_DIM, P_N_KEYS, P_TOPK = 8, 256, 128, 16
P_HALF = P_KEY_DIM // 2

TB = 256
PEER_TT = 1024
PEER_TC = 1024
PEER_ET = 1024
PEER_I1 = PEER_ET // P_N_KEYS
PEER_JG = 4
ROUTE_HPT = 8
ROUTE_LPT = 4
VMEM_LIMIT = 56 * 1024 * 1024


def _cparams(sem, flags=None):
    return pltpu.CompilerParams(dimension_semantics=sem, vmem_limit_bytes=VMEM_LIMIT, flags=flags)


def _dot_nt(a, b):
    return lax.dot_general(a, b, (((1,), (1,)), ((), ())), preferred_element_type=F32)


def _dot_tn(a, b):
    return lax.dot_general(a, b, (((0,), (0,)), ((), ())), preferred_element_type=F32)


def _dot(a, b):
    return jnp.dot(a, b, preferred_element_type=F32)


def _sigmoid(x):
    return 1.0 / (1.0 + jnp.exp(-x))


def _block_ones(n, blk, dtype):
    r = lax.broadcasted_iota(jnp.int32, (n, n), 0) // blk
    c = lax.broadcasted_iota(jnp.int32, (n, n), 1) // blk
    return (r == c).astype(dtype)


def _mod_kernel(c_ref, w_ref, b_ref, o_ref):
    c = c_ref[...]
    sc = c * _sigmoid(c)
    o_ref[0] = _dot(sc.astype(BF16), w_ref[0].astype(BF16)) + b_ref[0]


def _modulation(cvec, w_mod, b_mod):
    depth, d, n6 = w_mod.shape
    rows = cvec.shape[0]
    tn = 1024
    return pl.pallas_call(
        _mod_kernel,
        grid=(depth, n6 // tn),
        in_specs=[
            pl.BlockSpec((rows, d), lambda l, n: (0, 0)),
            pl.BlockSpec((1, d, tn), lambda l, n: (l, 0, n)),
            pl.BlockSpec((1, 1, tn), lambda l, n: (l, 0, n)),
        ],
        out_specs=pl.BlockSpec((1, rows, tn), lambda l, n: (l, 0, n)),
        out_shape=jax.ShapeDtypeStruct((depth, rows, n6), F32),
        compiler_params=_cparams(("parallel", "parallel")),
        name="adaln_mod",
    )(cvec, w_mod, b_mod.reshape(depth, 1, n6))


def _amix_body(ua, nw_ref, ws_ref, bias_ref):
    u = ua[:, 0:A_W]
    v = ua[:, A_W:2 * A_W]
    ssq = jnp.dot(v * v, _block_ones(A_W, A_HD, F32), precision=HIGHEST, preferred_element_type=F32)
    vn = v * lax.rsqrt(ssq * (1.0 / A_HD) + EPS) * nw_ref[...]
    lane_head = lax.broadcasted_iota(jnp.int32, (A_CHUNK, A_W), 1) // A_HD
    outs = []
    for c in range(TB // A_CHUNK):
        rows = slice(c * A_CHUNK, (c + 1) * A_CHUNK)
        acc = bias_ref[...]
        for h in range(A_HEADS):
            vm = jnp.where(lane_head == h, vn[rows], 0.0).astype(BF16)
            acc = acc + _dot(ws_ref[h], vm)
        outs.append((u[rows] * acc).astype(BF16))
    return jnp.concatenate(outs, axis=0)


def _mla_body(ob, cos, sin, qnw_ref, kvnw_ref, wq_ref, wqs_ref, wk_ref, wv_ref, hw_ref, q_ref, k_ref, v_ref):
    cq = ob[:, 0:B_Q_RANK]
    ckv = ob[:, B_Q_RANK:B_Q_RANK + B_KV_RANK]
    krp = ob[:, B_Q_RANK + B_KV_RANK:B_Q_RANK + B_KV_RANK + B_HP]
    krs = ob[:, B_Q_RANK + B_KV_RANK + B_HP:B_Q_RANK + B_KV_RANK + 2 * B_HP]
    cqn = (cq * lax.rsqrt(jnp.mean(cq * cq, axis=-1, keepdims=True) + EPS) * qnw_ref[...]).astype(BF16)
    ckn = (ckv * lax.rsqrt(jnp.mean(ckv * ckv, axis=-1, keepdims=True) + EPS) * kvnw_ref[...]).astype(BF16)
    q_raw = _dot(cqn, wq_ref[...])
    q_swp = _dot(cqn, wqs_ref[...])
    k_raw = _dot(ckn, wk_ref[...])
    v_all = _dot(ckn, wv_ref[...])
    qw, qws, kw, kws = hw_ref[0:1, :], hw_ref[1:2, :], hw_ref[2:3, :], hw_ref[3:4, :]
    k_rot_sw = krs * kws * sin
    for h in range(B_HEADS):
        sl = slice(h * B_HP, (h + 1) * B_HP)
        qh = q_raw[:, sl]
        rq = lax.rsqrt(jnp.sum(qh * qh, axis=-1, keepdims=True) * (1.0 / B_QK) + EPS)
        q_ref[0, h] = (rq * (qh * qw * cos + q_swp[:, sl] * qws * sin)).astype(BF16)
        kh = k_raw[:, sl] + krp
        rk = lax.rsqrt(jnp.sum(kh * kh, axis=-1, keepdims=True) * (1.0 / B_QK) + EPS)
        k_ref[0, h] = (rk * (kh * kw * cos + k_rot_sw)).astype(BF16)
        v_ref[0, h] = v_all[:, h * B_V:(h + 1) * B_V].astype(BF16)


def _front_kernel(x_ref, lnw_ref, mod_ref, w_ref, anw_ref, aws_ref, abias_ref, cos_ref, sin_ref,
                  qnw_ref, kvnw_ref, wq_ref, wqs_ref, wk_ref, wv_ref, hw_ref,
                  a_ref, q_ref, k_ref, v_ref, oc_ref, og_ref, *, d, widths):
    x = x_ref[0]
    ms = jnp.mean(x * x, axis=-1, keepdims=True)
    y = x * lax.rsqrt(ms + EPS) * lnw_ref[...]
    shift = mod_ref[0, 0, :, 0:d]
    scale = mod_ref[0, 0, :, d:2 * d]
    h = (y * (1.0 + scale) + shift).astype(BF16)
    p = _dot(h, w_ref[...])
    na, nb, nc, _ = widths
    oc_ref[0] = p[:, na + nb:na + nb + nc]
    og_ref[0] = p[:, na + nb + nc:]
    a_ref[0] = _amix_body(p[:, 0:na], anw_ref, aws_ref, abias_ref)
    _mla_body(p[:, na:na + nb], cos_ref[...], sin_ref[...], qnw_ref, kvnw_ref, wq_ref, wqs_ref, wk_ref,
              wv_ref, hw_ref, q_ref, k_ref, v_ref)


def _front(x, lnw, modsel, w, cos_t, sin_t):
    bsz, n, d = x.shape
    na, nb, nc, ng = w["widths"]
    full = lambda *s: pl.BlockSpec(s, lambda b, j: (0,) * len(s))
    tok = lambda width: pl.BlockSpec((1, TB, width), lambda b, j: (b, j, 0))
    head = lambda width: pl.BlockSpec((1, B_HEADS, TB, width), lambda b, j: (b, 0, j, 0))
    return pl.pallas_call(
        functools.partial(_front_kernel, d=d, widths=w["widths"]),
        grid=(bsz, n // TB),
        in_specs=[
            tok(d), full(1, d),
            pl.BlockSpec((1, 1, 1, modsel.shape[-1]), lambda b, j: (b, jnp.minimum(j, 1), 0, 0)),
            full(d, w["w_all"].shape[1]),
            full(1, A_W), full(A_HEADS, A_CHUNK, A_CHUNK), full(A_CHUNK, A_W),
            pl.BlockSpec((TB, B_HP), lambda b, j: (j, 0)),
            pl.BlockSpec((TB, B_HP), lambda b, j: (j, 0)),
            full(1, B_Q_RANK), full(1, B_KV_RANK),
            full(B_Q_RANK, B_HEADS * B_HP), full(B_Q_RANK, B_HEADS * B_HP),
            full(B_KV_RANK, B_HEADS * B_HP), full(B_KV_RANK, B_HEADS * B_V),
            full(8, B_HP),
        ],
        out_specs=[tok(A_W), head(B_HP), head(B_HP), head(B_V), tok(nc), tok(ng)],
        out_shape=[
            jax.ShapeDtypeStruct((bsz, n, A_W), BF16),
            jax.ShapeDtypeStruct((bsz, B_HEADS, n, B_HP), BF16),
            jax.ShapeDtypeStruct((bsz, B_HEADS, n, B_HP), BF16),
            jax.ShapeDtypeStruct((bsz, B_HEADS, n, B_V), BF16),
            jax.ShapeDtypeStruct((bsz, n, nc), F32),
            jax.ShapeDtypeStruct((bsz, n, ng), F32),
        ],
        compiler_params=_cparams(("parallel", "parallel")),
        name="in_proj_mix",
    )(x, lnw, modsel, w["w_all"], w["a_nw"], w["a_ws"], w["a_bias"], cos_t, sin_t, w["qnw"], w["kvnw"],
      w["wq_p"], w["wq_s"], w["wk_p"], w["wv"], w["hw"])


def _attn_kernel(*refs, nqb):
    q_refs, (k_ref, v_ref, o_ref, o_scr) = refs[:nqb], refs[nqb:]
    scale = (B_QK ** -0.5) * math.log2(math.e)
    for h in range(B_HEADS):
        q = q_refs[0][0, h] if nqb == 1 else jnp.concatenate([r[0, h] for r in q_refs], axis=0)
        s = _dot_nt(q, k_ref[0, h])
        m = jnp.max(s, axis=-1, keepdims=True)
        p = jnp.exp2((s - m) * scale)
        l = jnp.sum(p, axis=-1, keepdims=True)
        o = _dot(p.astype(BF16), v_ref[0, h])
        o_scr[:, h * B_V:(h + 1) * B_V] = o / l
    o_ref[0] = o_scr[...].astype(BF16)


def _attention(q, k, v, first_qblock, n_qblocks, n_keys, out_rows, nqb):
    bsz = q.shape[0]
    qspec = lambda u: pl.BlockSpec((1, B_HEADS, TB, B_HP), lambda b, j: (b, 0, j * nqb + u + first_qblock, 0))
    return pl.pallas_call(
        functools.partial(_attn_kernel, nqb=nqb),
        grid=(bsz, n_qblocks // nqb),
        in_specs=[qspec(u) for u in range(nqb)] + [
            pl.BlockSpec((1, B_HEADS, n_keys, B_HP), lambda b, j: (b, 0, 0, 0)),
            pl.BlockSpec((1, B_HEADS, n_keys, B_V), lambda b, j: (b, 0, 0, 0)),
        ],
        out_specs=pl.BlockSpec((1, nqb * TB, B_HEADS * B_V), lambda b, j: (b, j, 0)),
        out_shape=jax.ShapeDtypeStruct((bsz, out_rows, B_HEADS * B_V), BF16),
        scratch_shapes=[pltpu.VMEM((nqb * TB, B_HEADS * B_V), F32)],
        compiler_params=_cparams(("parallel", "arbitrary")),
        name="mla_attention",
    )(*([q] * nqb), k, v)


def _hgrn_chunk(blk, zcol, lbc, st_ref, rev):
    cc, w = C_CHUNK, C_W
    q = blk[:, 0:w] * (C_DK ** -0.5)
    z = blk[:, zcol * w:(zcol + 1) * w]
    v = blk[:, 3 * w:4 * w]
    log_lb, log1m_lb, one_m_lb = lbc[0:1, :], lbc[1:2, :], lbc[2:3, :]
    az = jnp.abs(z)
    sp = jnp.log1p(jnp.exp(-az))
    lsig = jnp.minimum(z, 0.0) - sp
    t2 = log1m_lb + lsig
    mx = jnp.maximum(log_lb, t2)
    mn = jnp.minimum(log_lb, t2)
    logf = mx + jnp.log1p(jnp.exp(mn - mx))
    kk = one_m_lb * _sigmoid(-z)

    ti = lax.broadcasted_iota(jnp.int32, (cc, cc), 0)
    ui = lax.broadcasted_iota(jnp.int32, (cc, cc), 1)
    tri = ((ui >= ti) if rev else (ui <= ti)).astype(F32)
    b = jnp.dot(tri, logf, precision=HIGHEST, preferred_element_type=F32)
    b_tot = b[0:1, :] if rev else b[cc - 1:cc, :]

    row = lax.broadcasted_iota(jnp.int32, (cc, w), 0)
    lane_head = lax.broadcasted_iota(jnp.int32, (cc, w), 1) // C_DK
    nsub = cc // C_SUB
    row_blk = row // C_SUB

    beta_rows = []
    for i in range(nsub):
        if rev:
            src = None if i == nsub - 1 else b[(i + 1) * C_SUB:(i + 1) * C_SUB + 1, :]
        else:
            src = None if i == 0 else b[i * C_SUB - 1:i * C_SUB, :]
        beta_rows.append(src)
    beta_full = jnp.concatenate(
        [jnp.broadcast_to(b[i * C_SUB:i * C_SUB + 1, :] if r is None else r, (C_SUB, w))
         for i, r in enumerate(beta_rows)], axis=0)
    has_prev = (row_blk < nsub - 1) if rev else (row_blk > 0)
    qs = jnp.where(has_prev, q * jnp.exp(b - beta_full), 0.0)

    q_stack = jnp.concatenate([jnp.where(lane_head == h, qs, 0.0) for h in range(C_HEADS)], axis=0).astype(BF16)
    qblocks = [i for i in range(nsub) if beta_rows[i] is not None]
    ks_parts = []
    for i in qblocks:
        prev = (row_blk > i) if rev else (row_blk < i)
        ks_parts.append(jnp.where(prev, kk * jnp.exp(beta_rows[i] - b), 0.0))
    ks_all = jnp.concatenate(ks_parts, axis=0).astype(BF16)
    a_all = _dot_nt(q_stack, ks_all)
    ar = lax.broadcasted_iota(jnp.int32, a_all.shape, 0)
    ac = lax.broadcasted_iota(jnp.int32, a_all.shape, 1)
    r_blk = (ar % cc) // C_SUB
    c_blk = ac // cc + (0 if rev else 1)
    a_all = jnp.where(r_blk == c_blk, a_all, 0.0).astype(BF16)
    v_bf = v.astype(BF16)
    r_all = _dot(a_all, jnp.concatenate([v_bf] * len(qblocks), axis=0))
    o = jnp.zeros((cc, w), F32)
    for h in range(C_HEADS):
        o = o + jnp.where(lane_head == h, r_all[h * cc:(h + 1) * cc, :], 0.0)

    ones_bd = _block_ones(w, C_DK, BF16)
    tsub = lax.broadcasted_iota(jnp.int32, (C_SUB, w), 0)
    diag_parts = []
    for i in range(nsub):
        r0 = i * C_SUB
        bb = b[r0:r0 + C_SUB, :]
        qq = q[r0:r0 + C_SUB, :]
        ps = []
        for s in range(C_SUB):
            keep = (tsub <= s) if rev else (tsub >= s)
            e = jnp.where(keep, jnp.exp(bb - b[r0 + s:r0 + s + 1, :]), 0.0)
            ps.append(qq * e * kk[r0 + s:r0 + s + 1, :])
        red = _dot(jnp.concatenate(ps, axis=0).astype(BF16), ones_bd)
        od = jnp.zeros((C_SUB, w), F32)
        for s in range(C_SUB):
            od = od + red[s * C_SUB:(s + 1) * C_SUB, :] * v[r0 + s:r0 + s + 1, :]
        diag_parts.append(od)
    o = o + jnp.concatenate(diag_parts, axis=0)

    st = st_ref[...]
    o = o + _dot_nt((q * jnp.exp(b)).astype(BF16), st.astype(BF16))
    kd = (kk * jnp.exp(b_tot - b)).astype(BF16)
    upd = _dot_tn(v_bf, kd)
    st_ref[...] = st * jnp.exp(b_tot) + upd * _block_ones(w, C_DK, F32)
    return o


def _hgrn_kernel(cf_ref, cb_ref, lbc_ref, of_ref, ob_ref, sf_ref, sb_ref):
    @pl.when(pl.program_id(1) == 0)
    def _():
        sf_ref[...] = jnp.zeros_like(sf_ref)
        sb_ref[...] = jnp.zeros_like(sb_ref)

    for bb in range(HGRN_BPS):
        of_ref[bb] = _hgrn_chunk(cf_ref[bb], 1, lbc_ref[0], sf_ref.at[bb], rev=False)
        ob_ref[bb] = _hgrn_chunk(cb_ref[bb], 2, lbc_ref[1], sb_ref.at[bb], rev=True)


def _hgrn(oc, lbc, n_ctx):
    bsz, n, wc = oc.shape
    nch = n // C_CHUNK
    nctx = n_ctx // C_CHUNK
    nb = HGRN_BPS

    def bwd_idx(c):
        return jnp.where(c < nctx, nctx - 1 - c, nch + nctx - 1 - c)

    return pl.pallas_call(
        _hgrn_kernel,
        grid=(bsz // nb, nch),
        in_specs=[
            pl.BlockSpec((nb, C_CHUNK, wc), lambda b, c: (b, c, 0)),
            pl.BlockSpec((nb, C_CHUNK, wc), lambda b, c: (b, bwd_idx(c), 0)),
            pl.BlockSpec((2, 8, C_W), lambda b, c: (0, 0, 0)),
        ],
        out_specs=[
            pl.BlockSpec((nb, C_CHUNK, C_W), lambda b, c: (b, c, 0)),
            pl.BlockSpec((nb, C_CHUNK, C_W), lambda b, c: (b, bwd_idx(c), 0)),
        ],
        out_shape=[jax.ShapeDtypeStruct((bsz, n, C_W), F32)] * 2,
        scratch_shapes=[pltpu.VMEM((nb, C_W, C_W), F32)] * 2,
        compiler_params=_cparams(("parallel", "arbitrary")),
        name="hgrn2_scan",
    )(oc, oc, lbc)


def _outproj_kernel(x_ref, a_ref, b_ref, of_ref, ob_ref, g_ref, cnw_ref, wa_ref, wb_ref, wc_ref,
                    mod_ref, ln2_ref, xo_ref, h2_ref, *, d):
    o = of_ref[0] + ob_ref[0]
    ssq = jnp.dot(o * o, _block_ones(C_W, C_DV, F32), precision=HIGHEST, preferred_element_type=F32)
    g = g_ref[0]
    c_out = o * lax.rsqrt(ssq * (1.0 / C_DV) + EPS) * cnw_ref[...] * (g * _sigmoid(g))
    mix = _dot(a_ref[0], wa_ref[...]) + _dot(b_ref[0], wb_ref[...]) + _dot(c_out.astype(BF16), wc_ref[...])
    gate1 = mod_ref[0, 0, :, 2 * d:3 * d]
    shift2 = mod_ref[0, 0, :, 3 * d:4 * d]
    scale2 = mod_ref[0, 0, :, 4 * d:5 * d]
    x = x_ref[0] + gate1 * mix
    xo_ref[0] = x
    y = x * lax.rsqrt(jnp.mean(x * x, axis=-1, keepdims=True) + EPS) * ln2_ref[...]
    h2_ref[0] = (y * (1.0 + scale2) + shift2).astype(BF16)


def _outproj(x, a_out, b_out, o_f, o_b, og, cnw, wa, wb, wc, modsel, ln2, first_block):
    bsz, n, d = x.shape
    nblk = n // TB - first_block
    full = lambda *s: pl.BlockSpec(s, lambda b, j: (0,) * len(s))
    tok = lambda w: pl.BlockSpec((1, TB, w), lambda b, j: (b, j + first_block, 0))
    return pl.pallas_call(
        functools.partial(_outproj_kernel, d=d),
        grid=(bsz, nblk),
        in_specs=[
            tok(d), tok(A_W),
            pl.BlockSpec((1, TB, B_HEADS * B_V), lambda b, j: (b, j, 0)) if first_block else tok(B_HEADS * B_V),
            tok(C_W), tok(C_W), tok(C_W),
            full(1, C_W), full(A_W, d), full(B_HEADS * B_V, d), full(C_W, d),
            pl.BlockSpec((1, 1, 1, modsel.shape[-1]), lambda b, j: (b, jnp.minimum(j + first_block, 1), 0, 0)),
            full(1, d),
        ],
        out_specs=[
            pl.BlockSpec((1, TB, d), lambda b, j: (b, j, 0)),
            pl.BlockSpec((1, TB, d), lambda b, j: (b, j, 0)),
        ],
        out_shape=[
            jax.ShapeDtypeStruct((bsz, nblk * TB, d), F32),
            jax.ShapeDtypeStruct((bsz, nblk * TB, d), BF16),
        ],
        compiler_params=_cparams(("parallel", "parallel")),
        name="out_proj",
    )(x, a_out, b_out, o_f, o_b, og, cnw, wa, wb, wc, modsel, ln2)


def _top16_exact(s):
    nrows = s.shape[0]
    iota = lax.broadcasted_iota(jnp.int32, s.shape, 0).astype(F32)
    rank = jnp.full(s.shape, P_TOPK, jnp.int32)
    vals = []
    for r in range(P_TOPK):
        m = jnp.max(s, axis=0, keepdims=True)
        idx = jnp.min(jnp.where(s == m, iota, float(nrows)), axis=0, keepdims=True)
        hit = iota == idx
        rank = jnp.where(hit, r, rank)
        s = jnp.where(hit, -jnp.inf, s)
        vals.append(m)
    return jnp.concatenate(vals, axis=0), rank


_MARK0 = int(np.array(0xFF7FFFFF, np.uint32).view(np.int32))


def _top16_marked(s):
    vals = []
    for r in range(P_TOPK):
        m = jnp.max(s, axis=0, keepdims=True)
        mark = float(np.array(_MARK0 - r, np.int32).view(np.float32))
        s = jnp.where(s == m, mark, s)
        vals.append(m)
    rr = _MARK0 - pltpu.bitcast(s, jnp.int32)
    rank = jnp.where(rr < 0, P_TOPK, jnp.where(rr > P_TOPK - 1, P_TOPK, rr))
    slack = jnp.sum(P_TOPK - rank, axis=0, keepdims=True) - (P_TOPK * (P_TOPK + 1)) // 2
    return jnp.concatenate(vals, axis=0), rank, slack


def _as_words(x):
    return pltpu.bitcast(x.astype(BF16), jnp.int32)


def _row_plane(row):
    n = row.shape[-1]
    tile = jnp.broadcast_to(row, (16, n)).astype(BF16)
    return jnp.concatenate([tile] * (P_N_KEYS // 16), axis=0)


def _route_kernel(h_ref, wq_ref, sk_ref, r2_ref, e2_ref, n1_ref, e1_ref,
                  q_scr, s_scr, v_scr, r1_scr, e1_scr):
    nl = P_N_KEYS
    q = _dot(h_ref[...], wq_ref[...]).astype(BF16)
    for l in range(2 * P_HEADS):
        q_scr[l] = q[:, l * P_HALF:(l + 1) * P_HALF]

    def lane_tile(lt, carry):
        row0 = pl.multiple_of(lt * nl, nl)

        def put_rank(h, p, rank):
            if p == 0:
                r1_scr[h] = rank
            else:
                r2_ref[lt, h] = _as_words(rank.astype(F32))

        def heads(hp, c):
            bad = jnp.zeros((1, nl), jnp.int32)
            for u in range(2 * ROUTE_HPT):
                h, p = hp * ROUTE_HPT + u // 2, u % 2
                s = _dot_nt(sk_ref[h, p], q_scr[2 * h + p, pl.ds(row0, nl), :])
                s_scr[u] = s
                v, rank, slack = _top16_marked(s)
                v_scr[p, h] = v
                put_rank(h, p, rank)
                e = jnp.exp(s - v[0:1, :])
                if p == 0:
                    e1_scr[h] = e
                else:
                    e2_ref[lt, h] = _as_words(e)
                bad = bad + slack

            @pl.when(jnp.max(bad) > 0)
            def _():
                for u in range(2 * ROUTE_HPT):
                    h, p = hp * ROUTE_HPT + u // 2, u % 2
                    ve, re = _top16_exact(s_scr[u])
                    v_scr[p, h] = ve
                    put_rank(h, p, re)

            return c

        lax.fori_loop(0, P_HEADS // ROUTE_HPT, heads, 0)

        v1 = v_scr[0]
        v2 = v_scr[1]
        ia = lax.broadcasted_iota(jnp.int32, v1.shape, 1).astype(F32)
        n = jnp.zeros(v1.shape, F32)
        g = jnp.broadcast_to(v2[:, 0:1, :], v1.shape)
        cmax = v1[:, 0:1, :] + v2[:, 0:1, :]
        z = jnp.zeros(cmax.shape, F32)
        for _ in range(P_TOPK):
            f = v1 + g
            m = jnp.max(f, axis=1, keepdims=True)
            a_star = jnp.min(jnp.where(f == m, ia, float(P_TOPK)), axis=1, keepdims=True)
            hit = ia == a_star
            n = n + jnp.where(hit, 1.0, 0.0)
            nsel = jnp.sum(jnp.where(hit, n, 0.0), axis=1, keepdims=True)
            nxt = jnp.sum(jnp.where(ia == nsel, v2, 0.0), axis=1, keepdims=True)
            nxt = jnp.where(nsel > P_TOPK - 0.5, -jnp.inf, nxt)
            g = jnp.where(hit, nxt, g)
            z = z + jnp.exp(m - cmax)
        zinv = 1.0 / z
        for h in range(P_HEADS):
            rank1 = r1_scr[h]
            n1 = jnp.zeros(rank1.shape, F32)
            for a in range(P_TOPK):
                n1 = jnp.where(rank1 == a, n[h, a:a + 1, :], n1)
            n1_ref[lt, h] = n1
            e1_ref[lt, h] = e1_scr[h] * zinv[h]
        return carry

    def lane_group(tg, carry):
        for u in range(ROUTE_LPT):
            lane_tile(ROUTE_LPT * tg + u, carry)
        return carry

    lax.fori_loop(0, h_ref.shape[0] // nl // ROUTE_LPT, lane_group, 0)


def _route(h2, wq, sk):
    t, d = h2.shape
    tr = PEER_TT
    nl = P_N_KEYS
    oshape = (t // nl, P_HEADS, P_N_KEYS, nl)
    hshape = (t // nl, P_HEADS, P_N_KEYS // 2, nl)
    ospec = pl.BlockSpec((tr // nl, P_HEADS, P_N_KEYS, nl), lambda i: (i, 0, 0, 0))
    hspec = pl.BlockSpec((tr // nl, P_HEADS, P_N_KEYS // 2, nl), lambda i: (i, 0, 0, 0))
    return pl.pallas_call(
        _route_kernel,
        grid=(t // tr,),
        in_specs=[
            pl.BlockSpec((tr, d), lambda i: (i, 0)),
            pl.BlockSpec((d, P_HEADS * P_KEY_DIM), lambda i: (0, 0)),
            pl.BlockSpec((P_HEADS, 2, P_N_KEYS, P_HALF), lambda i: (0, 0, 0, 0)),
        ],
        out_specs=[hspec, hspec, ospec, ospec],
        out_shape=[jax.ShapeDtypeStruct(hshape, jnp.int32), jax.ShapeDtypeStruct(hshape, jnp.int32),
                   jax.ShapeDtypeStruct(oshape, F32), jax.ShapeDtypeStruct(oshape, F32)],
        scratch_shapes=[
            pltpu.VMEM((2 * P_HEADS, tr, P_HALF), BF16),
            pltpu.VMEM((2 * ROUTE_HPT, P_N_KEYS, nl), F32),
            pltpu.VMEM((2, P_HEADS, P_TOPK, nl), F32),
            pltpu.VMEM((P_HEADS, P_N_KEYS, nl), jnp.int32),
            pltpu.VMEM((P_HEADS, P_N_KEYS, nl), F32),
        ],
        compiler_params=_cparams(("parallel",)),
        name="peer_route",
    )(h2, wq, sk)


def _peer_kernel(h_ref, u_ref, v_ref, r2_ref, e2_ref, n1_ref, e1_ref, x_ref, g5_ref, o_ref,
                 acc_ref, w_ref, a_scr, *, blocks_per_batch, ctx_blocks):
    i = pl.program_id(0)
    k = pl.program_id(1)

    @pl.when(k == 0)
    def _():
        acc_ref[...] = jnp.zeros_like(acc_ref)

    nl = P_N_KEYS
    zero = jnp.zeros((P_N_KEYS, nl), BF16)
    ltc = PEER_TC // nl
    njg = PEER_I1 // PEER_JG

    def gate_block(lt, j0):
        g = [None] * PEER_JG
        for h in range(P_HEADS):
            r2 = pltpu.bitcast(r2_ref[lt, h], BF16)
            e2 = pltpu.bitcast(e2_ref[lt, h], BF16)
            for jj in range(PEER_JG):
                n1row = _row_plane(n1_ref[lt, h, pl.ds(j0 + jj, 1), :])
                e1row = _row_plane(e1_ref[lt, h, pl.ds(j0 + jj, 1), :])
                term = jnp.where(r2 < n1row, e2, zero) * e1row
                g[jj] = term if h == 0 else g[jj] + term
        for jj in range(PEER_JG):
            rows = pl.ds(pl.multiple_of((j0 + jj) * P_N_KEYS, P_N_KEYS), P_N_KEYS)
            w_ref[lt, rows, :] = a_scr[lt, rows, :] * g[jj]

    for c in range(PEER_TT // PEER_TC):
        tok = slice(c * PEER_TC, (c + 1) * PEER_TC)
        a_t = _dot_nt(u_ref[0].astype(BF16), h_ref[tok, :])
        half_cdf = (0.5 * lax.erf(a_t * (2.0 ** -0.5))).astype(BF16) + 0.5
        act_t = a_t.astype(BF16) * half_cdf
        for lc in range(ltc):
            a_scr[c * ltc + lc] = act_t[:, lc * nl:(lc + 1) * nl]

        def body(it, carry, c=c):
            gate_block(c * ltc + it // njg, (it % njg) * PEER_JG)
            return carry

        lax.fori_loop(0, ltc * njg, body, 0)
        w_t = jnp.concatenate([w_ref[c * ltc + lc] for lc in range(ltc)], axis=1)
        acc_ref[:, tok] += _dot_tn(v_ref[0].astype(BF16), w_t)

    @pl.when(k == pl.num_programs(1) - 1)
    def _():
        y = acc_ref[...].T
        for u in range(PEER_TT // TB):
            sblk = i * (PEER_TT // TB) + u
            bidx = sblk // blocks_per_batch
            is_lat = (sblk - bidx * blocks_per_batch) >= ctx_blocks
            gate = g5_ref[2 * bidx + is_lat.astype(jnp.int32)]
            rs = slice(u * TB, (u + 1) * TB)
            o_ref[rs, :] = x_ref[rs, :] + gate * y[rs, :]


def _peer(h2, u_all, v_all, layer, r2, e2, n1, e1, x, g5, blocks_per_batch, ctx_blocks):
    t, d = h2.shape
    ne = u_all.shape[1]
    nl = P_N_KEYS
    rspec = pl.BlockSpec((PEER_TT // nl, P_HEADS, P_N_KEYS // 2, nl), lambda i, k: (i, 0, 0, 0))
    nspec = pl.BlockSpec((PEER_TT // nl, P_HEADS, PEER_I1, nl), lambda i, k: (i, 0, k, 0))
    return pl.pallas_call(
        functools.partial(_peer_kernel, blocks_per_batch=blocks_per_batch, ctx_blocks=ctx_blocks),
        grid=(t // PEER_TT, ne // PEER_ET),
        in_specs=[
            pl.BlockSpec((PEER_TT, d), lambda i, k: (i, 0)),
            pl.BlockSpec((1, PEER_ET, d), lambda i, k: (layer, k, 0)),
            pl.BlockSpec((1, PEER_ET, d), lambda i, k: (layer, k, 0)),
            rspec, rspec, nspec, nspec,
            pl.BlockSpec((PEER_TT, d), lambda i, k: (i, 0)),
            pl.BlockSpec(g5.shape, lambda i, k: (0, 0, 0)),
        ],
        out_specs=pl.BlockSpec((PEER_TT, d), lambda i, k: (i, 0)),
        out_shape=jax.ShapeDtypeStruct((t, d), F32),
        scratch_shapes=[pltpu.VMEM((d, PEER_TT), F32),
                        pltpu.VMEM((PEER_TT // nl, PEER_ET, nl), BF16),
                        pltpu.VMEM((PEER_TT // nl, PEER_ET, nl), BF16)],
        compiler_params=_cparams(("parallel", "arbitrary")),
        name="peer_experts",
    )(h2, u_all, v_all, r2, e2, n1, e1, x, g5)


def _rope_tables(n_ctx, n_lat):
    n_freq = B_ROPE // 4
    pos = np.arange(n_lat)
    inv_freq = ROPE_THETA ** (-np.arange(n_freq, dtype=np.float32) / n_freq)
    inv_freq = jnp.asarray(inv_freq, F32)
    rowp = jnp.asarray(pos // GRID_W, F32)
    colp = jnp.asarray(pos % GRID_W, F32)
    ang = jnp.stack([rowp[:, None] * inv_freq, colp[:, None] * inv_freq], axis=1)
    cos, sin = jnp.cos(ang), jnp.sin(ang)
    cos32 = jnp.concatenate([cos, cos], axis=2).reshape(n_lat, B_ROPE)
    sin32 = jnp.concatenate([-sin, sin], axis=2).reshape(n_lat, B_ROPE)
    pad_l = jnp.ones((n_lat, B_NOPE), F32)
    pad_r = jnp.ones((n_lat, B_HP - B_QK), F32)
    cos_l = jnp.concatenate([pad_l, cos32, pad_r], axis=1)
    sin_l = jnp.concatenate([0 * pad_l, sin32, 0 * pad_r], axis=1)
    cos_t = jnp.concatenate([jnp.ones((n_ctx, B_HP), F32), cos_l], axis=0)
    sin_t = jnp.concatenate([jnp.zeros((n_ctx, B_HP), F32), sin_l], axis=0)
    return cos_t, sin_t


_SWAP32 = np.arange(B_ROPE) ^ (B_ROPE // 4)


def _pad_head(nope, rope):
    z = jnp.zeros(nope.shape[:-1] + (B_HP - B_QK,), nope.dtype)
    out = jnp.concatenate([nope, rope, z], axis=-1)
    return out.reshape(out.shape[:-2] + (out.shape[-2] * B_HP,))


def _layer_weights(layer, w_in, w_out, a_norm_w, a_w_s, a_b_s, b_q_norm_w, b_w_uq, b_kv_norm_w, b_w_ukv,
                   b_q_head_norm_w, b_k_head_norm_w, c_out_norm_w, p_w_q, p_sub_keys):
    d = w_in.shape[1]
    wi = w_in[layer]
    offs = np.cumsum([0, A_W, A_W, B_Q_RANK, B_KV_RANK, B_ROPE, C_W, C_W, C_W, C_W, C_W])
    col = lambda i: wi[:, offs[i]:offs[i + 1]]
    w_kr = col(4)
    zl = jnp.zeros((d, B_NOPE), F32)
    zr = jnp.zeros((d, B_HP - B_QK), F32)
    kr_placed = jnp.concatenate([zl, w_kr, zr], axis=1)
    kr_swapped = jnp.concatenate([zl, w_kr[:, _SWAP32], zr], axis=1)
    w_all = jnp.concatenate(
        [col(0), col(1), col(2), col(3), kr_placed, kr_swapped, col(5), col(6), col(7), col(8), col(9)],
        axis=1).astype(BF16)
    widths = (2 * A_W, B_Q_RANK + B_KV_RANK + 2 * B_HP, 4 * C_W, C_W)

    wuq = b_w_uq[layer].reshape(B_Q_RANK, B_HEADS, B_QK)
    wq_p = _pad_head(wuq[..., :B_NOPE], wuq[..., B_NOPE:]).astype(BF16)
    wq_s = _pad_head(0 * wuq[..., :B_NOPE], wuq[..., B_NOPE:][..., _SWAP32]).astype(BF16)
    wukv = b_w_ukv[layer].reshape(B_KV_RANK, B_HEADS, B_NOPE + B_V)
    wk_p = _pad_head(wukv[..., :B_NOPE], jnp.zeros((B_KV_RANK, B_HEADS, B_ROPE), F32)).astype(BF16)
    wv = wukv[..., B_NOPE:].reshape(B_KV_RANK, B_HEADS * B_V).astype(BF16)
    qn, kn = b_q_head_norm_w[layer], b_k_head_norm_w[layer]
    zpad = jnp.zeros((B_HP - B_QK,), F32)
    z64 = jnp.zeros((B_NOPE,), F32)
    hw = jnp.stack([
        jnp.concatenate([qn, zpad]),
        jnp.concatenate([z64, qn[B_NOPE:][_SWAP32], zpad]),
        jnp.concatenate([kn, zpad]),
        jnp.concatenate([z64, kn[B_NOPE:][_SWAP32], zpad]),
    ] + [jnp.zeros((B_HP,), F32)] * 4, axis=0)

    wo = w_out[layer].astype(BF16)
    return dict(
        w_all=w_all, widths=widths,
        a_nw=a_norm_w[layer].reshape(1, A_W),
        a_ws=a_w_s[layer].astype(BF16),
        a_bias=jnp.repeat(a_b_s[layer].T, A_HD, axis=1),
        qnw=b_q_norm_w[layer].reshape(1, B_Q_RANK), kvnw=b_kv_norm_w[layer].reshape(1, B_KV_RANK),
        wq_p=wq_p, wq_s=wq_s, wk_p=wk_p, wv=wv, hw=hw,
        cnw=jnp.tile(c_out_norm_w[layer], C_HEADS).reshape(1, C_W),
        wo_a=wo[:A_W], wo_b=wo[A_W:A_W + B_HEADS * B_V], wo_c=wo[A_W + B_HEADS * B_V:],
        p_wq=p_w_q[layer].astype(BF16), p_sk=p_sub_keys[layer].astype(BF16),
    )


def kernel(x, c, ctx, c_ctx, ln1_w, ln2_w, w_mod, b_mod, w_in, w_out, a_norm_w, a_w_s, a_b_s, b_q_norm_w,
           b_w_uq, b_kv_norm_w, b_w_ukv, b_q_head_norm_w, b_k_head_norm_w, c_lb_logits, c_out_norm_w,
           p_w_q, p_sub_keys, p_u, p_v):
    bsz, n_lat, d = x.shape
    n_ctx = ctx.shape[1]
    depth = w_in.shape[0]
    n = n_ctx + n_lat
    ctx_blocks = n_ctx // TB
    blocks = n // TB

    mrows = -(-(bsz + 1) // 8) * 8
    cvec = jnp.concatenate([c, c_ctx[None, :], jnp.zeros((mrows - bsz - 1, d), F32)], axis=0)
    mod_all = _modulation(cvec, w_mod, b_mod)

    lb = jnp.cumsum(jax.nn.softmax(c_lb_logits.astype(F32), axis=0), axis=0)
    lb = lb - lb[0:1]
    lbc_all = jnp.stack([jnp.log(lb), jnp.log1p(-lb), 1.0 - lb] + [jnp.zeros_like(lb)] * 5, axis=2)

    cos_t, sin_t = _rope_tables(n_ctx, n_lat)
    xc = jnp.concatenate([ctx, x], axis=1)

    for layer in range(depth):
        last = layer == depth - 1
        w = _layer_weights(layer, w_in, w_out, a_norm_w, a_w_s, a_b_s, b_q_norm_w, b_w_uq, b_kv_norm_w,
                           b_w_ukv, b_q_head_norm_w, b_k_head_norm_w, c_out_norm_w, p_w_q, p_sub_keys)
        mod_b = mod_all[layer, :bsz]
        mod_c = jnp.broadcast_to(mod_all[layer, bsz][None, :], mod_b.shape)
        modsel = jnp.stack([mod_c, mod_b], axis=1)[:, :, None, :]
        first_block = ctx_blocks if last else 0

        a_out, q, k, v, oc, og = _front(xc, ln1_w[layer].reshape(1, d), modsel, w, cos_t, sin_t)
        b_lat = _attention(q, k, v, ctx_blocks, blocks - ctx_blocks, n, n_lat, 2)
        if last:
            b_out = b_lat
        else:
            b_ctx = _attention(q, k, v, 0, ctx_blocks, n_ctx, n_ctx, 1)
            b_out = jnp.concatenate([b_ctx, b_lat], axis=1)
        o_f, o_b = _hgrn(oc, lbc_all[layer], n_ctx)
        x_new, h2 = _outproj(xc, a_out, b_out, o_f, o_b, og, w["cnw"], w["wo_a"], w["wo_b"], w["wo_c"],
                             modsel, ln2_w[layer].reshape(1, d), first_block)

        t = x_new.shape[0] * x_new.shape[1]
        h2f = h2.reshape(t, d)
        r2, e2, n1, e1 = _route(h2f, w["p_wq"], w["p_sk"])
        g5 = modsel[:, :, :, 5 * d:6 * d].reshape(2 * bsz, 1, d)
        out = _peer(h2f, p_u, p_v, layer, r2, e2, n1, e1, x_new.reshape(t, d), g5,
                    blocks - first_block, ctx_blocks - first_block)
        xc = out.reshape(bsz, t // bsz, d)
    return xc
```

```python
import functools
import math

import jax
import jax.numpy as jnp
import numpy as np
from jax import lax
from jax.experimental import pallas as pl
from jax.experimental.pallas import tpu as pltpu

F32 = jnp.float32
BF16 = jnp.bfloat16
HIGHEST = lax.Precision.HIGHEST

EPS = 1e-6
GRID_W = 64
ROPE_THETA = 10000.0

A_HEADS, A_HD, A_CHUNK = 4, 64, 128
A_W = A_HEADS * A_HD
B_HEADS, B_NOPE, B_ROPE, B_V = 8, 64, 32, 64
B_QK = B_NOPE + B_ROPE
B_HP = 128
B_Q_RANK, B_KV_RANK = 256, 128
C_HEADS, C_DK, C_DV, C_CHUNK = 4, 64, 64, 64
C_W = C_HEADS * C_DK
C_SUB = 16
HGRN_BPS = 8
P_HEADS, P_KEY_DIM, P_N_KEYS, P_TOPK = 8, 256, 128, 16
P_HALF = P_KEY_DIM // 2

TB = 256
PEER_TT = 1024
PEER_TC = 1024
PEER_ET = 1024
PEER_I1 = PEER_ET // P_N_KEYS
PEER_JG = 4
ROUTE_HPT = 8
VMEM_LIMIT = 56 * 1024 * 1024


def _cparams(sem, flags=None):
    return pltpu.CompilerParams(dimension_semantics=sem, vmem_limit_bytes=VMEM_LIMIT, flags=flags)


def _dot_nt(a, b):
    return lax.dot_general(a, b, (((1,), (1,)), ((), ())), preferred_element_type=F32)


def _dot_tn(a, b):
    return lax.dot_general(a, b, (((0,), (0,)), ((), ())), preferred_element_type=F32)


def _dot(a, b):
    return jnp.dot(a, b, preferred_element_type=F32)


def _sigmoid(x):
    return 1.0 / (1.0 + jnp.exp(-x))


def _block_ones(n, blk, dtype):
    r = lax.broadcasted_iota(jnp.int32, (n, n), 0) // blk
    c = lax.broadcasted_iota(jnp.int32, (n, n), 1) // blk
    return (r == c).astype(dtype)


def _mod_kernel(c_ref, w_ref, b_ref, o_ref):
    c = c_ref[...]
    sc = c * _sigmoid(c)
    o_ref[0] = _dot(sc.astype(BF16), w_ref[0].astype(BF16)) + b_ref[0]


def _modulation(cvec, w_mod, b_mod):
    depth, d, n6 = w_mod.shape
    rows = cvec.shape[0]
    tn = 1024
    return pl.pallas_call(
        _mod_kernel,
        grid=(depth, n6 // tn),
        in_specs=[
            pl.BlockSpec((rows, d), lambda l, n: (0, 0)),
            pl.BlockSpec((1, d, tn), lambda l, n: (l, 0, n)),
            pl.BlockSpec((1, 1, tn), lambda l, n: (l, 0, n)),
        ],
        out_specs=pl.BlockSpec((1, rows, tn), lambda l, n: (l, 0, n)),
        out_shape=jax.ShapeDtypeStruct((depth, rows, n6), F32),
        compiler_params=_cparams(("parallel", "parallel")),
        name="adaln_mod",
    )(cvec, w_mod, b_mod.reshape(depth, 1, n6))


def _amix_body(ua, nw_ref, ws_ref, bias_ref):
    u = ua[:, 0:A_W]
    v = ua[:, A_W:2 * A_W]
    ssq = jnp.dot(v * v, _block_ones(A_W, A_HD, F32), precision=HIGHEST, preferred_element_type=F32)
    vn = v * lax.rsqrt(ssq * (1.0 / A_HD) + EPS) * nw_ref[...]
    lane_head = lax.broadcasted_iota(jnp.int32, (A_CHUNK, A_W), 1) // A_HD
    outs = []
    for c in range(TB // A_CHUNK):
        rows = slice(c * A_CHUNK, (c + 1) * A_CHUNK)
        acc = bias_ref[...]
        for h in range(A_HEADS):
            vm = jnp.where(lane_head == h, vn[rows], 0.0).astype(BF16)
            acc = acc + _dot(ws_ref[h], vm)
        outs.append((u[rows] * acc).astype(BF16))
    return jnp.concatenate(outs, axis=0)


def _mla_body(ob, cos, sin, qnw_ref, kvnw_ref, wq_ref, wqs_ref, wk_ref, wv_ref, hw_ref, q_ref, k_ref, v_ref):
    cq = ob[:, 0:B_Q_RANK]
    ckv = ob[:, B_Q_RANK:B_Q_RANK + B_KV_RANK]
    krp = ob[:, B_Q_RANK + B_KV_RANK:B_Q_RANK + B_KV_RANK + B_HP]
    krs = ob[:, B_Q_RANK + B_KV_RANK + B_HP:B_Q_RANK + B_KV_RANK + 2 * B_HP]
    cqn = (cq * lax.rsqrt(jnp.mean(cq * cq, axis=-1, keepdims=True) + EPS) * qnw_ref[...]).astype(BF16)
    ckn = (ckv * lax.rsqrt(jnp.mean(ckv * ckv, axis=-1, keepdims=True) + EPS) * kvnw_ref[...]).astype(BF16)
    q_raw = _dot(cqn, wq_ref[...])
    q_swp = _dot(cqn, wqs_ref[...])
    k_raw = _dot(ckn, wk_ref[...])
    v_all = _dot(ckn, wv_ref[...])
    qw, qws, kw, kws = hw_ref[0:1, :], hw_ref[1:2, :], hw_ref[2:3, :], hw_ref[3:4, :]
    k_rot_sw = krs * kws * sin
    for h in range(B_HEADS):
        sl = slice(h * B_HP, (h + 1) * B_HP)
        qh = q_raw[:, sl]
        rq = lax.rsqrt(jnp.sum(qh * qh, axis=-1, keepdims=True) * (1.0 / B_QK) + EPS)
        q_ref[0, h] = (rq * (qh * qw * cos + q_swp[:, sl] * qws * sin)).astype(BF16)
        kh = k_raw[:, sl] + krp
        rk = lax.rsqrt(jnp.sum(kh * kh, axis=-1, keepdims=True) * (1.0 / B_QK) + EPS)
        k_ref[0, h] = (rk * (kh * kw * cos + k_rot_sw)).astype(BF16)
        v_ref[0, h] = v_all[:, h * B_V:(h + 1) * B_V].astype(BF16)


def _front_kernel(x_ref, lnw_ref, mod_ref, w_ref, anw_ref, aws_ref, abias_ref, cos_ref, sin_ref,
                  qnw_ref, kvnw_ref, wq_ref, wqs_ref, wk_ref, wv_ref, hw_ref,
                  a_ref, q_ref, k_ref, v_ref, oc_ref, og_ref, *, d, widths):
    x = x_ref[0]
    ms = jnp.mean(x * x, axis=-1, keepdims=True)
    y = x * lax.rsqrt(ms + EPS) * lnw_ref[...]
    shift = mod_ref[0, 0, :, 0:d]
    scale = mod_ref[0, 0, :, d:2 * d]
    h = (y * (1.0 + scale) + shift).astype(BF16)
    p = _dot(h, w_ref[...])
    na, nb, nc, _ = widths
    oc_ref[0] = p[:, na + nb:na + nb + nc]
    og_ref[0] = p[:, na + nb + nc:]
    a_ref[0] = _amix_body(p[:, 0:na], anw_ref, aws_ref, abias_ref)
    _mla_body(p[:, na:na + nb], cos_ref[...], sin_ref[...], qnw_ref, kvnw_ref, wq_ref, wqs_ref, wk_ref,
              wv_ref, hw_ref, q_ref, k_ref, v_ref)


def _front(x, lnw, modsel, w, cos_t, sin_t):
    bsz, n, d = x.shape
    na, nb, nc, ng = w["widths"]
    full = lambda *s: pl.BlockSpec(s, lambda b, j: (0,) * len(s))
    tok = lambda width: pl.BlockSpec((1, TB, width), lambda b, j: (b, j, 0))
    head = lambda width: pl.BlockSpec((1, B_HEADS, TB, width), lambda b, j: (b, 0, j, 0))
    return pl.pallas_call(
        functools.partial(_front_kernel, d=d, widths=w["widths"]),
        grid=(bsz, n // TB),
        in_specs=[
            tok(d), full(1, d),
            pl.BlockSpec((1, 1, 1, modsel.shape[-1]), lambda b, j: (b, jnp.minimum(j, 1), 0, 0)),
            full(d, w["w_all"].shape[1]),
            full(1, A_W), full(A_HEADS, A_CHUNK, A_CHUNK), full(A_CHUNK, A_W),
            pl.BlockSpec((TB, B_HP), lambda b, j: (j, 0)),
            pl.BlockSpec((TB, B_HP), lambda b, j: (j, 0)),
            full(1, B_Q_RANK), full(1, B_KV_RANK),
            full(B_Q_RANK, B_HEADS * B_HP), full(B_Q_RANK, B_HEADS * B_HP),
            full(B_KV_RANK, B_HEADS * B_HP), full(B_KV_RANK, B_HEADS * B_V),
            full(8, B_HP),
        ],
        out_specs=[tok(A_W), head(B_HP), head(B_HP), head(B_V), tok(nc), tok(ng)],
        out_shape=[
            jax.ShapeDtypeStruct((bsz, n, A_W), BF16),
            jax.ShapeDtypeStruct((bsz, B_HEADS, n, B_HP), BF16),
            jax.ShapeDtypeStruct((bsz, B_HEADS, n, B_HP), BF16),
            jax.ShapeDtypeStruct((bsz, B_HEADS, n, B_V), BF16),
            jax.ShapeDtypeStruct((bsz, n, nc), F32),
            jax.ShapeDtypeStruct((bsz, n, ng), F32),
        ],
        compiler_params=_cparams(("parallel", "parallel")),
        name="in_proj_mix",
    )(x, lnw, modsel, w["w_all"], w["a_nw"], w["a_ws"], w["a_bias"], cos_t, sin_t, w["qnw"], w["kvnw"],
      w["wq_p"], w["wq_s"], w["wk_p"], w["wv"], w["hw"])


def _attn_kernel(*refs, nqb):
    q_refs, (k_ref, v_ref, o_ref, o_scr) = refs[:nqb], refs[nqb:]
    scale = (B_QK ** -0.5) * math.log2(math.e)
    for h in range(B_HEADS):
        q = q_refs[0][0, h] if nqb == 1 else jnp.concatenate([r[0, h] for r in q_refs], axis=0)
        s = _dot_nt(q, k_ref[0, h])
        m = jnp.max(s, axis=-1, keepdims=True)
        p = jnp.exp2((s - m) * scale)
        l = jnp.sum(p, axis=-1, keepdims=True)
        o = _dot(p.astype(BF16), v_ref[0, h])
        o_scr[:, h * B_V:(h + 1) * B_V] = o / l
    o_ref[0] = o_scr[...].astype(BF16)


def _attention(q, k, v, first_qblock, n_qblocks, n_keys, out_rows, nqb):
    bsz = q.shape[0]
    qspec = lambda u: pl.BlockSpec((1, B_HEADS, TB, B_HP), lambda b, j: (b, 0, j * nqb + u + first_qblock, 0))
    return pl.pallas_call(
        functools.partial(_attn_kernel, nqb=nqb),
        grid=(bsz, n_qblocks // nqb),
        in_specs=[qspec(u) for u in range(nqb)] + [
            pl.BlockSpec((1, B_HEADS, n_keys, B_HP), lambda b, j: (b, 0, 0, 0)),
            pl.BlockSpec((1, B_HEADS, n_keys, B_V), lambda b, j: (b, 0, 0, 0)),
        ],
        out_specs=pl.BlockSpec((1, nqb * TB, B_HEADS * B_V), lambda b, j: (b, j, 0)),
        out_shape=jax.ShapeDtypeStruct((bsz, out_rows, B_HEADS * B_V), BF16),
        scratch_shapes=[pltpu.VMEM((nqb * TB, B_HEADS * B_V), F32)],
        compiler_params=_cparams(("parallel", "arbitrary")),
        name="mla_attention",
    )(*([q] * nqb), k, v)


def _hgrn_chunk(blk, zcol, lbc, st_ref, rev):
    cc, w = C_CHUNK, C_W
    q = blk[:, 0:w] * (C_DK ** -0.5)
    z = blk[:, zcol * w:(zcol + 1) * w]
    v = blk[:, 3 * w:4 * w]
    log_lb, log1m_lb, one_m_lb = lbc[0:1, :], lbc[1:2, :], lbc[2:3, :]
    az = jnp.abs(z)
    sp = jnp.log1p(jnp.exp(-az))
    lsig = jnp.minimum(z, 0.0) - sp
    t2 = log1m_lb + lsig
    mx = jnp.maximum(log_lb, t2)
    mn = jnp.minimum(log_lb, t2)
    logf = mx + jnp.log1p(jnp.exp(mn - mx))
    kk = one_m_lb * _sigmoid(-z)

    ti = lax.broadcasted_iota(jnp.int32, (cc, cc), 0)
    ui = lax.broadcasted_iota(jnp.int32, (cc, cc), 1)
    tri = ((ui >= ti) if rev else (ui <= ti)).astype(F32)
    b = jnp.dot(tri, logf, precision=HIGHEST, preferred_element_type=F32)
    b_tot = b[0:1, :] if rev else b[cc - 1:cc, :]

    row = lax.broadcasted_iota(jnp.int32, (cc, w), 0)
    lane_head = lax.broadcasted_iota(jnp.int32, (cc, w), 1) // C_DK
    nsub = cc // C_SUB
    row_blk = row // C_SUB

    beta_rows = []
    for i in range(nsub):
        if rev:
            src = None if i == nsub - 1 else b[(i + 1) * C_SUB:(i + 1) * C_SUB + 1, :]
        else:
            src = None if i == 0 else b[i * C_SUB - 1:i * C_SUB, :]
        beta_rows.append(src)
    beta_full = jnp.concatenate(
        [jnp.broadcast_to(b[i * C_SUB:i * C_SUB + 1, :] if r is None else r, (C_SUB, w))
         for i, r in enumerate(beta_rows)], axis=0)
    has_prev = (row_blk < nsub - 1) if rev else (row_blk > 0)
    qs = jnp.where(has_prev, q * jnp.exp(b - beta_full), 0.0)

    q_stack = jnp.concatenate([jnp.where(lane_head == h, qs, 0.0) for h in range(C_HEADS)], axis=0).astype(BF16)
    qblocks = [i for i in range(nsub) if beta_rows[i] is not None]
    ks_parts = []
    for i in qblocks:
        prev = (row_blk > i) if rev else (row_blk < i)
        ks_parts.append(jnp.where(prev, kk * jnp.exp(beta_rows[i] - b), 0.0))
    ks_all = jnp.concatenate(ks_parts, axis=0).astype(BF16)
    a_all = _dot_nt(q_stack, ks_all)
    ar = lax.broadcasted_iota(jnp.int32, a_all.shape, 0)
    ac = lax.broadcasted_iota(jnp.int32, a_all.shape, 1)
    r_blk = (ar % cc) // C_SUB
    c_blk = ac // cc + (0 if rev else 1)
    a_all = jnp.where(r_blk == c_blk, a_all, 0.0).astype(BF16)
    v_bf = v.astype(BF16)
    r_all = _dot(a_all, jnp.concatenate([v_bf] * len(qblocks), axis=0))
    o = jnp.zeros((cc, w), F32)
    for h in range(C_HEADS):
        o = o + jnp.where(lane_head == h, r_all[h * cc:(h + 1) * cc, :], 0.0)

    ones_bd = _block_ones(w, C_DK, BF16)
    tsub = lax.broadcasted_iota(jnp.int32, (C_SUB, w), 0)
    diag_parts = []
    for i in range(nsub):
        r0 = i * C_SUB
        bb = b[r0:r0 + C_SUB, :]
        qq = q[r0:r0 + C_SUB, :]
        ps = []
        for s in range(C_SUB):
            keep = (tsub <= s) if rev else (tsub >= s)
            e = jnp.where(keep, jnp.exp(bb - b[r0 + s:r0 + s + 1, :]), 0.0)
            ps.append(qq * e * kk[r0 + s:r0 + s + 1, :])
        red = _dot(jnp.concatenate(ps, axis=0).astype(BF16), ones_bd)
        od = jnp.zeros((C_SUB, w), F32)
        for s in range(C_SUB):
            od = od + red[s * C_SUB:(s + 1) * C_SUB, :] * v[r0 + s:r0 + s + 1, :]
        diag_parts.append(od)
    o = o + jnp.concatenate(diag_parts, axis=0)

    st = st_ref[...]
    o = o + _dot_nt((q * jnp.exp(b)).astype(BF16), st.astype(BF16))
    kd = (kk * jnp.exp(b_tot - b)).astype(BF16)
    upd = _dot_tn(v_bf, kd)
    st_ref[...] = st * jnp.exp(b_tot) + upd * _block_ones(w, C_DK, F32)
    return o


def _hgrn_kernel(cf_ref, cb_ref, lbc_ref, of_ref, ob_ref, sf_ref, sb_ref):
    @pl.when(pl.program_id(1) == 0)
    def _():
        sf_ref[...] = jnp.zeros_like(sf_ref)
        sb_ref[...] = jnp.zeros_like(sb_ref)

    for bb in range(HGRN_BPS):
        of_ref[bb] = _hgrn_chunk(cf_ref[bb], 1, lbc_ref[0], sf_ref.at[bb], rev=False)
        ob_ref[bb] = _hgrn_chunk(cb_ref[bb], 2, lbc_ref[1], sb_ref.at[bb], rev=True)


def _hgrn(oc, lbc, n_ctx):
    bsz, n, wc = oc.shape
    nch = n // C_CHUNK
    nctx = n_ctx // C_CHUNK
    nb = HGRN_BPS

    def bwd_idx(c):
        return jnp.where(c < nctx, nctx - 1 - c, nch + nctx - 1 - c)

    return pl.pallas_call(
        _hgrn_kernel,
        grid=(bsz // nb, nch),
        in_specs=[
            pl.BlockSpec((nb, C_CHUNK, wc), lambda b, c: (b, c, 0)),
            pl.BlockSpec((nb, C_CHUNK, wc), lambda b, c: (b, bwd_idx(c), 0)),
            pl.BlockSpec((2, 8, C_W), lambda b, c: (0, 0, 0)),
        ],
        out_specs=[
            pl.BlockSpec((nb, C_CHUNK, C_W), lambda b, c: (b, c, 0)),
            pl.BlockSpec((nb, C_CHUNK, C_W), lambda b, c: (b, bwd_idx(c), 0)),
        ],
        out_shape=[jax.ShapeDtypeStruct((bsz, n, C_W), F32)] * 2,
        scratch_shapes=[pltpu.VMEM((nb, C_W, C_W), F32)] * 2,
        compiler_params=_cparams(("parallel", "arbitrary")),
        name="hgrn2_scan",
    )(oc, oc, lbc)


def _outproj_kernel(x_ref, a_ref, b_ref, of_ref, ob_ref, g_ref, cnw_ref, wa_ref, wb_ref, wc_ref,
                    mod_ref, ln2_ref, xo_ref, h2_ref, *, d):
    o = of_ref[0] + ob_ref[0]
    ssq = jnp.dot(o * o, _block_ones(C_W, C_DV, F32), precision=HIGHEST, preferred_element_type=F32)
    g = g_ref[0]
    c_out = o * lax.rsqrt(ssq * (1.0 / C_DV) + EPS) * cnw_ref[...] * (g * _sigmoid(g))
    mix = _dot(a_ref[0], wa_ref[...]) + _dot(b_ref[0], wb_ref[...]) + _dot(c_out.astype(BF16), wc_ref[...])
    gate1 = mod_ref[0, 0, :, 2 * d:3 * d]
    shift2 = mod_ref[0, 0, :, 3 * d:4 * d]
    scale2 = mod_ref[0, 0, :, 4 * d:5 * d]
    x = x_ref[0] + gate1 * mix
    xo_ref[0] = x
    y = x * lax.rsqrt(jnp.mean(x * x, axis=-1, keepdims=True) + EPS) * ln2_ref[...]
    h2_ref[0] = (y * (1.0 + scale2) + shift2).astype(BF16)


def _outproj(x, a_out, b_out, o_f, o_b, og, cnw, wa, wb, wc, modsel, ln2, first_block):
    bsz, n, d = x.shape
    nblk = n // TB - first_block
    full = lambda *s: pl.BlockSpec(s, lambda b, j: (0,) * len(s))
    tok = lambda w: pl.BlockSpec((1, TB, w), lambda b, j: (b, j + first_block, 0))
    return pl.pallas_call(
        functools.partial(_outproj_kernel, d=d),
        grid=(bsz, nblk),
        in_specs=[
            tok(d), tok(A_W),
            pl.BlockSpec((1, TB, B_HEADS * B_V), lambda b, j: (b, j, 0)) if first_block else tok(B_HEADS * B_V),
            tok(C_W), tok(C_W), tok(C_W),
            full(1, C_W), full(A_W, d), full(B_HEADS * B_V, d), full(C_W, d),
            pl.BlockSpec((1, 1, 1, modsel.shape[-1]), lambda b, j: (b, jnp.minimum(j + first_block, 1), 0, 0)),
            full(1, d),
        ],
        out_specs=[
            pl.BlockSpec((1, TB, d), lambda b, j: (b, j, 0)),
            pl.BlockSpec((1, TB, d), lambda b, j: (b, j, 0)),
        ],
        out_shape=[
            jax.ShapeDtypeStruct((bsz, nblk * TB, d), F32),
            jax.ShapeDtypeStruct((bsz, nblk * TB, d), BF16),
        ],
        compiler_params=_cparams(("parallel", "parallel")),
        name="out_proj",
    )(x, a_out, b_out, o_f, o_b, og, cnw, wa, wb, wc, modsel, ln2)


def _top16_exact(s):
    nrows = s.shape[0]
    iota = lax.broadcasted_iota(jnp.int32, s.shape, 0).astype(F32)
    rank = jnp.full(s.shape, P_TOPK, jnp.int32)
    vals = []
    for r in range(P_TOPK):
        m = jnp.max(s, axis=0, keepdims=True)
        idx = jnp.min(jnp.where(s == m, iota, float(nrows)), axis=0, keepdims=True)
        hit = iota == idx
        rank = jnp.where(hit, r, rank)
        s = jnp.where(hit, -jnp.inf, s)
        vals.append(m)
    return jnp.concatenate(vals, axis=0), rank


_MARK0 = int(np.array(0xFF7FFFFF, np.uint32).view(np.int32))


def _top16_marked(s):
    vals = []
    for r in range(P_TOPK):
        m = jnp.max(s, axis=0, keepdims=True)
        mark = float(np.array(_MARK0 - r, np.int32).view(np.float32))
        s = jnp.where(s == m, mark, s)
        vals.append(m)
    rr = _MARK0 - pltpu.bitcast(s, jnp.int32)
    rank = jnp.where(rr < 0, P_TOPK, jnp.where(rr > P_TOPK - 1, P_TOPK, rr))
    slack = jnp.sum(P_TOPK - rank, axis=0, keepdims=True) - (P_TOPK * (P_TOPK + 1)) // 2
    return jnp.concatenate(vals, axis=0), rank, slack


def _as_words(x):
    return pltpu.bitcast(x.astype(BF16), jnp.int32)


def _row_plane(row):
    n = row.shape[-1]
    tile = jnp.broadcast_to(row, (16, n)).astype(BF16)
    return jnp.concatenate([tile] * (P_N_KEYS // 16), axis=0)


def _route_kernel(h_ref, wq_ref, sk_ref, r2_ref, e2_ref, n1_ref, e1_ref,
                  q_scr, s_scr, v_scr, r1_scr, e1_scr):
    nl = P_N_KEYS
    q = _dot(h_ref[...], wq_ref[...]).astype(BF16)
    for l in range(2 * P_HEADS):
        q_scr[l] = q[:, l * P_HALF:(l + 1) * P_HALF]

    def lane_tile(lt, carry):
        row0 = pl.multiple_of(lt * nl, nl)

        def put_rank(h, p, rank):
            if p == 0:
                r1_scr[h] = rank
            else:
                r2_ref[lt, h] = _as_words(rank.astype(F32))

        def heads(hp, c):
            bad = jnp.zeros((1, nl), jnp.int32)
            for u in range(2 * ROUTE_HPT):
                h, p = hp * ROUTE_HPT + u // 2, u % 2
                s = _dot_nt(sk_ref[h, p], q_scr[2 * h + p, pl.ds(row0, nl), :])
                s_scr[u] = s
                v, rank, slack = _top16_marked(s)
                v_scr[p, h] = v
                put_rank(h, p, rank)
                e = jnp.exp(s - v[0:1, :])
                if p == 0:
                    e1_scr[h] = e
                else:
                    e2_ref[lt, h] = _as_words(e)
                bad = bad + slack

            @pl.when(jnp.max(bad) > 0)
            def _():
                for u in range(2 * ROUTE_HPT):
                    h, p = hp * ROUTE_HPT + u // 2, u % 2
                    ve, re = _top16_exact(s_scr[u])
                    v_scr[p, h] = ve
                    put_rank(h, p, re)

            return c

        lax.fori_loop(0, P_HEADS // ROUTE_HPT, heads, 0)

        i8 = lax.broadcasted_iota(jnp.int32, (8, nl), 0)
        for h in range(P_HEADS):
            v1 = v_scr[0, h]
            v2 = v_scr[1, h]
            slabs = [v1[0:1, :] + v2, v1[1:2, :] + v2[0:8, :]]
            for a in range(2, 8):
                slabs.append(jnp.where(i8 < P_TOPK // (a + 1), v1[a:a + 1, :] + v2[0:8, :], -jnp.inf))
            slabs.append(v1[8:16, :] + v2[0:1, :])
            cand = jnp.concatenate(slabs, axis=0)
            cmax = v1[0:1, :] + v2[0:1, :]

            def counts(rank, h=h, cand=cand, cmax=cmax):
                sel = rank < P_TOPK
                one = jnp.where(sel, 1.0, 0.0)
                rows = [jnp.sum(one[0:16, :], axis=0, keepdims=True)]
                rows += [jnp.sum(one[8 + 8 * a:16 + 8 * a, :], axis=0, keepdims=True) for a in range(1, 8)]
                s_scr[h, 0:16, :] = jnp.concatenate(rows + [one[72:80, :]], axis=0)
                s_scr[h, 16:17, :] = jnp.sum(jnp.where(sel, jnp.exp(cand - cmax), 0.0), axis=0, keepdims=True)

            _, crank, cslack = _top16_marked(cand)
            counts(crank)

            @pl.when(jnp.max(cslack) > 0)
            def _():
                counts(_top16_exact(cand)[1])

            n = s_scr[h, 0:16, :]
            zinv = 1.0 / s_scr[h, 16:17, :]
            rank1 = r1_scr[h]
            n1 = jnp.zeros(rank1.shape, F32)
            for a in range(P_TOPK):
                n1 = jnp.where(rank1 == a, n[a:a + 1, :], n1)
            n1_ref[lt, h] = n1
            e1_ref[lt, h] = e1_scr[h] * zinv
        return carry

    def lane_pair(t2, carry):
        lane_tile(2 * t2, carry)
        return lane_tile(2 * t2 + 1, carry)

    lax.fori_loop(0, h_ref.shape[0] // nl // 2, lane_pair, 0)


def _route(h2, wq, sk):
    t, d = h2.shape
    tr = PEER_TT
    nl = P_N_KEYS
    oshape = (t // nl, P_HEADS, P_N_KEYS, nl)
    hshape = (t // nl, P_HEADS, P_N_KEYS // 2, nl)
    ospec = pl.BlockSpec((tr // nl, P_HEADS, P_N_KEYS, nl), lambda i: (i, 0, 0, 0))
    hspec = pl.BlockSpec((tr // nl, P_HEADS, P_N_KEYS // 2, nl), lambda i: (i, 0, 0, 0))
    return pl.pallas_call(
        _route_kernel,
        grid=(t // tr,),
        in_specs=[
            pl.BlockSpec((tr, d), lambda i: (i, 0)),
            pl.BlockSpec((d, P_HEADS * P_KEY_DIM), lambda i: (0, 0)),
            pl.BlockSpec((P_HEADS, 2, P_N_KEYS, P_HALF), lambda i: (0, 0, 0, 0)),
        ],
        out_specs=[hspec, hspec, ospec, ospec],
        out_shape=[jax.ShapeDtypeStruct(hshape, jnp.int32), jax.ShapeDtypeStruct(hshape, jnp.int32),
                   jax.ShapeDtypeStruct(oshape, F32), jax.ShapeDtypeStruct(oshape, F32)],
        scratch_shapes=[
            pltpu.VMEM((2 * P_HEADS, tr, P_HALF), BF16),
            pltpu.VMEM((2 * ROUTE_HPT, P_N_KEYS, nl), F32),
            pltpu.VMEM((2, P_HEADS, P_TOPK, nl), F32),
            pltpu.VMEM((P_HEADS, P_N_KEYS, nl), jnp.int32),
            pltpu.VMEM((P_HEADS, P_N_KEYS, nl), F32),
        ],
        compiler_params=_cparams(("parallel",)),
        name="peer_route",
    )(h2, wq, sk)


def _peer_kernel(h_ref, u_ref, v_ref, r2_ref, e2_ref, n1_ref, e1_ref, x_ref, g5_ref, o_ref,
                 acc_ref, w_ref, a_scr, *, blocks_per_batch, ctx_blocks):
    i = pl.program_id(0)
    k = pl.program_id(1)

    @pl.when(k == 0)
    def _():
        acc_ref[...] = jnp.zeros_like(acc_ref)

    nl = P_N_KEYS
    zero = jnp.zeros((P_N_KEYS, nl), BF16)
    ltc = PEER_TC // nl
    njg = PEER_I1 // PEER_JG

    def gate_block(lt, j0):
        g = [None] * PEER_JG
        for h in range(P_HEADS):
            r2 = pltpu.bitcast(r2_ref[lt, h], BF16)
            e2 = pltpu.bitcast(e2_ref[lt, h], BF16)
            for jj in range(PEER_JG):
                n1row = _row_plane(n1_ref[lt, h, pl.ds(j0 + jj, 1), :])
                e1row = _row_plane(e1_ref[lt, h, pl.ds(j0 + jj, 1), :])
                term = jnp.where(r2 < n1row, e2, zero) * e1row
                g[jj] = term if h == 0 else g[jj] + term
        for jj in range(PEER_JG):
            rows = pl.ds(pl.multiple_of((j0 + jj) * P_N_KEYS, P_N_KEYS), P_N_KEYS)
            w_ref[lt, rows, :] = a_scr[lt, rows, :] * g[jj]

    for c in range(PEER_TT // PEER_TC):
        tok = slice(c * PEER_TC, (c + 1) * PEER_TC)
        a_t = _dot_nt(u_ref[0].astype(BF16), h_ref[tok, :])
        half_cdf = (0.5 * lax.erf(a_t * (2.0 ** -0.5))).astype(BF16) + 0.5
        act_t = a_t.astype(BF16) * half_cdf
        for lc in range(ltc):
            a_scr[c * ltc + lc] = act_t[:, lc * nl:(lc + 1) * nl]

        def body(it, carry, c=c):
            gate_block(c * ltc + it // njg, (it % njg) * PEER_JG)
            return carry

        lax.fori_loop(0, ltc * njg, body, 0)
        w_t = jnp.concatenate([w_ref[c * ltc + lc] for lc in range(ltc)], axis=1)
        acc_ref[:, tok] += _dot_tn(v_ref[0].astype(BF16), w_t)

    @pl.when(k == pl.num_programs(1) - 1)
    def _():
        y = acc_ref[...].T
        for u in range(PEER_TT // TB):
            sblk = i * (PEER_TT // TB) + u
            bidx = sblk // blocks_per_batch
            is_lat = (sblk - bidx * blocks_per_batch) >= ctx_blocks
            gate = g5_ref[2 * bidx + is_lat.astype(jnp.int32)]
            rs = slice(u * TB, (u + 1) * TB)
            o_ref[rs, :] = x_ref[rs, :] + gate * y[rs, :]


def _peer(h2, u_all, v_all, layer, r2, e2, n1, e1, x, g5, blocks_per_batch, ctx_blocks):
    t, d = h2.shape
    ne = u_all.shape[1]
    nl = P_N_KEYS
    rspec = pl.BlockSpec((PEER_TT // nl, P_HEADS, P_N_KEYS // 2, nl), lambda i, k: (i, 0, 0, 0))
    nspec = pl.BlockSpec((PEER_TT // nl, P_HEADS, PEER_I1, nl), lambda i, k: (i, 0, k, 0))
    return pl.pallas_call(
        functools.partial(_peer_kernel, blocks_per_batch=blocks_per_batch, ctx_blocks=ctx_blocks),
        grid=(t // PEER_TT, ne // PEER_ET),
        in_specs=[
            pl.BlockSpec((PEER_TT, d), lambda i, k: (i, 0)),
            pl.BlockSpec((1, PEER_ET, d), lambda i, k: (layer, k, 0)),
            pl.BlockSpec((1, PEER_ET, d), lambda i, k: (layer, k, 0)),
            rspec, rspec, nspec, nspec,
            pl.BlockSpec((PEER_TT, d), lambda i, k: (i, 0)),
            pl.BlockSpec(g5.shape, lambda i, k: (0, 0, 0)),
        ],
        out_specs=pl.BlockSpec((PEER_TT, d), lambda i, k: (i, 0)),
        out_shape=jax.ShapeDtypeStruct((t, d), F32),
        scratch_shapes=[pltpu.VMEM((d, PEER_TT), F32),
                        pltpu.VMEM((PEER_TT // nl, PEER_ET, nl), BF16),
                        pltpu.VMEM((PEER_TT // nl, PEER_ET, nl), BF16)],
        compiler_params=_cparams(("parallel", "arbitrary")),
        name="peer_experts",
    )(h2, u_all, v_all, r2, e2, n1, e1, x, g5)


def _rope_tables(n_ctx, n_lat):
    n_freq = B_ROPE // 4
    pos = np.arange(n_lat)
    inv_freq = ROPE_THETA ** (-np.arange(n_freq, dtype=np.float32) / n_freq)
    inv_freq = jnp.asarray(inv_freq, F32)
    rowp = jnp.asarray(pos // GRID_W, F32)
    colp = jnp.asarray(pos % GRID_W, F32)
    ang = jnp.stack([rowp[:, None] * inv_freq, colp[:, None] * inv_freq], axis=1)
    cos, sin = jnp.cos(ang), jnp.sin(ang)
    cos32 = jnp.concatenate([cos, cos], axis=2).reshape(n_lat, B_ROPE)
    sin32 = jnp.concatenate([-sin, sin], axis=2).reshape(n_lat, B_ROPE)
    pad_l = jnp.ones((n_lat, B_NOPE), F32)
    pad_r = jnp.ones((n_lat, B_HP - B_QK), F32)
    cos_l = jnp.concatenate([pad_l, cos32, pad_r], axis=1)
    sin_l = jnp.concatenate([0 * pad_l, sin32, 0 * pad_r], axis=1)
    cos_t = jnp.concatenate([jnp.ones((n_ctx, B_HP), F32), cos_l], axis=0)
    sin_t = jnp.concatenate([jnp.zeros((n_ctx, B_HP), F32), sin_l], axis=0)
    return cos_t, sin_t


_SWAP32 = np.arange(B_ROPE) ^ (B_ROPE // 4)


def _pad_head(nope, rope):
    z = jnp.zeros(nope.shape[:-1] + (B_HP - B_QK,), nope.dtype)
    out = jnp.concatenate([nope, rope, z], axis=-1)
    return out.reshape(out.shape[:-2] + (out.shape[-2] * B_HP,))


def _layer_weights(layer, w_in, w_out, a_norm_w, a_w_s, a_b_s, b_q_norm_w, b_w_uq, b_kv_norm_w, b_w_ukv,
                   b_q_head_norm_w, b_k_head_norm_w, c_out_norm_w, p_w_q, p_sub_keys):
    d = w_in.shape[1]
    wi = w_in[layer]
    offs = np.cumsum([0, A_W, A_W, B_Q_RANK, B_KV_RANK, B_ROPE, C_W, C_W, C_W, C_W, C_W])
    col = lambda i: wi[:, offs[i]:offs[i + 1]]
    w_kr = col(4)
    zl = jnp.zeros((d, B_NOPE), F32)
    zr = jnp.zeros((d, B_HP - B_QK), F32)
    kr_placed = jnp.concatenate([zl, w_kr, zr], axis=1)
    kr_swapped = jnp.concatenate([zl, w_kr[:, _SWAP32], zr], axis=1)
    w_all = jnp.concatenate(
        [col(0), col(1), col(2), col(3), kr_placed, kr_swapped, col(5), col(6), col(7), col(8), col(9)],
        axis=1).astype(BF16)
    widths = (2 * A_W, B_Q_RANK + B_KV_RANK + 2 * B_HP, 4 * C_W, C_W)

    wuq = b_w_uq[layer].reshape(B_Q_RANK, B_HEADS, B_QK)
    wq_p = _pad_head(wuq[..., :B_NOPE], wuq[..., B_NOPE:]).astype(BF16)
    wq_s = _pad_head(0 * wuq[..., :B_NOPE], wuq[..., B_NOPE:][..., _SWAP32]).astype(BF16)
    wukv = b_w_ukv[layer].reshape(B_KV_RANK, B_HEADS, B_NOPE + B_V)
    wk_p = _pad_head(wukv[..., :B_NOPE], jnp.zeros((B_KV_RANK, B_HEADS, B_ROPE), F32)).astype(BF16)
    wv = wukv[..., B_NOPE:].reshape(B_KV_RANK, B_HEADS * B_V).astype(BF16)
    qn, kn = b_q_head_norm_w[layer], b_k_head_norm_w[layer]
    zpad = jnp.zeros((B_HP - B_QK,), F32)
    z64 = jnp.zeros((B_NOPE,), F32)
    hw = jnp.stack([
        jnp.concatenate([qn, zpad]),
        jnp.concatenate([z64, qn[B_NOPE:][_SWAP32], zpad]),
        jnp.concatenate([kn, zpad]),
        jnp.concatenate([z64, kn[B_NOPE:][_SWAP32], zpad]),
    ] + [jnp.zeros((B_HP,), F32)] * 4, axis=0)

    wo = w_out[layer].astype(BF16)
    return dict(
        w_all=w_all, widths=widths,
        a_nw=a_norm_w[layer].reshape(1, A_W),
        a_ws=a_w_s[layer].astype(BF16),
        a_bias=jnp.repeat(a_b_s[layer].T, A_HD, axis=1),
        qnw=b_q_norm_w[layer].reshape(1, B_Q_RANK), kvnw=b_kv_norm_w[layer].reshape(1, B_KV_RANK),
        wq_p=wq_p, wq_s=wq_s, wk_p=wk_p, wv=wv, hw=hw,
        cnw=jnp.tile(c_out_norm_w[layer], C_HEADS).reshape(1, C_W),
        wo_a=wo[:A_W], wo_b=wo[A_W:A_W + B_HEADS * B_V], wo_c=wo[A_W + B_HEADS * B_V:],
        p_wq=p_w_q[layer].astype(BF16), p_sk=p_sub_keys[layer].astype(BF16),
    )


def kernel(x, c, ctx, c_ctx, ln1_w, ln2_w, w_mod, b_mod, w_in, w_out, a_norm_w, a_w_s, a_b_s, b_q_norm_w,
           b_w_uq, b_kv_norm_w, b_w_ukv, b_q_head_norm_w, b_k_head_norm_w, c_lb_logits, c_out_norm_w,
           p_w_q, p_sub_keys, p_u, p_v):
    bsz, n_lat, d = x.shape
    n_ctx = ctx.shape[1]
    depth = w_in.shape[0]
    n = n_ctx + n_lat
    ctx_blocks = n_ctx // TB
    blocks = n // TB

    mrows = -(-(bsz + 1) // 8) * 8
    cvec = jnp.concatenate([c, c_ctx[None, :], jnp.zeros((mrows - bsz - 1, d), F32)], axis=0)
    mod_all = _modulation(cvec, w_mod, b_mod)

    lb = jnp.cumsum(jax.nn.softmax(c_lb_logits.astype(F32), axis=0), axis=0)
    lb = lb - lb[0:1]
    lbc_all = jnp.stack([jnp.log(lb), jnp.log1p(-lb), 1.0 - lb] + [jnp.zeros_like(lb)] * 5, axis=2)

    cos_t, sin_t = _rope_tables(n_ctx, n_lat)
    xc = jnp.concatenate([ctx, x], axis=1)

    for layer in range(depth):
        last = layer == depth - 1
        w = _layer_weights(layer, w_in, w_out, a_norm_w, a_w_s, a_b_s, b_q_norm_w, b_w_uq, b_kv_norm_w,
                           b_w_ukv, b_q_head_norm_w, b_k_head_norm_w, c_out_norm_w, p_w_q, p_sub_keys)
        mod_b = mod_all[layer, :bsz]
        mod_c = jnp.broadcast_to(mod_all[layer, bsz][None, :], mod_b.shape)
        modsel = jnp.stack([mod_c, mod_b], axis=1)[:, :, None, :]
        first_block = ctx_blocks if last else 0

        a_out, q, k, v, oc, og = _front(xc, ln1_w[layer].reshape(1, d), modsel, w, cos_t, sin_t)
        b_lat = _attention(q, k, v, ctx_blocks, blocks - ctx_blocks, n, n_lat, 2)
        if last:
            b_out = b_lat
        else:
            b_ctx = _attention(q, k, v, 0, ctx_blocks, n_ctx, n_ctx, 1)
            b_out = jnp.concatenate([b_ctx, b_lat], axis=1)
        o_f, o_b = _hgrn(oc, lbc_all[layer], n_ctx)
        x_new, h2 = _outproj(xc, a_out, b_out, o_f, o_b, og, w["cnw"], w["wo_a"], w["wo_b"], w["wo_c"],
                             modsel, ln2_w[layer].reshape(1, d), first_block)

        t = x_new.shape[0] * x_new.shape[1]
        h2f = h2.reshape(t, d)
        r2, e2, n1, e1 = _route(h2f, w["p_wq"], w["p_sk"])
        g5 = modsel[:, :, :, 5 * d:6 * d].reshape(2 * bsz, 1, d)
        out = _peer(h2f, p_u, p_v, layer, r2, e2, n1, e1, x_new.reshape(t, d), g5,
                    blocks - first_block, ctx_blocks - first_block)
        xc = out.reshape(bsz, t // bsz, d)
    return xc
```
